```python
import math
import jax
import jax.numpy as jnp
from jax import lax
import numpy as np

D_MODEL = 1024
BATCH = 8
SEQ = 2048
DEPTH = 4
DEC_BATCH = 32
DEC_SEQ = 1
PAST_LEN = 8192
PAGE_SIZE = 128

N_MIXERS = 3
N_LAYERS_NSA = (DEPTH + 2) // 3
N_LAYERS_HGRN = (DEPTH + 1) // 3
N_LAYERS_SSD = DEPTH // 3

ADA_CHUNKS = 6
NORM_EPS = 1e-6
D_FF = 4 * D_MODEL

NSA_HEADS = 16
NSA_HEAD_DIM = D_MODEL // NSA_HEADS
NSA_KV_HEADS = 4
NSA_GROUP = NSA_HEADS // NSA_KV_HEADS
CMP_STRIDE = 16
CMP_LEN = 2 * CMP_STRIDE
CMP_HIDDEN = 2 * NSA_HEAD_DIM
SEL_BLOCK = 64
SEL_TOP_N = 16
WINDOW = 512
WIN_Q_BLOCK = 128
SEL_Q_BLOCK = 16
NSA_Q_W = NSA_HEADS * NSA_HEAD_DIM
NSA_KV_W = NSA_KV_HEADS * NSA_HEAD_DIM
NSA_IN_W = NSA_Q_W + 6 * NSA_KV_W + 3 * NSA_HEADS

REL_BUCKETS = 32
REL_MAX_DIST = 128

HG_EXPAND = 128
HG_HEADS = D_MODEL // HG_EXPAND
HG_DK = HG_EXPAND
HG_DV = D_MODEL // HG_HEADS
HG_CHUNK = 64
HG_IN_W = 2 * HG_HEADS * HG_DK + 2 * HG_HEADS * HG_DV

SSD_D_INNER = 2 * D_MODEL
SSD_HEAD_DIM = 64
SSD_HEADS = SSD_D_INNER // SSD_HEAD_DIM
SSD_GROUPS = 8
SSD_HPG = SSD_HEADS // SSD_GROUPS
SSD_STATE = 128
SSD_CONV_W = 4
SSD_CONV_DIM = SSD_D_INNER + 2 * SSD_GROUPS * SSD_STATE
SSD_IN_W = SSD_D_INNER + SSD_CONV_DIM + SSD_HEADS
SSD_CHUNK = 128

NEG_INF = -1e30
FORCE_SCORE = 1e4

kernel_name = 'hybrid_nsa_hgrn2_ssd_step'


def _rmsnorm(x, g):
    x32 = x.astype(jnp.float32)
    y = x32 * lax.rsqrt(jnp.mean(x32 * x32, axis=-1, keepdims=True) + NORM_EPS)
    return (y * g.astype(jnp.float32)).astype(x.dtype)


def _masked_softmax(logits, mask):
    l = jnp.where(mask, logits.astype(jnp.float32), NEG_INF)
    m = jnp.max(l, axis=-1, keepdims=True)
    e = jnp.where(mask, jnp.exp(l - m), 0.0)
    return e / jnp.maximum(jnp.sum(e, axis=-1, keepdims=True), 1e-30)


def _rel_bucket(dist):
    n = jnp.maximum(dist, 0)
    n_exact = REL_BUCKETS // 2
    nf = jnp.maximum(n, 1).astype(jnp.float32)
    large = n_exact + (jnp.log(nf / n_exact) / math.log(REL_MAX_DIST / n_exact)
                       * (REL_BUCKETS - n_exact)).astype(jnp.int32)
    return jnp.where(n < n_exact, n, jnp.minimum(large, REL_BUCKETS - 1))


def _head_bias(rel_bias, dist):
    q, t = dist.shape
    b = rel_bias[_rel_bucket(dist)].astype(jnp.float32)
    return b.reshape(q, t, NSA_KV_HEADS, NSA_GROUP).transpose(0, 2, 3, 1)


def _ada(c, w, b):
    mod = (jax.nn.silu(c) @ w + b)[:, None, :]
    return jnp.split(mod, ADA_CHUNKS, axis=-1)


def _sq_relu_mlp(h, w_in, w_out):
    return jnp.square(jax.nn.relu(h @ w_in)) @ w_out


def _nsa_project(h, w_in):
    b, t, _ = h.shape
    proj = h @ w_in
    o1 = NSA_Q_W
    o2 = o1 + 4 * NSA_KV_W
    o3 = o2 + 2 * NSA_KV_W
    qg = proj[..., :o1].reshape(b, t, NSA_KV_HEADS, NSA_GROUP, NSA_HEAD_DIM)
    kv_cs = proj[..., o1:o2].reshape(b, t, 4, NSA_KV_HEADS, NSA_HEAD_DIM)
    kv_win = proj[..., o2:o3].reshape(b, t, 2, NSA_KV_HEADS, NSA_HEAD_DIM)
    gates = jax.nn.sigmoid(proj[..., o3:].astype(jnp.float32)).reshape(b, t, 3, NSA_KV_HEADS, NSA_GROUP)
    return qg, kv_cs, kv_win, gates


def _nsa_compress(kv_pad, cmp_pos, w1, w2):
    b, tpad = kv_pad.shape[:2]
    n_chunks = tpad // CMP_STRIDE
    chunks = kv_pad[:, :, :2].reshape(b, n_chunks, CMP_STRIDE, 2, NSA_KV_HEADS, NSA_HEAD_DIM)
    w1 = w1.reshape(2, CMP_LEN, NSA_HEAD_DIM, CMP_HIDDEN)
    pos_bias = jnp.einsum('rld,rlde->re', cmp_pos, w1)
    pa = jnp.einsum('bnlrkd,rlde->bnrke', chunks, w1[:, :CMP_STRIDE])
    pb = jnp.einsum('bnlrkd,rlde->bnrke', chunks, w1[:, CMP_STRIDE:])
    hidden = jax.nn.gelu(pa[:, :-1] + pb[:, 1:] + pos_bias[:, None, :])
    out = jnp.einsum('bnrke,ref->bnrkf', hidden, w2)
    return out[:, :, 0], out[:, :, 1]


def _nsa_sparse_block(qg, q_pos, k_cmp, v_cmp, cmp_end, sel_kv, rel_bias):
    b, qc = qg.shape[:2]
    n_sel = sel_kv.shape[1]
    scale = NSA_HEAD_DIM ** -0.5
    dist_c = q_pos[:, None] - cmp_end[None, :]
    logits_c = jnp.einsum('bqkgd,bnkd->bqkgn', qg, k_cmp) * scale + _head_bias(rel_bias, dist_c)
    p_c = _masked_softmax(logits_c, (dist_c >= 0)[:, None, None, :])
    o_cmp = jnp.einsum('bqkgn,bnkd->bqkgd', p_c, v_cmp.astype(jnp.float32))
    imp = jnp.pad(p_c.sum(axis=3), ((0, 0), (0, 0), (0, 0), (1, 1)))
    imp = 0.5 * (imp[..., 1:] + imp[..., :-1])
    imp = imp.reshape(b, qc, NSA_KV_HEADS, n_sel, SEL_BLOCK // CMP_STRIDE).sum(-1)
    blk = jnp.arange(n_sel)[None, :]
    cur = (q_pos // SEL_BLOCK)[:, None]
    forced = (blk == 0) | (blk == cur) | (blk == cur - 1)
    valid = blk * SEL_BLOCK <= q_pos[:, None]
    score = jnp.where(forced[None, :, None, :], FORCE_SCORE, imp)
    score = jnp.where(valid[None, :, None, :], score, NEG_INF)
    n_cand = max(n_sel, SEL_TOP_N)
    score = jnp.pad(score, ((0, 0), (0, 0), (0, 0), (0, n_cand - n_sel)), constant_values=NEG_INF)
    top_val, top_idx = lax.top_k(score, SEL_TOP_N)
    sel_ok = top_val > 0.5 * NEG_INF
    top_idx = jnp.minimum(top_idx, n_sel - 1)
    bi = jnp.arange(b)[:, None, None, None]
    ki = jnp.arange(NSA_KV_HEADS)[None, None, :, None]
    kv_sel = sel_kv[bi, top_idx, :, :, ki]
    kpos = top_idx[..., None] * SEL_BLOCK + jnp.arange(SEL_BLOCK)
    dist_s = q_pos[None, :, None, None, None] - kpos
    mask_s = (sel_ok[..., None] & (dist_s >= 0)).reshape(b, qc, NSA_KV_HEADS, 1, SEL_TOP_N * SEL_BLOCK)
    table = rel_bias.reshape(REL_BUCKETS, NSA_KV_HEADS, NSA_GROUP).transpose(1, 0, 2)
    bias_s = table[jnp.arange(NSA_KV_HEADS)[None, None, :, None, None], _rel_bucket(dist_s)]
    bias_s = jnp.moveaxis(bias_s, -1, 3).reshape(b, qc, NSA_KV_HEADS, NSA_GROUP, SEL_TOP_N * SEL_BLOCK)
    logits_s = jnp.einsum('bqkgd,bqknsd->bqkgns', qg, kv_sel[..., 0, :])
    logits_s = logits_s.reshape(b, qc, NSA_KV_HEADS, NSA_GROUP, SEL_TOP_N * SEL_BLOCK) * scale + bias_s.astype(jnp.float32)
    p_s = _masked_softmax(logits_s, mask_s)
    v_sel = kv_sel[..., 1, :].reshape(b, qc, NSA_KV_HEADS, SEL_TOP_N * SEL_BLOCK, NSA_HEAD_DIM)
    o_sel = jnp.einsum('bqkgm,bqkmd->bqkgd', p_s, v_sel.astype(jnp.float32))
    return o_cmp, o_sel


def _nsa_sparse(qg, q_pos, kv_all, cmp_pos, w1, w2, rel_bias):
    b, t_kv = kv_all.shape[:2]
    t_pad = -(-t_kv // SEL_BLOCK) * SEL_BLOCK
    kv_pad = jnp.pad(kv_all, ((0, 0), (0, t_pad - t_kv), (0, 0), (0, 0), (0, 0)))
    k_cmp, v_cmp = _nsa_compress(kv_pad, cmp_pos, w1, w2)
    cmp_end = jnp.arange(k_cmp.shape[1]) * CMP_STRIDE + CMP_LEN - 1
    sel_kv = kv_pad[:, :, 2:].reshape(b, t_pad // SEL_BLOCK, SEL_BLOCK, 2, NSA_KV_HEADS, NSA_HEAD_DIM)
    tq = qg.shape[1]
    qb = SEL_Q_BLOCK if tq % SEL_Q_BLOCK == 0 else tq
    nq = tq // qb
    q_blocks = jnp.moveaxis(qg.reshape(b, nq, qb, NSA_KV_HEADS, NSA_GROUP, NSA_HEAD_DIM), 1, 0)
    pos_blocks = q_pos.reshape(nq, qb)
    o_cmp, o_sel = lax.map(
        lambda a: _nsa_sparse_block(a[0], a[1], k_cmp, v_cmp, cmp_end, sel_kv, rel_bias),
        (q_blocks, pos_blocks))
    shape = (b, tq, NSA_KV_HEADS, NSA_GROUP, NSA_HEAD_DIM)
    return jnp.moveaxis(o_cmp, 0, 1).reshape(shape), jnp.moveaxis(o_sel, 0, 1).reshape(shape)


def _window_attend(qg, q_pos, kv, k_pos, rel_bias):
    dist = q_pos[:, None] - k_pos[None, :]
    mask = (dist >= 0) & (dist <= WINDOW) & (k_pos >= 0)[None, :]
    logits = jnp.einsum('bqkgd,btkd->bqkgt', qg, kv[:, :, 0]) * NSA_HEAD_DIM ** -0.5 + _head_bias(rel_bias, dist)
    p = _masked_softmax(logits, mask[:, None, None, :])
    return jnp.einsum('bqkgt,btkd->bqkgd', p, kv[:, :, 1].astype(jnp.float32))


def _nsa_window_prompt(qg, kv_win, rel_bias):
    b, t = qg.shape[:2]
    nb = t // WIN_Q_BLOCK
    kv_pad = jnp.pad(kv_win, ((0, 0), (WINDOW, 0), (0, 0), (0, 0), (0, 0)))
    q_blocks = jnp.moveaxis(qg.reshape(b, nb, WIN_Q_BLOCK, NSA_KV_HEADS, NSA_GROUP, NSA_HEAD_DIM), 1, 0)

    def block(args):
        qblk, i = args
        start = i * WIN_Q_BLOCK
        kv = lax.dynamic_slice_in_dim(kv_pad, start, WINDOW + WIN_Q_BLOCK, axis=1)
        k_pos = start - WINDOW + jnp.arange(WINDOW + WIN_Q_BLOCK)
        q_pos = start + jnp.arange(WIN_Q_BLOCK)
        return _window_attend(qblk, q_pos, kv, k_pos, rel_bias)

    o = lax.map(block, (q_blocks, jnp.arange(nb)))
    return jnp.moveaxis(o, 0, 1).reshape(b, t, NSA_KV_HEADS, NSA_GROUP, NSA_HEAD_DIM)


def _nsa_output(o_cmp, o_sel, o_win, gates, w_out, dtype):
    b, t = gates.shape[:2]
    o = (gates[:, :, 0, ..., None] * o_cmp + gates[:, :, 1, ..., None] * o_sel
         + gates[:, :, 2, ..., None] * o_win)
    return o.reshape(b, t, NSA_Q_W).astype(dtype) @ w_out


def _nsa_prompt(h, w_in, cmp_pos, cmp_w1, cmp_w2, w_out, rel_bias):
    t = h.shape[1]
    qg, kv_cs, kv_win, gates = _nsa_project(h, w_in)
    o_cmp, o_sel = _nsa_sparse(qg, jnp.arange(t), kv_cs, cmp_pos, cmp_w1, cmp_w2, rel_bias)
    o_win = _nsa_window_prompt(qg, kv_win, rel_bias)
    out = _nsa_output(o_cmp, o_sel, o_win, gates, w_out, h.dtype)
    return out, kv_cs, kv_win[:, t - min(WINDOW, t):]


def _nsa_sample(h, kv_pages, win_buf, page_table, w_in, cmp_pos, cmp_w1, cmp_w2, w_out, rel_bias):
    b, t, _ = h.shape
    past_len = page_table.shape[1] * PAGE_SIZE
    qg, kv_cs, kv_win, gates = _nsa_project(h, w_in)
    past = kv_pages[page_table].reshape(b, past_len, 4, NSA_KV_HEADS, NSA_HEAD_DIM)
    kv_all = jnp.concatenate([past, kv_cs.astype(past.dtype)], axis=1)
    q_pos = past_len + jnp.arange(t)
    o_cmp, o_sel = _nsa_sparse(qg, q_pos, kv_all, cmp_pos, cmp_w1, cmp_w2, rel_bias)
    n_buf = win_buf.shape[1]
    kv_w_all = jnp.concatenate([win_buf, kv_win.astype(win_buf.dtype)], axis=1)
    k_pos = past_len - n_buf + jnp.arange(n_buf + t)
    o_win = _window_attend(qg, q_pos, kv_w_all, k_pos, rel_bias)
    out = _nsa_output(o_cmp, o_sel, o_win, gates, w_out, h.dtype)
    new_win = kv_w_all[:, kv_w_all.shape[1] - min(WINDOW, past_len + t):]
    return out, kv_cs, new_win


def _gla_scan(q, k, v, log_f, s0):
    b, t = q.shape[:2]
    c = HG_CHUNK if t % HG_CHUNK == 0 else t
    nc = t // c
    causal = np.tril(np.ones((c, c), dtype=bool))[None, :, :, None, None]

    def to_chunks(a):
        return jnp.moveaxis(a.astype(jnp.float32).reshape(b, nc, c, *a.shape[2:]), 1, 0)

    def step(s, inp):
        qc, kc, vc, gc = inp
        cum = jnp.cumsum(gc, axis=1)
        decay = jnp.exp(jnp.where(causal, cum[:, :, None] - cum[:, None, :], -jnp.inf))
        attn = jnp.einsum('bthk,btshk,bshk->bhts', qc, decay, kc)
        o = (jnp.einsum('bhts,bshv->bthv', attn, vc)
             + jnp.einsum('bthk,bhkv->bthv', qc * jnp.exp(cum), s))
        last = cum[:, -1]
        s_new = (jnp.exp(last)[..., None] * s
                 + jnp.einsum('bshk,bshv->bhkv', kc * jnp.exp(last[:, None] - cum), vc))
        return s_new, o

    s_final, o = lax.scan(step, s0.astype(jnp.float32),
                          (to_chunks(q), to_chunks(k), to_chunks(v), to_chunks(log_f)))
    return jnp.moveaxis(o, 0, 1).reshape(b, t, HG_HEADS, HG_DV), s_final


def _hgrn2(h, s0, lb, w_in, g_norm, w_out):
    b, t, _ = h.shape
    proj = h @ w_in
    wk = HG_HEADS * HG_DK
    wv = HG_HEADS * HG_DV
    q = jax.nn.silu(proj[..., :wk]).reshape(b, t, HG_HEADS, HG_DK)
    f = lb + (1.0 - lb) * jax.nn.sigmoid(proj[..., wk:2 * wk].astype(jnp.float32))
    v = proj[..., 2 * wk:2 * wk + wv].reshape(b, t, HG_HEADS, HG_DV)
    g = proj[..., 2 * wk + wv:].reshape(b, t, HG_HEADS, HG_DV)
    k = (1.0 - f).reshape(b, t, HG_HEADS, HG_DK)
    log_f = jnp.log(f).reshape(b, t, HG_HEADS, HG_DK)
    o, s_new = _gla_scan(q, k, v, log_f, s0)
    o = _rmsnorm(o, g_norm) * jax.nn.silu(g.astype(jnp.float32))
    return o.reshape(b, t, wv).astype(h.dtype) @ w_out, s_new.astype(h.dtype)


def _causal_conv(u, prev, w, bias):
    t = u.shape[1]
    full = jnp.concatenate([prev.astype(u.dtype), u], axis=1)
    y = sum(full[:, i:i + t] * w[i] for i in range(SSD_CONV_W)) + bias
    return y, full[:, t:]


def _ssd_scan(x, dt, a, bm, cm, s0):
    b, t = x.shape[:2]
    c = SSD_CHUNK if t % SSD_CHUNK == 0 else t
    nc = t // c
    causal = np.tril(np.ones((c, c), dtype=bool))[None, :, :, None, None]

    def to_chunks(arr):
        return jnp.moveaxis(arr.reshape(b, nc, c, *arr.shape[2:]), 1, 0)

    def step(s, inp):
        xc, dtc, bc, cc = inp
        cum = jnp.cumsum(dtc * a, axis=1)
        decay = jnp.exp(jnp.where(causal, cum[:, :, None] - cum[:, None, :], -jnp.inf))
        cb = jnp.einsum('btgn,bsgn->btsg', cc, bc)
        w = cb[..., None] * decay * dtc[:, None]
        y = (jnp.einsum('btsgh,bsghp->btghp', w, xc)
             + jnp.einsum('btgn,bghpn->btghp', cc, s) * jnp.exp(cum)[..., None])
        last = cum[:, -1]
        s_new = (jnp.exp(last)[..., None, None] * s
                 + jnp.einsum('bsgn,bsgh,bsghp->bghpn', bc, dtc * jnp.exp(last[:, None] - cum), xc))
        return s_new, y

    s_final, y = lax.scan(step, s0, (to_chunks(x), to_chunks(dt), to_chunks(bm), to_chunks(cm)))
    return jnp.moveaxis(y, 0, 1).reshape(b, t, SSD_GROUPS, SSD_HPG, SSD_HEAD_DIM), s_final


def _ssd(h, ssm0, conv0, w_in, conv_w, conv_b, dt_bias, a_log, d_skip, norm_w, w_out):
    b, t, _ = h.shape
    di = SSD_D_INNER
    gn = SSD_GROUPS * SSD_STATE
    proj = h @ w_in
    z = proj[..., :di]
    xbc, conv_new = _causal_conv(proj[..., di:di + SSD_CONV_DIM], conv0, conv_w, conv_b)
    xbc = jax.nn.silu(xbc.astype(jnp.float32))
    x = xbc[..., :di].reshape(b, t, SSD_GROUPS, SSD_HPG, SSD_HEAD_DIM)
    bm = xbc[..., di:di + gn].reshape(b, t, SSD_GROUPS, SSD_STATE)
    cm = xbc[..., di + gn:].reshape(b, t, SSD_GROUPS, SSD_STATE)
    dt = jax.nn.softplus(proj[..., di + SSD_CONV_DIM:].astype(jnp.float32) + dt_bias)
    dt = dt.reshape(b, t, SSD_GROUPS, SSD_HPG)
    a = -jnp.exp(a_log.astype(jnp.float32)).reshape(SSD_GROUPS, SSD_HPG)
    s0 = ssm0.astype(jnp.float32).reshape(b, SSD_GROUPS, SSD_HPG, SSD_HEAD_DIM, SSD_STATE)
    y, s_new = _ssd_scan(x, dt, a, bm, cm, s0)
    y = y + d_skip.astype(jnp.float32).reshape(SSD_GROUPS, SSD_HPG)[..., None] * x
    y = y.reshape(b, t, di) * jax.nn.silu(z.astype(jnp.float32))
    y = _rmsnorm(y.reshape(b, t, SSD_GROUPS, di // SSD_GROUPS), norm_w.reshape(SSD_GROUPS, di // SSD_GROUPS))
    out = y.reshape(b, t, di).astype(h.dtype) @ w_out
    return out, s_new.reshape(b, SSD_HEADS, SSD_HEAD_DIM, SSD_STATE).astype(h.dtype), conv_new


def setup_inputs(seed: int = 0) -> dict:
    key = jax.random.key(seed)
    keys = iter(jax.random.split(key, 48))
    f32 = jnp.float32

    def nrm(shape, scale):
        return jax.random.normal(next(keys), shape, f32) * scale

    n_pages = PAST_LEN // PAGE_SIZE
    n_phys = (DEC_BATCH * n_pages * 5) // 4
    n_buf = min(WINDOW, PAST_LEN)
    page_table = jax.random.permutation(next(keys), n_phys)[: DEC_BATCH * n_pages]
    page_table = page_table.reshape(DEC_BATCH, n_pages).astype(jnp.int32)
    dt0 = jnp.exp(jax.random.uniform(next(keys), (N_LAYERS_SSD, SSD_HEADS), f32,
                                     minval=math.log(1e-3), maxval=math.log(1e-1)))
    a0 = jax.random.uniform(next(keys), (N_LAYERS_SSD, SSD_HEADS), f32, minval=1.0, maxval=16.0)
    return {
        'x_prompt': nrm((BATCH, SEQ, D_MODEL), 1.0),
        'x_sample': nrm((DEC_BATCH, DEC_SEQ, D_MODEL), 1.0),
        'cache_nsa_kv': nrm((N_LAYERS_NSA, n_phys, PAGE_SIZE, 4, NSA_KV_HEADS, NSA_HEAD_DIM), 1.0),
        'cache_nsa_win': nrm((N_LAYERS_NSA, DEC_BATCH, n_buf, 2, NSA_KV_HEADS, NSA_HEAD_DIM), 1.0),
        'state_hgrn': nrm((N_LAYERS_HGRN, DEC_BATCH, HG_HEADS, HG_DK, HG_DV), 0.5),
        'state_ssd': nrm((N_LAYERS_SSD, DEC_BATCH, SSD_HEADS, SSD_HEAD_DIM, SSD_STATE), 0.1),
        'state_ssd_conv': nrm((N_LAYERS_SSD, DEC_BATCH, SSD_CONV_W - 1, SSD_CONV_DIM), 1.0),
        'page_table': page_table,
        'c_prompt': nrm((BATCH, D_MODEL), 1.0),
        'c_sample': nrm((DEC_BATCH, D_MODEL), 1.0),
        'rel_bias': nrm((REL_BUCKETS, NSA_HEADS), 0.5),
        'hgrn_lower_bounds': nrm((DEPTH, HG_HEADS * HG_DK), 0.5),
        'w_ada': nrm((DEPTH, D_MODEL, ADA_CHUNKS * D_MODEL), 0.5 * D_MODEL ** -0.5),
        'b_ada': nrm((DEPTH, ADA_CHUNKS * D_MODEL), 0.02),
        'norm_gains': 1.0 + nrm((DEPTH, 4, D_MODEL), 0.05),
        'w_mlp_in': nrm((DEPTH, D_MODEL, D_FF), D_MODEL ** -0.5),
        'w_mlp_out': nrm((DEPTH, D_FF, D_MODEL), D_FF ** -0.5),
        'nsa_w_in': nrm((N_LAYERS_NSA, D_MODEL, NSA_IN_W), D_MODEL ** -0.5),
        'nsa_cmp_pos': nrm((N_LAYERS_NSA, 2, CMP_LEN, NSA_HEAD_DIM), 0.1),
        'nsa_cmp_w1': nrm((N_LAYERS_NSA, 2, CMP_LEN * NSA_HEAD_DIM, CMP_HIDDEN), (CMP_LEN * NSA_HEAD_DIM) ** -0.5),
        'nsa_cmp_w2': nrm((N_LAYERS_NSA, 2, CMP_HIDDEN, NSA_HEAD_DIM), CMP_HIDDEN ** -0.5),
        'nsa_w_out': nrm((N_LAYERS_NSA, NSA_Q_W, D_MODEL), NSA_Q_W ** -0.5),
        'hg_w_in': nrm((N_LAYERS_HGRN, D_MODEL, HG_IN_W), D_MODEL ** -0.5),
        'hg_norm': 1.0 + nrm((N_LAYERS_HGRN, HG_DV), 0.05),
        'hg_w_out': nrm((N_LAYERS_HGRN, HG_HEADS * HG_DV, D_MODEL), (HG_HEADS * HG_DV) ** -0.5),
        'ssd_w_in': nrm((N_LAYERS_SSD, D_MODEL, SSD_IN_W), D_MODEL ** -0.5),
        'ssd_conv_w': nrm((N_LAYERS_SSD, SSD_CONV_W, SSD_CONV_DIM), SSD_CONV_W ** -0.5),
        'ssd_conv_b': nrm((N_LAYERS_SSD, SSD_CONV_DIM), 0.02),
        'ssd_dt_bias': dt0 + jnp.log(-jnp.expm1(-dt0)),
        'ssd_a_log': jnp.log(a0),
        'ssd_d': 1.0 + nrm((N_LAYERS_SSD, SSD_HEADS), 0.1),
        'ssd_norm': 1.0 + nrm((N_LAYERS_SSD, SSD_D_INNER), 0.05),
        'ssd_w_out': nrm((N_LAYERS_SSD, SSD_D_INNER, D_MODEL), SSD_D_INNER ** -0.5),
    }


def reference(x_prompt, x_sample, cache_nsa_kv, cache_nsa_win, state_hgrn, state_ssd, state_ssd_conv,
              page_table, c_prompt, c_sample, rel_bias, hgrn_lower_bounds, w_ada, b_ada, norm_gains,
              w_mlp_in, w_mlp_out, nsa_w_in, nsa_cmp_pos, nsa_cmp_w1, nsa_cmp_w2, nsa_w_out,
              hg_w_in, hg_norm, hg_w_out, ssd_w_in, ssd_conv_w, ssd_conv_b, ssd_dt_bias, ssd_a_log,
              ssd_d, ssd_norm, ssd_w_out):
    lb_p = jax.nn.softmax(hgrn_lower_bounds.astype(jnp.float32), axis=0)
    lower_bounds = jnp.cumsum(lb_p, axis=0) - lb_p[0]
    xp, xs = x_prompt, x_sample
    bp = xp.shape[0]
    kv_p, kv_s, win_p, win_s = [], [], [], []
    hg_p, hg_s, ssd_p, ssd_s, conv_p, conv_s = [], [], [], [], [], []
    for i in range(DEPTH):
        j = i // N_MIXERS
        kind = i % N_MIXERS
        shp_m, scp_m, gtp_m, shp_f, scp_f, gtp_f = _ada(c_prompt, w_ada[i], b_ada[i])
        shs_m, scs_m, gts_m, shs_f, scs_f, gts_f = _ada(c_sample, w_ada[i], b_ada[i])
        hp = _rmsnorm(xp, norm_gains[i, 0]) * (1.0 + scp_m) + shp_m
        hs = _rmsnorm(xs, norm_gains[i, 0]) * (1.0 + scs_m) + shs_m
        if kind == 0:
            mp, new_kv_p, new_win_p = _nsa_prompt(hp, nsa_w_in[j], nsa_cmp_pos[j], nsa_cmp_w1[j],
                                                  nsa_cmp_w2[j], nsa_w_out[j], rel_bias)
            ms, new_kv_s, new_win_s = _nsa_sample(hs, cache_nsa_kv[j], cache_nsa_win[j], page_table,
                                                  nsa_w_in[j], nsa_cmp_pos[j], nsa_cmp_w1[j],
                                                  nsa_cmp_w2[j], nsa_w_out[j], rel_bias)
            kv_p.append(new_kv_p)
            kv_s.append(new_kv_s)
            win_p.append(new_win_p)
            win_s.append(new_win_s)
        elif kind == 1:
            s0 = jnp.zeros((bp, HG_HEADS, HG_DK, HG_DV), jnp.float32)
            mp, new_hp = _hgrn2(hp, s0, lower_bounds[i], hg_w_in[j], hg_norm[j], hg_w_out[j])
            ms, new_hs = _hgrn2(hs, state_hgrn[j], lower_bounds[i], hg_w_in[j], hg_norm[j], hg_w_out[j])
            hg_p.append(new_hp)
            hg_s.append(new_hs)
        else:
            ssm0 = jnp.zeros((bp, SSD_HEADS, SSD_HEAD_DIM, SSD_STATE), jnp.float32)
            conv0 = jnp.zeros((bp, SSD_CONV_W - 1, SSD_CONV_DIM), xp.dtype)
            mp, new_sp, new_cp = _ssd(hp, ssm0, conv0, ssd_w_in[j], ssd_conv_w[j], ssd_conv_b[j],
                                      ssd_dt_bias[j], ssd_a_log[j], ssd_d[j], ssd_norm[j], ssd_w_out[j])
            ms, new_ss, new_cs = _ssd(hs, state_ssd[j], state_ssd_conv[j], ssd_w_in[j], ssd_conv_w[j],
                                      ssd_conv_b[j], ssd_dt_bias[j], ssd_a_log[j], ssd_d[j], ssd_norm[j],
                                      ssd_w_out[j])
            ssd_p.append(new_sp)
            ssd_s.append(new_ss)
            conv_p.append(new_cp)
            conv_s.append(new_cs)
        xp = xp + gtp_m * _rmsnorm(mp, norm_gains[i, 1])
        xs = xs + gts_m * _rmsnorm(ms, norm_gains[i, 1])
        hp = _rmsnorm(xp, norm_gains[i, 2]) * (1.0 + scp_f) + shp_f
        hs = _rmsnorm(xs, norm_gains[i, 2]) * (1.0 + scs_f) + shs_f
        xp = xp + gtp_f * _rmsnorm(_sq_relu_mlp(hp, w_mlp_in[i], w_mlp_out[i]), norm_gains[i, 3])
        xs = xs + gts_f * _rmsnorm(_sq_relu_mlp(hs, w_mlp_in[i], w_mlp_out[i]), norm_gains[i, 3])
    return (xp, xs, jnp.stack(kv_p), jnp.stack(kv_s), jnp.stack(win_p), jnp.stack(win_s),
            jnp.stack(hg_p), jnp.stack(hg_s), jnp.stack(ssd_p), jnp.stack(ssd_s),
            jnp.stack(conv_p), jnp.stack(conv_s))
```

```python
import functools
import math

import jax
import jax.numpy as jnp
import numpy as np
from jax import lax
from jax.experimental import pallas as pl
from jax.experimental.pallas import tpu as pltpu

D_MODEL = 1024
DEPTH = 4
PAGE_SIZE = 128
N_MIXERS = 3
ADA_CHUNKS = 6
NORM_EPS = 1e-6
D_FF = 4 * D_MODEL

NSA_HEADS = 16
NSA_HEAD_DIM = D_MODEL // NSA_HEADS
NSA_KV_HEADS = 4
NSA_GROUP = NSA_HEADS // NSA_KV_HEADS
CMP_STRIDE = 16
CMP_LEN = 2 * CMP_STRIDE
CMP_HIDDEN = 2 * NSA_HEAD_DIM
SEL_BLOCK = 64
SEL_TOP_N = 16
WINDOW = 512
WIN_Q_BLOCK = 128
SEL_Q_BLOCK = 16
NSA_Q_W = NSA_HEADS * NSA_HEAD_DIM
NSA_KV_W = NSA_KV_HEADS * NSA_HEAD_DIM
NSA_IN_W = NSA_Q_W + 6 * NSA_KV_W + 3 * NSA_HEADS

REL_BUCKETS = 32
REL_MAX_DIST = 128

HG_EXPAND = 128
HG_HEADS = D_MODEL // HG_EXPAND
HG_DK = HG_EXPAND
HG_DV = D_MODEL // HG_HEADS
HG_CHUNK = 64

SSD_D_INNER = 2 * D_MODEL
SSD_HEAD_DIM = 64
SSD_HEADS = SSD_D_INNER // SSD_HEAD_DIM
SSD_GROUPS = 8
SSD_HPG = SSD_HEADS // SSD_GROUPS
SSD_STATE = 128
SSD_CONV_W = 4
SSD_CONV_DIM = SSD_D_INNER + 2 * SSD_GROUPS * SSD_STATE
SSD_IN_W = SSD_D_INNER + SSD_CONV_DIM + SSD_HEADS
SSD_CHUNK = 128

NEG_INF = -1e30
FORCE_SCORE = 1e4

LANES = 128
VMEM_LIMIT_BYTES = 48 * 1024 * 1024
PROMPT_ROW_TILE = 512


def _round_up(n, m):
    return -(-n // m) * m


def _col_tile(n, cap=1024):
    best = LANES
    for t in range(LANES, cap + 1, LANES):
        if n % t == 0:
            best = t
    return best


def _rms(x, g):
    return x * lax.rsqrt(jnp.mean(x * x, axis=-1, keepdims=True) + NORM_EPS) * g


def _mod_spec(mod, rows_per_mod, tm, ngrid):
    r = mod.shape[1]
    if r == 1:
        per = rows_per_mod // tm
        if ngrid == 1:
            return pl.BlockSpec((None, 1, mod.shape[2]), lambda i: (i // per, 0, 0))
        return pl.BlockSpec((None, 1, mod.shape[2]), lambda i, j: (i // per, 0, 0))
    if ngrid == 1:
        return pl.BlockSpec((None, r, mod.shape[2]), lambda i: (0, 0, 0))
    return pl.BlockSpec((None, r, mod.shape[2]), lambda i, j: (0, 0, 0))


def _ada_kernel(c_ref, w_ref, b_ref, o_ref):
    c = c_ref[...]
    s = (c * jax.nn.sigmoid(c)).astype(jnp.bfloat16)
    o_ref[...] = jnp.dot(s, w_ref[...].astype(jnp.bfloat16),
                         preferred_element_type=jnp.float32) + b_ref[...]


def _ada_all(c_all, w_ada, b_ada):
    rows = c_all.shape[0]
    n = ADA_CHUNKS * D_MODEL
    tn = 1024
    return pl.pallas_call(
        _ada_kernel,
        grid=(DEPTH, n // tn),
        in_specs=[pl.BlockSpec((rows, D_MODEL), lambda l, j: (0, 0)),
                  pl.BlockSpec((None, D_MODEL, tn), lambda l, j: (l, 0, j)),
                  pl.BlockSpec((None, 1, tn), lambda l, j: (l, 0, j))],
        out_specs=pl.BlockSpec((None, rows, tn), lambda l, j: (l, 0, j)),
        out_shape=jax.ShapeDtypeStruct((DEPTH, rows, n), jnp.float32),
        compiler_params=pltpu.CompilerParams(
            dimension_semantics=("parallel", "parallel"), vmem_limit_bytes=VMEM_LIMIT_BYTES),
        name="ada",
    )(c_all, w_ada, b_ada.reshape(DEPTH, 1, n))


def _norm_mod_matmul_kernel(x_ref, g_ref, sc_ref, sh_ref, w_ref, o_ref, h_ref):
    @pl.when(pl.program_id(1) == 0)
    def _():
        h = _rms(x_ref[...], g_ref[...]) * (1.0 + sc_ref[...]) + sh_ref[...]
        h_ref[...] = h.astype(jnp.bfloat16)

    o_ref[...] = jnp.dot(h_ref[...], w_ref[...], preferred_element_type=jnp.float32)


def _norm_mod_matmul(x, g, sc, sh, w, rows_per_mod, tm):
    m, d = x.shape
    n = w.shape[1]
    tn = _col_tile(n)
    return pl.pallas_call(
        _norm_mod_matmul_kernel,
        grid=(m // tm, n // tn),
        in_specs=[pl.BlockSpec((tm, d), lambda i, j: (i, 0)),
                  pl.BlockSpec((1, d), lambda i, j: (0, 0)),
                  _mod_spec(sc, rows_per_mod, tm, 2),
                  _mod_spec(sh, rows_per_mod, tm, 2),
                  pl.BlockSpec((d, tn), lambda i, j: (0, j))],
        out_specs=pl.BlockSpec((tm, tn), lambda i, j: (i, j)),
        out_shape=jax.ShapeDtypeStruct((m, n), jnp.float32),
        scratch_shapes=[pltpu.VMEM((tm, d), jnp.bfloat16)],
        compiler_params=pltpu.CompilerParams(
            dimension_semantics=("parallel", "arbitrary"), vmem_limit_bytes=VMEM_LIMIT_BYTES),
        name="norm_mod_matmul",
    )(x, g.reshape(1, d), sc, sh, w)


def _matmul_norm_res_kernel(a_ref, w_ref, x_ref, g_ref, gt_ref, o_ref):
    y = jnp.dot(a_ref[...].astype(jnp.bfloat16), w_ref[...], preferred_element_type=jnp.float32)
    o_ref[...] = x_ref[...] + gt_ref[...] * _rms(y, g_ref[...])


def _matmul_norm_res(a, w, x, g, gate, rows_per_mod, tm):
    m, k = a.shape
    d = w.shape[1]
    return pl.pallas_call(
        _matmul_norm_res_kernel,
        grid=(m // tm,),
        in_specs=[pl.BlockSpec((tm, k), lambda i: (i, 0)),
                  pl.BlockSpec((k, d), lambda i: (0, 0)),
                  pl.BlockSpec((tm, d), lambda i: (i, 0)),
                  pl.BlockSpec((1, d), lambda i: (0, 0)),
                  _mod_spec(gate, rows_per_mod, tm, 1)],
        out_specs=pl.BlockSpec((tm, d), lambda i: (i, 0)),
        out_shape=jax.ShapeDtypeStruct((m, d), jnp.float32),
        compiler_params=pltpu.CompilerParams(
            dimension_semantics=("parallel",), vmem_limit_bytes=VMEM_LIMIT_BYTES),
        name="matmul_norm_res",
    )(a, w, x, g.reshape(1, d), gate)


def _mlp_kernel(x_ref, g2_ref, sc_ref, sh_ref, w1_ref, w2_ref, g3_ref, gt_ref, o_ref, h_ref, acc_ref):
    j = pl.program_id(1)

    @pl.when(j == 0)
    def _():
        h = _rms(x_ref[...], g2_ref[...]) * (1.0 + sc_ref[...]) + sh_ref[...]
        h_ref[...] = h.astype(jnp.bfloat16)

    u = jnp.dot(h_ref[...], w1_ref[...], preferred_element_type=jnp.float32)
    u = jnp.square(jnp.maximum(u, 0.0)).astype(jnp.bfloat16)
    part = jnp.dot(u, w2_ref[...], preferred_element_type=jnp.float32)

    @pl.when(j == 0)
    def _():
        acc_ref[...] = part

    @pl.when(j > 0)
    def _():
        acc_ref[...] += part

    @pl.when(j == pl.num_programs(1) - 1)
    def _():
        o_ref[...] = x_ref[...] + gt_ref[...] * _rms(acc_ref[...], g3_ref[...])


def _mlp(x, g2, sc, sh, w1, w2, g3, gate, rows_per_mod, tm):
    m, d = x.shape
    f = w1.shape[1]
    tf = 1024
    return pl.pallas_call(
        _mlp_kernel,
        grid=(m // tm, f // tf),
        in_specs=[pl.BlockSpec((tm, d), lambda i, j: (i, 0)),
                  pl.BlockSpec((1, d), lambda i, j: (0, 0)),
                  _mod_spec(sc, rows_per_mod, tm, 2),
                  _mod_spec(sh, rows_per_mod, tm, 2),
                  pl.BlockSpec((d, tf), lambda i, j: (0, j)),
                  pl.BlockSpec((tf, d), lambda i, j: (j, 0)),
                  pl.BlockSpec((1, d), lambda i, j: (0, 0)),
                  _mod_spec(gate, rows_per_mod, tm, 2)],
        out_specs=pl.BlockSpec((tm, d), lambda i, j: (i, 0)),
        out_shape=jax.ShapeDtypeStruct((m, d), jnp.float32),
        scratch_shapes=[pltpu.VMEM((tm, d), jnp.bfloat16), pltpu.VMEM((tm, d), jnp.float32)],
        compiler_params=pltpu.CompilerParams(
            dimension_semantics=("parallel", "arbitrary"), vmem_limit_bytes=VMEM_LIMIT_BYTES),
        name="mlp",
    )(x, g2.reshape(1, d), sc, sh, w1, w2, g3.reshape(1, d), gate)


def _masked_softmax(logits, mask):
    l = jnp.where(mask, logits.astype(jnp.float32), NEG_INF)
    m = jnp.max(l, axis=-1, keepdims=True)
    e = jnp.where(mask, jnp.exp(l - m), 0.0)
    return e / jnp.maximum(jnp.sum(e, axis=-1, keepdims=True), 1e-30)


def _rel_bucket(dist):
    n = jnp.maximum(dist, 0)
    n_exact = REL_BUCKETS // 2
    nf = jnp.maximum(n, 1).astype(jnp.float32)
    large = n_exact + (jnp.log(nf / n_exact) / math.log(REL_MAX_DIST / n_exact)
                       * (REL_BUCKETS - n_exact)).astype(jnp.int32)
    return jnp.where(n < n_exact, n, jnp.minimum(large, REL_BUCKETS - 1))


def _head_bias(rel_bias, dist):
    q, t = dist.shape
    b = rel_bias[_rel_bucket(dist)].astype(jnp.float32)
    return b.reshape(q, t, NSA_KV_HEADS, NSA_GROUP).transpose(0, 2, 3, 1)


def _nsa_split(proj):
    b, t, _ = proj.shape
    o1 = NSA_Q_W
    o2 = o1 + 4 * NSA_KV_W
    o3 = o2 + 2 * NSA_KV_W
    qg = proj[..., :o1].reshape(b, t, NSA_KV_HEADS, NSA_GROUP, NSA_HEAD_DIM)
    kv_cs = proj[..., o1:o2].reshape(b, t, 4, NSA_KV_HEADS, NSA_HEAD_DIM)
    kv_win = proj[..., o2:o3].reshape(b, t, 2, NSA_KV_HEADS, NSA_HEAD_DIM)
    gates = jax.nn.sigmoid(proj[..., o3:NSA_IN_W]).reshape(b, t, 3, NSA_KV_HEADS, NSA_GROUP)
    return qg, kv_cs, kv_win, gates


def _nsa_compress(kv_pad, cmp_pos, w1, w2):
    b, tpad = kv_pad.shape[:2]
    n_chunks = tpad // CMP_STRIDE
    chunks = kv_pad[:, :, :2].reshape(b, n_chunks, CMP_STRIDE, 2, NSA_KV_HEADS, NSA_HEAD_DIM)
    w1 = w1.reshape(2, CMP_LEN, NSA_HEAD_DIM, CMP_HIDDEN)
    pos_bias = jnp.einsum('rld,rlde->re', cmp_pos, w1)
    pa = jnp.einsum('bnlrkd,rlde->bnrke', chunks, w1[:, :CMP_STRIDE])
    pb = jnp.einsum('bnlrkd,rlde->bnrke', chunks, w1[:, CMP_STRIDE:])
    hidden = jax.nn.gelu(pa[:, :-1] + pb[:, 1:] + pos_bias[:, None, :])
    out = jnp.einsum('bnrke,ref->bnrkf', hidden, w2)
    return out[:, :, 0], out[:, :, 1]


def _nsa_sparse_block(qg, q_pos, k_cmp, v_cmp, cmp_end, sel_kv, rel_bias):
    b, qc = qg.shape[:2]
    n_sel = sel_kv.shape[1]
    scale = NSA_HEAD_DIM ** -0.5
    dist_c = q_pos[:, None] - cmp_end[None, :]
    logits_c = jnp.einsum('bqkgd,bnkd->bqkgn', qg, k_cmp) * scale + _head_bias(rel_bias, dist_c)
    p_c = _masked_softmax(logits_c, (dist_c >= 0)[:, None, None, :])
    o_cmp = jnp.einsum('bqkgn,bnkd->bqkgd', p_c, v_cmp.astype(jnp.float32))
    imp = jnp.pad(p_c.sum(axis=3), ((0, 0), (0, 0), (0, 0), (1, 1)))
    imp = 0.5 * (imp[..., 1:] + imp[..., :-1])
    imp = imp.reshape(b, qc, NSA_KV_HEADS, n_sel, SEL_BLOCK // CMP_STRIDE).sum(-1)
    blk = jnp.arange(n_sel)[None, :]
    cur = (q_pos // SEL_BLOCK)[:, None]
    forced = (blk == 0) | (blk == cur) | (blk == cur - 1)
    valid = blk * SEL_BLOCK <= q_pos[:, None]
    score = jnp.where(forced[None, :, None, :], FORCE_SCORE, imp)
    score = jnp.where(valid[None, :, None, :], score, NEG_INF)
    n_cand = max(n_sel, SEL_TOP_N)
    score = jnp.pad(score, ((0, 0), (0, 0), (0, 0), (0, n_cand - n_sel)), constant_values=NEG_INF)
    top_val, top_idx = lax.top_k(score, SEL_TOP_N)
    sel_ok = top_val > 0.5 * NEG_INF
    top_idx = jnp.minimum(top_idx, n_sel - 1)
    bi = jnp.arange(b)[:, None, None, None]
    ki = jnp.arange(NSA_KV_HEADS)[None, None, :, None]
    kv_sel = sel_kv[bi, top_idx, :, :, ki]
    kpos = top_idx[..., None] * SEL_BLOCK + jnp.arange(SEL_BLOCK)
    dist_s = q_pos[None, :, None, None, None] - kpos
    mask_s = (sel_ok[..., None] & (dist_s >= 0)).reshape(b, qc, NSA_KV_HEADS, 1, SEL_TOP_N * SEL_BLOCK)
    table = rel_bias.reshape(REL_BUCKETS, NSA_KV_HEADS, NSA_GROUP).transpose(1, 0, 2)
    bias_s = table[jnp.arange(NSA_KV_HEADS)[None, None, :, None, None], _rel_bucket(dist_s)]
    bias_s = jnp.moveaxis(bias_s, -1, 3).reshape(b, qc, NSA_KV_HEADS, NSA_GROUP, SEL_TOP_N * SEL_BLOCK)
    logits_s = jnp.einsum('bqkgd,bqknsd->bqkgns', qg, kv_sel[..., 0, :])
    logits_s = logits_s.reshape(b, qc, NSA_KV_HEADS, NSA_GROUP, SEL_TOP_N * SEL_BLOCK) * scale + bias_s.astype(jnp.float32)
    p_s = _masked_softmax(logits_s, mask_s)
    v_sel = kv_sel[..., 1, :].reshape(b, qc, NSA_KV_HEADS, SEL_TOP_N * SEL_BLOCK, NSA_HEAD_DIM)
    o_sel = jnp.einsum('bqkgm,bqkmd->bqkgd', p_s, v_sel.astype(jnp.float32))
    return o_cmp, o_sel


def _nsa_sparse(qg, q_pos, kv_all, cmp_pos, w1, w2, rel_bias):
    b, t_kv = kv_all.shape[:2]
    t_pad = -(-t_kv // SEL_BLOCK) * SEL_BLOCK
    kv_pad = jnp.pad(kv_all, ((0, 0), (0, t_pad - t_kv), (0, 0), (0, 0), (0, 0)))
    k_cmp, v_cmp = _nsa_compress(kv_pad, cmp_pos, w1, w2)
    cmp_end = jnp.arange(k_cmp.shape[1]) * CMP_STRIDE + CMP_LEN - 1
    sel_kv = kv_pad[:, :, 2:].reshape(b, t_pad // SEL_BLOCK, SEL_BLOCK, 2, NSA_KV_HEADS, NSA_HEAD_DIM)
    tq = qg.shape[1]
    qb = SEL_Q_BLOCK if tq % SEL_Q_BLOCK == 0 else tq
    nq = tq // qb
    q_blocks = jnp.moveaxis(qg.reshape(b, nq, qb, NSA_KV_HEADS, NSA_GROUP, NSA_HEAD_DIM), 1, 0)
    pos_blocks = q_pos.reshape(nq, qb)
    o_cmp, o_sel = lax.map(
        lambda a: _nsa_sparse_block(a[0], a[1], k_cmp, v_cmp, cmp_end, sel_kv, rel_bias),
        (q_blocks, pos_blocks))
    shape = (b, tq, NSA_KV_HEADS, NSA_GROUP, NSA_HEAD_DIM)
    return jnp.moveaxis(o_cmp, 0, 1).reshape(shape), jnp.moveaxis(o_sel, 0, 1).reshape(shape)


def _window_attend(qg, q_pos, kv, k_pos, rel_bias):
    dist = q_pos[:, None] - k_pos[None, :]
    mask = (dist >= 0) & (dist <= WINDOW) & (k_pos >= 0)[None, :]
    logits = jnp.einsum('bqkgd,btkd->bqkgt', qg, kv[:, :, 0]) * NSA_HEAD_DIM ** -0.5 + _head_bias(rel_bias, dist)
    p = _masked_softmax(logits, mask[:, None, None, :])
    return jnp.einsum('bqkgt,btkd->bqkgd', p, kv[:, :, 1].astype(jnp.float32))


def _nsa_window_prompt(qg, kv_win, rel_bias):
    b, t = qg.shape[:2]
    nb = t // WIN_Q_BLOCK
    kv_pad = jnp.pad(kv_win, ((0, 0), (WINDOW, 0), (0, 0), (0, 0), (0, 0)))
    q_blocks = jnp.moveaxis(qg.reshape(b, nb, WIN_Q_BLOCK, NSA_KV_HEADS, NSA_GROUP, NSA_HEAD_DIM), 1, 0)

    def block(args):
        qblk, i = args
        start = i * WIN_Q_BLOCK
        kv = lax.dynamic_slice_in_dim(kv_pad, start, WINDOW + WIN_Q_BLOCK, axis=1)
        k_pos = start - WINDOW + jnp.arange(WINDOW + WIN_Q_BLOCK)
        q_pos = start + jnp.arange(WIN_Q_BLOCK)
        return _window_attend(qblk, q_pos, kv, k_pos, rel_bias)

    o = lax.map(block, (q_blocks, jnp.arange(nb)))
    return jnp.moveaxis(o, 0, 1).reshape(b, t, NSA_KV_HEADS, NSA_GROUP, NSA_HEAD_DIM)


def _nsa_merge(o_cmp, o_sel, o_win, gates):
    b, t = gates.shape[:2]
    o = (gates[:, :, 0, ..., None] * o_cmp + gates[:, :, 1, ..., None] * o_sel
         + gates[:, :, 2, ..., None] * o_win)
    return o.reshape(b * t, NSA_Q_W)


def _nsa_prompt_core(proj, cmp_pos, cmp_w1, cmp_w2, rel_bias):
    t = proj.shape[1]
    qg, kv_cs, kv_win, gates = _nsa_split(proj)
    o_cmp, o_sel = _nsa_sparse(qg, jnp.arange(t), kv_cs, cmp_pos, cmp_w1, cmp_w2, rel_bias)
    o_win = _nsa_window_prompt(qg, kv_win, rel_bias)
    return _nsa_merge(o_cmp, o_sel, o_win, gates), kv_cs, kv_win[:, t - min(WINDOW, t):]


def _page_copy_kernel(pt_ref, src_ref, dst_ref):
    del pt_ref
    dst_ref[...] = src_ref[...]


def _gather_pages(kv_pages, page_table):
    n_phys = kv_pages.shape[0]
    b, n_pages = page_table.shape
    width = math.prod(kv_pages.shape[2:])
    pages = kv_pages.reshape(n_phys, PAGE_SIZE, width)
    return pl.pallas_call(
        _page_copy_kernel,
        grid_spec=pltpu.PrefetchScalarGridSpec(
            num_scalar_prefetch=1,
            grid=(b, n_pages),
            in_specs=[pl.BlockSpec((None, PAGE_SIZE, width), lambda i, p, pt: (pt[i, p], 0, 0))],
            out_specs=pl.BlockSpec((None, None, PAGE_SIZE, width), lambda i, p, pt: (i, p, 0, 0))),
        out_shape=jax.ShapeDtypeStruct((b, n_pages, PAGE_SIZE, width), kv_pages.dtype),
        compiler_params=pltpu.CompilerParams(
            dimension_semantics=("parallel", "parallel"), vmem_limit_bytes=VMEM_LIMIT_BYTES),
        name="gather_pages",
    )(page_table, pages)


def _nsa_sample_core(proj, kv_pages, win_buf, page_table, cmp_pos, cmp_w1, cmp_w2, rel_bias):
    b, t, _ = proj.shape
    past_len = page_table.shape[1] * PAGE_SIZE
    qg, kv_cs, kv_win, gates = _nsa_split(proj)
    past = _gather_pages(kv_pages, page_table).reshape(b, past_len, 4, NSA_KV_HEADS, NSA_HEAD_DIM)
    kv_all = jnp.concatenate([past, kv_cs.astype(past.dtype)], axis=1)
    q_pos = past_len + jnp.arange(t)
    o_cmp, o_sel = _nsa_sparse(qg, q_pos, kv_all, cmp_pos, cmp_w1, cmp_w2, rel_bias)
    n_buf = win_buf.shape[1]
    kv_w_all = jnp.concatenate([win_buf, kv_win.astype(win_buf.dtype)], axis=1)
    k_pos = past_len - n_buf + jnp.arange(n_buf + t)
    o_win = _window_attend(qg, q_pos, kv_w_all, k_pos, rel_bias)
    new_win = kv_w_all[:, kv_w_all.shape[1] - min(WINDOW, past_len + t):]
    return _nsa_merge(o_cmp, o_sel, o_win, gates), kv_cs, new_win


def _gla_scan(q, k, v, log_f, s0):
    b, t = q.shape[:2]
    c = HG_CHUNK if t % HG_CHUNK == 0 else t
    nc = t // c
    causal = np.tril(np.ones((c, c), dtype=bool))[None, :, :, None, None]

    def to_chunks(a):
        return jnp.moveaxis(a.astype(jnp.float32).reshape(b, nc, c, *a.shape[2:]), 1, 0)

    def step(s, inp):
        qc, kc, vc, gc = inp
        cum = jnp.cumsum(gc, axis=1)
        decay = jnp.exp(jnp.where(causal, cum[:, :, None] - cum[:, None, :], -jnp.inf))
        attn = jnp.einsum('bthk,btshk,bshk->bhts', qc, decay, kc)
        o = (jnp.einsum('bhts,bshv->bthv', attn, vc)
             + jnp.einsum('bthk,bhkv->bthv', qc * jnp.exp(cum), s))
        last = cum[:, -1]
        s_new = (jnp.exp(last)[..., None] * s
                 + jnp.einsum('bshk,bshv->bhkv', kc * jnp.exp(last[:, None] - cum), vc))
        return s_new, o

    s_final, o = lax.scan(step, s0.astype(jnp.float32),
                          (to_chunks(q), to_chunks(k), to_chunks(v), to_chunks(log_f)))
    return jnp.moveaxis(o, 0, 1).reshape(b, t, HG_HEADS, HG_DV), s_final


def _hgrn2_core(proj, s0, lb, g_norm):
    b, t, _ = proj.shape
    wk = HG_HEADS * HG_DK
    wv = HG_HEADS * HG_DV
    q = jax.nn.silu(proj[..., :wk]).reshape(b, t, HG_HEADS, HG_DK)
    f = lb + (1.0 - lb) * jax.nn.sigmoid(proj[..., wk:2 * wk])
    v = proj[..., 2 * wk:2 * wk + wv].reshape(b, t, HG_HEADS, HG_DV)
    g = proj[..., 2 * wk + wv:].reshape(b, t, HG_HEADS, HG_DV)
    k = (1.0 - f).reshape(b, t, HG_HEADS, HG_DK)
    log_f = jnp.log(f).reshape(b, t, HG_HEADS, HG_DK)
    o, s_new = _gla_scan(q, k, v, log_f, s0)
    o = _rms(o, g_norm) * jax.nn.silu(g)
    return o.reshape(b * t, wv), s_new


def _causal_conv(u, prev, w, bias):
    t = u.shape[1]
    full = jnp.concatenate([prev.astype(u.dtype), u], axis=1)
    y = sum(full[:, i:i + t] * w[i] for i in range(SSD_CONV_W)) + bias
    return y, full[:, t:]


def _ssd_scan(x, dt, a, bm, cm, s0):
    b, t = x.shape[:2]
    c = SSD_CHUNK if t % SSD_CHUNK == 0 else t
    nc = t // c
    causal = np.tril(np.ones((c, c), dtype=bool))[None, :, :, None, None]

    def to_chunks(arr):
        return jnp.moveaxis(arr.reshape(b, nc, c, *arr.shape[2:]), 1, 0)

    def step(s, inp):
        xc, dtc, bc, cc = inp
        cum = jnp.cumsum(dtc * a, axis=1)
        decay = jnp.exp(jnp.where(causal, cum[:, :, None] - cum[:, None, :], -jnp.inf))
        cb = jnp.einsum('btgn,bsgn->btsg', cc, bc)
        w = cb[..., None] * decay * dtc[:, None]
        y = (jnp.einsum('btsgh,bsghp->btghp', w, xc)
             + jnp.einsum('btgn,bghpn->btghp', cc, s) * jnp.exp(cum)[..., None])
        last = cum[:, -1]
        s_new = (jnp.exp(last)[..., None, None] * s
                 + jnp.einsum('bsgn,bsgh,bsghp->bghpn', bc, dtc * jnp.exp(last[:, None] - cum), xc))
        return s_new, y

    s_final, y = lax.scan(step, s0, (to_chunks(x), to_chunks(dt), to_chunks(bm), to_chunks(cm)))
    return jnp.moveaxis(y, 0, 1).reshape(b, t, SSD_GROUPS, SSD_HPG, SSD_HEAD_DIM), s_final


def _ssd_core(proj, ssm0, conv0, conv_w, conv_b, dt_bias, a_log, d_skip, norm_w):
    b, t, _ = proj.shape
    di = SSD_D_INNER
    gn = SSD_GROUPS * SSD_STATE
    z = proj[..., :di]
    xbc, conv_new = _causal_conv(proj[..., di:di + SSD_CONV_DIM], conv0, conv_w, conv_b)
    xbc = jax.nn.silu(xbc)
    x = xbc[..., :di].reshape(b, t, SSD_GROUPS, SSD_HPG, SSD_HEAD_DIM)
    bm = xbc[..., di:di + gn].reshape(b, t, SSD_GROUPS, SSD_STATE)
    cm = xbc[..., di + gn:].reshape(b, t, SSD_GROUPS, SSD_STATE)
    dt = jax.nn.softplus(proj[..., di + SSD_CONV_DIM:SSD_IN_W] + dt_bias)
    dt = dt.reshape(b, t, SSD_GROUPS, SSD_HPG)
    a = -jnp.exp(a_log).reshape(SSD_GROUPS, SSD_HPG)
    s0 = ssm0.reshape(b, SSD_GROUPS, SSD_HPG, SSD_HEAD_DIM, SSD_STATE)
    y, s_new = _ssd_scan(x, dt, a, bm, cm, s0)
    y = y + d_skip.reshape(SSD_GROUPS, SSD_HPG)[..., None] * x
    y = y.reshape(b, t, di) * jax.nn.silu(z)
    y = _rms(y.reshape(b, t, SSD_GROUPS, di // SSD_GROUPS), norm_w.reshape(SSD_GROUPS, di // SSD_GROUPS))
    return y.reshape(b * t, di), s_new.reshape(b, SSD_HEADS, SSD_HEAD_DIM, SSD_STATE), conv_new


def _pad_cols(w, n):
    return jnp.pad(w, ((0, 0), (0, n - w.shape[1])))


def kernel(x_prompt, x_sample, cache_nsa_kv, cache_nsa_win, state_hgrn, state_ssd, state_ssd_conv, page_table, c_prompt, c_sample, rel_bias, hgrn_lower_bounds, w_ada, b_ada, norm_gains, w_mlp_in, w_mlp_out, nsa_w_in, nsa_cmp_pos, nsa_cmp_w1, nsa_cmp_w2, nsa_w_out, hg_w_in, hg_norm, hg_w_out, ssd_w_in, ssd_conv_w, ssd_conv_b, ssd_dt_bias, ssd_a_log, ssd_d, ssd_norm, ssd_w_out):
    bf16 = jnp.bfloat16
    bp, tp, d = x_prompt.shape
    bs, ts, _ = x_sample.shape
    mp, ms = bp * tp, bs * ts
    lb_p = jax.nn.softmax(hgrn_lower_bounds, axis=0)
    lower_bounds = jnp.cumsum(lb_p, axis=0) - lb_p[0]

    mod = _ada_all(jnp.concatenate([c_prompt, c_sample], axis=0), w_ada, b_ada)
    mod = mod.reshape(DEPTH, bp + bs, ADA_CHUNKS, d)
    mod_p = mod[:, :bp].transpose(0, 2, 1, 3)[:, :, :, None, :]
    mod_s = mod[:, bp:].transpose(0, 2, 1, 3)[:, :, None, :, :]

    xp = x_prompt.reshape(mp, d)
    xs = x_sample.reshape(ms, d)
    tm_p, tm_s = PROMPT_ROW_TILE, ms

    kv_p, kv_s, win_p, win_s = [], [], [], []
    hg_p, hg_s, ssd_p, ssd_s, conv_p, conv_s = [], [], [], [], [], []
    for i in range(DEPTH):
        j = i // N_MIXERS
        kind = i % N_MIXERS
        g = norm_gains[i]
        shp_m, scp_m, gtp_m, shp_f, scp_f, gtp_f = [mod_p[i, c] for c in range(ADA_CHUNKS)]
        shs_m, scs_m, gts_m, shs_f, scs_f, gts_f = [mod_s[i, c] for c in range(ADA_CHUNKS)]
        if kind == 0:
            n_pad = _round_up(NSA_IN_W, 7 * LANES)
            w_in = _pad_cols(nsa_w_in[j], n_pad).astype(bf16)
            w_out = nsa_w_out[j].astype(bf16)
            pp = _norm_mod_matmul(xp, g[0], scp_m, shp_m, w_in, tp, tm_p).reshape(bp, tp, n_pad)
            ps = _norm_mod_matmul(xs, g[0], scs_m, shs_m, w_in, ts, tm_s).reshape(bs, ts, n_pad)
            ap, new_kv_p, new_win_p = _nsa_prompt_core(pp, nsa_cmp_pos[j], nsa_cmp_w1[j], nsa_cmp_w2[j], rel_bias)
            as_, new_kv_s, new_win_s = _nsa_sample_core(ps, cache_nsa_kv[j], cache_nsa_win[j], page_table,
                                                        nsa_cmp_pos[j], nsa_cmp_w1[j], nsa_cmp_w2[j], rel_bias)
            kv_p.append(new_kv_p)
            kv_s.append(new_kv_s)
            win_p.append(new_win_p)
            win_s.append(new_win_s)
        elif kind == 1:
            w_in = hg_w_in[j].astype(bf16)
            w_out = hg_w_out[j].astype(bf16)
            pp = _norm_mod_matmul(xp, g[0], scp_m, shp_m, w_in, tp, tm_p).reshape(bp, tp, -1)
            ps = _norm_mod_matmul(xs, g[0], scs_m, shs_m, w_in, ts, tm_s).reshape(bs, ts, -1)
            s0 = jnp.zeros((bp, HG_HEADS, HG_DK, HG_DV), jnp.float32)
            ap, new_hp = _hgrn2_core(pp, s0, lower_bounds[i], hg_norm[j])
            as_, new_hs = _hgrn2_core(ps, state_hgrn[j], lower_bounds[i], hg_norm[j])
            hg_p.append(new_hp)
            hg_s.append(new_hs)
        else:
            n_pad = _round_up(SSD_IN_W, 7 * LANES)
            w_in = _pad_cols(ssd_w_in[j], n_pad).astype(bf16)
            w_out = ssd_w_out[j].astype(bf16)
            pp = _norm_mod_matmul(xp, g[0], scp_m, shp_m, w_in, tp, tm_p).reshape(bp, tp, n_pad)
            ps = _norm_mod_matmul(xs, g[0], scs_m, shs_m, w_in, ts, tm_s).reshape(bs, ts, n_pad)
            ssm0 = jnp.zeros((bp, SSD_HEADS, SSD_HEAD_DIM, SSD_STATE), jnp.float32)
            conv0 = jnp.zeros((bp, SSD_CONV_W - 1, SSD_CONV_DIM), jnp.float32)
            ap, new_sp, new_cp = _ssd_core(pp, ssm0, conv0, ssd_conv_w[j], ssd_conv_b[j], ssd_dt_bias[j],
                                           ssd_a_log[j], ssd_d[j], ssd_norm[j])
            as_, new_ss, new_cs = _ssd_core(ps, state_ssd[j], state_ssd_conv[j], ssd_conv_w[j], ssd_conv_b[j],
                                            ssd_dt_bias[j], ssd_a_log[j], ssd_d[j], ssd_norm[j])
            ssd_p.append(new_sp)
            ssd_s.append(new_ss)
            conv_p.append(new_cp)
            conv_s.append(new_cs)
        xp = _matmul_norm_res(ap, w_out, xp, g[1], gtp_m, tp, tm_p)
        xs = _matmul_norm_res(as_, w_out, xs, g[1], gts_m, ts, tm_s)
        w1 = w_mlp_in[i].astype(bf16)
        w2 = w_mlp_out[i].astype(bf16)
        xp = _mlp(xp, g[2], scp_f, shp_f, w1, w2, g[3], gtp_f, tp, tm_p)
        xs = _mlp(xs, g[2], scs_f, shs_f, w1, w2, g[3], gts_f, ts, tm_s)
    return (xp.reshape(bp, tp, d), xs.reshape(bs, ts, d),
            jnp.stack(kv_p), jnp.stack(kv_s), jnp.stack(win_p), jnp.stack(win_s),
            jnp.stack(hg_p), jnp.stack(hg_s), jnp.stack(ssd_p), jnp.stack(ssd_s),
            jnp.stack(conv_p), jnp.stack(conv_s))
```

```python
import functools
import math

import jax
import jax.numpy as jnp
import numpy as np
from jax import lax
from jax.experimental import pallas as pl
from jax.experimental.pallas import tpu as pltpu

D_MODEL = 1024
DEPTH = 4
PAGE_SIZE = 128
N_MIXERS = 3
ADA_CHUNKS = 6
NORM_EPS = 1e-6
D_FF = 4 * D_MODEL

NSA_HEADS = 16
NSA_HEAD_DIM = D_MODEL // NSA_HEADS
NSA_KV_HEADS = 4
NSA_GROUP = NSA_HEADS // NSA_KV_HEADS
CMP_STRIDE = 16
CMP_LEN = 2 * CMP_STRIDE
CMP_HIDDEN = 2 * NSA_HEAD_DIM
SEL_BLOCK = 64
SEL_TOP_N = 16
WINDOW = 512
WIN_Q_BLOCK = 128
SEL_Q_BLOCK = 16
NSA_Q_W = NSA_HEADS * NSA_HEAD_DIM
NSA_KV_W = NSA_KV_HEADS * NSA_HEAD_DIM
NSA_IN_W = NSA_Q_W + 6 * NSA_KV_W + 3 * NSA_HEADS

REL_BUCKETS = 32
REL_MAX_DIST = 128

HG_EXPAND = 128
HG_HEADS = D_MODEL // HG_EXPAND
HG_DK = HG_EXPAND
HG_DV = D_MODEL // HG_HEADS
HG_CHUNK = 64

SSD_D_INNER = 2 * D_MODEL
SSD_HEAD_DIM = 64
SSD_HEADS = SSD_D_INNER // SSD_HEAD_DIM
SSD_GROUPS = 8
SSD_HPG = SSD_HEADS // SSD_GROUPS
SSD_STATE = 128
SSD_CONV_W = 4
SSD_CONV_DIM = SSD_D_INNER + 2 * SSD_GROUPS * SSD_STATE
SSD_IN_W = SSD_D_INNER + SSD_CONV_DIM + SSD_HEADS
SSD_CHUNK = 128

NEG_INF = -1e30
FORCE_SCORE = 1e4

LANES = 128
VMEM_LIMIT_BYTES = 48 * 1024 * 1024
PROMPT_ROW_TILE = 512


def _round_up(n, m):
    return -(-n // m) * m


def _col_tile(n, cap=1536):
    best = LANES
    for t in range(LANES, cap + 1, LANES):
        if n % t == 0:
            best = t
    return best


def _rms(x, g):
    return x * lax.rsqrt(jnp.mean(x * x, axis=-1, keepdims=True) + NORM_EPS) * g


def _mod_spec(mod, rows_per_mod, tm, ngrid):
    r = mod.shape[1]
    if r == 1:
        per = rows_per_mod // tm
        if ngrid == 1:
            return pl.BlockSpec((None, 1, mod.shape[2]), lambda i: (i // per, 0, 0))
        return pl.BlockSpec((None, 1, mod.shape[2]), lambda i, j: (i // per, 0, 0))
    if ngrid == 1:
        return pl.BlockSpec((None, r, mod.shape[2]), lambda i: (0, 0, 0))
    return pl.BlockSpec((None, r, mod.shape[2]), lambda i, j: (0, 0, 0))


def _ada_kernel(c_ref, w_ref, b_ref, o_ref):
    c = c_ref[...]
    s = (c * jax.nn.sigmoid(c)).astype(jnp.bfloat16)
    o_ref[...] = jnp.dot(s, w_ref[...].astype(jnp.bfloat16),
                         preferred_element_type=jnp.float32) + b_ref[...]


def _ada_all(c_all, w_ada, b_ada):
    rows = c_all.shape[0]
    n = ADA_CHUNKS * D_MODEL
    tn = 1024
    return pl.pallas_call(
        _ada_kernel,
        grid=(DEPTH, n // tn),
        in_specs=[pl.BlockSpec((rows, D_MODEL), lambda l, j: (0, 0)),
                  pl.BlockSpec((None, D_MODEL, tn), lambda l, j: (l, 0, j)),
                  pl.BlockSpec((None, 1, tn), lambda l, j: (l, 0, j))],
        out_specs=pl.BlockSpec((None, rows, tn), lambda l, j: (l, 0, j)),
        out_shape=jax.ShapeDtypeStruct((DEPTH, rows, n), jnp.float32),
        compiler_params=pltpu.CompilerParams(
            dimension_semantics=("parallel", "parallel"), vmem_limit_bytes=VMEM_LIMIT_BYTES),
        name="ada",
    )(c_all, w_ada, b_ada.reshape(DEPTH, 1, n))


def _norm_mod_matmul_kernel(x_ref, g_ref, sc_ref, sh_ref, w_ref, o_ref, h_ref):
    @pl.when(pl.program_id(1) == 0)
    def _():
        h = _rms(x_ref[...], g_ref[...]) * (1.0 + sc_ref[...]) + sh_ref[...]
        h_ref[...] = h.astype(jnp.bfloat16)

    o_ref[...] = jnp.dot(h_ref[...], w_ref[...], preferred_element_type=jnp.float32)


def _norm_mod_matmul(x, g, sc, sh, w, rows_per_mod, tm):
    m, d = x.shape
    n = w.shape[1]
    tn = _col_tile(n)
    return pl.pallas_call(
        _norm_mod_matmul_kernel,
        grid=(m // tm, n // tn),
        in_specs=[pl.BlockSpec((tm, d), lambda i, j: (i, 0)),
                  pl.BlockSpec((1, d), lambda i, j: (0, 0)),
                  _mod_spec(sc, rows_per_mod, tm, 2),
                  _mod_spec(sh, rows_per_mod, tm, 2),
                  pl.BlockSpec((d, tn), lambda i, j: (0, j))],
        out_specs=pl.BlockSpec((tm, tn), lambda i, j: (i, j)),
        out_shape=jax.ShapeDtypeStruct((m, n), jnp.float32),
        scratch_shapes=[pltpu.VMEM((tm, d), jnp.bfloat16)],
        compiler_params=pltpu.CompilerParams(
            dimension_semantics=("parallel", "arbitrary"), vmem_limit_bytes=VMEM_LIMIT_BYTES),
        name="norm_mod_matmul",
    )(x, g.reshape(1, d), sc, sh, w)


def _matmul_norm_res_kernel(a_ref, w_ref, x_ref, g_ref, gt_ref, o_ref):
    y = jnp.dot(a_ref[...].astype(jnp.bfloat16), w_ref[...], preferred_element_type=jnp.float32)
    o_ref[...] = x_ref[...] + gt_ref[...] * _rms(y, g_ref[...])


def _matmul_norm_res(a, w, x, g, gate, rows_per_mod, tm):
    m, k = a.shape
    d = w.shape[1]
    return pl.pallas_call(
        _matmul_norm_res_kernel,
        grid=(m // tm,),
        in_specs=[pl.BlockSpec((tm, k), lambda i: (i, 0)),
                  pl.BlockSpec((k, d), lambda i: (0, 0)),
                  pl.BlockSpec((tm, d), lambda i: (i, 0)),
                  pl.BlockSpec((1, d), lambda i: (0, 0)),
                  _mod_spec(gate, rows_per_mod, tm, 1)],
        out_specs=pl.BlockSpec((tm, d), lambda i: (i, 0)),
        out_shape=jax.ShapeDtypeStruct((m, d), jnp.float32),
        compiler_params=pltpu.CompilerParams(
            dimension_semantics=("parallel",), vmem_limit_bytes=VMEM_LIMIT_BYTES),
        name="matmul_norm_res",
    )(a, w, x, g.reshape(1, d), gate)


def _mlp_kernel(x_ref, g2_ref, sc_ref, sh_ref, w1_ref, w2_ref, g3_ref, gt_ref, o_ref, h_ref, acc_ref):
    j = pl.program_id(1)

    @pl.when(j == 0)
    def _():
        h = _rms(x_ref[...], g2_ref[...]) * (1.0 + sc_ref[...]) + sh_ref[...]
        h_ref[...] = h.astype(jnp.bfloat16)

    u = jnp.dot(h_ref[...], w1_ref[...], preferred_element_type=jnp.float32)
    u = jnp.square(jnp.maximum(u, 0.0)).astype(jnp.bfloat16)
    part = jnp.dot(u, w2_ref[...], preferred_element_type=jnp.float32)

    @pl.when(j == 0)
    def _():
        acc_ref[...] = part

    @pl.when(j > 0)
    def _():
        acc_ref[...] += part

    @pl.when(j == pl.num_programs(1) - 1)
    def _():
        o_ref[...] = x_ref[...] + gt_ref[...] * _rms(acc_ref[...], g3_ref[...])


def _mlp(x, g2, sc, sh, w1, w2, g3, gate, rows_per_mod, tm):
    m, d = x.shape
    f = w1.shape[1]
    tf = 1024
    return pl.pallas_call(
        _mlp_kernel,
        grid=(m // tm, f // tf),
        in_specs=[pl.BlockSpec((tm, d), lambda i, j: (i, 0)),
                  pl.BlockSpec((1, d), lambda i, j: (0, 0)),
                  _mod_spec(sc, rows_per_mod, tm, 2),
                  _mod_spec(sh, rows_per_mod, tm, 2),
                  pl.BlockSpec((d, tf), lambda i, j: (0, j)),
                  pl.BlockSpec((tf, d), lambda i, j: (j, 0)),
                  pl.BlockSpec((1, d), lambda i, j: (0, 0)),
                  _mod_spec(gate, rows_per_mod, tm, 2)],
        out_specs=pl.BlockSpec((tm, d), lambda i, j: (i, 0)),
        out_shape=jax.ShapeDtypeStruct((m, d), jnp.float32),
        scratch_shapes=[pltpu.VMEM((tm, d), jnp.bfloat16), pltpu.VMEM((tm, d), jnp.float32)],
        compiler_params=pltpu.CompilerParams(
            dimension_semantics=("parallel", "arbitrary"), vmem_limit_bytes=VMEM_LIMIT_BYTES),
        name="mlp",
    )(x, g2.reshape(1, d), sc, sh, w1, w2, g3.reshape(1, d), gate)


NSA_COL_BLOCK = NSA_KV_W
NSA_PROJ_W = 11 * NSA_COL_BLOCK
NSA_GATE_BLOCK = (NSA_Q_W + 6 * NSA_KV_W) // NSA_COL_BLOCK
ATT_TILE = 128
ROWS = NSA_GROUP * ATT_TILE


def _dot_nt(a, b):
    return lax.dot_general(a, b, (((1,), (1,)), ((), ())), preferred_element_type=jnp.float32)


def _dot_tn(a, b):
    return lax.dot_general(a, b, (((0,), (0,)), ((), ())), preferred_element_type=jnp.float32)


def _gelu_tanh(x):
    return 0.5 * x * (1.0 + jnp.tanh(math.sqrt(2.0 / math.pi) * (x + 0.044715 * (x * x * x))))


def _split3_bf16(x):
    hi = x.astype(jnp.bfloat16)
    r1 = x - hi.astype(jnp.float32)
    mid = r1.astype(jnp.bfloat16)
    lo = (r1 - mid.astype(jnp.float32)).astype(jnp.bfloat16)
    return hi, mid, lo


def _nsa_compress_kernel(x0_ref, x1_ref, x2_ref, x3_ref, w1_ref, w1f_ref, pos_ref, w2_ref, o_ref):
    n = x0_ref.shape[0] // CMP_STRIDE
    hd = NSA_HEAD_DIM
    x_refs = ((x0_ref, x1_ref), (x2_ref, x3_ref))
    for r in range(2):
        pos_b = jnp.dot(pos_ref[r].astype(jnp.bfloat16), w1f_ref[r], preferred_element_type=jnp.float32)
        acc = [jnp.zeros((n, 2 * CMP_HIDDEN), jnp.float32) for _ in range(NSA_KV_HEADS)]
        for l in range(CMP_STRIDE):
            w = w1_ref[r, l]
            for pair in range(2):
                xl = x_refs[r][pair][pl.ds(l, n, stride=CMP_STRIDE), :].astype(jnp.bfloat16)
                for half in range(2):
                    k = 2 * pair + half
                    acc[k] = acc[k] + jnp.dot(xl[:, half * hd:(half + 1) * hd], w,
                                              preferred_element_type=jnp.float32)
        for k in range(NSA_KV_HEADS):
            pa = acc[k][:, :CMP_HIDDEN]
            pb_next = pltpu.roll(acc[k][:, CMP_HIDDEN:], n - 1, 0)
            hid = _gelu_tanh(pa + pb_next + pos_b)
            out = jnp.dot(hid.astype(jnp.bfloat16), w2_ref[r], preferred_element_type=jnp.float32)
            o_ref[r, :, k * hd:(k + 1) * hd] = out.astype(o_ref.dtype)


def _compress_weights(cmp_pos, w1, w2):
    bf16 = jnp.bfloat16
    w1r = w1.reshape(2, CMP_LEN, NSA_HEAD_DIM, CMP_HIDDEN)
    w1cat = jnp.concatenate([w1r[:, :CMP_STRIDE], w1r[:, CMP_STRIDE:]], axis=-1).astype(bf16)
    return w1cat, w1.astype(bf16), cmp_pos.reshape(2, 1, CMP_LEN * NSA_HEAD_DIM), w2.astype(bf16)


def _nsa_compress_prompt(proj, cmp_pos, w1, w2):
    b, t, _ = proj.shape
    n = t // CMP_STRIDE
    w1cat, w1f, pos, w2b = _compress_weights(cmp_pos, w1, w2)
    return pl.pallas_call(
        _nsa_compress_kernel,
        grid=(b,),
        in_specs=[pl.BlockSpec((None, t, LANES), lambda i, c=c: (i, 0, NSA_Q_W // LANES + c)) for c in range(4)]
                 + [pl.BlockSpec(w1cat.shape, lambda i: (0, 0, 0, 0)),
                  pl.BlockSpec(w1f.shape, lambda i: (0, 0, 0)),
                  pl.BlockSpec(pos.shape, lambda i: (0, 0, 0)),
                  pl.BlockSpec(w2b.shape, lambda i: (0, 0, 0))],
        out_specs=pl.BlockSpec((None, 2, n, NSA_KV_W), lambda i: (i, 0, 0, 0)),
        out_shape=jax.ShapeDtypeStruct((b, 2, n, NSA_KV_W), jnp.bfloat16),
        compiler_params=pltpu.CompilerParams(
            dimension_semantics=("parallel",), vmem_limit_bytes=VMEM_LIMIT_BYTES),
        name="nsa_compress",
    )(proj, proj, proj, proj, w1cat, w1f, pos, w2b)


def _softmax_step(s, v, m, l, acc):
    m_new = jnp.maximum(m, jnp.max(s, axis=-1, keepdims=True))
    alpha = jnp.exp(m - m_new)
    e = jnp.exp(s - m_new)
    l = alpha * l + jnp.sum(e, axis=-1, keepdims=True)
    acc = alpha * acc + jnp.dot(e.astype(jnp.bfloat16), v, preferred_element_type=jnp.float32)
    return m_new, l, acc


def _nsa_attn_kernel(q_ref, g_ref, c_ref, ks_ref, vs_ref, kw_ref, vw_ref, bc_ref, bt_ref,
                     mimp_ref, esel_ref, eg_ref, o_ref, oc_s, os_s, ow_s, selk_s):
    i = pl.program_id(1)
    hd = NSA_HEAD_DIM
    tq = ATT_TILE
    bf16 = jnp.bfloat16
    n_cmp_pad = c_ref.shape[1]
    n_sel = mimp_ref.shape[0]
    qi = lax.broadcasted_iota(jnp.int32, (tq, tq), 0)
    kj = lax.broadcasted_iota(jnp.int32, (tq, tq), 1)
    causal = (kj <= qi)[None]
    win_edge = (kj >= qi)[None]
    q_pos_c = i * tq + lax.broadcasted_iota(jnp.int32, (tq, n_cmp_pad), 0)
    cmp_end = CMP_STRIDE * lax.broadcasted_iota(jnp.int32, (tq, n_cmp_pad), 1) + (CMP_LEN - 1)
    mask_c = (cmp_end <= q_pos_c)[None]
    blk = lax.broadcasted_iota(jnp.int32, (n_sel, tq), 0)
    cur = (i * tq + lax.broadcasted_iota(jnp.int32, (n_sel, tq), 1)) // SEL_BLOCK
    forced = (blk == 0) | (blk == cur) | (blk == cur - 1)
    valid = blk <= cur

    def masked(s, mask3, fill=NEG_INF):
        return jnp.where(mask3, s.reshape(NSA_GROUP, tq, -1), fill).reshape(s.shape)

    for k in range(NSA_KV_HEADS):
        lanes = slice(k * hd, (k + 1) * hd)
        qk = jnp.concatenate(
            [q_ref[:, (k * NSA_GROUP + g) * hd:(k * NSA_GROUP + g + 1) * hd] for g in range(NSA_GROUP)], axis=0)
        qk = (qk * (hd ** -0.5)).astype(bf16)

        s = masked(_dot_nt(qk, c_ref[0, :, lanes]) + bc_ref[k], mask_c)
        m = jnp.max(s, axis=-1, keepdims=True)
        e = masked(jnp.exp(s - m), mask_c, 0.0)
        p = e / jnp.maximum(jnp.sum(e, axis=-1, keepdims=True), 1e-30)
        o_cmp = jnp.dot(p.astype(bf16), c_ref[1, :, lanes], preferred_element_type=jnp.float32)
        p_sum = p.reshape(NSA_GROUP, tq, n_cmp_pad).sum(axis=0)
        imp = sum(_dot_nt(mimp_ref[...], part) for part in _split3_bf16(p_sum))
        score = jnp.where(valid, jnp.where(forced, FORCE_SCORE, imp), NEG_INF)
        rank = jnp.zeros((n_sel, tq), jnp.float32)
        for j in range(n_sel):
            row = score[j:j + 1, :]
            beats = (row > score) | ((row == score) & (blk > j))
            rank = rank + jnp.where(beats, 1.0, 0.0)
        sel = jnp.where((rank < SEL_TOP_N) & (score > 0.5 * NEG_INF), 1.0, 0.0)
        selk_s[...] = _dot_tn(sel.astype(bf16), esel_ref[...])

        def sel_chunk(c, carry, bias, extra):
            rows = pl.ds(pl.multiple_of(c * tq, tq), tq)
            kk = ks_ref[rows, lanes].astype(bf16)
            vv = vs_ref[rows, lanes].astype(bf16)
            s = _dot_nt(qk, kk)
            if bias is not None:
                s = s + bias
            mk = (selk_s[:, rows] > 0.5)[None]
            if extra is not None:
                mk = mk & extra
            return _softmax_step(masked(s, mk), vv, *carry)

        init = (jnp.full((ROWS, 1), NEG_INF, jnp.float32), jnp.zeros((ROWS, 1), jnp.float32),
                jnp.zeros((ROWS, hd), jnp.float32))
        carry = sel_chunk(i, init, bt_ref[k, 0], causal)
        prev_ok = jnp.broadcast_to(i >= 1, (1, tq, tq))
        carry = sel_chunk(jnp.maximum(i - 1, 0), carry, bt_ref[k, 1], prev_ok)
        carry = lax.fori_loop(0, jnp.maximum(i - 1, 0), lambda c, cr: sel_chunk(c, cr, None, None), carry)
        o_sel = carry[2] / carry[1]

        def win_chunk(back, carry, bias, extra):
            c = i - back
            ok = jnp.broadcast_to(c >= 0, (1, tq, tq))
            rows = pl.ds(pl.multiple_of(jnp.maximum(c, 0) * tq, tq), tq)
            kk = kw_ref[rows, lanes].astype(bf16)
            vv = vw_ref[rows, lanes].astype(bf16)
            s = _dot_nt(qk, kk)
            if bias is not None:
                s = s + bias
            mk = ok if extra is None else ok & extra
            return _softmax_step(masked(s, mk), vv, *carry)

        carry = win_chunk(0, init, bt_ref[k, 0], causal)
        carry = win_chunk(1, carry, bt_ref[k, 1], None)
        for back in range(2, WINDOW // tq):
            carry = win_chunk(back, carry, None, None)
        carry = win_chunk(WINDOW // tq, carry, None, win_edge)
        o_win = carry[2] / carry[1]

        for g in range(NSA_GROUP):
            cols = slice((k * NSA_GROUP + g) * hd, (k * NSA_GROUP + g + 1) * hd)
            oc_s[:, cols] = o_cmp[g * tq:(g + 1) * tq]
            os_s[:, cols] = o_sel[g * tq:(g + 1) * tq]
            ow_s[:, cols] = o_win[g * tq:(g + 1) * tq]

    gate = jax.nn.sigmoid(g_ref[...])
    g_hi = gate.astype(bf16)
    g_lo = (gate - g_hi.astype(jnp.float32)).astype(bf16)
    out = jnp.zeros(o_ref.shape, jnp.float32)
    for br, o_s in enumerate((oc_s, os_s, ow_s)):
        g_full = (jnp.dot(g_hi, eg_ref[br], preferred_element_type=jnp.float32)
                  + jnp.dot(g_lo, eg_ref[br], preferred_element_type=jnp.float32))
        out = out + g_full * o_s[...]
    o_ref[...] = out.astype(o_ref.dtype)


def _bias_lookup(rel_bias, dist):
    onehot = jax.nn.one_hot(_rel_bucket(dist), REL_BUCKETS, dtype=jnp.float32)
    return jnp.einsum('...c,ch->...h', onehot, rel_bias, precision=lax.Precision.HIGHEST)


def _rows_by_kv_head(tab):
    *lead, q, t, _ = tab.shape
    tab = tab.reshape(*lead, q, t, NSA_KV_HEADS, NSA_GROUP)
    nl = len(lead)
    tab = jnp.transpose(tab, (*range(nl), nl + 2, nl + 3, nl, nl + 1))
    return tab.reshape(*lead, NSA_KV_HEADS, NSA_GROUP * q, t)


def _nsa_prompt_tables(rel_bias, t):
    tq = ATT_TILE
    n_chunks = t // CMP_STRIDE
    n_sel = t // SEL_BLOCK
    far = rel_bias[REL_BUCKETS - 1]
    ar = jnp.arange(tq)
    d_tile = (jnp.arange(2) * tq)[:, None, None] + ar[None, :, None] - ar[None, None, :]
    bt = _rows_by_kv_head(_bias_lookup(rel_bias, d_tile) - far)
    bt = jnp.transpose(bt, (1, 0, 2, 3))
    q_pos = jnp.arange(t).reshape(t // tq, tq)
    cmp_end = jnp.arange(n_chunks) * CMP_STRIDE + CMP_LEN - 1
    bc = _rows_by_kv_head(_bias_lookup(rel_bias, q_pos[:, :, None] - cmp_end[None, None, :]) - far)
    n_idx = np.arange(n_chunks)
    j_idx = np.arange(n_sel)[:, None]
    per = SEL_BLOCK // CMP_STRIDE
    mimp = 0.5 * ((n_idx // per == j_idx).astype(np.float32) + ((n_idx + 1) // per == j_idx).astype(np.float32))
    mimp[:, n_chunks - 1] = 0.0
    esel = (np.arange(t)[None, :] // SEL_BLOCK == np.arange(n_sel)[:, None]).astype(np.float32)
    col = np.arange(NSA_Q_W) // NSA_HEAD_DIM
    eg = np.zeros((3, NSA_COL_BLOCK, NSA_Q_W), np.float32)
    for br in range(3):
        eg[br, br * NSA_HEADS + col, np.arange(NSA_Q_W)] = 1.0
    return bt, bc, jnp.asarray(mimp, jnp.bfloat16), jnp.asarray(esel, jnp.bfloat16), jnp.asarray(eg, jnp.bfloat16)


def _nsa_attn_prompt(proj, cmp, tables):
    b, t, _ = proj.shape
    bt, bc, mimp, esel, eg = tables
    tq = ATT_TILE
    cb = NSA_COL_BLOCK
    first_kv = NSA_Q_W // cb

    def kv_spec(slab):
        return pl.BlockSpec((None, t, cb), lambda bi, i: (bi, 0, first_kv + slab))

    def const_spec(a):
        nd = a.ndim
        return pl.BlockSpec(a.shape, lambda bi, i: (0,) * nd)

    return pl.pallas_call(
        _nsa_attn_kernel,
        grid=(b, t // tq),
        in_specs=[pl.BlockSpec((None, tq, NSA_Q_W), lambda bi, i: (bi, i, 0)),
                  pl.BlockSpec((None, tq, cb), lambda bi, i: (bi, i, NSA_GATE_BLOCK)),
                  pl.BlockSpec((None,) + cmp.shape[1:], lambda bi, i: (bi, 0, 0, 0)),
                  kv_spec(2), kv_spec(3), kv_spec(4), kv_spec(5),
                  pl.BlockSpec((None,) + bc.shape[1:], lambda bi, i: (i, 0, 0, 0)),
                  const_spec(bt), const_spec(mimp), const_spec(esel), const_spec(eg)],
        out_specs=pl.BlockSpec((None, tq, NSA_Q_W), lambda bi, i: (bi, i, 0)),
        out_shape=jax.ShapeDtypeStruct((b, t, NSA_Q_W), jnp.bfloat16),
        scratch_shapes=[pltpu.VMEM((tq, NSA_Q_W), jnp.float32)] * 3 + [pltpu.VMEM((tq, t), jnp.float32)],
        compiler_params=pltpu.CompilerParams(
            dimension_semantics=("parallel", "arbitrary"), vmem_limit_bytes=VMEM_LIMIT_BYTES),
        name="nsa_attn",
    )(proj, proj, cmp, proj, proj, proj, proj, bc, bt, mimp, esel, eg).reshape(b * t, NSA_Q_W)


def _masked_softmax(logits, mask):
    l = jnp.where(mask, logits.astype(jnp.float32), NEG_INF)
    m = jnp.max(l, axis=-1, keepdims=True)
    e = jnp.where(mask, jnp.exp(l - m), 0.0)
    return e / jnp.maximum(jnp.sum(e, axis=-1, keepdims=True), 1e-30)


def _rel_bucket(dist):
    n = jnp.maximum(dist, 0)
    n_exact = REL_BUCKETS // 2
    nf = jnp.maximum(n, 1).astype(jnp.float32)
    large = n_exact + (jnp.log(nf / n_exact) / math.log(REL_MAX_DIST / n_exact)
                       * (REL_BUCKETS - n_exact)).astype(jnp.int32)
    return jnp.where(n < n_exact, n, jnp.minimum(large, REL_BUCKETS - 1))


def _head_bias(rel_bias, dist):
    q, t = dist.shape
    b = rel_bias[_rel_bucket(dist)].astype(jnp.float32)
    return b.reshape(q, t, NSA_KV_HEADS, NSA_GROUP).transpose(0, 2, 3, 1)


def _nsa_split(proj):
    b, t, _ = proj.shape
    o1 = NSA_Q_W
    o2 = o1 + 4 * NSA_KV_W
    o3 = o2 + 2 * NSA_KV_W
    qg = proj[..., :o1].reshape(b, t, NSA_KV_HEADS, NSA_GROUP, NSA_HEAD_DIM)
    kv_cs = proj[..., o1:o2].reshape(b, t, 4, NSA_KV_HEADS, NSA_HEAD_DIM)
    kv_win = proj[..., o2:o3].reshape(b, t, 2, NSA_KV_HEADS, NSA_HEAD_DIM)
    gates = jax.nn.sigmoid(proj[..., o3:NSA_IN_W]).reshape(b, t, 3, NSA_KV_HEADS, NSA_GROUP)
    return qg, kv_cs, kv_win, gates


def _nsa_compress(kv_pad, cmp_pos, w1, w2):
    b, tpad = kv_pad.shape[:2]
    n_chunks = tpad // CMP_STRIDE
    chunks = kv_pad[:, :, :2].reshape(b, n_chunks, CMP_STRIDE, 2, NSA_KV_HEADS, NSA_HEAD_DIM)
    w1 = w1.reshape(2, CMP_LEN, NSA_HEAD_DIM, CMP_HIDDEN)
    pos_bias = jnp.einsum('rld,rlde->re', cmp_pos, w1)
    pa = jnp.einsum('bnlrkd,rlde->bnrke', chunks, w1[:, :CMP_STRIDE])
    pb = jnp.einsum('bnlrkd,rlde->bnrke', chunks, w1[:, CMP_STRIDE:])
    hidden = jax.nn.gelu(pa[:, :-1] + pb[:, 1:] + pos_bias[:, None, :])
    out = jnp.einsum('bnrke,ref->bnrkf', hidden, w2)
    return out[:, :, 0], out[:, :, 1]


def _nsa_sparse_block(qg, q_pos, k_cmp, v_cmp, cmp_end, sel_kv, rel_bias):
    b, qc = qg.shape[:2]
    n_sel = sel_kv.shape[1]
    scale = NSA_HEAD_DIM ** -0.5
    dist_c = q_pos[:, None] - cmp_end[None, :]
    logits_c = jnp.einsum('bqkgd,bnkd->bqkgn', qg, k_cmp) * scale + _head_bias(rel_bias, dist_c)
    p_c = _masked_softmax(logits_c, (dist_c >= 0)[:, None, None, :])
    o_cmp = jnp.einsum('bqkgn,bnkd->bqkgd', p_c, v_cmp.astype(jnp.float32))
    imp = jnp.pad(p_c.sum(axis=3), ((0, 0), (0, 0), (0, 0), (1, 1)))
    imp = 0.5 * (imp[..., 1:] + imp[..., :-1])
    imp = imp.reshape(b, qc, NSA_KV_HEADS, n_sel, SEL_BLOCK // CMP_STRIDE).sum(-1)
    blk = jnp.arange(n_sel)[None, :]
    cur = (q_pos // SEL_BLOCK)[:, None]
    forced = (blk == 0) | (blk == cur) | (blk == cur - 1)
    valid = blk * SEL_BLOCK <= q_pos[:, None]
    score = jnp.where(forced[None, :, None, :], FORCE_SCORE, imp)
    score = jnp.where(valid[None, :, None, :], score, NEG_INF)
    n_cand = max(n_sel, SEL_TOP_N)
    score = jnp.pad(score, ((0, 0), (0, 0), (0, 0), (0, n_cand - n_sel)), constant_values=NEG_INF)
    top_val, top_idx = lax.top_k(score, SEL_TOP_N)
    sel_ok = top_val > 0.5 * NEG_INF
    top_idx = jnp.minimum(top_idx, n_sel - 1)
    bi = jnp.arange(b)[:, None, None, None]
    ki = jnp.arange(NSA_KV_HEADS)[None, None, :, None]
    kv_sel = sel_kv[bi, top_idx, :, :, ki]
    kpos = top_idx[..., None] * SEL_BLOCK + jnp.arange(SEL_BLOCK)
    dist_s = q_pos[None, :, None, None, None] - kpos
    mask_s = (sel_ok[..., None] & (dist_s >= 0)).reshape(b, qc, NSA_KV_HEADS, 1, SEL_TOP_N * SEL_BLOCK)
    table = rel_bias.reshape(REL_BUCKETS, NSA_KV_HEADS, NSA_GROUP).transpose(1, 0, 2)
    bias_s = table[jnp.arange(NSA_KV_HEADS)[None, None, :, None, None], _rel_bucket(dist_s)]
    bias_s = jnp.moveaxis(bias_s, -1, 3).reshape(b, qc, NSA_KV_HEADS, NSA_GROUP, SEL_TOP_N * SEL_BLOCK)
    logits_s = jnp.einsum('bqkgd,bqknsd->bqkgns', qg, kv_sel[..., 0, :])
    logits_s = logits_s.reshape(b, qc, NSA_KV_HEADS, NSA_GROUP, SEL_TOP_N * SEL_BLOCK) * scale + bias_s.astype(jnp.float32)
    p_s = _masked_softmax(logits_s, mask_s)
    v_sel = kv_sel[..., 1, :].reshape(b, qc, NSA_KV_HEADS, SEL_TOP_N * SEL_BLOCK, NSA_HEAD_DIM)
    o_sel = jnp.einsum('bqkgm,bqkmd->bqkgd', p_s, v_sel.astype(jnp.float32))
    return o_cmp, o_sel


def _nsa_sparse(qg, q_pos, kv_all, cmp_pos, w1, w2, rel_bias):
    b, t_kv = kv_all.shape[:2]
    t_pad = -(-t_kv // SEL_BLOCK) * SEL_BLOCK
    kv_pad = jnp.pad(kv_all, ((0, 0), (0, t_pad - t_kv), (0, 0), (0, 0), (0, 0)))
    k_cmp, v_cmp = _nsa_compress(kv_pad, cmp_pos, w1, w2)
    cmp_end = jnp.arange(k_cmp.shape[1]) * CMP_STRIDE + CMP_LEN - 1
    sel_kv = kv_pad[:, :, 2:].reshape(b, t_pad // SEL_BLOCK, SEL_BLOCK, 2, NSA_KV_HEADS, NSA_HEAD_DIM)
    tq = qg.shape[1]
    qb = SEL_Q_BLOCK if tq % SEL_Q_BLOCK == 0 else tq
    nq = tq // qb
    q_blocks = jnp.moveaxis(qg.reshape(b, nq, qb, NSA_KV_HEADS, NSA_GROUP, NSA_HEAD_DIM), 1, 0)
    pos_blocks = q_pos.reshape(nq, qb)
    o_cmp, o_sel = lax.map(
        lambda a: _nsa_sparse_block(a[0], a[1], k_cmp, v_cmp, cmp_end, sel_kv, rel_bias),
        (q_blocks, pos_blocks))
    shape = (b, tq, NSA_KV_HEADS, NSA_GROUP, NSA_HEAD_DIM)
    return jnp.moveaxis(o_cmp, 0, 1).reshape(shape), jnp.moveaxis(o_sel, 0, 1).reshape(shape)


def _window_attend(qg, q_pos, kv, k_pos, rel_bias):
    dist = q_pos[:, None] - k_pos[None, :]
    mask = (dist >= 0) & (dist <= WINDOW) & (k_pos >= 0)[None, :]
    logits = jnp.einsum('bqkgd,btkd->bqkgt', qg, kv[:, :, 0]) * NSA_HEAD_DIM ** -0.5 + _head_bias(rel_bias, dist)
    p = _masked_softmax(logits, mask[:, None, None, :])
    return jnp.einsum('bqkgt,btkd->bqkgd', p, kv[:, :, 1].astype(jnp.float32))


def _nsa_window_prompt(qg, kv_win, rel_bias):
    b, t = qg.shape[:2]
    nb = t // WIN_Q_BLOCK
    kv_pad = jnp.pad(kv_win, ((0, 0), (WINDOW, 0), (0, 0), (0, 0), (0, 0)))
    q_blocks = jnp.moveaxis(qg.reshape(b, nb, WIN_Q_BLOCK, NSA_KV_HEADS, NSA_GROUP, NSA_HEAD_DIM), 1, 0)

    def block(args):
        qblk, i = args
        start = i * WIN_Q_BLOCK
        kv = lax.dynamic_slice_in_dim(kv_pad, start, WINDOW + WIN_Q_BLOCK, axis=1)
        k_pos = start - WINDOW + jnp.arange(WINDOW + WIN_Q_BLOCK)
        q_pos = start + jnp.arange(WIN_Q_BLOCK)
        return _window_attend(qblk, q_pos, kv, k_pos, rel_bias)

    o = lax.map(block, (q_blocks, jnp.arange(nb)))
    return jnp.moveaxis(o, 0, 1).reshape(b, t, NSA_KV_HEADS, NSA_GROUP, NSA_HEAD_DIM)


def _nsa_merge(o_cmp, o_sel, o_win, gates):
    b, t = gates.shape[:2]
    o = (gates[:, :, 0, ..., None] * o_cmp + gates[:, :, 1, ..., None] * o_sel
         + gates[:, :, 2, ..., None] * o_win)
    return o.reshape(b * t, NSA_Q_W)


def _nsa_prompt_core(proj, cmp_pos, cmp_w1, cmp_w2, tables):
    b, t, _ = proj.shape
    cmp = _nsa_compress_prompt(proj, cmp_pos, cmp_w1, cmp_w2)
    merged = _nsa_attn_prompt(proj, cmp, tables)
    o1 = NSA_Q_W
    o2 = o1 + 4 * NSA_KV_W
    o3 = o2 + 2 * NSA_KV_W
    kv_cs = proj[..., o1:o2].reshape(b, t, 4, NSA_KV_HEADS, NSA_HEAD_DIM)
    kv_win = proj[:, t - min(WINDOW, t):, o2:o3].reshape(b, min(WINDOW, t), 2, NSA_KV_HEADS, NSA_HEAD_DIM)
    return merged, kv_cs, kv_win


def _page_copy_kernel(pt_ref, src_ref, dst_ref):
    del pt_ref
    dst_ref[...] = src_ref[...]


def _gather_pages(kv_pages, page_table):
    n_phys = kv_pages.shape[0]
    b, n_pages = page_table.shape
    width = math.prod(kv_pages.shape[2:])
    pages = kv_pages.reshape(n_phys, PAGE_SIZE, width)
    return pl.pallas_call(
        _page_copy_kernel,
        grid_spec=pltpu.PrefetchScalarGridSpec(
            num_scalar_prefetch=1,
            grid=(b, n_pages),
            in_specs=[pl.BlockSpec((None, PAGE_SIZE, width), lambda i, p, pt: (pt[i, p], 0, 0))],
            out_specs=pl.BlockSpec((None, None, PAGE_SIZE, width), lambda i, p, pt: (i, p, 0, 0))),
        out_shape=jax.ShapeDtypeStruct((b, n_pages, PAGE_SIZE, width), kv_pages.dtype),
        compiler_params=pltpu.CompilerParams(
            dimension_semantics=("parallel", "parallel"), vmem_limit_bytes=VMEM_LIMIT_BYTES),
        name="gather_pages",
    )(page_table, pages)


def _nsa_sample_core(proj, kv_pages, win_buf, page_table, cmp_pos, cmp_w1, cmp_w2, rel_bias):
    b, t, _ = proj.shape
    past_len = page_table.shape[1] * PAGE_SIZE
    qg, kv_cs, kv_win, gates = _nsa_split(proj)
    past = _gather_pages(kv_pages, page_table).reshape(b, past_len, 4, NSA_KV_HEADS, NSA_HEAD_DIM)
    kv_all = jnp.concatenate([past, kv_cs.astype(past.dtype)], axis=1)
    q_pos = past_len + jnp.arange(t)
    o_cmp, o_sel = _nsa_sparse(qg, q_pos, kv_all, cmp_pos, cmp_w1, cmp_w2, rel_bias)
    n_buf = win_buf.shape[1]
    kv_w_all = jnp.concatenate([win_buf, kv_win.astype(win_buf.dtype)], axis=1)
    k_pos = past_len - n_buf + jnp.arange(n_buf + t)
    o_win = _window_attend(qg, q_pos, kv_w_all, k_pos, rel_bias)
    new_win = kv_w_all[:, kv_w_all.shape[1] - min(WINDOW, past_len + t):]
    return _nsa_merge(o_cmp, o_sel, o_win, gates), kv_cs, new_win


def _gla_scan(q, k, v, log_f, s0):
    b, t = q.shape[:2]
    c = HG_CHUNK if t % HG_CHUNK == 0 else t
    nc = t // c
    causal = np.tril(np.ones((c, c), dtype=bool))[None, :, :, None, None]

    def to_chunks(a):
        return jnp.moveaxis(a.astype(jnp.float32).reshape(b, nc, c, *a.shape[2:]), 1, 0)

    def step(s, inp):
        qc, kc, vc, gc = inp
        cum = jnp.cumsum(gc, axis=1)
        decay = jnp.exp(jnp.where(causal, cum[:, :, None] - cum[:, None, :], -jnp.inf))
        attn = jnp.einsum('bthk,btshk,bshk->bhts', qc, decay, kc)
        o = (jnp.einsum('bhts,bshv->bthv', attn, vc)
             + jnp.einsum('bthk,bhkv->bthv', qc * jnp.exp(cum), s))
        last = cum[:, -1]
        s_new = (jnp.exp(last)[..., None] * s
                 + jnp.einsum('bshk,bshv->bhkv', kc * jnp.exp(last[:, None] - cum), vc))
        return s_new, o

    s_final, o = lax.scan(step, s0.astype(jnp.float32),
                          (to_chunks(q), to_chunks(k), to_chunks(v), to_chunks(log_f)))
    return jnp.moveaxis(o, 0, 1).reshape(b, t, HG_HEADS, HG_DV), s_final


def _hgrn2_core(proj, s0, lb, g_norm):
    b, t, _ = proj.shape
    wk = HG_HEADS * HG_DK
    wv = HG_HEADS * HG_DV
    q = jax.nn.silu(proj[..., :wk]).reshape(b, t, HG_HEADS, HG_DK)
    f = lb + (1.0 - lb) * jax.nn.sigmoid(proj[..., wk:2 * wk])
    v = proj[..., 2 * wk:2 * wk + wv].reshape(b, t, HG_HEADS, HG_DV)
    g = proj[..., 2 * wk + wv:].reshape(b, t, HG_HEADS, HG_DV)
    k = (1.0 - f).reshape(b, t, HG_HEADS, HG_DK)
    log_f = jnp.log(f).reshape(b, t, HG_HEADS, HG_DK)
    o, s_new = _gla_scan(q, k, v, log_f, s0)
    o = _rms(o, g_norm) * jax.nn.silu(g)
    return o.reshape(b * t, wv), s_new


def _causal_conv(u, prev, w, bias):
    t = u.shape[1]
    full = jnp.concatenate([prev.astype(u.dtype), u], axis=1)
    y = sum(full[:, i:i + t] * w[i] for i in range(SSD_CONV_W)) + bias
    return y, full[:, t:]


def _ssd_scan(x, dt, a, bm, cm, s0):
    b, t = x.shape[:2]
    c = SSD_CHUNK if t % SSD_CHUNK == 0 else t
    nc = t // c
    causal = np.tril(np.ones((c, c), dtype=bool))[None, :, :, None, None]

    def to_chunks(arr):
        return jnp.moveaxis(arr.reshape(b, nc, c, *arr.shape[2:]), 1, 0)

    def step(s, inp):
        xc, dtc, bc, cc = inp
        cum = jnp.cumsum(dtc * a, axis=1)
        decay = jnp.exp(jnp.where(causal, cum[:, :, None] - cum[:, None, :], -jnp.inf))
        cb = jnp.einsum('btgn,bsgn->btsg', cc, bc)
        w = cb[..., None] * decay * dtc[:, None]
        y = (jnp.einsum('btsgh,bsghp->btghp', w, xc)
             + jnp.einsum('btgn,bghpn->btghp', cc, s) * jnp.exp(cum)[..., None])
        last = cum[:, -1]
        s_new = (jnp.exp(last)[..., None, None] * s
                 + jnp.einsum('bsgn,bsgh,bsghp->bghpn', bc, dtc * jnp.exp(last[:, None] - cum), xc))
        return s_new, y

    s_final, y = lax.scan(step, s0, (to_chunks(x), to_chunks(dt), to_chunks(bm), to_chunks(cm)))
    return jnp.moveaxis(y, 0, 1).reshape(b, t, SSD_GROUPS, SSD_HPG, SSD_HEAD_DIM), s_final


def _ssd_core(proj, ssm0, conv0, conv_w, conv_b, dt_bias, a_log, d_skip, norm_w):
    b, t, _ = proj.shape
    di = SSD_D_INNER
    gn = SSD_GROUPS * SSD_STATE
    z = proj[..., :di]
    xbc, conv_new = _causal_conv(proj[..., di:di + SSD_CONV_DIM], conv0, conv_w, conv_b)
    xbc = jax.nn.silu(xbc)
    x = xbc[..., :di].reshape(b, t, SSD_GROUPS, SSD_HPG, SSD_HEAD_DIM)
    bm = xbc[..., di:di + gn].reshape(b, t, SSD_GROUPS, SSD_STATE)
    cm = xbc[..., di + gn:].reshape(b, t, SSD_GROUPS, SSD_STATE)
    dt = jax.nn.softplus(proj[..., di + SSD_CONV_DIM:SSD_IN_W] + dt_bias)
    dt = dt.reshape(b, t, SSD_GROUPS, SSD_HPG)
    a = -jnp.exp(a_log).reshape(SSD_GROUPS, SSD_HPG)
    s0 = ssm0.reshape(b, SSD_GROUPS, SSD_HPG, SSD_HEAD_DIM, SSD_STATE)
    y, s_new = _ssd_scan(x, dt, a, bm, cm, s0)
    y = y + d_skip.reshape(SSD_GROUPS, SSD_HPG)[..., None] * x
    y = y.reshape(b, t, di) * jax.nn.silu(z)
    y = _rms(y.reshape(b, t, SSD_GROUPS, di // SSD_GROUPS), norm_w.reshape(SSD_GROUPS, di // SSD_GROUPS))
    return y.reshape(b * t, di), s_new.reshape(b, SSD_HEADS, SSD_HEAD_DIM, SSD_STATE), conv_new


def _pad_cols(w, n):
    return jnp.pad(w, ((0, 0), (0, n - w.shape[1])))


def kernel(x_prompt, x_sample, cache_nsa_kv, cache_nsa_win, state_hgrn, state_ssd, state_ssd_conv, page_table, c_prompt, c_sample, rel_bias, hgrn_lower_bounds, w_ada, b_ada, norm_gains, w_mlp_in, w_mlp_out, nsa_w_in, nsa_cmp_pos, nsa_cmp_w1, nsa_cmp_w2, nsa_w_out, hg_w_in, hg_norm, hg_w_out, ssd_w_in, ssd_conv_w, ssd_conv_b, ssd_dt_bias, ssd_a_log, ssd_d, ssd_norm, ssd_w_out):
    bf16 = jnp.bfloat16
    bp, tp, d = x_prompt.shape
    bs, ts, _ = x_sample.shape
    mp, ms = bp * tp, bs * ts
    lb_p = jax.nn.softmax(hgrn_lower_bounds, axis=0)
    lower_bounds = jnp.cumsum(lb_p, axis=0) - lb_p[0]

    mod = _ada_all(jnp.concatenate([c_prompt, c_sample], axis=0), w_ada, b_ada)
    mod = mod.reshape(DEPTH, bp + bs, ADA_CHUNKS, d)
    mod_p = mod[:, :bp].transpose(0, 2, 1, 3)[:, :, :, None, :]
    mod_s = mod[:, bp:].transpose(0, 2, 1, 3)[:, :, None, :, :]

    xp = x_prompt.reshape(mp, d)
    xs = x_sample.reshape(ms, d)
    tm_p, tm_s = PROMPT_ROW_TILE, ms
    nsa_tables = _nsa_prompt_tables(rel_bias, tp)

    kv_p, kv_s, win_p, win_s = [], [], [], []
    hg_p, hg_s, ssd_p, ssd_s, conv_p, conv_s = [], [], [], [], [], []
    for i in range(DEPTH):
        j = i // N_MIXERS
        kind = i % N_MIXERS
        g = norm_gains[i]
        shp_m, scp_m, gtp_m, shp_f, scp_f, gtp_f = [mod_p[i, c] for c in range(ADA_CHUNKS)]
        shs_m, scs_m, gts_m, shs_f, scs_f, gts_f = [mod_s[i, c] for c in range(ADA_CHUNKS)]
        if kind == 0:
            n_pad = NSA_PROJ_W
            w_in = _pad_cols(nsa_w_in[j], n_pad).astype(bf16)
            w_out = nsa_w_out[j].astype(bf16)
            pp = _norm_mod_matmul(xp, g[0], scp_m, shp_m, w_in, tp, tm_p).reshape(bp, tp, n_pad)
            ps = _norm_mod_matmul(xs, g[0], scs_m, shs_m, w_in, ts, tm_s).reshape(bs, ts, n_pad)
            ap, new_kv_p, new_win_p = _nsa_prompt_core(pp, nsa_cmp_pos[j], nsa_cmp_w1[j], nsa_cmp_w2[j], nsa_tables)
            as_, new_kv_s, new_win_s = _nsa_sample_core(ps, cache_nsa_kv[j], cache_nsa_win[j], page_table,
                                                        nsa_cmp_pos[j], nsa_cmp_w1[j], nsa_cmp_w2[j], rel_bias)
            kv_p.append(new_kv_p)
            kv_s.append(new_kv_s)
            win_p.append(new_win_p)
            win_s.append(new_win_s)
        elif kind == 1:
            w_in = hg_w_in[j].astype(bf16)
            w_out = hg_w_out[j].astype(bf16)
            pp = _norm_mod_matmul(xp, g[0], scp_m, shp_m, w_in, tp, tm_p).reshape(bp, tp, -1)
            ps = _norm_mod_matmul(xs, g[0], scs_m, shs_m, w_in, ts, tm_s).reshape(bs, ts, -1)
            s0 = jnp.zeros((bp, HG_HEADS, HG_DK, HG_DV), jnp.float32)
            ap, new_hp = _hgrn2_core(pp, s0, lower_bounds[i], hg_norm[j])
            as_, new_hs = _hgrn2_core(ps, state_hgrn[j], lower_bounds[i], hg_norm[j])
            hg_p.append(new_hp)
            hg_s.append(new_hs)
        else:
            n_pad = _round_up(SSD_IN_W, 7 * LANES)
            w_in = _pad_cols(ssd_w_in[j], n_pad).astype(bf16)
            w_out = ssd_w_out[j].astype(bf16)
            pp = _norm_mod_matmul(xp, g[0], scp_m, shp_m, w_in, tp, tm_p).reshape(bp, tp, n_pad)
            ps = _norm_mod_matmul(xs, g[0], scs_m, shs_m, w_in, ts, tm_s).reshape(bs, ts, n_pad)
            ssm0 = jnp.zeros((bp, SSD_HEADS, SSD_HEAD_DIM, SSD_STATE), jnp.float32)
            conv0 = jnp.zeros((bp, SSD_CONV_W - 1, SSD_CONV_DIM), jnp.float32)
            ap, new_sp, new_cp = _ssd_core(pp, ssm0, conv0, ssd_conv_w[j], ssd_conv_b[j], ssd_dt_bias[j],
                                           ssd_a_log[j], ssd_d[j], ssd_norm[j])
            as_, new_ss, new_cs = _ssd_core(ps, state_ssd[j], state_ssd_conv[j], ssd_conv_w[j], ssd_conv_b[j],
                                            ssd_dt_bias[j], ssd_a_log[j], ssd_d[j], ssd_norm[j])
            ssd_p.append(new_sp)
            ssd_s.append(new_ss)
            conv_p.append(new_cp)
            conv_s.append(new_cs)
        xp = _matmul_norm_res(ap, w_out, xp, g[1], gtp_m, tp, tm_p)
        xs = _matmul_norm_res(as_, w_out, xs, g[1], gts_m, ts, tm_s)
        w1 = w_mlp_in[i].astype(bf16)
        w2 = w_mlp_out[i].astype(bf16)
        xp = _mlp(xp, g[2], scp_f, shp_f, w1, w2, g[3], gtp_f, tp, tm_p)
        xs = _mlp(xs, g[2], scs_f, shs_f, w1, w2, g[3], gts_f, ts, tm_s)
    return (xp.reshape(bp, tp, d), xs.reshape(bs, ts, d),
            jnp.stack(kv_p), jnp.stack(kv_s), jnp.stack(win_p), jnp.stack(win_s),
            jnp.stack(hg_p), jnp.stack(hg_s), jnp.stack(ssd_p), jnp.stack(ssd_s),
            jnp.stack(conv_p), jnp.stack(conv_s))
```

```python
import functools
import math

import jax
import jax.numpy as jnp
import numpy as np
from jax import lax
from jax.experimental import pallas as pl
from jax.experimental.pallas import tpu as pltpu

D_MODEL = 1024
DEPTH = 4
PAGE_SIZE = 128
N_MIXERS = 3
ADA_CHUNKS = 6
NORM_EPS = 1e-6
D_FF = 4 * D_MODEL

NSA_HEADS = 16
NSA_HEAD_DIM = D_MODEL // NSA_HEADS
NSA_KV_HEADS = 4
NSA_GROUP = NSA_HEADS // NSA_KV_HEADS
CMP_STRIDE = 16
CMP_LEN = 2 * CMP_STRIDE
CMP_HIDDEN = 2 * NSA_HEAD_DIM
SEL_BLOCK = 64
SEL_TOP_N = 16
WINDOW = 512
WIN_Q_BLOCK = 128
SEL_Q_BLOCK = 16
NSA_Q_W = NSA_HEADS * NSA_HEAD_DIM
NSA_KV_W = NSA_KV_HEADS * NSA_HEAD_DIM
NSA_IN_W = NSA_Q_W + 6 * NSA_KV_W + 3 * NSA_HEADS

REL_BUCKETS = 32
REL_MAX_DIST = 128

HG_EXPAND = 128
HG_HEADS = D_MODEL // HG_EXPAND
HG_DK = HG_EXPAND
HG_DV = D_MODEL // HG_HEADS
HG_CHUNK = 64

SSD_D_INNER = 2 * D_MODEL
SSD_HEAD_DIM = 64
SSD_HEADS = SSD_D_INNER // SSD_HEAD_DIM
SSD_GROUPS = 8
SSD_HPG = SSD_HEADS // SSD_GROUPS
SSD_STATE = 128
SSD_CONV_W = 4
SSD_CONV_DIM = SSD_D_INNER + 2 * SSD_GROUPS * SSD_STATE
SSD_IN_W = SSD_D_INNER + SSD_CONV_DIM + SSD_HEADS
SSD_CHUNK = 128

NEG_INF = -1e30
FORCE_SCORE = 1e4

LANES = 128
VMEM_LIMIT_BYTES = 48 * 1024 * 1024
PROMPT_ROW_TILE = 512


def _round_up(n, m):
    return -(-n // m) * m


def _col_tile(n, cap=1536):
    best = LANES
    for t in range(LANES, cap + 1, LANES):
        if n % t == 0:
            best = t
    return best


def _rms(x, g):
    return x * lax.rsqrt(jnp.mean(x * x, axis=-1, keepdims=True) + NORM_EPS) * g


def _mod_spec(mod, rows_per_mod, tm, ngrid):
    r = mod.shape[1]
    if r == 1:
        per = rows_per_mod // tm
        if ngrid == 1:
            return pl.BlockSpec((None, 1, mod.shape[2]), lambda i: (i // per, 0, 0))
        return pl.BlockSpec((None, 1, mod.shape[2]), lambda i, j: (i // per, 0, 0))
    if ngrid == 1:
        return pl.BlockSpec((None, r, mod.shape[2]), lambda i: (0, 0, 0))
    return pl.BlockSpec((None, r, mod.shape[2]), lambda i, j: (0, 0, 0))


def _ada_kernel(c_ref, w_ref, b_ref, o_ref):
    c = c_ref[...]
    s = (c * jax.nn.sigmoid(c)).astype(jnp.bfloat16)
    o_ref[...] = jnp.dot(s, w_ref[...].astype(jnp.bfloat16),
                         preferred_element_type=jnp.float32) + b_ref[...]


def _ada_all(c_all, w_ada, b_ada):
    rows = c_all.shape[0]
    n = ADA_CHUNKS * D_MODEL
    tn = 1024
    return pl.pallas_call(
        _ada_kernel,
        grid=(DEPTH, n // tn),
        in_specs=[pl.BlockSpec((rows, D_MODEL), lambda l, j: (0, 0)),
                  pl.BlockSpec((None, D_MODEL, tn), lambda l, j: (l, 0, j)),
                  pl.BlockSpec((None, 1, tn), lambda l, j: (l, 0, j))],
        out_specs=pl.BlockSpec((None, rows, tn), lambda l, j: (l, 0, j)),
        out_shape=jax.ShapeDtypeStruct((DEPTH, rows, n), jnp.float32),
        compiler_params=pltpu.CompilerParams(
            dimension_semantics=("parallel", "parallel"), vmem_limit_bytes=VMEM_LIMIT_BYTES),
        name="ada",
    )(c_all, w_ada, b_ada.reshape(DEPTH, 1, n))


def _norm_mod_matmul_kernel(x_ref, g_ref, sc_ref, sh_ref, w_ref, o_ref, h_ref):
    @pl.when(pl.program_id(1) == 0)
    def _():
        h = _rms(x_ref[...], g_ref[...]) * (1.0 + sc_ref[...]) + sh_ref[...]
        h_ref[...] = h.astype(jnp.bfloat16)

    o_ref[...] = jnp.dot(h_ref[...], w_ref[...], preferred_element_type=jnp.float32)


def _norm_mod_matmul(x, g, sc, sh, w, rows_per_mod, tm):
    m, d = x.shape
    n = w.shape[1]
    tn = _col_tile(n)
    return pl.pallas_call(
        _norm_mod_matmul_kernel,
        grid=(m // tm, n // tn),
        in_specs=[pl.BlockSpec((tm, d), lambda i, j: (i, 0)),
                  pl.BlockSpec((1, d), lambda i, j: (0, 0)),
                  _mod_spec(sc, rows_per_mod, tm, 2),
                  _mod_spec(sh, rows_per_mod, tm, 2),
                  pl.BlockSpec((d, tn), lambda i, j: (0, j))],
        out_specs=pl.BlockSpec((tm, tn), lambda i, j: (i, j)),
        out_shape=jax.ShapeDtypeStruct((m, n), jnp.float32),
        scratch_shapes=[pltpu.VMEM((tm, d), jnp.bfloat16)],
        compiler_params=pltpu.CompilerParams(
            dimension_semantics=("parallel", "arbitrary"), vmem_limit_bytes=VMEM_LIMIT_BYTES),
        name="norm_mod_matmul",
    )(x, g.reshape(1, d), sc, sh, w)


def _matmul_norm_res_kernel(a_ref, w_ref, x_ref, g_ref, gt_ref, o_ref):
    y = jnp.dot(a_ref[...].astype(jnp.bfloat16), w_ref[...], preferred_element_type=jnp.float32)
    o_ref[...] = x_ref[...] + gt_ref[...] * _rms(y, g_ref[...])


def _matmul_norm_res(a, w, x, g, gate, rows_per_mod, tm):
    m, k = a.shape
    d = w.shape[1]
    return pl.pallas_call(
        _matmul_norm_res_kernel,
        grid=(m // tm,),
        in_specs=[pl.BlockSpec((tm, k), lambda i: (i, 0)),
                  pl.BlockSpec((k, d), lambda i: (0, 0)),
                  pl.BlockSpec((tm, d), lambda i: (i, 0)),
                  pl.BlockSpec((1, d), lambda i: (0, 0)),
                  _mod_spec(gate, rows_per_mod, tm, 1)],
        out_specs=pl.BlockSpec((tm, d), lambda i: (i, 0)),
        out_shape=jax.ShapeDtypeStruct((m, d), jnp.float32),
        compiler_params=pltpu.CompilerParams(
            dimension_semantics=("parallel",), vmem_limit_bytes=VMEM_LIMIT_BYTES),
        name="matmul_norm_res",
    )(a, w, x, g.reshape(1, d), gate)


def _mlp_kernel(x_ref, g2_ref, sc_ref, sh_ref, w1_ref, w2_ref, g3_ref, gt_ref, o_ref, h_ref, acc_ref):
    j = pl.program_id(1)

    @pl.when(j == 0)
    def _():
        h = _rms(x_ref[...], g2_ref[...]) * (1.0 + sc_ref[...]) + sh_ref[...]
        h_ref[...] = h.astype(jnp.bfloat16)

    u = jnp.dot(h_ref[...], w1_ref[...], preferred_element_type=jnp.float32)
    u = jnp.square(jnp.maximum(u, 0.0)).astype(jnp.bfloat16)
    part = jnp.dot(u, w2_ref[...], preferred_element_type=jnp.float32)

    @pl.when(j == 0)
    def _():
        acc_ref[...] = part

    @pl.when(j > 0)
    def _():
        acc_ref[...] += part

    @pl.when(j == pl.num_programs(1) - 1)
    def _():
        o_ref[...] = x_ref[...] + gt_ref[...] * _rms(acc_ref[...], g3_ref[...])


def _mlp(x, g2, sc, sh, w1, w2, g3, gate, rows_per_mod, tm):
    m, d = x.shape
    f = w1.shape[1]
    tf = 1024
    return pl.pallas_call(
        _mlp_kernel,
        grid=(m // tm, f // tf),
        in_specs=[pl.BlockSpec((tm, d), lambda i, j: (i, 0)),
                  pl.BlockSpec((1, d), lambda i, j: (0, 0)),
                  _mod_spec(sc, rows_per_mod, tm, 2),
                  _mod_spec(sh, rows_per_mod, tm, 2),
                  pl.BlockSpec((d, tf), lambda i, j: (0, j)),
                  pl.BlockSpec((tf, d), lambda i, j: (j, 0)),
                  pl.BlockSpec((1, d), lambda i, j: (0, 0)),
                  _mod_spec(gate, rows_per_mod, tm, 2)],
        out_specs=pl.BlockSpec((tm, d), lambda i, j: (i, 0)),
        out_shape=jax.ShapeDtypeStruct((m, d), jnp.float32),
        scratch_shapes=[pltpu.VMEM((tm, d), jnp.bfloat16), pltpu.VMEM((tm, d), jnp.float32)],
        compiler_params=pltpu.CompilerParams(
            dimension_semantics=("parallel", "arbitrary"), vmem_limit_bytes=VMEM_LIMIT_BYTES),
        name="mlp",
    )(x, g2.reshape(1, d), sc, sh, w1, w2, g3.reshape(1, d), gate)


NSA_COL_BLOCK = NSA_KV_W
NSA_PROJ_W = 11 * NSA_COL_BLOCK
NSA_GATE_BLOCK = (NSA_Q_W + 6 * NSA_KV_W) // NSA_COL_BLOCK
ATT_TILE = 128
ROWS = NSA_GROUP * ATT_TILE


def _dot_nt(a, b):
    return lax.dot_general(a, b, (((1,), (1,)), ((), ())), preferred_element_type=jnp.float32)


def _dot_tn(a, b):
    return lax.dot_general(a, b, (((0,), (0,)), ((), ())), preferred_element_type=jnp.float32)


def _gelu_tanh(x):
    return 0.5 * x * (1.0 + jnp.tanh(math.sqrt(2.0 / math.pi) * (x + 0.044715 * (x * x * x))))


def _split3_bf16(x):
    hi = x.astype(jnp.bfloat16)
    r1 = x - hi.astype(jnp.float32)
    mid = r1.astype(jnp.bfloat16)
    lo = (r1 - mid.astype(jnp.float32)).astype(jnp.bfloat16)
    return hi, mid, lo


def _nsa_compress_kernel(x0_ref, x1_ref, x2_ref, x3_ref, w1_ref, w1f_ref, pos_ref, w2_ref, o_ref):
    n = x0_ref.shape[0] // CMP_STRIDE
    hd = NSA_HEAD_DIM
    x_refs = ((x0_ref, x1_ref), (x2_ref, x3_ref))
    for r in range(2):
        pos_b = jnp.dot(pos_ref[r].astype(jnp.bfloat16), w1f_ref[r], preferred_element_type=jnp.float32)
        acc = [jnp.zeros((n, 2 * CMP_HIDDEN), jnp.float32) for _ in range(NSA_KV_HEADS)]
        for l in range(CMP_STRIDE):
            w = w1_ref[r, l]
            for pair in range(2):
                xl = x_refs[r][pair][pl.ds(l, n, stride=CMP_STRIDE), :].astype(jnp.bfloat16)
                for half in range(2):
                    k = 2 * pair + half
                    acc[k] = acc[k] + jnp.dot(xl[:, half * hd:(half + 1) * hd], w,
                                              preferred_element_type=jnp.float32)
        for k in range(NSA_KV_HEADS):
            pa = acc[k][:, :CMP_HIDDEN]
            pb_next = pltpu.roll(acc[k][:, CMP_HIDDEN:], n - 1, 0)
            hid = _gelu_tanh(pa + pb_next + pos_b)
            out = jnp.dot(hid.astype(jnp.bfloat16), w2_ref[r], preferred_element_type=jnp.float32)
            o_ref[r, :, k * hd:(k + 1) * hd] = out.astype(o_ref.dtype)


def _compress_weights(cmp_pos, w1, w2):
    bf16 = jnp.bfloat16
    w1r = w1.reshape(2, CMP_LEN, NSA_HEAD_DIM, CMP_HIDDEN)
    w1cat = jnp.concatenate([w1r[:, :CMP_STRIDE], w1r[:, CMP_STRIDE:]], axis=-1).astype(bf16)
    return w1cat, w1.astype(bf16), cmp_pos.reshape(2, 1, CMP_LEN * NSA_HEAD_DIM), w2.astype(bf16)


def _nsa_compress_prompt(proj, cmp_pos, w1, w2):
    b, t, _ = proj.shape
    n = t // CMP_STRIDE
    w1cat, w1f, pos, w2b = _compress_weights(cmp_pos, w1, w2)
    return pl.pallas_call(
        _nsa_compress_kernel,
        grid=(b,),
        in_specs=[pl.BlockSpec((None, t, LANES), lambda i, c=c: (i, 0, NSA_Q_W // LANES + c)) for c in range(4)]
                 + [pl.BlockSpec(w1cat.shape, lambda i: (0, 0, 0, 0)),
                  pl.BlockSpec(w1f.shape, lambda i: (0, 0, 0)),
                  pl.BlockSpec(pos.shape, lambda i: (0, 0, 0)),
                  pl.BlockSpec(w2b.shape, lambda i: (0, 0, 0))],
        out_specs=pl.BlockSpec((None, 2, n, NSA_KV_W), lambda i: (i, 0, 0, 0)),
        out_shape=jax.ShapeDtypeStruct((b, 2, n, NSA_KV_W), jnp.bfloat16),
        compiler_params=pltpu.CompilerParams(
            dimension_semantics=("parallel",), vmem_limit_bytes=VMEM_LIMIT_BYTES),
        name="nsa_compress",
    )(proj, proj, proj, proj, w1cat, w1f, pos, w2b)


def _softmax_step(s, v, m, l, acc):
    m_new = jnp.maximum(m, jnp.max(s, axis=-1, keepdims=True))
    alpha = jnp.exp(m - m_new)
    e = jnp.exp(s - m_new)
    l = alpha * l + jnp.sum(e, axis=-1, keepdims=True)
    acc = alpha * acc + jnp.dot(e.astype(jnp.bfloat16), v, preferred_element_type=jnp.float32)
    return m_new, l, acc


def _nsa_attn_kernel(q_ref, g_ref, c_ref, ks_ref, vs_ref, kw_ref, vw_ref, bc_ref, bt_ref,
                     mimp_ref, esel_ref, eg_ref, o_ref, oc_s, os_s, ow_s, selk_s):
    i = pl.program_id(1)
    hd = NSA_HEAD_DIM
    tq = ATT_TILE
    bf16 = jnp.bfloat16
    n_cmp_pad = c_ref.shape[1]
    n_sel = mimp_ref.shape[0]
    qi = lax.broadcasted_iota(jnp.int32, (tq, tq), 0)
    kj = lax.broadcasted_iota(jnp.int32, (tq, tq), 1)
    causal = (kj <= qi)[None]
    win_edge = (kj >= qi)[None]
    q_pos_c = i * tq + lax.broadcasted_iota(jnp.int32, (tq, n_cmp_pad), 0)
    cmp_end = CMP_STRIDE * lax.broadcasted_iota(jnp.int32, (tq, n_cmp_pad), 1) + (CMP_LEN - 1)
    mask_c = (cmp_end <= q_pos_c)[None]
    blk = lax.broadcasted_iota(jnp.int32, (n_sel, tq), 0)
    cur = (i * tq + lax.broadcasted_iota(jnp.int32, (n_sel, tq), 1)) // SEL_BLOCK
    forced = (blk == 0) | (blk == cur) | (blk == cur - 1)
    valid = blk <= cur

    def masked(s, mask3, fill=NEG_INF):
        return jnp.where(mask3, s.reshape(NSA_GROUP, tq, -1), fill).reshape(s.shape)

    for k in range(NSA_KV_HEADS):
        lanes = slice(k * hd, (k + 1) * hd)
        qk = jnp.concatenate(
            [q_ref[:, (k * NSA_GROUP + g) * hd:(k * NSA_GROUP + g + 1) * hd] for g in range(NSA_GROUP)], axis=0)
        qk = (qk * (hd ** -0.5)).astype(bf16)

        s = masked(_dot_nt(qk, c_ref[0, :, lanes]) + bc_ref[k], mask_c)
        m = jnp.max(s, axis=-1, keepdims=True)
        e = masked(jnp.exp(s - m), mask_c, 0.0)
        p = e / jnp.maximum(jnp.sum(e, axis=-1, keepdims=True), 1e-30)
        o_cmp = jnp.dot(p.astype(bf16), c_ref[1, :, lanes], preferred_element_type=jnp.float32)
        p_sum = p.reshape(NSA_GROUP, tq, n_cmp_pad).sum(axis=0)
        imp = sum(_dot_nt(mimp_ref[...], part) for part in _split3_bf16(p_sum))
        score = jnp.where(valid, jnp.where(forced, FORCE_SCORE, imp), NEG_INF)
        rank = jnp.zeros((n_sel, tq), jnp.float32)
        for j in range(n_sel):
            row = score[j:j + 1, :]
            beats = (row > score) | ((row == score) & (blk > j))
            rank = rank + jnp.where(beats, 1.0, 0.0)
        sel = jnp.where((rank < SEL_TOP_N) & (score > 0.5 * NEG_INF), 1.0, 0.0)
        selk_s[...] = _dot_tn(sel.astype(bf16), esel_ref[...])

        def sel_chunk(c, carry, bias, extra):
            rows = pl.ds(pl.multiple_of(c * tq, tq), tq)
            kk = ks_ref[rows, lanes].astype(bf16)
            vv = vs_ref[rows, lanes].astype(bf16)
            s = _dot_nt(qk, kk)
            if bias is not None:
                s = s + bias
            mk = (selk_s[:, rows] > 0.5)[None]
            if extra is not None:
                mk = mk & extra
            return _softmax_step(masked(s, mk), vv, *carry)

        init = (jnp.full((ROWS, 1), NEG_INF, jnp.float32), jnp.zeros((ROWS, 1), jnp.float32),
                jnp.zeros((ROWS, hd), jnp.float32))
        carry = sel_chunk(i, init, bt_ref[k, 0], causal)
        prev_ok = jnp.broadcast_to(i >= 1, (1, tq, tq))
        carry = sel_chunk(jnp.maximum(i - 1, 0), carry, bt_ref[k, 1], prev_ok)
        carry = lax.fori_loop(0, jnp.maximum(i - 1, 0), lambda c, cr: sel_chunk(c, cr, None, None), carry)
        o_sel = carry[2] / carry[1]

        def win_chunk(back, carry, bias, extra):
            c = i - back
            ok = jnp.broadcast_to(c >= 0, (1, tq, tq))
            rows = pl.ds(pl.multiple_of(jnp.maximum(c, 0) * tq, tq), tq)
            kk = kw_ref[rows, lanes].astype(bf16)
            vv = vw_ref[rows, lanes].astype(bf16)
            s = _dot_nt(qk, kk)
            if bias is not None:
                s = s + bias
            mk = ok if extra is None else ok & extra
            return _softmax_step(masked(s, mk), vv, *carry)

        carry = win_chunk(0, init, bt_ref[k, 0], causal)
        carry = win_chunk(1, carry, bt_ref[k, 1], None)
        for back in range(2, WINDOW // tq):
            carry = win_chunk(back, carry, None, None)
        carry = win_chunk(WINDOW // tq, carry, None, win_edge)
        o_win = carry[2] / carry[1]

        for g in range(NSA_GROUP):
            cols = slice((k * NSA_GROUP + g) * hd, (k * NSA_GROUP + g + 1) * hd)
            oc_s[:, cols] = o_cmp[g * tq:(g + 1) * tq]
            os_s[:, cols] = o_sel[g * tq:(g + 1) * tq]
            ow_s[:, cols] = o_win[g * tq:(g + 1) * tq]

    gate = jax.nn.sigmoid(g_ref[...])
    g_hi = gate.astype(bf16)
    g_lo = (gate - g_hi.astype(jnp.float32)).astype(bf16)
    out = jnp.zeros(o_ref.shape, jnp.float32)
    for br, o_s in enumerate((oc_s, os_s, ow_s)):
        g_full = (jnp.dot(g_hi, eg_ref[br], preferred_element_type=jnp.float32)
                  + jnp.dot(g_lo, eg_ref[br], preferred_element_type=jnp.float32))
        out = out + g_full * o_s[...]
    o_ref[...] = out.astype(o_ref.dtype)


def _bias_lookup(rel_bias, dist):
    onehot = jax.nn.one_hot(_rel_bucket(dist), REL_BUCKETS, dtype=jnp.float32)
    return jnp.einsum('...c,ch->...h', onehot, rel_bias, precision=lax.Precision.HIGHEST)


def _rows_by_kv_head(tab):
    *lead, q, t, _ = tab.shape
    tab = tab.reshape(*lead, q, t, NSA_KV_HEADS, NSA_GROUP)
    nl = len(lead)
    tab = jnp.transpose(tab, (*range(nl), nl + 2, nl + 3, nl, nl + 1))
    return tab.reshape(*lead, NSA_KV_HEADS, NSA_GROUP * q, t)


def _nsa_prompt_tables(rel_bias, t):
    tq = ATT_TILE
    n_chunks = t // CMP_STRIDE
    n_sel = t // SEL_BLOCK
    far = rel_bias[REL_BUCKETS - 1]
    ar = jnp.arange(tq)
    d_tile = (jnp.arange(2) * tq)[:, None, None] + ar[None, :, None] - ar[None, None, :]
    bt = _rows_by_kv_head(_bias_lookup(rel_bias, d_tile) - far)
    bt = jnp.transpose(bt, (1, 0, 2, 3))
    q_pos = jnp.arange(t).reshape(t // tq, tq)
    cmp_end = jnp.arange(n_chunks) * CMP_STRIDE + CMP_LEN - 1
    bc = _rows_by_kv_head(_bias_lookup(rel_bias, q_pos[:, :, None] - cmp_end[None, None, :]) - far)
    n_idx = np.arange(n_chunks)
    j_idx = np.arange(n_sel)[:, None]
    per = SEL_BLOCK // CMP_STRIDE
    mimp = 0.5 * ((n_idx // per == j_idx).astype(np.float32) + ((n_idx + 1) // per == j_idx).astype(np.float32))
    mimp[:, n_chunks - 1] = 0.0
    esel = (np.arange(t)[None, :] // SEL_BLOCK == np.arange(n_sel)[:, None]).astype(np.float32)
    col = np.arange(NSA_Q_W) // NSA_HEAD_DIM
    eg = np.zeros((3, NSA_COL_BLOCK, NSA_Q_W), np.float32)
    for br in range(3):
        eg[br, br * NSA_HEADS + col, np.arange(NSA_Q_W)] = 1.0
    return bt, bc, jnp.asarray(mimp, jnp.bfloat16), jnp.asarray(esel, jnp.bfloat16), jnp.asarray(eg, jnp.bfloat16)


def _nsa_attn_prompt(proj, cmp, tables):
    b, t, _ = proj.shape
    bt, bc, mimp, esel, eg = tables
    tq = ATT_TILE
    cb = NSA_COL_BLOCK
    first_kv = NSA_Q_W // cb

    def kv_spec(slab):
        return pl.BlockSpec((None, t, cb), lambda bi, i: (bi, 0, first_kv + slab))

    def const_spec(a):
        nd = a.ndim
        return pl.BlockSpec(a.shape, lambda bi, i: (0,) * nd)

    return pl.pallas_call(
        _nsa_attn_kernel,
        grid=(b, t // tq),
        in_specs=[pl.BlockSpec((None, tq, NSA_Q_W), lambda bi, i: (bi, i, 0)),
                  pl.BlockSpec((None, tq, cb), lambda bi, i: (bi, i, NSA_GATE_BLOCK)),
                  pl.BlockSpec((None,) + cmp.shape[1:], lambda bi, i: (bi, 0, 0, 0)),
                  kv_spec(2), kv_spec(3), kv_spec(4), kv_spec(5),
                  pl.BlockSpec((None,) + bc.shape[1:], lambda bi, i: (i, 0, 0, 0)),
                  const_spec(bt), const_spec(mimp), const_spec(esel), const_spec(eg)],
        out_specs=pl.BlockSpec((None, tq, NSA_Q_W), lambda bi, i: (bi, i, 0)),
        out_shape=jax.ShapeDtypeStruct((b, t, NSA_Q_W), jnp.bfloat16),
        scratch_shapes=[pltpu.VMEM((tq, NSA_Q_W), jnp.float32)] * 3 + [pltpu.VMEM((tq, t), jnp.float32)],
        compiler_params=pltpu.CompilerParams(
            dimension_semantics=("parallel", "arbitrary"), vmem_limit_bytes=VMEM_LIMIT_BYTES),
        name="nsa_attn",
    )(proj, proj, cmp, proj, proj, proj, proj, bc, bt, mimp, esel, eg).reshape(b * t, NSA_Q_W)


PAGE_GROUP = 4
SEL_PAD = 8


def _col_softmax_step(kk, vv, qbd, bias, mask, state):
    m, l, acc = state
    s = jnp.dot(kk, qbd, preferred_element_type=jnp.float32)
    if bias is not None:
        s = s + bias
    s = jnp.where(mask, s, NEG_INF)
    m_new = jnp.maximum(m, jnp.max(s, axis=0, keepdims=True))
    alpha = jnp.exp(m - m_new)
    e = jnp.where(mask, jnp.exp(s - m_new), 0.0)
    l = alpha * l + jnp.sum(e, axis=0, keepdims=True)
    acc = alpha * acc + _dot_tn(vv, e.astype(jnp.bfloat16))
    return m_new, l, acc


def _nsa_sample_kernel(pt_ref, *refs):
    n_cmp_in = 4 * PAGE_GROUP
    n_sel_in = 2 * PAGE_GROUP
    cmp_pages = refs[:n_cmp_in]
    sel_pages = refs[n_cmp_in:n_cmp_in + n_sel_in]
    (qbd_ref, new_ref, win_ref, w1_ref, w1f_ref, pos_ref, w2_ref, bcmp_ref, bsel_ref, bwin_ref,
     mimp_ref, hk_ref, bd_ref, rep_ref, gsel_ref, o_ref,
     a_s, kc_s, vc_s, score_s, selh_s, kw_s, vw_s, kn_s, vn_s, ocmp_s, m_s, l_s, acc_s) = refs[n_cmp_in + n_sel_in:]
    del pt_ref
    ph = pl.program_id(1)
    g = pl.program_id(2)
    n_groups = pl.num_programs(2)
    bf16 = jnp.bfloat16
    hd = NSA_HEAD_DIM
    n_chunks = a_s.shape[2]
    n_sel = mimp_ref.shape[1]
    per_page = PAGE_SIZE // CMP_STRIDE
    qbd = qbd_ref[...]

    def heads_out(acc_t, l):
        o = (acc_t / l) * bd_ref[...]
        hi = o.astype(bf16)
        lo = (o - hi.astype(jnp.float32)).astype(bf16)
        return _dot_tn(hi, rep_ref[...]) + _dot_tn(lo, rep_ref[...])

    @pl.when(ph == 0)
    def _():
        for u in range(PAGE_GROUP):
            row0 = pl.multiple_of((g * PAGE_GROUP + u) * per_page, per_page)
            for c in range(4):
                for l in range(CMP_STRIDE):
                    a_s[c, l, pl.ds(row0, per_page), :] = cmp_pages[u * 4 + c][pl.ds(l, per_page, stride=CMP_STRIDE), :]

    @pl.when((ph == 0) & (g == n_groups - 1))
    def _():
        for r in range(2):
            pos_b = jnp.dot(pos_ref[r].astype(bf16), w1f_ref[r], preferred_element_type=jnp.float32)
            acc = [jnp.zeros((n_chunks, 2 * CMP_HIDDEN), jnp.float32) for _ in range(NSA_KV_HEADS)]
            for l in range(CMP_STRIDE):
                w = w1_ref[r, l]
                for pair in range(2):
                    xl = a_s[2 * r + pair, l].astype(bf16)
                    for half in range(2):
                        k = 2 * pair + half
                        acc[k] = acc[k] + jnp.dot(xl[:, half * hd:(half + 1) * hd], w,
                                                  preferred_element_type=jnp.float32)
            dst = kc_s if r == 0 else vc_s
            for k in range(NSA_KV_HEADS):
                pa = acc[k][:, :CMP_HIDDEN]
                pb_next = pltpu.roll(acc[k][:, CMP_HIDDEN:], n_chunks - 1, 0)
                hid = _gelu_tanh(pa + pb_next + pos_b)
                out = jnp.dot(hid.astype(bf16), w2_ref[r], preferred_element_type=jnp.float32)
                dst[:, k * hd:(k + 1) * hd] = out.astype(bf16)
        rows = lax.broadcasted_iota(jnp.int32, (n_chunks, NSA_HEADS), 0)
        mask_c = rows <= n_chunks - 2
        s = jnp.dot(kc_s[...], qbd, preferred_element_type=jnp.float32) + bcmp_ref[...]
        s = jnp.where(mask_c, s, NEG_INF)
        m = jnp.max(s, axis=0, keepdims=True)
        e = jnp.where(mask_c, jnp.exp(s - m), 0.0)
        l = jnp.maximum(jnp.sum(e, axis=0, keepdims=True), 1e-30)
        p = e / l
        ocmp_s[...] = heads_out(_dot_tn(vc_s[...], p.astype(bf16)), jnp.ones_like(l))
        p_kv = sum(jnp.dot(part, hk_ref[...], preferred_element_type=jnp.float32) for part in _split3_bf16(p))
        imp = sum(jnp.dot(mimp_ref[...], part, preferred_element_type=jnp.float32) for part in _split3_bf16(p_kv))
        blk = lax.broadcasted_iota(jnp.int32, imp.shape, 0)
        cur = n_chunks * CMP_STRIDE // SEL_BLOCK
        forced = (blk == 0) | (blk == cur) | (blk == cur - 1)
        score = jnp.where(blk <= cur, jnp.where(forced, FORCE_SCORE, imp), NEG_INF)
        score_s[...] = score

        def rank_body(j, rank):
            row = score_s[pl.ds(j, 1), :]
            beats = (row > score) | ((row == score) & (blk > j))
            return rank + jnp.where(beats, 1.0, 0.0)

        rank = lax.fori_loop(0, cur + 1, rank_body, jnp.zeros(imp.shape, jnp.float32))
        sel = jnp.where((rank < SEL_TOP_N) & (score > 0.5 * NEG_INF), 1.0, 0.0)
        selh_s[...] = jnp.dot(sel.astype(bf16), gsel_ref[...], preferred_element_type=jnp.float32)
        m_s[...] = jnp.full(m_s.shape, NEG_INF, jnp.float32)
        l_s[...] = jnp.zeros(l_s.shape, jnp.float32)
        acc_s[...] = jnp.zeros(acc_s.shape, jnp.float32)

    @pl.when(ph == 1)
    def _():
        state = (m_s[...], l_s[...], acc_s[...])
        half = lax.broadcasted_iota(jnp.int32, (PAGE_SIZE, NSA_HEADS), 0) < SEL_BLOCK
        for u in range(PAGE_GROUP):
            page = g * PAGE_GROUP + u
            pair = selh_s[pl.ds(page * (PAGE_SIZE // SEL_BLOCK), PAGE_SIZE // SEL_BLOCK), :]
            mask = jnp.where(half, pair[0:1, :], pair[1:2, :]) > 0.5
            near = jnp.where(page == n_groups * PAGE_GROUP - 1, 1.0, 0.0)
            state = _col_softmax_step(sel_pages[2 * u][...].astype(bf16), sel_pages[2 * u + 1][...].astype(bf16),
                                      qbd, near * bsel_ref[0:PAGE_SIZE, :], mask, state)
        m_s[...], l_s[...], acc_s[...] = state

    @pl.when((ph == 1) & (g == n_groups - 1))
    def _():
        kn_s[...] = jnp.zeros(kn_s.shape, bf16)
        vn_s[...] = jnp.zeros(vn_s.shape, bf16)
        kv0 = NSA_Q_W
        kn_s[0:1, :] = new_ref[:, kv0 + 2 * NSA_KV_W:kv0 + 3 * NSA_KV_W].astype(bf16)
        vn_s[0:1, :] = new_ref[:, kv0 + 3 * NSA_KV_W:kv0 + 4 * NSA_KV_W].astype(bf16)
        first = lax.broadcasted_iota(jnp.int32, (SEL_PAD, NSA_HEADS), 0) < 1
        state = _col_softmax_step(kn_s[...], vn_s[...], qbd, bsel_ref[PAGE_SIZE:PAGE_SIZE + SEL_PAD, :], first,
                                  (m_s[...], l_s[...], acc_s[...]))
        o_sel = heads_out(state[2], state[1])
        n_buf = win_ref.shape[0]
        kw_s[...] = jnp.zeros(kw_s.shape, bf16)
        vw_s[...] = jnp.zeros(vw_s.shape, bf16)
        kw_s[0:n_buf, :] = win_ref[:, 0:NSA_KV_W].astype(bf16)
        vw_s[0:n_buf, :] = win_ref[:, NSA_KV_W:2 * NSA_KV_W].astype(bf16)
        kw_s[n_buf:n_buf + 1, :] = new_ref[:, kv0 + 4 * NSA_KV_W:kv0 + 5 * NSA_KV_W].astype(bf16)
        vw_s[n_buf:n_buf + 1, :] = new_ref[:, kv0 + 5 * NSA_KV_W:kv0 + 6 * NSA_KV_W].astype(bf16)
        mask_w = lax.broadcasted_iota(jnp.int32, (kw_s.shape[0], NSA_HEADS), 0) <= n_buf
        init = (jnp.full((1, NSA_HEADS), NEG_INF, jnp.float32), jnp.zeros((1, NSA_HEADS), jnp.float32),
                jnp.zeros((NSA_KV_W, NSA_HEADS), jnp.float32))
        state = _col_softmax_step(kw_s[...], vw_s[...], qbd, bwin_ref[...], mask_w, init)
        o_win = heads_out(state[2], state[1])
        gate = jax.nn.sigmoid(new_ref[:, kv0 + 6 * NSA_KV_W:kv0 + 7 * NSA_KV_W])
        out = jnp.zeros(o_ref.shape, jnp.float32)
        for br, o_b in enumerate((ocmp_s[...], o_sel, o_win)):
            onehot = lax.broadcasted_iota(jnp.int32, (NSA_HEADS, NSA_COL_BLOCK), 1) == (
                lax.broadcasted_iota(jnp.int32, (NSA_HEADS, NSA_COL_BLOCK), 0) + br * NSA_HEADS)
            g_col = jnp.sum(jnp.where(onehot, gate, 0.0), axis=1, keepdims=True)
            out = out + g_col * o_b
        o_ref[...] = out


def _nsa_sample_tables(rel_bias, past_len, n_buf):
    far = rel_bias[REL_BUCKETS - 1]
    n_chunks = past_len // CMP_STRIDE
    n_sel = past_len // SEL_BLOCK + 1
    n_sel_pad = _round_up(n_sel, 8)
    cmp_end = jnp.arange(n_chunks) * CMP_STRIDE + CMP_LEN - 1
    bcmp = _bias_lookup(rel_bias, past_len - cmp_end) - far
    k_last = past_len - PAGE_SIZE + jnp.arange(PAGE_SIZE + SEL_PAD)
    bsel = _bias_lookup(rel_bias, past_len - k_last) - far
    n_win = _round_up(n_buf + 1, 8)
    bwin = _bias_lookup(rel_bias, n_buf - jnp.arange(n_win)) - far
    per = SEL_BLOCK // CMP_STRIDE
    n_idx = np.arange(n_chunks)
    j_idx = np.arange(n_sel_pad)[:, None]
    mimp = 0.5 * ((n_idx // per == j_idx).astype(np.float32) + ((n_idx + 1) // per == j_idx).astype(np.float32))
    mimp[:, n_chunks - 1] = 0.0
    heads = np.arange(NSA_HEADS)
    hk = (heads[:, None] // NSA_GROUP == np.arange(NSA_KV_HEADS)[None, :]).astype(np.float32)
    rowk = np.arange(NSA_KV_W) // NSA_HEAD_DIM
    bd = (rowk[:, None] == heads[None, :] // NSA_GROUP).astype(np.float32)
    rep = (np.arange(NSA_KV_W)[:, None] % NSA_HEAD_DIM == np.arange(NSA_HEAD_DIM)[None, :]).astype(np.float32)
    bf16 = jnp.bfloat16
    return (bcmp, bsel, bwin, jnp.asarray(mimp, bf16), jnp.asarray(hk, bf16), jnp.asarray(bd, jnp.float32),
            jnp.asarray(rep, bf16), jnp.asarray(hk.T, bf16))


def _nsa_attn_sample(proj, kv_pages, win_buf, page_table, cmp_pos, w1, w2, tables):
    b = proj.shape[0]
    n_pages = page_table.shape[1]
    past_len = n_pages * PAGE_SIZE
    n_buf = win_buf.shape[1]
    n_chunks = past_len // CMP_STRIDE
    n_groups = n_pages // PAGE_GROUP
    bcmp, bsel, bwin, mimp, hk, bd, rep, gsel = tables
    n_sel_pad = mimp.shape[0]
    n_win = bwin.shape[0]
    w1cat, w1f, pos, w2b = _compress_weights(cmp_pos, w1, w2)
    pages = kv_pages.reshape(kv_pages.shape[0], PAGE_SIZE, 4 * NSA_KV_W)
    q = proj[:, :NSA_Q_W].reshape(b, NSA_KV_HEADS, NSA_GROUP, NSA_HEAD_DIM) * (NSA_HEAD_DIM ** -0.5)
    eye = jnp.eye(NSA_KV_HEADS, dtype=q.dtype)
    qbd = jnp.einsum('bkgd,kc->bkdcg', q, eye).reshape(b, NSA_KV_W, NSA_HEADS).astype(jnp.bfloat16)
    proj3 = proj.reshape(b, 1, proj.shape[1])
    win = win_buf.reshape(b, n_buf, 2 * NSA_KV_W)

    def cmp_spec(u, c):
        def imap(i, ph, g, pt):
            gg = jnp.where(ph == 0, g, n_groups - 1)
            return (pt[i, gg * PAGE_GROUP + u], 0, c)
        return pl.BlockSpec((None, PAGE_SIZE, LANES), imap)

    def sel_spec(u, slab):
        def imap(i, ph, g, pt):
            gg = jnp.where(ph == 1, g, 0)
            return (pt[i, gg * PAGE_GROUP + u], 0, slab)
        return pl.BlockSpec((None, PAGE_SIZE, NSA_KV_W), imap)

    def const_spec(a):
        nd = a.ndim
        return pl.BlockSpec(a.shape, lambda i, ph, g, pt: (0,) * nd)

    in_specs = ([cmp_spec(u, c) for u in range(PAGE_GROUP) for c in range(4)]
                + [sel_spec(u, slab) for u in range(PAGE_GROUP) for slab in (2, 3)]
                + [pl.BlockSpec((None, NSA_KV_W, NSA_HEADS), lambda i, ph, g, pt: (i, 0, 0)),
                   pl.BlockSpec((None, 1, proj.shape[1]), lambda i, ph, g, pt: (i, 0, 0)),
                   pl.BlockSpec((None, n_buf, 2 * NSA_KV_W), lambda i, ph, g, pt: (i, 0, 0))]
                + [const_spec(a) for a in (w1cat, w1f, pos, w2b, bcmp, bsel, bwin, mimp, hk, bd, rep, gsel)])
    f32, bf16 = jnp.float32, jnp.bfloat16
    out = pl.pallas_call(
        _nsa_sample_kernel,
        grid_spec=pltpu.PrefetchScalarGridSpec(
            num_scalar_prefetch=1,
            grid=(b, 2, n_groups),
            in_specs=in_specs,
            out_specs=pl.BlockSpec((None, NSA_HEADS, NSA_HEAD_DIM), lambda i, ph, g, pt: (i, 0, 0)),
            scratch_shapes=[pltpu.VMEM((4, CMP_STRIDE, n_chunks, LANES), f32),
                            pltpu.VMEM((n_chunks, NSA_KV_W), bf16), pltpu.VMEM((n_chunks, NSA_KV_W), bf16),
                            pltpu.VMEM((n_sel_pad, NSA_KV_HEADS), f32), pltpu.VMEM((n_sel_pad, NSA_HEADS), f32),
                            pltpu.VMEM((n_win, NSA_KV_W), bf16), pltpu.VMEM((n_win, NSA_KV_W), bf16),
                            pltpu.VMEM((SEL_PAD, NSA_KV_W), bf16), pltpu.VMEM((SEL_PAD, NSA_KV_W), bf16),
                            pltpu.VMEM((NSA_HEADS, NSA_HEAD_DIM), f32),
                            pltpu.VMEM((1, NSA_HEADS), f32), pltpu.VMEM((1, NSA_HEADS), f32),
                            pltpu.VMEM((NSA_KV_W, NSA_HEADS), f32)]),
        out_shape=jax.ShapeDtypeStruct((b, NSA_HEADS, NSA_HEAD_DIM), f32),
        compiler_params=pltpu.CompilerParams(
            dimension_semantics=("parallel", "arbitrary", "arbitrary"), vmem_limit_bytes=VMEM_LIMIT_BYTES),
        name="nsa_sample",
    )(page_table, *([pages] * (6 * PAGE_GROUP)), qbd, proj3, win,
      w1cat, w1f, pos, w2b, bcmp, bsel, bwin, mimp, hk, bd, rep, gsel)
    return out.reshape(b, NSA_Q_W)


SSD_PROJ_W = _round_up(SSD_IN_W, 7 * LANES)
SSD_TILE = 128
SSD_GN = SSD_GROUPS * SSD_STATE
SSD_GW = SSD_HPG * SSD_HEAD_DIM
CONV_PAD = 8


def _silu(x):
    return x * jax.nn.sigmoid(x)


def _softplus(x):
    return jnp.maximum(x, 0.0) + jnp.log(1.0 + jnp.exp(-jnp.abs(x)))


def _dot3(x, table, dot=jnp.dot):
    return sum(dot(part, table, preferred_element_type=jnp.float32) for part in _split3_bf16(x))


def _dot3_tn(x, table):
    return sum(_dot_tn(part, table) for part in _split3_bf16(x))


def _ssd_prompt_kernel(z_ref, x_ref, bc_ref, dt_ref, cw_ref, cb_ref, dtb_ref, alog_ref, d_ref, nw_ref,
                       eh_ref, tril_ref, triu_ref, eye_ref, y_ref, st_ref, conv_ref, ux_s, ubc_s, st_s):
    c = pl.program_id(1)
    t = SSD_TILE
    di = SSD_D_INNER
    bf16 = jnp.bfloat16

    @pl.when(c == 0)
    def _():
        ux_s[0:CONV_PAD, :] = jnp.zeros((CONV_PAD, di), jnp.float32)
        ubc_s[0:CONV_PAD, :] = jnp.zeros((CONV_PAD, 2 * SSD_GN), jnp.float32)
        st_s[...] = jnp.zeros(st_s.shape, jnp.float32)

    ux_s[CONV_PAD:CONV_PAD + t, :] = x_ref[...]
    ubc_s[CONV_PAD:CONV_PAD + t, :] = bc_ref[...]

    def conv(buf, col0, width):
        y = cb_ref[:, col0:col0 + width]
        for i in range(SSD_CONV_W):
            y = y + buf[pl.ds(CONV_PAD - (SSD_CONV_W - 1 - i), t), :] * cw_ref[i:i + 1, col0:col0 + width]
        return _silu(y)

    xs = conv(ux_s, 0, di)
    bcs = conv(ubc_s, di, 2 * SSD_GN)
    dt = _softplus(dt_ref[:, 0:SSD_HEADS] + dtb_ref[...])
    dta = dt * (-jnp.exp(alog_ref[...]))
    cum = _dot3(dta, tril_ref[...], lambda a, b, **kw: jnp.dot(b, a, **kw))
    cum_t = _dot3_tn(dta, triu_ref[...])
    dt_t = _dot3_tn(dt, eye_ref[...])
    last = cum[t - 1:t, :]
    dec_in = _dot3(jnp.exp(cum), eh_ref[...])
    wgt = _dot3(jnp.exp(last - cum) * dt, eh_ref[...])
    st_scale = _dot3(jnp.broadcast_to(jnp.exp(last), (8, SSD_HEADS)), eh_ref[...])[0:1, :]
    causal = lax.broadcasted_iota(jnp.int32, (t, t), 1) <= lax.broadcasted_iota(jnp.int32, (t, t), 0)

    for g in range(SSD_GROUPS):
        gl = slice(g * SSD_GW, (g + 1) * SSD_GW)
        bg = bcs[:, g * SSD_STATE:(g + 1) * SSD_STATE].astype(bf16)
        cg = bcs[:, SSD_GN + g * SSD_STATE:SSD_GN + (g + 1) * SSD_STATE].astype(bf16)
        cb = _dot_nt(cg, bg)
        xg = xs[:, gl]
        y_heads = []
        for j in range(SSD_HPG):
            h = g * SSD_HPG + j
            decay = jnp.exp(jnp.minimum(cum[:, h:h + 1] - cum_t[h:h + 1, :], 0.0))
            w = jnp.where(causal, cb * decay * dt_t[h:h + 1, :], 0.0).astype(bf16)
            y_heads.append(jnp.dot(w, xg[:, j * SSD_HEAD_DIM:(j + 1) * SSD_HEAD_DIM].astype(bf16),
                                   preferred_element_type=jnp.float32))
        st = st_s[g]
        y = jnp.concatenate(y_heads, axis=1)
        y = y + jnp.dot(cg, st.astype(bf16), preferred_element_type=jnp.float32) * dec_in[:, gl]
        st_s[g] = st_scale[:, gl] * st + _dot_tn(bg, (xg * wgt[:, gl]).astype(bf16))
        y = (y + d_ref[:, gl] * xg) * _silu(z_ref[:, gl])
        y = y * lax.rsqrt(jnp.mean(y * y, axis=-1, keepdims=True) + NORM_EPS) * nw_ref[:, gl]
        y_ref[:, gl] = y.astype(y_ref.dtype)

    ux_s[0:CONV_PAD, :] = ux_s[t:t + CONV_PAD, :]
    ubc_s[0:CONV_PAD, :] = ubc_s[t:t + CONV_PAD, :]

    @pl.when(c == pl.num_programs(1) - 1)
    def _():
        st_ref[...] = st_s[...]
        keep = SSD_CONV_W - 1
        conv_ref[:, 0:di] = ux_s[CONV_PAD - keep:CONV_PAD, :]
        conv_ref[:, di:] = ubc_s[CONV_PAD - keep:CONV_PAD, :]


def _ssd_tables():
    eh = (np.arange(SSD_HEADS)[:, None] == np.arange(SSD_D_INNER)[None, :] // SSD_HEAD_DIM).astype(np.float32)
    tril = np.tril(np.ones((SSD_TILE, SSD_TILE), np.float32))
    bf16 = jnp.bfloat16
    return (jnp.asarray(eh, bf16), jnp.asarray(tril, bf16), jnp.asarray(tril.T, bf16),
            jnp.asarray(np.eye(SSD_TILE, dtype=np.float32), bf16))


def _state_from_transposed(st_t):
    b = st_t.shape[0]
    st = st_t.reshape(b, SSD_GROUPS, SSD_STATE, SSD_HPG, SSD_HEAD_DIM)
    return jnp.transpose(st, (0, 1, 3, 4, 2)).reshape(b, SSD_HEADS, SSD_HEAD_DIM, SSD_STATE)


def _ssd_prompt(proj, conv_w, conv_b, dt_bias, a_log, d_skip, norm_w):
    b, t, _ = proj.shape
    di = SSD_D_INNER
    tt = SSD_TILE
    d_exp = jnp.repeat(d_skip, SSD_HEAD_DIM).reshape(1, di)
    consts = (conv_w, conv_b.reshape(1, -1), dt_bias.reshape(1, -1), a_log.reshape(1, -1), d_exp,
              norm_w.reshape(1, di)) + _ssd_tables()

    def const_spec(a):
        nd = a.ndim
        return pl.BlockSpec(a.shape, lambda i, c: (0,) * nd)

    y, st_t, conv_new = pl.pallas_call(
        _ssd_prompt_kernel,
        grid=(b, t // tt),
        in_specs=[pl.BlockSpec((None, tt, di), lambda i, c: (i, c, 0)),
                  pl.BlockSpec((None, tt, di), lambda i, c: (i, c, 1)),
                  pl.BlockSpec((None, tt, 2 * SSD_GN), lambda i, c: (i, c, 2)),
                  pl.BlockSpec((None, tt, LANES), lambda i, c: (i, c, (di + SSD_CONV_DIM) // LANES))]
                 + [const_spec(a) for a in consts],
        out_specs=[pl.BlockSpec((None, tt, di), lambda i, c: (i, c, 0)),
                   pl.BlockSpec((None, SSD_GROUPS, SSD_STATE, SSD_GW), lambda i, c: (i, 0, 0, 0)),
                   pl.BlockSpec((None, SSD_CONV_W - 1, SSD_CONV_DIM), lambda i, c: (i, 0, 0))],
        out_shape=[jax.ShapeDtypeStruct((b, t, di), jnp.bfloat16),
                   jax.ShapeDtypeStruct((b, SSD_GROUPS, SSD_STATE, SSD_GW), jnp.float32),
                   jax.ShapeDtypeStruct((b, SSD_CONV_W - 1, SSD_CONV_DIM), jnp.float32)],
        scratch_shapes=[pltpu.VMEM((CONV_PAD + tt, di), jnp.float32),
                        pltpu.VMEM((CONV_PAD + tt, 2 * SSD_GN), jnp.float32),
                        pltpu.VMEM((SSD_GROUPS, SSD_STATE, SSD_GW), jnp.float32)],
        compiler_params=pltpu.CompilerParams(
            dimension_semantics=("parallel", "arbitrary"), vmem_limit_bytes=VMEM_LIMIT_BYTES),
        name="ssd_prompt",
    )(proj, proj, proj, proj, *consts)
    return y.reshape(b * t, di), _state_from_transposed(st_t), conv_new


ROW_PAD = 8


def _row8(x):
    return jnp.concatenate([x, jnp.zeros((ROW_PAD - 1, x.shape[1]), x.dtype)], axis=0)


def _ssd_step_kernel(p_ref, conv_ref, st_ref, cw_ref, cb_ref, dtb_ref, alog_ref, d_ref, nw_ref,
                     y_ref, st_out, conv_out, y_s):
    di = SSD_D_INNER
    bf16 = jnp.bfloat16
    u = p_ref[:, di:di + SSD_CONV_DIM]
    keep = SSD_CONV_W - 1
    y = cb_ref[...] + u * cw_ref[keep:keep + 1, :]
    for i in range(keep):
        y = y + conv_ref[i:i + 1, :] * cw_ref[i:i + 1, :]
    conv_out[0:keep - 1, :] = conv_ref[1:keep, :]
    conv_out[keep - 1:keep, :] = u
    xbc = _silu(y)
    xs = xbc[:, :di]
    dt = _softplus(p_ref[:, di + SSD_CONV_DIM:di + SSD_CONV_DIM + SSD_HEADS] + dtb_ref[...])
    decay = jnp.exp(dt * (-jnp.exp(alog_ref[...])))
    for g in range(SSD_GROUPS):
        bg = _row8(xbc[:, di + g * SSD_STATE:di + (g + 1) * SSD_STATE]).astype(bf16)
        cg = _row8(xbc[:, di + SSD_GN + g * SSD_STATE:di + SSD_GN + (g + 1) * SSD_STATE]).astype(bf16)
        for j in range(SSD_HPG):
            h = g * SSD_HPG + j
            cols = slice(h * SSD_HEAD_DIM, (h + 1) * SSD_HEAD_DIM)
            xh = _row8(xs[:, cols] * dt[:, h:h + 1]).astype(bf16)
            st = decay[:, h:h + 1] * st_ref[h] + _dot_tn(xh, bg)
            st_out[h] = st
            y_s[:, cols] = _dot_nt(cg, st.astype(bf16))
    yv = y_s[0:1, :]
    yv = (yv + d_ref[...] * xs) * _silu(p_ref[:, 0:di])
    for g in range(SSD_GROUPS):
        gl = slice(g * SSD_GW, (g + 1) * SSD_GW)
        yg = yv[:, gl]
        y_ref[:, gl] = yg * lax.rsqrt(jnp.mean(yg * yg, axis=-1, keepdims=True) + NORM_EPS) * nw_ref[:, gl]


def _ssd_step(proj, ssm0, conv0, conv_w, conv_b, dt_bias, a_log, d_skip, norm_w):
    b = proj.shape[0]
    di = SSD_D_INNER
    consts = (conv_w, conv_b.reshape(1, -1), dt_bias.reshape(1, -1), a_log.reshape(1, -1),
              jnp.repeat(d_skip, SSD_HEAD_DIM).reshape(1, di), norm_w.reshape(1, di))

    def const_spec(a):
        nd = a.ndim
        return pl.BlockSpec(a.shape, lambda i: (0,) * nd)

    y, st, conv_new = pl.pallas_call(
        _ssd_step_kernel,
        grid=(b,),
        in_specs=[pl.BlockSpec((None, 1, proj.shape[1]), lambda i: (i, 0, 0)),
                  pl.BlockSpec((None,) + conv0.shape[1:], lambda i: (i, 0, 0)),
                  pl.BlockSpec((None,) + ssm0.shape[1:], lambda i: (i, 0, 0, 0))]
                 + [const_spec(a) for a in consts],
        out_specs=[pl.BlockSpec((None, 1, di), lambda i: (i, 0, 0)),
                   pl.BlockSpec((None,) + ssm0.shape[1:], lambda i: (i, 0, 0, 0)),
                   pl.BlockSpec((None,) + conv0.shape[1:], lambda i: (i, 0, 0))],
        out_shape=[jax.ShapeDtypeStruct((b, 1, di), jnp.float32),
                   jax.ShapeDtypeStruct(ssm0.shape, jnp.float32),
                   jax.ShapeDtypeStruct(conv0.shape, jnp.float32)],
        scratch_shapes=[pltpu.VMEM((ROW_PAD, di), jnp.float32)],
        compiler_params=pltpu.CompilerParams(
            dimension_semantics=("parallel",), vmem_limit_bytes=VMEM_LIMIT_BYTES),
        name="ssd_step",
    )(proj.reshape(b, 1, -1), conv0, ssm0, *consts)
    return y.reshape(b, di), st, conv_new


def _hgrn_gates(p_ref, lb_ref):
    wk = HG_HEADS * HG_DK
    q = _silu(p_ref[:, 0:wk])
    f = lb_ref[...] + (1.0 - lb_ref[...]) * jax.nn.sigmoid(p_ref[:, wk:2 * wk])
    return q, f


HG_TILE = 128
HG_SUB = 16


def _hgrn_prompt_kernel(p_ref, lb_ref, gn_ref, tril_ref, subend_ref, ones_ref, y_ref, st_ref, st_s):
    c = pl.program_id(1)
    t = HG_TILE
    wk = HG_HEADS * HG_DK
    wv = HG_HEADS * HG_DV
    bf16 = jnp.bfloat16
    n_sub = t // HG_SUB

    @pl.when(c == 0)
    def _():
        st_s[...] = jnp.zeros(st_s.shape, jnp.float32)

    row = lax.broadcasted_iota(jnp.int32, (t, HG_DK), 0)
    sub_pos = row % HG_SUB
    row_sub = lax.broadcasted_iota(jnp.int32, (t, t), 0) // HG_SUB
    col_sub = lax.broadcasted_iota(jnp.int32, (t, t), 1) // HG_SUB
    left = lambda a, b, **kw: jnp.dot(b, a, **kw)

    def head(h, carry):
        kc = pl.ds(pl.multiple_of(h * HG_DK, HG_DK), HG_DK)
        q = _silu(p_ref[:, kc])
        lb = lb_ref[:, kc]
        f = lb + (1.0 - lb) * jax.nn.sigmoid(p_ref[:, pl.ds(pl.multiple_of(wk + h * HG_DK, HG_DK), HG_DK)])
        k = 1.0 - f
        v = p_ref[:, pl.ds(pl.multiple_of(2 * wk + h * HG_DV, HG_DV), HG_DV)]
        gate = p_ref[:, pl.ds(pl.multiple_of(2 * wk + wv + h * HG_DV, HG_DV), HG_DV)]
        cum = _dot3(jnp.log(f), tril_ref[...], left)
        sub_end = _dot3(cum, subend_ref[...], left)
        k_hat = k * jnp.exp(sub_end - cum)
        a_off = jnp.zeros((t, t), jnp.float32)
        for j in range(n_sub - 1):
            end_j = cum[(j + 1) * HG_SUB - 1:(j + 1) * HG_SUB, :]
            q_j = (q * jnp.exp(jnp.minimum(cum - end_j, 0.0))).astype(bf16)
            k_j = jnp.where(row // HG_SUB == j, k_hat, 0.0).astype(bf16)
            a_off = a_off + _dot_nt(q_j, k_j)
        a_off = jnp.where(col_sub < row_sub, a_off, 0.0)
        v16 = v.astype(bf16)
        o = jnp.dot(a_off.astype(bf16), v16, preferred_element_type=jnp.float32)
        for d in range(HG_SUB):
            k_d, cum_d, v_d = (k, cum, v) if d == 0 else (pltpu.roll(k, d, 0), pltpu.roll(cum, d, 0), pltpu.roll(v, d, 0))
            e = (q * k_d * jnp.exp(jnp.minimum(cum - cum_d, 0.0))).astype(bf16)
            a_d = jnp.dot(e, ones_ref[...], preferred_element_type=jnp.float32)
            o = o + jnp.where(sub_pos >= d, a_d, 0.0) * v_d
        st = st_s[h]
        o = o + _dot_nt((q * jnp.exp(cum)).astype(bf16), st.astype(bf16))
        last = cum[t - 1:t, :]
        st_s[h] = st * jnp.exp(last) + _dot_tn(v16, (k * jnp.exp(last - cum)).astype(bf16))
        o = o * lax.rsqrt(jnp.mean(o * o, axis=-1, keepdims=True) + NORM_EPS) * gn_ref[...]
        y_ref[:, pl.ds(pl.multiple_of(h * HG_DV, HG_DV), HG_DV)] = (o * _silu(gate)).astype(y_ref.dtype)
        return carry

    lax.fori_loop(0, HG_HEADS, head, 0)

    @pl.when(c == pl.num_programs(1) - 1)
    def _():
        st_ref[...] = st_s[...]


def _hgrn_prompt(proj, lb, g_norm):
    b, t, w = proj.shape
    tt = HG_TILE
    wv = HG_HEADS * HG_DV
    idx = np.arange(tt)
    tril = np.tril(np.ones((tt, tt), np.float32))
    subend = (idx[None, :] == (idx[:, None] // HG_SUB) * HG_SUB + HG_SUB - 1).astype(np.float32)
    bf16 = jnp.bfloat16
    consts = (lb.reshape(1, -1), g_norm.reshape(1, -1), jnp.asarray(tril, bf16), jnp.asarray(subend, bf16),
              jnp.ones((HG_DK, HG_DK), bf16))

    def const_spec(a):
        nd = a.ndim
        return pl.BlockSpec(a.shape, lambda i, c: (0,) * nd)

    y, st_t = pl.pallas_call(
        _hgrn_prompt_kernel,
        grid=(b, t // tt),
        in_specs=[pl.BlockSpec((None, tt, w), lambda i, c: (i, c, 0))] + [const_spec(a) for a in consts],
        out_specs=[pl.BlockSpec((None, tt, wv), lambda i, c: (i, c, 0)),
                   pl.BlockSpec((None, HG_HEADS, HG_DV, HG_DK), lambda i, c: (i, 0, 0, 0))],
        out_shape=[jax.ShapeDtypeStruct((b, t, wv), bf16),
                   jax.ShapeDtypeStruct((b, HG_HEADS, HG_DV, HG_DK), jnp.float32)],
        scratch_shapes=[pltpu.VMEM((HG_HEADS, HG_DV, HG_DK), jnp.float32)],
        compiler_params=pltpu.CompilerParams(
            dimension_semantics=("parallel", "arbitrary"), vmem_limit_bytes=VMEM_LIMIT_BYTES),
        name="hgrn_prompt",
    )(proj, *consts)
    return y.reshape(b * t, wv), jnp.swapaxes(st_t, 2, 3)


def _hgrn_step_kernel(p_ref, st_ref, lb_ref, gn_ref, y_ref, st_out):
    wk = HG_HEADS * HG_DK
    wv = HG_HEADS * HG_DV
    bf16 = jnp.bfloat16
    q, f = _hgrn_gates(p_ref, lb_ref)
    eye = lax.broadcasted_iota(jnp.int32, (HG_DK, HG_DK), 0) == lax.broadcasted_iota(jnp.int32, (HG_DK, HG_DK), 1)
    for h in range(HG_HEADS):
        kc = slice(h * HG_DK, (h + 1) * HG_DK)
        vc = slice(2 * wk + h * HG_DV, 2 * wk + (h + 1) * HG_DV)
        gc = slice(2 * wk + wv + h * HG_DV, 2 * wk + wv + (h + 1) * HG_DV)
        fh = f[:, kc]
        f_col = jnp.sum(jnp.where(eye, fh, 0.0), axis=1, keepdims=True)
        kv = _dot_tn(_row8(1.0 - fh).astype(bf16), _row8(p_ref[:, vc]).astype(bf16))
        st = f_col * st_ref[h] + kv
        st_out[h] = st
        o = jnp.dot(_row8(q[:, kc]).astype(bf16), st.astype(bf16), preferred_element_type=jnp.float32)[0:1, :]
        o = o * lax.rsqrt(jnp.mean(o * o, axis=-1, keepdims=True) + NORM_EPS) * gn_ref[...]
        y_ref[:, h * HG_DV:(h + 1) * HG_DV] = o * _silu(p_ref[:, gc])


def _hgrn_step(proj, s0, lb, g_norm):
    b = proj.shape[0]
    wv = HG_HEADS * HG_DV
    y, st = pl.pallas_call(
        _hgrn_step_kernel,
        grid=(b,),
        in_specs=[pl.BlockSpec((None, 1, proj.shape[1]), lambda i: (i, 0, 0)),
                  pl.BlockSpec((None,) + s0.shape[1:], lambda i: (i, 0, 0, 0)),
                  pl.BlockSpec((1, HG_HEADS * HG_DK), lambda i: (0, 0)),
                  pl.BlockSpec((1, HG_DV), lambda i: (0, 0))],
        out_specs=[pl.BlockSpec((None, 1, wv), lambda i: (i, 0, 0)),
                   pl.BlockSpec((None,) + s0.shape[1:], lambda i: (i, 0, 0, 0))],
        out_shape=[jax.ShapeDtypeStruct((b, 1, wv), jnp.float32), jax.ShapeDtypeStruct(s0.shape, jnp.float32)],
        compiler_params=pltpu.CompilerParams(
            dimension_semantics=("parallel",), vmem_limit_bytes=VMEM_LIMIT_BYTES),
        name="hgrn_step",
    )(proj.reshape(b, 1, -1), s0, lb.reshape(1, -1), g_norm.reshape(1, -1))
    return y.reshape(b, wv), st


def _rel_bucket(dist):
    n = jnp.maximum(dist, 0)
    n_exact = REL_BUCKETS // 2
    nf = jnp.maximum(n, 1).astype(jnp.float32)
    large = n_exact + (jnp.log(nf / n_exact) / math.log(REL_MAX_DIST / n_exact)
                       * (REL_BUCKETS - n_exact)).astype(jnp.int32)
    return jnp.where(n < n_exact, n, jnp.minimum(large, REL_BUCKETS - 1))


def _nsa_prompt_core(proj, cmp_pos, cmp_w1, cmp_w2, tables):
    b, t, _ = proj.shape
    cmp = _nsa_compress_prompt(proj, cmp_pos, cmp_w1, cmp_w2)
    merged = _nsa_attn_prompt(proj, cmp, tables)
    o1 = NSA_Q_W
    o2 = o1 + 4 * NSA_KV_W
    o3 = o2 + 2 * NSA_KV_W
    kv_cs = proj[..., o1:o2].reshape(b, t, 4, NSA_KV_HEADS, NSA_HEAD_DIM)
    kv_win = proj[:, t - min(WINDOW, t):, o2:o3].reshape(b, min(WINDOW, t), 2, NSA_KV_HEADS, NSA_HEAD_DIM)
    return merged, kv_cs, kv_win


def _nsa_sample_core(proj, kv_pages, win_buf, page_table, cmp_pos, cmp_w1, cmp_w2, tables):
    b, t, _ = proj.shape
    assert t == 1 and win_buf.shape[1] == WINDOW and page_table.shape[1] % PAGE_GROUP == 0
    merged = _nsa_attn_sample(proj.reshape(b, -1), kv_pages, win_buf, page_table, cmp_pos, cmp_w1, cmp_w2, tables)
    o1 = NSA_Q_W
    o2 = o1 + 4 * NSA_KV_W
    o3 = o2 + 2 * NSA_KV_W
    kv_cs = proj[..., o1:o2].reshape(b, t, 4, NSA_KV_HEADS, NSA_HEAD_DIM)
    kv_win = proj[..., o2:o3].reshape(b, t, 2, NSA_KV_HEADS, NSA_HEAD_DIM)
    new_win = jnp.concatenate([win_buf[:, t:], kv_win], axis=1)
    return merged, kv_cs, new_win


def _pad_cols(w, n):
    return jnp.pad(w, ((0, 0), (0, n - w.shape[1])))


def kernel(x_prompt, x_sample, cache_nsa_kv, cache_nsa_win, state_hgrn, state_ssd, state_ssd_conv, page_table, c_prompt, c_sample, rel_bias, hgrn_lower_bounds, w_ada, b_ada, norm_gains, w_mlp_in, w_mlp_out, nsa_w_in, nsa_cmp_pos, nsa_cmp_w1, nsa_cmp_w2, nsa_w_out, hg_w_in, hg_norm, hg_w_out, ssd_w_in, ssd_conv_w, ssd_conv_b, ssd_dt_bias, ssd_a_log, ssd_d, ssd_norm, ssd_w_out):
    bf16 = jnp.bfloat16
    bp, tp, d = x_prompt.shape
    bs, ts, _ = x_sample.shape
    mp, ms = bp * tp, bs * ts
    lb_p = jax.nn.softmax(hgrn_lower_bounds, axis=0)
    lower_bounds = jnp.cumsum(lb_p, axis=0) - lb_p[0]

    mod = _ada_all(jnp.concatenate([c_prompt, c_sample], axis=0), w_ada, b_ada)
    mod = mod.reshape(DEPTH, bp + bs, ADA_CHUNKS, d)
    mod_p = mod[:, :bp].transpose(0, 2, 1, 3)[:, :, :, None, :]
    mod_s = mod[:, bp:].transpose(0, 2, 1, 3)[:, :, None, :, :]

    xp = x_prompt.reshape(mp, d)
    xs = x_sample.reshape(ms, d)
    tm_p, tm_s = PROMPT_ROW_TILE, ms
    nsa_tables = _nsa_prompt_tables(rel_bias, tp)
    nsa_tables_s = _nsa_sample_tables(rel_bias, page_table.shape[1] * PAGE_SIZE, cache_nsa_win.shape[2])

    kv_p, kv_s, win_p, win_s = [], [], [], []
    hg_p, hg_s, ssd_p, ssd_s, conv_p, conv_s = [], [], [], [], [], []
    for i in range(DEPTH):
        j = i // N_MIXERS
        kind = i % N_MIXERS
        g = norm_gains[i]
        shp_m, scp_m, gtp_m, shp_f, scp_f, gtp_f = [mod_p[i, c] for c in range(ADA_CHUNKS)]
        shs_m, scs_m, gts_m, shs_f, scs_f, gts_f = [mod_s[i, c] for c in range(ADA_CHUNKS)]
        if kind == 0:
            n_pad = NSA_PROJ_W
            w_in = _pad_cols(nsa_w_in[j], n_pad).astype(bf16)
            w_out = nsa_w_out[j].astype(bf16)
            pp = _norm_mod_matmul(xp, g[0], scp_m, shp_m, w_in, tp, tm_p).reshape(bp, tp, n_pad)
            ps = _norm_mod_matmul(xs, g[0], scs_m, shs_m, w_in, ts, tm_s).reshape(bs, ts, n_pad)
            ap, new_kv_p, new_win_p = _nsa_prompt_core(pp, nsa_cmp_pos[j], nsa_cmp_w1[j], nsa_cmp_w2[j], nsa_tables)
            as_, new_kv_s, new_win_s = _nsa_sample_core(ps, cache_nsa_kv[j], cache_nsa_win[j], page_table,
                                                        nsa_cmp_pos[j], nsa_cmp_w1[j], nsa_cmp_w2[j], nsa_tables_s)
            kv_p.append(new_kv_p)
            kv_s.append(new_kv_s)
            win_p.append(new_win_p)
            win_s.append(new_win_s)
        elif kind == 1:
            w_in = hg_w_in[j].astype(bf16)
            w_out = hg_w_out[j].astype(bf16)
            pp = _norm_mod_matmul(xp, g[0], scp_m, shp_m, w_in, tp, tm_p).reshape(bp, tp, -1)
            ps = _norm_mod_matmul(xs, g[0], scs_m, shs_m, w_in, ts, tm_s).reshape(bs, ts, -1)
            ap, new_hp = _hgrn_prompt(pp, lower_bounds[i], hg_norm[j])
            as_, new_hs = _hgrn_step(ps.reshape(bs, -1), state_hgrn[j], lower_bounds[i], hg_norm[j])
            hg_p.append(new_hp)
            hg_s.append(new_hs)
        else:
            n_pad = SSD_PROJ_W
            w_in = _pad_cols(ssd_w_in[j], n_pad).astype(bf16)
            w_out = ssd_w_out[j].astype(bf16)
            pp = _norm_mod_matmul(xp, g[0], scp_m, shp_m, w_in, tp, tm_p).reshape(bp, tp, n_pad)
            ps = _norm_mod_matmul(xs, g[0], scs_m, shs_m, w_in, ts, tm_s).reshape(bs, ts, n_pad)
            ap, new_sp, new_cp = _ssd_prompt(pp, ssd_conv_w[j], ssd_conv_b[j], ssd_dt_bias[j],
                                             ssd_a_log[j], ssd_d[j], ssd_norm[j])
            as_, new_ss, new_cs = _ssd_step(ps.reshape(bs, -1), state_ssd[j], state_ssd_conv[j], ssd_conv_w[j],
                                            ssd_conv_b[j], ssd_dt_bias[j], ssd_a_log[j], ssd_d[j], ssd_norm[j])
            ssd_p.append(new_sp)
            ssd_s.append(new_ss)
            conv_p.append(new_cp)
            conv_s.append(new_cs)
        xp = _matmul_norm_res(ap, w_out, xp, g[1], gtp_m, tp, tm_p)
        xs = _matmul_norm_res(as_, w_out, xs, g[1], gts_m, ts, tm_s)
        w1 = w_mlp_in[i].astype(bf16)
        w2 = w_mlp_out[i].astype(bf16)
        xp = _mlp(xp, g[2], scp_f, shp_f, w1, w2, g[3], gtp_f, tp, tm_p)
        xs = _mlp(xs, g[2], scs_f, shs_f, w1, w2, g[3], gts_f, ts, tm_s)
    return (xp.reshape(bp, tp, d), xs.reshape(bs, ts, d),
            jnp.stack(kv_p), jnp.stack(kv_s), jnp.stack(win_p), jnp.stack(win_s),
            jnp.stack(hg_p), jnp.stack(hg_s), jnp.stack(ssd_p), jnp.stack(ssd_s),
            jnp.stack(conv_p), jnp.stack(conv_s))
```

```python
import functools
import math

import jax
import jax.numpy as jnp
import numpy as np
from jax import lax
from jax.experimental import pallas as pl
from jax.experimental.pallas import tpu as pltpu

D_MODEL = 1024
DEPTH = 4
PAGE_SIZE = 128
N_MIXERS = 3
ADA_CHUNKS = 6
NORM_EPS = 1e-6
D_FF = 4 * D_MODEL

NSA_HEADS = 16
NSA_HEAD_DIM = D_MODEL // NSA_HEADS
NSA_KV_HEADS = 4
NSA_GROUP = NSA_HEADS // NSA_KV_HEADS
CMP_STRIDE = 16
CMP_LEN = 2 * CMP_STRIDE
CMP_HIDDEN = 2 * NSA_HEAD_DIM
SEL_BLOCK = 64
SEL_TOP_N = 16
WINDOW = 512
WIN_Q_BLOCK = 128
SEL_Q_BLOCK = 16
NSA_Q_W = NSA_HEADS * NSA_HEAD_DIM
NSA_KV_W = NSA_KV_HEADS * NSA_HEAD_DIM
NSA_IN_W = NSA_Q_W + 6 * NSA_KV_W + 3 * NSA_HEADS

REL_BUCKETS = 32
REL_MAX_DIST = 128

HG_EXPAND = 128
HG_HEADS = D_MODEL // HG_EXPAND
HG_DK = HG_EXPAND
HG_DV = D_MODEL // HG_HEADS
HG_CHUNK = 64

SSD_D_INNER = 2 * D_MODEL
SSD_HEAD_DIM = 64
SSD_HEADS = SSD_D_INNER // SSD_HEAD_DIM
SSD_GROUPS = 8
SSD_HPG = SSD_HEADS // SSD_GROUPS
SSD_STATE = 128
SSD_CONV_W = 4
SSD_CONV_DIM = SSD_D_INNER + 2 * SSD_GROUPS * SSD_STATE
SSD_IN_W = SSD_D_INNER + SSD_CONV_DIM + SSD_HEADS
SSD_CHUNK = 128

NEG_INF = -1e30
FORCE_SCORE = 1e4

LANES = 128
VMEM_LIMIT_BYTES = 48 * 1024 * 1024
PROMPT_ROW_TILE = 512


def _round_up(n, m):
    return -(-n // m) * m


def _col_tile(n, cap=1536):
    best = LANES
    for t in range(LANES, cap + 1, LANES):
        if n % t == 0:
            best = t
    return best


def _rms(x, g):
    return x * lax.rsqrt(jnp.mean(x * x, axis=-1, keepdims=True) + NORM_EPS) * g


def _mod_spec(mod, rows_per_mod, tm, ngrid):
    r = mod.shape[1]
    if r == 1:
        per = rows_per_mod // tm
        if ngrid == 1:
            return pl.BlockSpec((None, 1, mod.shape[2]), lambda i: (i // per, 0, 0))
        return pl.BlockSpec((None, 1, mod.shape[2]), lambda i, j: (i // per, 0, 0))
    if ngrid == 1:
        return pl.BlockSpec((None, r, mod.shape[2]), lambda i: (0, 0, 0))
    return pl.BlockSpec((None, r, mod.shape[2]), lambda i, j: (0, 0, 0))


def _ada_kernel(c_ref, w_ref, b_ref, o_ref):
    c = c_ref[...]
    s = (c * jax.nn.sigmoid(c)).astype(jnp.bfloat16)
    o_ref[...] = jnp.dot(s, w_ref[...].astype(jnp.bfloat16),
                         preferred_element_type=jnp.float32) + b_ref[...]


def _ada_all(c_all, w_ada, b_ada):
    rows = c_all.shape[0]
    n = ADA_CHUNKS * D_MODEL
    tn = 1024
    return pl.pallas_call(
        _ada_kernel,
        grid=(DEPTH, n // tn),
        in_specs=[pl.BlockSpec((rows, D_MODEL), lambda l, j: (0, 0)),
                  pl.BlockSpec((None, D_MODEL, tn), lambda l, j: (l, 0, j)),
                  pl.BlockSpec((None, 1, tn), lambda l, j: (l, 0, j))],
        out_specs=pl.BlockSpec((None, rows, tn), lambda l, j: (l, 0, j)),
        out_shape=jax.ShapeDtypeStruct((DEPTH, rows, n), jnp.float32),
        compiler_params=pltpu.CompilerParams(
            dimension_semantics=("parallel", "parallel"), vmem_limit_bytes=VMEM_LIMIT_BYTES),
        name="ada",
    )(c_all, w_ada, b_ada.reshape(DEPTH, 1, n))


def _norm_mod_matmul_kernel(x_ref, g_ref, sc_ref, sh_ref, w_ref, o_ref, h_ref):
    @pl.when(pl.program_id(1) == 0)
    def _():
        h = _rms(x_ref[...], g_ref[...]) * (1.0 + sc_ref[...]) + sh_ref[...]
        h_ref[...] = h.astype(jnp.bfloat16)

    o_ref[...] = jnp.dot(h_ref[...], w_ref[...], preferred_element_type=jnp.float32)


def _norm_mod_matmul(x, g, sc, sh, w, rows_per_mod, tm):
    m, d = x.shape
    n = w.shape[1]
    tn = _col_tile(n)
    return pl.pallas_call(
        _norm_mod_matmul_kernel,
        grid=(m // tm, n // tn),
        in_specs=[pl.BlockSpec((tm, d), lambda i, j: (i, 0)),
                  pl.BlockSpec((1, d), lambda i, j: (0, 0)),
                  _mod_spec(sc, rows_per_mod, tm, 2),
                  _mod_spec(sh, rows_per_mod, tm, 2),
                  pl.BlockSpec((d, tn), lambda i, j: (0, j))],
        out_specs=pl.BlockSpec((tm, tn), lambda i, j: (i, j)),
        out_shape=jax.ShapeDtypeStruct((m, n), jnp.float32),
        scratch_shapes=[pltpu.VMEM((tm, d), jnp.bfloat16)],
        compiler_params=pltpu.CompilerParams(
            dimension_semantics=("parallel", "arbitrary"), vmem_limit_bytes=VMEM_LIMIT_BYTES),
        name="norm_mod_matmul",
    )(x, g.reshape(1, d), sc, sh, w)


def _matmul_norm_res_kernel(a_ref, w_ref, x_ref, g_ref, gt_ref, o_ref):
    y = jnp.dot(a_ref[...].astype(jnp.bfloat16), w_ref[...], preferred_element_type=jnp.float32)
    o_ref[...] = x_ref[...] + gt_ref[...] * _rms(y, g_ref[...])


def _matmul_norm_res(a, w, x, g, gate, rows_per_mod, tm):
    m, k = a.shape
    d = w.shape[1]
    return pl.pallas_call(
        _matmul_norm_res_kernel,
        grid=(m // tm,),
        in_specs=[pl.BlockSpec((tm, k), lambda i: (i, 0)),
                  pl.BlockSpec((k, d), lambda i: (0, 0)),
                  pl.BlockSpec((tm, d), lambda i: (i, 0)),
                  pl.BlockSpec((1, d), lambda i: (0, 0)),
                  _mod_spec(gate, rows_per_mod, tm, 1)],
        out_specs=pl.BlockSpec((tm, d), lambda i: (i, 0)),
        out_shape=jax.ShapeDtypeStruct((m, d), jnp.float32),
        compiler_params=pltpu.CompilerParams(
            dimension_semantics=("parallel",), vmem_limit_bytes=VMEM_LIMIT_BYTES),
        name="matmul_norm_res",
    )(a, w, x, g.reshape(1, d), gate)


def _mlp_kernel(x_ref, g2_ref, sc_ref, sh_ref, w1_ref, w2_ref, g3_ref, gt_ref, o_ref, h_ref, acc_ref):
    j = pl.program_id(1)

    @pl.when(j == 0)
    def _():
        h = _rms(x_ref[...], g2_ref[...]) * (1.0 + sc_ref[...]) + sh_ref[...]
        h_ref[...] = h.astype(jnp.bfloat16)

    u = jnp.dot(h_ref[...], w1_ref[...], preferred_element_type=jnp.float32)
    u = jnp.square(jnp.maximum(u, 0.0)).astype(jnp.bfloat16)
    part = jnp.dot(u, w2_ref[...], preferred_element_type=jnp.float32)

    @pl.when(j == 0)
    def _():
        acc_ref[...] = part

    @pl.when(j > 0)
    def _():
        acc_ref[...] += part

    @pl.when(j == pl.num_programs(1) - 1)
    def _():
        o_ref[...] = x_ref[...] + gt_ref[...] * _rms(acc_ref[...], g3_ref[...])


def _mlp(x, g2, sc, sh, w1, w2, g3, gate, rows_per_mod, tm):
    m, d = x.shape
    f = w1.shape[1]
    tf = 1024
    return pl.pallas_call(
        _mlp_kernel,
        grid=(m // tm, f // tf),
        in_specs=[pl.BlockSpec((tm, d), lambda i, j: (i, 0)),
                  pl.BlockSpec((1, d), lambda i, j: (0, 0)),
                  _mod_spec(sc, rows_per_mod, tm, 2),
                  _mod_spec(sh, rows_per_mod, tm, 2),
                  pl.BlockSpec((d, tf), lambda i, j: (0, j)),
                  pl.BlockSpec((tf, d), lambda i, j: (j, 0)),
                  pl.BlockSpec((1, d), lambda i, j: (0, 0)),
                  _mod_spec(gate, rows_per_mod, tm, 2)],
        out_specs=pl.BlockSpec((tm, d), lambda i, j: (i, 0)),
        out_shape=jax.ShapeDtypeStruct((m, d), jnp.float32),
        scratch_shapes=[pltpu.VMEM((tm, d), jnp.bfloat16), pltpu.VMEM((tm, d), jnp.float32)],
        compiler_params=pltpu.CompilerParams(
            dimension_semantics=("parallel", "arbitrary"), vmem_limit_bytes=VMEM_LIMIT_BYTES),
        name="mlp",
    )(x, g2.reshape(1, d), sc, sh, w1, w2, g3.reshape(1, d), gate)


NSA_COL_BLOCK = NSA_KV_W
NSA_PROJ_W = 11 * NSA_COL_BLOCK
NSA_GATE_BLOCK = (NSA_Q_W + 6 * NSA_KV_W) // NSA_COL_BLOCK
ATT_TILE = 128
ROWS = NSA_GROUP * ATT_TILE
ATT_TILE_GROUP = 4


def _dot_nt(a, b):
    return lax.dot_general(a, b, (((1,), (1,)), ((), ())), preferred_element_type=jnp.float32)


def _dot_tn(a, b):
    return lax.dot_general(a, b, (((0,), (0,)), ((), ())), preferred_element_type=jnp.float32)


def _gelu_tanh(x):
    return 0.5 * x * (1.0 + jnp.tanh(math.sqrt(2.0 / math.pi) * (x + 0.044715 * (x * x * x))))


def _split3_bf16(x):
    hi = x.astype(jnp.bfloat16)
    r1 = x - hi.astype(jnp.float32)
    mid = r1.astype(jnp.bfloat16)
    lo = (r1 - mid.astype(jnp.float32)).astype(jnp.bfloat16)
    return hi, mid, lo


def _nsa_compress_kernel(x0_ref, x1_ref, x2_ref, x3_ref, w1_ref, w1f_ref, pos_ref, w2_ref, o_ref):
    n = x0_ref.shape[0] // CMP_STRIDE
    hd = NSA_HEAD_DIM
    x_refs = ((x0_ref, x1_ref), (x2_ref, x3_ref))
    for r in range(2):
        pos_b = jnp.dot(pos_ref[r].astype(jnp.bfloat16), w1f_ref[r], preferred_element_type=jnp.float32)
        acc = [jnp.zeros((n, 2 * CMP_HIDDEN), jnp.float32) for _ in range(NSA_KV_HEADS)]
        for l in range(CMP_STRIDE):
            w = w1_ref[r, l]
            for pair in range(2):
                xl = x_refs[r][pair][pl.ds(l, n, stride=CMP_STRIDE), :].astype(jnp.bfloat16)
                for half in range(2):
                    k = 2 * pair + half
                    acc[k] = acc[k] + jnp.dot(xl[:, half * hd:(half + 1) * hd], w,
                                              preferred_element_type=jnp.float32)
        for k in range(NSA_KV_HEADS):
            pa = acc[k][:, :CMP_HIDDEN]
            pb_next = pltpu.roll(acc[k][:, CMP_HIDDEN:], n - 1, 0)
            hid = _gelu_tanh(pa + pb_next + pos_b)
            out = jnp.dot(hid.astype(jnp.bfloat16), w2_ref[r], preferred_element_type=jnp.float32)
            o_ref[r, :, k * hd:(k + 1) * hd] = out.astype(o_ref.dtype)


def _compress_weights(cmp_pos, w1, w2):
    bf16 = jnp.bfloat16
    w1r = w1.reshape(2, CMP_LEN, NSA_HEAD_DIM, CMP_HIDDEN)
    w1cat = jnp.concatenate([w1r[:, :CMP_STRIDE], w1r[:, CMP_STRIDE:]], axis=-1).astype(bf16)
    return w1cat, w1.astype(bf16), cmp_pos.reshape(2, 1, CMP_LEN * NSA_HEAD_DIM), w2.astype(bf16)


def _nsa_compress_prompt(proj, cmp_pos, w1, w2):
    b, t, _ = proj.shape
    n = t // CMP_STRIDE
    w1cat, w1f, pos, w2b = _compress_weights(cmp_pos, w1, w2)
    return pl.pallas_call(
        _nsa_compress_kernel,
        grid=(b,),
        in_specs=[pl.BlockSpec((None, t, LANES), lambda i, c=c: (i, 0, NSA_Q_W // LANES + c)) for c in range(4)]
                 + [pl.BlockSpec(w1cat.shape, lambda i: (0, 0, 0, 0)),
                  pl.BlockSpec(w1f.shape, lambda i: (0, 0, 0)),
                  pl.BlockSpec(pos.shape, lambda i: (0, 0, 0)),
                  pl.BlockSpec(w2b.shape, lambda i: (0, 0, 0))],
        out_specs=pl.BlockSpec((None, 2, n, NSA_KV_W), lambda i: (i, 0, 0, 0)),
        out_shape=jax.ShapeDtypeStruct((b, 2, n, NSA_KV_W), jnp.bfloat16),
        compiler_params=pltpu.CompilerParams(
            dimension_semantics=("parallel",), vmem_limit_bytes=VMEM_LIMIT_BYTES),
        name="nsa_compress",
    )(proj, proj, proj, proj, w1cat, w1f, pos, w2b)


def _bias_lookup(rel_bias, dist):
    onehot = jax.nn.one_hot(_rel_bucket(dist), REL_BUCKETS, dtype=jnp.float32)
    return jnp.einsum('...c,ch->...h', onehot, rel_bias, precision=lax.Precision.HIGHEST)


DEN_ROWS = 8


def _with_ones(v):
    return jnp.concatenate([v, jnp.ones((v.shape[0], DEN_ROWS), v.dtype)], axis=1)


def _key_softmax_step(s, v, m, acc):
    m_new = jnp.maximum(m, jnp.max(s, axis=0, keepdims=True))
    e = jnp.exp(s - m_new).astype(jnp.bfloat16)
    acc = jnp.exp(m - m_new) * acc + _dot_tn(_with_ones(v), e)
    return m_new, acc


def _softmax_out(acc):
    hd = acc.shape[0] - DEN_ROWS
    return acc[:hd] / acc[hd:hd + 1]


def _nsa_attn_t_kernel(q_ref, g_ref, c_ref, ks_ref, vs_ref, kw_ref, vw_ref, bc_ref, bt_ref,
                       mimp_ref, eg_ref, o_ref, oc_s, os_s, ow_s, sel_s, *, tile0, n_far):
    i = tile0 + pl.program_id(1)
    hd = NSA_HEAD_DIM
    tq = ATT_TILE
    bf16 = jnp.bfloat16
    n_cmp_pad = c_ref.shape[1]
    n_sel = mimp_ref.shape[0]
    kj = lax.broadcasted_iota(jnp.int32, (tq, ROWS), 0)
    qi = lax.broadcasted_iota(jnp.int32, (tq, ROWS), 1) % tq
    causal = kj <= qi
    win_edge = kj >= qi
    first_half = kj < SEL_BLOCK
    cmp_end = CMP_STRIDE * lax.broadcasted_iota(jnp.int32, (n_cmp_pad, ROWS), 0) + (CMP_LEN - 1)
    mask_c = cmp_end <= i * tq + lax.broadcasted_iota(jnp.int32, (n_cmp_pad, ROWS), 1) % tq
    blk = lax.broadcasted_iota(jnp.int32, (n_sel, tq), 0)
    cur = (i * tq + lax.broadcasted_iota(jnp.int32, (n_sel, tq), 1)) // SEL_BLOCK
    forced = (blk == 0) | (blk == cur) | (blk == cur - 1)
    valid = blk <= cur

    for k in range(NSA_KV_HEADS):
        lanes = slice(k * hd, (k + 1) * hd)
        qk = jnp.concatenate(
            [q_ref[:, (k * NSA_GROUP + g) * hd:(k * NSA_GROUP + g + 1) * hd] for g in range(NSA_GROUP)], axis=0)
        qk = (qk * (hd ** -0.5)).astype(bf16)

        s = jnp.where(mask_c, _dot_nt(c_ref[0, :, lanes], qk) + bc_ref[k], NEG_INF)
        m = jnp.max(s, axis=0, keepdims=True)
        e = jnp.where(mask_c, jnp.exp(s - m), 0.0)
        p = e / jnp.maximum(jnp.sum(e, axis=0, keepdims=True), 1e-30)
        o_cmp = _dot_tn(c_ref[1, :, lanes], p.astype(bf16))
        p_sum = sum(p[:, g * tq:(g + 1) * tq] for g in range(NSA_GROUP))
        imp = _dot3(p_sum, mimp_ref[...], lambda a, b, **kw: jnp.dot(b, a, **kw))
        score = jnp.where(valid, jnp.where(forced, FORCE_SCORE, imp), NEG_INF)
        rank = jnp.zeros((n_sel, tq), jnp.float32)
        for j in range(n_sel):
            row = score[j:j + 1, :]
            beats = (row > score) | ((row == score) & (blk > j))
            rank = rank + jnp.where(beats, 1.0, 0.0)
        sel = jnp.where((rank < SEL_TOP_N) & (score > 0.5 * NEG_INF), 1.0, 0.0)
        sel = jnp.concatenate([sel] * NSA_GROUP, axis=1)
        per_chunk = tq // SEL_BLOCK
        for c in range(n_sel // per_chunk):
            sel_s[c, 0:per_chunk, :] = sel[c * per_chunk:(c + 1) * per_chunk, :]

        def sel_chunk(c, carry, bias, extra):
            rows = pl.ds(pl.multiple_of(c * tq, tq), tq)
            s = _dot_nt(ks_ref[rows, lanes].astype(bf16), qk)
            if bias is not None:
                s = s + bias
            pair = sel_s[c, 0:tq // SEL_BLOCK, :]
            mk = jnp.where(first_half, pair[0:1, :], pair[1:2, :]) > 0.5
            if extra is not None:
                mk = mk & extra
            return _key_softmax_step(jnp.where(mk, s, NEG_INF), vs_ref[rows, lanes].astype(bf16), *carry)

        init = (jnp.full((1, ROWS), NEG_INF, jnp.float32), jnp.zeros((hd + DEN_ROWS, ROWS), jnp.float32))
        carry = sel_chunk(i, init, bt_ref[k, 0], causal)
        carry = sel_chunk(jnp.maximum(i - 1, 0), carry, bt_ref[k, 1], jnp.broadcast_to(i >= 1, (tq, ROWS)))
        for c in range(n_far):
            carry = sel_chunk(c, carry, None, jnp.broadcast_to(c < i - 1, (tq, ROWS)))
        o_sel = _softmax_out(carry[1])

        def win_chunk(back, carry, bias, extra):
            c = i - back
            rows = pl.ds(pl.multiple_of(jnp.maximum(c, 0) * tq, tq), tq)
            s = _dot_nt(kw_ref[rows, lanes].astype(bf16), qk)
            if bias is not None:
                s = s + bias
            mk = jnp.broadcast_to(c >= 0, (tq, ROWS))
            if extra is not None:
                mk = mk & extra
            return _key_softmax_step(jnp.where(mk, s, NEG_INF), vw_ref[rows, lanes].astype(bf16), *carry)

        carry = win_chunk(0, init, bt_ref[k, 0], causal)
        carry = win_chunk(1, carry, bt_ref[k, 1], None)
        for back in range(2, WINDOW // tq):
            carry = win_chunk(back, carry, None, None)
        carry = win_chunk(WINDOW // tq, carry, None, win_edge)
        o_win = _softmax_out(carry[1])

        for g in range(NSA_GROUP):
            rows = slice((k * NSA_GROUP + g) * hd, (k * NSA_GROUP + g + 1) * hd)
            oc_s[rows, :] = o_cmp[:, g * tq:(g + 1) * tq]
            os_s[rows, :] = o_sel[:, g * tq:(g + 1) * tq]
            ow_s[rows, :] = o_win[:, g * tq:(g + 1) * tq]

    gate = jax.nn.sigmoid(g_ref[...])
    g_hi = gate.astype(bf16)
    g_lo = (gate - g_hi.astype(jnp.float32)).astype(bf16)
    out = jnp.zeros((NSA_Q_W, tq), jnp.float32)
    for br, o_s in enumerate((oc_s, os_s, ow_s)):
        out = out + (_dot_nt(eg_ref[br], g_hi) + _dot_nt(eg_ref[br], g_lo)) * o_s[...]
    for r in range(NSA_Q_W // tq):
        o_ref[:, r * tq:(r + 1) * tq] = out[r * tq:(r + 1) * tq, :].T.astype(o_ref.dtype)


def _keys_by_kv_head(tab):
    *lead, q, t, _ = tab.shape
    tab = tab.reshape(*lead, q, t, NSA_KV_HEADS, NSA_GROUP)
    nl = len(lead)
    tab = jnp.transpose(tab, (*range(nl), nl + 2, nl + 1, nl + 3, nl))
    return tab.reshape(*lead, NSA_KV_HEADS, t, NSA_GROUP * q)


def _nsa_prompt_tables_t(rel_bias, t):
    tq = ATT_TILE
    n_chunks = t // CMP_STRIDE
    n_sel = t // SEL_BLOCK
    far = rel_bias[REL_BUCKETS - 1]
    ar = jnp.arange(tq)
    d_tile = (jnp.arange(2) * tq)[:, None, None] + ar[None, :, None] - ar[None, None, :]
    bt = _keys_by_kv_head(_bias_lookup(rel_bias, d_tile) - far)
    bt = jnp.transpose(bt, (1, 0, 2, 3))
    q_pos = jnp.arange(t).reshape(t // tq, tq)
    cmp_end = jnp.arange(n_chunks) * CMP_STRIDE + CMP_LEN - 1
    bc = _keys_by_kv_head(_bias_lookup(rel_bias, q_pos[:, :, None] - cmp_end[None, None, :]) - far)
    n_idx = np.arange(n_chunks)
    j_idx = np.arange(n_sel)[:, None]
    per = SEL_BLOCK // CMP_STRIDE
    mimp = 0.5 * ((n_idx // per == j_idx).astype(np.float32) + ((n_idx + 1) // per == j_idx).astype(np.float32))
    mimp[:, n_chunks - 1] = 0.0
    col = np.arange(NSA_Q_W) // NSA_HEAD_DIM
    eg = np.zeros((3, NSA_Q_W, NSA_COL_BLOCK), np.float32)
    for br in range(3):
        eg[br, np.arange(NSA_Q_W), br * NSA_HEADS + col] = 1.0
    return bt, bc, jnp.asarray(mimp, jnp.bfloat16), jnp.asarray(eg, jnp.bfloat16)


def _nsa_attn_prompt_t(proj, cmp, tables):
    b, t, _ = proj.shape
    bt, bc, mimp, eg = tables
    tq = ATT_TILE
    cb = NSA_COL_BLOCK
    first_kv = NSA_Q_W // cb

    def kv_spec(slab):
        return pl.BlockSpec((None, t, cb), lambda bi, i: (bi, 0, first_kv + slab))

    def const_spec(a):
        nd = a.ndim
        return pl.BlockSpec(a.shape, lambda bi, i: (0,) * nd)

    def tile_group(tile0):
        n_far = max(tile0 + ATT_TILE_GROUP - 2, 0)
        return pl.pallas_call(
            functools.partial(_nsa_attn_t_kernel, tile0=tile0, n_far=n_far),
            grid=(b, ATT_TILE_GROUP),
            in_specs=[pl.BlockSpec((None, tq, NSA_Q_W), lambda bi, i: (bi, tile0 + i, 0)),
                      pl.BlockSpec((None, tq, cb), lambda bi, i: (bi, tile0 + i, NSA_GATE_BLOCK)),
                      pl.BlockSpec((None,) + cmp.shape[1:], lambda bi, i: (bi, 0, 0, 0)),
                      kv_spec(2), kv_spec(3), kv_spec(4), kv_spec(5),
                      pl.BlockSpec((None,) + bc.shape[1:], lambda bi, i: (tile0 + i, 0, 0, 0)),
                      const_spec(bt), const_spec(mimp), const_spec(eg)],
            out_specs=pl.BlockSpec((None, tq, NSA_Q_W), lambda bi, i: (bi, i, 0)),
            out_shape=jax.ShapeDtypeStruct((b, ATT_TILE_GROUP * tq, NSA_Q_W), jnp.bfloat16),
            scratch_shapes=[pltpu.VMEM((NSA_Q_W, tq), jnp.float32)] * 3 + [pltpu.VMEM((t // tq, 8, ROWS), jnp.float32)],
            compiler_params=pltpu.CompilerParams(
                dimension_semantics=("parallel", "arbitrary"), vmem_limit_bytes=VMEM_LIMIT_BYTES),
            name="nsa_attn",
        )(proj, proj, cmp, proj, proj, proj, proj, bc, bt, mimp, eg)

    parts = [tile_group(tile0) for tile0 in range(0, t // tq, ATT_TILE_GROUP)]
    return jnp.concatenate(parts, axis=1).reshape(b * t, NSA_Q_W)


PAGE_GROUP = 4
SEL_PAD = 8


def _col_softmax_step(kk, vv, qbd, bias, mask, state):
    m, l, acc = state
    s = jnp.dot(kk, qbd, preferred_element_type=jnp.float32)
    if bias is not None:
        s = s + bias
    s = jnp.where(mask, s, NEG_INF)
    m_new = jnp.maximum(m, jnp.max(s, axis=0, keepdims=True))
    alpha = jnp.exp(m - m_new)
    e = jnp.where(mask, jnp.exp(s - m_new), 0.0)
    l = alpha * l + jnp.sum(e, axis=0, keepdims=True)
    acc = alpha * acc + _dot_tn(vv, e.astype(jnp.bfloat16))
    return m_new, l, acc


def _nsa_sample_kernel(pt_ref, *refs):
    n_cmp_in = 4 * PAGE_GROUP
    n_sel_in = 2 * PAGE_GROUP
    cmp_pages = refs[:n_cmp_in]
    sel_pages = refs[n_cmp_in:n_cmp_in + n_sel_in]
    (qbd_ref, new_ref, win_ref, w1_ref, w1f_ref, pos_ref, w2_ref, bcmp_ref, bsel_ref, bwin_ref,
     mimp_ref, hk_ref, bd_ref, rep_ref, gsel_ref, o_ref,
     a_s, kc_s, vc_s, score_s, selh_s, kw_s, vw_s, kn_s, vn_s, ocmp_s, m_s, l_s, acc_s) = refs[n_cmp_in + n_sel_in:]
    del pt_ref
    ph = pl.program_id(1)
    g = pl.program_id(2)
    n_groups = pl.num_programs(2)
    bf16 = jnp.bfloat16
    hd = NSA_HEAD_DIM
    n_chunks = a_s.shape[1]
    n_sel = mimp_ref.shape[1]
    per_page = PAGE_SIZE // CMP_STRIDE
    qbd = qbd_ref[...]

    def heads_out(acc_t, l):
        o = (acc_t / l) * bd_ref[...]
        hi = o.astype(bf16)
        lo = (o - hi.astype(jnp.float32)).astype(bf16)
        return _dot_tn(hi, rep_ref[...]) + _dot_tn(lo, rep_ref[...])

    @pl.when(ph == 0)
    def _():
        for u in range(PAGE_GROUP):
            row0 = pl.multiple_of((g * PAGE_GROUP + u) * per_page, per_page)
            for c in range(4):
                for l in range(CMP_STRIDE):
                    a_s[c, pl.ds(row0, per_page), l * LANES:(l + 1) * LANES] = (
                        cmp_pages[u * 4 + c][pl.ds(l, per_page, stride=CMP_STRIDE), :])

    @pl.when((ph == 0) & (g == n_groups - 1))
    def _():
        for r in range(2):
            pos_b = jnp.dot(pos_ref[r].astype(bf16), w1f_ref[r], preferred_element_type=jnp.float32)
            acc = [jnp.dot(a_s[2 * r + pair].astype(bf16), w1_ref[r], preferred_element_type=jnp.float32)
                   for pair in range(2)]
            dst = kc_s if r == 0 else vc_s
            for k in range(NSA_KV_HEADS):
                cols = (k % 2) * 2 * CMP_HIDDEN
                pa = acc[k // 2][:, cols:cols + CMP_HIDDEN]
                pb_next = pltpu.roll(acc[k // 2][:, cols + CMP_HIDDEN:cols + 2 * CMP_HIDDEN], n_chunks - 1, 0)
                hid = _gelu_tanh(pa + pb_next + pos_b)
                out = jnp.dot(hid.astype(bf16), w2_ref[r], preferred_element_type=jnp.float32)
                dst[:, k * hd:(k + 1) * hd] = out.astype(bf16)
        rows = lax.broadcasted_iota(jnp.int32, (n_chunks, NSA_HEADS), 0)
        mask_c = rows <= n_chunks - 2
        s = jnp.dot(kc_s[...], qbd, preferred_element_type=jnp.float32) + bcmp_ref[...]
        s = jnp.where(mask_c, s, NEG_INF)
        m = jnp.max(s, axis=0, keepdims=True)
        e = jnp.where(mask_c, jnp.exp(s - m), 0.0)
        l = jnp.maximum(jnp.sum(e, axis=0, keepdims=True), 1e-30)
        p = e / l
        ocmp_s[...] = heads_out(_dot_tn(vc_s[...], p.astype(bf16)), jnp.ones_like(l))
        p_kv = sum(jnp.dot(part, hk_ref[...], preferred_element_type=jnp.float32) for part in _split3_bf16(p))
        imp = sum(jnp.dot(mimp_ref[...], part, preferred_element_type=jnp.float32) for part in _split3_bf16(p_kv))
        blk = lax.broadcasted_iota(jnp.int32, imp.shape, 0)
        cur = n_chunks * CMP_STRIDE // SEL_BLOCK
        forced = (blk == 0) | (blk == cur) | (blk == cur - 1)
        score = jnp.where(blk <= cur, jnp.where(forced, FORCE_SCORE, imp), NEG_INF)
        score_s[...] = score

        def rank_body(j, rank):
            row = score_s[pl.ds(j, 1), :]
            beats = (row > score) | ((row == score) & (blk > j))
            return rank + jnp.where(beats, 1.0, 0.0)

        rank = lax.fori_loop(0, cur + 1, rank_body, jnp.zeros(imp.shape, jnp.float32))
        sel = jnp.where((rank < SEL_TOP_N) & (score > 0.5 * NEG_INF), 1.0, 0.0)
        selh_s[...] = jnp.dot(sel.astype(bf16), gsel_ref[...], preferred_element_type=jnp.float32)
        m_s[...] = jnp.full(m_s.shape, NEG_INF, jnp.float32)
        l_s[...] = jnp.zeros(l_s.shape, jnp.float32)
        acc_s[...] = jnp.zeros(acc_s.shape, jnp.float32)

    @pl.when(ph == 1)
    def _():
        state = (m_s[...], l_s[...], acc_s[...])
        half = lax.broadcasted_iota(jnp.int32, (PAGE_SIZE, NSA_HEADS), 0) < SEL_BLOCK
        for u in range(PAGE_GROUP):
            page = g * PAGE_GROUP + u
            pair = selh_s[pl.ds(page * (PAGE_SIZE // SEL_BLOCK), PAGE_SIZE // SEL_BLOCK), :]
            mask = jnp.where(half, pair[0:1, :], pair[1:2, :]) > 0.5
            near = jnp.where(page == n_groups * PAGE_GROUP - 1, 1.0, 0.0)
            state = _col_softmax_step(sel_pages[2 * u][...].astype(bf16), sel_pages[2 * u + 1][...].astype(bf16),
                                      qbd, near * bsel_ref[0:PAGE_SIZE, :], mask, state)
        m_s[...], l_s[...], acc_s[...] = state

    @pl.when((ph == 1) & (g == n_groups - 1))
    def _():
        kn_s[...] = jnp.zeros(kn_s.shape, bf16)
        vn_s[...] = jnp.zeros(vn_s.shape, bf16)
        kv0 = NSA_Q_W
        kn_s[0:1, :] = new_ref[:, kv0 + 2 * NSA_KV_W:kv0 + 3 * NSA_KV_W].astype(bf16)
        vn_s[0:1, :] = new_ref[:, kv0 + 3 * NSA_KV_W:kv0 + 4 * NSA_KV_W].astype(bf16)
        first = lax.broadcasted_iota(jnp.int32, (SEL_PAD, NSA_HEADS), 0) < 1
        state = _col_softmax_step(kn_s[...], vn_s[...], qbd, bsel_ref[PAGE_SIZE:PAGE_SIZE + SEL_PAD, :], first,
                                  (m_s[...], l_s[...], acc_s[...]))
        o_sel = heads_out(state[2], state[1])
        n_buf = win_ref.shape[0]
        kw_s[...] = jnp.zeros(kw_s.shape, bf16)
        vw_s[...] = jnp.zeros(vw_s.shape, bf16)
        kw_s[0:n_buf, :] = win_ref[:, 0:NSA_KV_W].astype(bf16)
        vw_s[0:n_buf, :] = win_ref[:, NSA_KV_W:2 * NSA_KV_W].astype(bf16)
        kw_s[n_buf:n_buf + 1, :] = new_ref[:, kv0 + 4 * NSA_KV_W:kv0 + 5 * NSA_KV_W].astype(bf16)
        vw_s[n_buf:n_buf + 1, :] = new_ref[:, kv0 + 5 * NSA_KV_W:kv0 + 6 * NSA_KV_W].astype(bf16)
        mask_w = lax.broadcasted_iota(jnp.int32, (kw_s.shape[0], NSA_HEADS), 0) <= n_buf
        init = (jnp.full((1, NSA_HEADS), NEG_INF, jnp.float32), jnp.zeros((1, NSA_HEADS), jnp.float32),
                jnp.zeros((NSA_KV_W, NSA_HEADS), jnp.float32))
        state = _col_softmax_step(kw_s[...], vw_s[...], qbd, bwin_ref[...], mask_w, init)
        o_win = heads_out(state[2], state[1])
        gate = jax.nn.sigmoid(new_ref[:, kv0 + 6 * NSA_KV_W:kv0 + 7 * NSA_KV_W])
        out = jnp.zeros(o_ref.shape, jnp.float32)
        for br, o_b in enumerate((ocmp_s[...], o_sel, o_win)):
            onehot = lax.broadcasted_iota(jnp.int32, (NSA_HEADS, NSA_COL_BLOCK), 1) == (
                lax.broadcasted_iota(jnp.int32, (NSA_HEADS, NSA_COL_BLOCK), 0) + br * NSA_HEADS)
            g_col = jnp.sum(jnp.where(onehot, gate, 0.0), axis=1, keepdims=True)
            out = out + g_col * o_b
        o_ref[...] = out


def _nsa_sample_tables(rel_bias, past_len, n_buf):
    far = rel_bias[REL_BUCKETS - 1]
    n_chunks = past_len // CMP_STRIDE
    n_sel = past_len // SEL_BLOCK + 1
    n_sel_pad = _round_up(n_sel, 8)
    cmp_end = jnp.arange(n_chunks) * CMP_STRIDE + CMP_LEN - 1
    bcmp = _bias_lookup(rel_bias, past_len - cmp_end) - far
    k_last = past_len - PAGE_SIZE + jnp.arange(PAGE_SIZE + SEL_PAD)
    bsel = _bias_lookup(rel_bias, past_len - k_last) - far
    n_win = _round_up(n_buf + 1, 8)
    bwin = _bias_lookup(rel_bias, n_buf - jnp.arange(n_win)) - far
    per = SEL_BLOCK // CMP_STRIDE
    n_idx = np.arange(n_chunks)
    j_idx = np.arange(n_sel_pad)[:, None]
    mimp = 0.5 * ((n_idx // per == j_idx).astype(np.float32) + ((n_idx + 1) // per == j_idx).astype(np.float32))
    mimp[:, n_chunks - 1] = 0.0
    heads = np.arange(NSA_HEADS)
    hk = (heads[:, None] // NSA_GROUP == np.arange(NSA_KV_HEADS)[None, :]).astype(np.float32)
    rowk = np.arange(NSA_KV_W) // NSA_HEAD_DIM
    bd = (rowk[:, None] == heads[None, :] // NSA_GROUP).astype(np.float32)
    rep = (np.arange(NSA_KV_W)[:, None] % NSA_HEAD_DIM == np.arange(NSA_HEAD_DIM)[None, :]).astype(np.float32)
    bf16 = jnp.bfloat16
    return (bcmp, bsel, bwin, jnp.asarray(mimp, bf16), jnp.asarray(hk, bf16), jnp.asarray(bd, jnp.float32),
            jnp.asarray(rep, bf16), jnp.asarray(hk.T, bf16))


def _nsa_attn_sample(proj, pages, win_buf, page_table, cmp_pos, w1, w2, tables):
    b = proj.shape[0]
    n_pages = page_table.shape[1]
    past_len = n_pages * PAGE_SIZE
    n_buf = win_buf.shape[1]
    n_chunks = past_len // CMP_STRIDE
    n_groups = n_pages // PAGE_GROUP
    bcmp, bsel, bwin, mimp, hk, bd, rep, gsel = tables
    n_sel_pad = mimp.shape[0]
    n_win = bwin.shape[0]
    w1cat, w1f, pos, w2b = _compress_weights(cmp_pos, w1, w2)
    w1cat = jnp.einsum('rlde,hg->rlhdge', w1cat, jnp.eye(2, dtype=w1cat.dtype)).reshape(
        2, CMP_STRIDE * LANES, 4 * CMP_HIDDEN)
    q = proj[:, :NSA_Q_W].reshape(b, NSA_KV_HEADS, NSA_GROUP, NSA_HEAD_DIM) * (NSA_HEAD_DIM ** -0.5)
    eye = jnp.eye(NSA_KV_HEADS, dtype=q.dtype)
    qbd = jnp.einsum('bkgd,kc->bkdcg', q, eye).reshape(b, NSA_KV_W, NSA_HEADS).astype(jnp.bfloat16)
    proj3 = proj.reshape(b, 1, proj.shape[1])
    win = win_buf.reshape(b, n_buf, 2 * NSA_KV_W)

    def cmp_spec(u, c):
        def imap(i, ph, g, pt):
            gg = jnp.where(ph == 0, g, n_groups - 1)
            return (pt[i, gg * PAGE_GROUP + u], 0, c)
        return pl.BlockSpec((None, PAGE_SIZE, LANES), imap)

    def sel_spec(u, slab):
        def imap(i, ph, g, pt):
            gg = jnp.where(ph == 1, g, 0)
            return (pt[i, gg * PAGE_GROUP + u], 0, slab)
        return pl.BlockSpec((None, PAGE_SIZE, NSA_KV_W), imap)

    def const_spec(a):
        nd = a.ndim
        return pl.BlockSpec(a.shape, lambda i, ph, g, pt: (0,) * nd)

    in_specs = ([cmp_spec(u, c) for u in range(PAGE_GROUP) for c in range(4)]
                + [sel_spec(u, slab) for u in range(PAGE_GROUP) for slab in (2, 3)]
                + [pl.BlockSpec((None, NSA_KV_W, NSA_HEADS), lambda i, ph, g, pt: (i, 0, 0)),
                   pl.BlockSpec((None, 1, proj.shape[1]), lambda i, ph, g, pt: (i, 0, 0)),
                   pl.BlockSpec((None, n_buf, 2 * NSA_KV_W), lambda i, ph, g, pt: (i, 0, 0))]
                + [const_spec(a) for a in (w1cat, w1f, pos, w2b, bcmp, bsel, bwin, mimp, hk, bd, rep, gsel)])
    f32, bf16 = jnp.float32, jnp.bfloat16
    out = pl.pallas_call(
        _nsa_sample_kernel,
        grid_spec=pltpu.PrefetchScalarGridSpec(
            num_scalar_prefetch=1,
            grid=(b, 2, n_groups),
            in_specs=in_specs,
            out_specs=pl.BlockSpec((None, NSA_HEADS, NSA_HEAD_DIM), lambda i, ph, g, pt: (i, 0, 0)),
            scratch_shapes=[pltpu.VMEM((4, n_chunks, CMP_STRIDE * LANES), f32),
                            pltpu.VMEM((n_chunks, NSA_KV_W), bf16), pltpu.VMEM((n_chunks, NSA_KV_W), bf16),
                            pltpu.VMEM((n_sel_pad, NSA_KV_HEADS), f32), pltpu.VMEM((n_sel_pad, NSA_HEADS), f32),
                            pltpu.VMEM((n_win, NSA_KV_W), bf16), pltpu.VMEM((n_win, NSA_KV_W), bf16),
                            pltpu.VMEM((SEL_PAD, NSA_KV_W), bf16), pltpu.VMEM((SEL_PAD, NSA_KV_W), bf16),
                            pltpu.VMEM((NSA_HEADS, NSA_HEAD_DIM), f32),
                            pltpu.VMEM((1, NSA_HEADS), f32), pltpu.VMEM((1, NSA_HEADS), f32),
                            pltpu.VMEM((NSA_KV_W, NSA_HEADS), f32)]),
        out_shape=jax.ShapeDtypeStruct((b, NSA_HEADS, NSA_HEAD_DIM), f32),
        compiler_params=pltpu.CompilerParams(
            dimension_semantics=("parallel", "arbitrary", "arbitrary"), vmem_limit_bytes=VMEM_LIMIT_BYTES),
        name="nsa_sample",
    )(page_table, *([pages] * (6 * PAGE_GROUP)), qbd, proj3, win,
      w1cat, w1f, pos, w2b, bcmp, bsel, bwin, mimp, hk, bd, rep, gsel)
    return out.reshape(b, NSA_Q_W)


SSD_PROJ_W = _round_up(SSD_IN_W, 7 * LANES)
SSD_TILE = 128
SSD_GN = SSD_GROUPS * SSD_STATE
SSD_GW = SSD_HPG * SSD_HEAD_DIM
CONV_PAD = 8


def _silu(x):
    return x * jax.nn.sigmoid(x)


def _softplus(x):
    return jnp.maximum(x, 0.0) + jnp.log(1.0 + jnp.exp(-jnp.abs(x)))


def _dot3(x, table, dot=jnp.dot):
    return sum(dot(part, table, preferred_element_type=jnp.float32) for part in _split3_bf16(x))


def _dot3_tn(x, table):
    return sum(_dot_tn(part, table) for part in _split3_bf16(x))


def _ssd_prompt_kernel(z_ref, x_ref, bc_ref, dt_ref, cw_ref, cb_ref, dtb_ref, alog_ref, d_ref, nw_ref,
                       eh_ref, tril_ref, triu_ref, eye_ref, y_ref, st_ref, conv_ref, ux_s, ubc_s, st_s):
    c = pl.program_id(1)
    t = SSD_TILE
    di = SSD_D_INNER
    bf16 = jnp.bfloat16

    @pl.when(c == 0)
    def _():
        ux_s[0:CONV_PAD, :] = jnp.zeros((CONV_PAD, di), jnp.float32)
        ubc_s[0:CONV_PAD, :] = jnp.zeros((CONV_PAD, 2 * SSD_GN), jnp.float32)
        st_s[...] = jnp.zeros(st_s.shape, jnp.float32)

    ux_s[CONV_PAD:CONV_PAD + t, :] = x_ref[...]
    ubc_s[CONV_PAD:CONV_PAD + t, :] = bc_ref[...]

    def conv(buf, col0, width):
        y = cb_ref[:, col0:col0 + width]
        for i in range(SSD_CONV_W):
            y = y + buf[pl.ds(CONV_PAD - (SSD_CONV_W - 1 - i), t), :] * cw_ref[i:i + 1, col0:col0 + width]
        return _silu(y)

    xs = conv(ux_s, 0, di)
    bcs = conv(ubc_s, di, 2 * SSD_GN)
    dt = _softplus(dt_ref[:, 0:SSD_HEADS] + dtb_ref[...])
    dta = dt * (-jnp.exp(alog_ref[...]))
    cum = _dot3(dta, tril_ref[...], lambda a, b, **kw: jnp.dot(b, a, **kw))
    cum_t = _dot3_tn(dta, triu_ref[...])
    dt_t = _dot3_tn(dt, eye_ref[...])
    last = cum[t - 1:t, :]
    dec_in = _dot3(jnp.exp(cum), eh_ref[...])
    wgt = _dot3(jnp.exp(last - cum) * dt, eh_ref[...])
    st_scale = _dot3(jnp.broadcast_to(jnp.exp(last), (8, SSD_HEADS)), eh_ref[...])[0:1, :]
    causal = lax.broadcasted_iota(jnp.int32, (t, t), 1) <= lax.broadcasted_iota(jnp.int32, (t, t), 0)

    for g in range(SSD_GROUPS):
        gl = slice(g * SSD_GW, (g + 1) * SSD_GW)
        bg = bcs[:, g * SSD_STATE:(g + 1) * SSD_STATE].astype(bf16)
        cg = bcs[:, SSD_GN + g * SSD_STATE:SSD_GN + (g + 1) * SSD_STATE].astype(bf16)
        cb = _dot_nt(cg, bg)
        xg = xs[:, gl]
        y_heads = []
        for j in range(SSD_HPG):
            h = g * SSD_HPG + j
            decay = jnp.exp(jnp.minimum(cum[:, h:h + 1] - cum_t[h:h + 1, :], 0.0))
            w = jnp.where(causal, cb * decay * dt_t[h:h + 1, :], 0.0).astype(bf16)
            y_heads.append(jnp.dot(w, xg[:, j * SSD_HEAD_DIM:(j + 1) * SSD_HEAD_DIM].astype(bf16),
                                   preferred_element_type=jnp.float32))
        st = st_s[g]
        y = jnp.concatenate(y_heads, axis=1)
        y = y + jnp.dot(cg, st.astype(bf16), preferred_element_type=jnp.float32) * dec_in[:, gl]
        st_s[g] = st_scale[:, gl] * st + _dot_tn(bg, (xg * wgt[:, gl]).astype(bf16))
        y = (y + d_ref[:, gl] * xg) * _silu(z_ref[:, gl])
        y = y * lax.rsqrt(jnp.mean(y * y, axis=-1, keepdims=True) + NORM_EPS) * nw_ref[:, gl]
        y_ref[:, gl] = y.astype(y_ref.dtype)

    ux_s[0:CONV_PAD, :] = ux_s[t:t + CONV_PAD, :]
    ubc_s[0:CONV_PAD, :] = ubc_s[t:t + CONV_PAD, :]

    @pl.when(c == pl.num_programs(1) - 1)
    def _():
        st_ref[...] = st_s[...]
        keep = SSD_CONV_W - 1
        conv_ref[:, 0:di] = ux_s[CONV_PAD - keep:CONV_PAD, :]
        conv_ref[:, di:] = ubc_s[CONV_PAD - keep:CONV_PAD, :]


def _ssd_tables():
    eh = (np.arange(SSD_HEADS)[:, None] == np.arange(SSD_D_INNER)[None, :] // SSD_HEAD_DIM).astype(np.float32)
    tril = np.tril(np.ones((SSD_TILE, SSD_TILE), np.float32))
    bf16 = jnp.bfloat16
    return (jnp.asarray(eh, bf16), jnp.asarray(tril, bf16), jnp.asarray(tril.T, bf16),
            jnp.asarray(np.eye(SSD_TILE, dtype=np.float32), bf16))


def _state_from_transposed(st_t):
    b = st_t.shape[0]
    st = st_t.reshape(b, SSD_GROUPS, SSD_STATE, SSD_HPG, SSD_HEAD_DIM)
    return jnp.transpose(st, (0, 1, 3, 4, 2)).reshape(b, SSD_HEADS, SSD_HEAD_DIM, SSD_STATE)


def _ssd_prompt(proj, conv_w, conv_b, dt_bias, a_log, d_skip, norm_w):
    b, t, _ = proj.shape
    di = SSD_D_INNER
    tt = SSD_TILE
    d_exp = jnp.repeat(d_skip, SSD_HEAD_DIM).reshape(1, di)
    consts = (conv_w, conv_b.reshape(1, -1), dt_bias.reshape(1, -1), a_log.reshape(1, -1), d_exp,
              norm_w.reshape(1, di)) + _ssd_tables()

    def const_spec(a):
        nd = a.ndim
        return pl.BlockSpec(a.shape, lambda i, c: (0,) * nd)

    y, st_t, conv_new = pl.pallas_call(
        _ssd_prompt_kernel,
        grid=(b, t // tt),
        in_specs=[pl.BlockSpec((None, tt, di), lambda i, c: (i, c, 0)),
                  pl.BlockSpec((None, tt, di), lambda i, c: (i, c, 1)),
                  pl.BlockSpec((None, tt, 2 * SSD_GN), lambda i, c: (i, c, 2)),
                  pl.BlockSpec((None, tt, LANES), lambda i, c: (i, c, (di + SSD_CONV_DIM) // LANES))]
                 + [const_spec(a) for a in consts],
        out_specs=[pl.BlockSpec((None, tt, di), lambda i, c: (i, c, 0)),
                   pl.BlockSpec((None, SSD_GROUPS, SSD_STATE, SSD_GW), lambda i, c: (i, 0, 0, 0)),
                   pl.BlockSpec((None, SSD_CONV_W - 1, SSD_CONV_DIM), lambda i, c: (i, 0, 0))],
        out_shape=[jax.ShapeDtypeStruct((b, t, di), jnp.bfloat16),
                   jax.ShapeDtypeStruct((b, SSD_GROUPS, SSD_STATE, SSD_GW), jnp.float32),
                   jax.ShapeDtypeStruct((b, SSD_CONV_W - 1, SSD_CONV_DIM), jnp.float32)],
        scratch_shapes=[pltpu.VMEM((CONV_PAD + tt, di), jnp.float32),
                        pltpu.VMEM((CONV_PAD + tt, 2 * SSD_GN), jnp.float32),
                        pltpu.VMEM((SSD_GROUPS, SSD_STATE, SSD_GW), jnp.float32)],
        compiler_params=pltpu.CompilerParams(
            dimension_semantics=("parallel", "arbitrary"), vmem_limit_bytes=VMEM_LIMIT_BYTES),
        name="ssd_prompt",
    )(proj, proj, proj, proj, *consts)
    return y.reshape(b * t, di), _state_from_transposed(st_t), conv_new


ROW_PAD = 8


def _row8(x):
    return jnp.concatenate([x, jnp.zeros((ROW_PAD - 1, x.shape[1]), x.dtype)], axis=0)


def _ssd_step_kernel(p_ref, conv_ref, st_ref, cw_ref, cb_ref, dtb_ref, alog_ref, d_ref, nw_ref,
                     y_ref, st_out, conv_out, y_s):
    di = SSD_D_INNER
    bf16 = jnp.bfloat16
    u = p_ref[:, di:di + SSD_CONV_DIM]
    keep = SSD_CONV_W - 1
    y = cb_ref[...] + u * cw_ref[keep:keep + 1, :]
    for i in range(keep):
        y = y + conv_ref[i:i + 1, :] * cw_ref[i:i + 1, :]
    conv_out[0:keep - 1, :] = conv_ref[1:keep, :]
    conv_out[keep - 1:keep, :] = u
    xbc = _silu(y)
    xs = xbc[:, :di]
    dt = _softplus(p_ref[:, di + SSD_CONV_DIM:di + SSD_CONV_DIM + SSD_HEADS] + dtb_ref[...])
    decay = jnp.exp(dt * (-jnp.exp(alog_ref[...])))
    for g in range(SSD_GROUPS):
        bg = _row8(xbc[:, di + g * SSD_STATE:di + (g + 1) * SSD_STATE]).astype(bf16)
        cg = _row8(xbc[:, di + SSD_GN + g * SSD_STATE:di + SSD_GN + (g + 1) * SSD_STATE]).astype(bf16)
        for j in range(SSD_HPG):
            h = g * SSD_HPG + j
            cols = slice(h * SSD_HEAD_DIM, (h + 1) * SSD_HEAD_DIM)
            xh = _row8(xs[:, cols] * dt[:, h:h + 1]).astype(bf16)
            st = decay[:, h:h + 1] * st_ref[h] + _dot_tn(xh, bg)
            st_out[h] = st
            y_s[:, cols] = _dot_nt(cg, st.astype(bf16))
    yv = y_s[0:1, :]
    yv = (yv + d_ref[...] * xs) * _silu(p_ref[:, 0:di])
    for g in range(SSD_GROUPS):
        gl = slice(g * SSD_GW, (g + 1) * SSD_GW)
        yg = yv[:, gl]
        y_ref[:, gl] = yg * lax.rsqrt(jnp.mean(yg * yg, axis=-1, keepdims=True) + NORM_EPS) * nw_ref[:, gl]


def _ssd_step(proj, ssm0, conv0, conv_w, conv_b, dt_bias, a_log, d_skip, norm_w):
    b = proj.shape[0]
    di = SSD_D_INNER
    consts = (conv_w, conv_b.reshape(1, -1), dt_bias.reshape(1, -1), a_log.reshape(1, -1),
              jnp.repeat(d_skip, SSD_HEAD_DIM).reshape(1, di), norm_w.reshape(1, di))

    def const_spec(a):
        nd = a.ndim
        return pl.BlockSpec(a.shape, lambda i: (0,) * nd)

    y, st, conv_new = pl.pallas_call(
        _ssd_step_kernel,
        grid=(b,),
        in_specs=[pl.BlockSpec((None, 1, proj.shape[1]), lambda i: (i, 0, 0)),
                  pl.BlockSpec((None,) + conv0.shape[1:], lambda i: (i, 0, 0)),
                  pl.BlockSpec((None,) + ssm0.shape[1:], lambda i: (i, 0, 0, 0))]
                 + [const_spec(a) for a in consts],
        out_specs=[pl.BlockSpec((None, 1, di), lambda i: (i, 0, 0)),
                   pl.BlockSpec((None,) + ssm0.shape[1:], lambda i: (i, 0, 0, 0)),
                   pl.BlockSpec((None,) + conv0.shape[1:], lambda i: (i, 0, 0))],
        out_shape=[jax.ShapeDtypeStruct((b, 1, di), jnp.float32),
                   jax.ShapeDtypeStruct(ssm0.shape, jnp.float32),
                   jax.ShapeDtypeStruct(conv0.shape, jnp.float32)],
        scratch_shapes=[pltpu.VMEM((ROW_PAD, di), jnp.float32)],
        compiler_params=pltpu.CompilerParams(
            dimension_semantics=("parallel",), vmem_limit_bytes=VMEM_LIMIT_BYTES),
        name="ssd_step",
    )(proj.reshape(b, 1, -1), conv0, ssm0, *consts)
    return y.reshape(b, di), st, conv_new


def _hgrn_gates(p_ref, lb_ref):
    wk = HG_HEADS * HG_DK
    q = _silu(p_ref[:, 0:wk])
    f = lb_ref[...] + (1.0 - lb_ref[...]) * jax.nn.sigmoid(p_ref[:, wk:2 * wk])
    return q, f


HG_TILE = 128
HG_SUB = 16


def _hgrn_prompt_kernel(p_ref, lb_ref, gn_ref, tril_ref, subend_ref, ones_ref, y_ref, st_ref, st_s):
    c = pl.program_id(1)
    t = HG_TILE
    wk = HG_HEADS * HG_DK
    wv = HG_HEADS * HG_DV
    bf16 = jnp.bfloat16
    n_sub = t // HG_SUB

    @pl.when(c == 0)
    def _():
        st_s[...] = jnp.zeros(st_s.shape, jnp.float32)

    row = lax.broadcasted_iota(jnp.int32, (t, HG_DK), 0)
    sub_pos = row % HG_SUB
    row_sub = lax.broadcasted_iota(jnp.int32, (t, t), 0) // HG_SUB
    col_sub = lax.broadcasted_iota(jnp.int32, (t, t), 1) // HG_SUB
    left = lambda a, b, **kw: jnp.dot(b, a, **kw)

    def head(h, carry):
        kc = pl.ds(pl.multiple_of(h * HG_DK, HG_DK), HG_DK)
        q = _silu(p_ref[:, kc])
        lb = lb_ref[:, kc]
        f = lb + (1.0 - lb) * jax.nn.sigmoid(p_ref[:, pl.ds(pl.multiple_of(wk + h * HG_DK, HG_DK), HG_DK)])
        k = 1.0 - f
        v = p_ref[:, pl.ds(pl.multiple_of(2 * wk + h * HG_DV, HG_DV), HG_DV)]
        gate = p_ref[:, pl.ds(pl.multiple_of(2 * wk + wv + h * HG_DV, HG_DV), HG_DV)]
        cum = _dot3(jnp.log(f), tril_ref[...], left)
        sub_end = _dot3(cum, subend_ref[...], left)
        k_hat = k * jnp.exp(sub_end - cum)
        a_off = jnp.zeros((t, t), jnp.float32)
        for j in range(n_sub - 1):
            end_j = cum[(j + 1) * HG_SUB - 1:(j + 1) * HG_SUB, :]
            q_j = (q * jnp.exp(jnp.minimum(cum - end_j, 0.0))).astype(bf16)
            k_j = jnp.where(row // HG_SUB == j, k_hat, 0.0).astype(bf16)
            a_off = a_off + _dot_nt(q_j, k_j)
        a_off = jnp.where(col_sub < row_sub, a_off, 0.0)
        v16 = v.astype(bf16)
        o = jnp.dot(a_off.astype(bf16), v16, preferred_element_type=jnp.float32)
        for d in range(HG_SUB):
            k_d, cum_d, v_d = (k, cum, v) if d == 0 else (pltpu.roll(k, d, 0), pltpu.roll(cum, d, 0), pltpu.roll(v, d, 0))
            e = (q * k_d * jnp.exp(jnp.minimum(cum - cum_d, 0.0))).astype(bf16)
            a_d = jnp.dot(e, ones_ref[...], preferred_element_type=jnp.float32)
            o = o + jnp.where(sub_pos >= d, a_d, 0.0) * v_d
        st = st_s[h]
        o = o + _dot_nt((q * jnp.exp(cum)).astype(bf16), st.astype(bf16))
        last = cum[t - 1:t, :]
        st_s[h] = st * jnp.exp(last) + _dot_tn(v16, (k * jnp.exp(last - cum)).astype(bf16))
        o = o * lax.rsqrt(jnp.mean(o * o, axis=-1, keepdims=True) + NORM_EPS) * gn_ref[...]
        y_ref[:, pl.ds(pl.multiple_of(h * HG_DV, HG_DV), HG_DV)] = (o * _silu(gate)).astype(y_ref.dtype)
        return carry

    lax.fori_loop(0, HG_HEADS, head, 0)

    @pl.when(c == pl.num_programs(1) - 1)
    def _():
        st_ref[...] = st_s[...]


def _hgrn_prompt(proj, lb, g_norm):
    b, t, w = proj.shape
    tt = HG_TILE
    wv = HG_HEADS * HG_DV
    idx = np.arange(tt)
    tril = np.tril(np.ones((tt, tt), np.float32))
    subend = (idx[None, :] == (idx[:, None] // HG_SUB) * HG_SUB + HG_SUB - 1).astype(np.float32)
    bf16 = jnp.bfloat16
    consts = (lb.reshape(1, -1), g_norm.reshape(1, -1), jnp.asarray(tril, bf16), jnp.asarray(subend, bf16),
              jnp.ones((HG_DK, HG_DK), bf16))

    def const_spec(a):
        nd = a.ndim
        return pl.BlockSpec(a.shape, lambda i, c: (0,) * nd)

    y, st_t = pl.pallas_call(
        _hgrn_prompt_kernel,
        grid=(b, t // tt),
        in_specs=[pl.BlockSpec((None, tt, w), lambda i, c: (i, c, 0))] + [const_spec(a) for a in consts],
        out_specs=[pl.BlockSpec((None, tt, wv), lambda i, c: (i, c, 0)),
                   pl.BlockSpec((None, HG_HEADS, HG_DV, HG_DK), lambda i, c: (i, 0, 0, 0))],
        out_shape=[jax.ShapeDtypeStruct((b, t, wv), bf16),
                   jax.ShapeDtypeStruct((b, HG_HEADS, HG_DV, HG_DK), jnp.float32)],
        scratch_shapes=[pltpu.VMEM((HG_HEADS, HG_DV, HG_DK), jnp.float32)],
        compiler_params=pltpu.CompilerParams(
            dimension_semantics=("parallel", "arbitrary"), vmem_limit_bytes=VMEM_LIMIT_BYTES),
        name="hgrn_prompt",
    )(proj, *consts)
    return y.reshape(b * t, wv), jnp.swapaxes(st_t, 2, 3)


def _hgrn_step_kernel(p_ref, st_ref, lb_ref, gn_ref, y_ref, st_out):
    wk = HG_HEADS * HG_DK
    wv = HG_HEADS * HG_DV
    bf16 = jnp.bfloat16
    q, f = _hgrn_gates(p_ref, lb_ref)
    eye = lax.broadcasted_iota(jnp.int32, (HG_DK, HG_DK), 0) == lax.broadcasted_iota(jnp.int32, (HG_DK, HG_DK), 1)
    for h in range(HG_HEADS):
        kc = slice(h * HG_DK, (h + 1) * HG_DK)
        vc = slice(2 * wk + h * HG_DV, 2 * wk + (h + 1) * HG_DV)
        gc = slice(2 * wk + wv + h * HG_DV, 2 * wk + wv + (h + 1) * HG_DV)
        fh = f[:, kc]
        f_col = jnp.sum(jnp.where(eye, fh, 0.0), axis=1, keepdims=True)
        kv = _dot_tn(_row8(1.0 - fh).astype(bf16), _row8(p_ref[:, vc]).astype(bf16))
        st = f_col * st_ref[h] + kv
        st_out[h] = st
        o = jnp.dot(_row8(q[:, kc]).astype(bf16), st.astype(bf16), preferred_element_type=jnp.float32)[0:1, :]
        o = o * lax.rsqrt(jnp.mean(o * o, axis=-1, keepdims=True) + NORM_EPS) * gn_ref[...]
        y_ref[:, h * HG_DV:(h + 1) * HG_DV] = o * _silu(p_ref[:, gc])


def _hgrn_step(proj, s0, lb, g_norm):
    b = proj.shape[0]
    wv = HG_HEADS * HG_DV
    y, st = pl.pallas_call(
        _hgrn_step_kernel,
        grid=(b,),
        in_specs=[pl.BlockSpec((None, 1, proj.shape[1]), lambda i: (i, 0, 0)),
                  pl.BlockSpec((None,) + s0.shape[1:], lambda i: (i, 0, 0, 0)),
                  pl.BlockSpec((1, HG_HEADS * HG_DK), lambda i: (0, 0)),
                  pl.BlockSpec((1, HG_DV), lambda i: (0, 0))],
        out_specs=[pl.BlockSpec((None, 1, wv), lambda i: (i, 0, 0)),
                   pl.BlockSpec((None,) + s0.shape[1:], lambda i: (i, 0, 0, 0))],
        out_shape=[jax.ShapeDtypeStruct((b, 1, wv), jnp.float32), jax.ShapeDtypeStruct(s0.shape, jnp.float32)],
        compiler_params=pltpu.CompilerParams(
            dimension_semantics=("parallel",), vmem_limit_bytes=VMEM_LIMIT_BYTES),
        name="hgrn_step",
    )(proj.reshape(b, 1, -1), s0, lb.reshape(1, -1), g_norm.reshape(1, -1))
    return y.reshape(b, wv), st


def _rel_bucket(dist):
    n = jnp.maximum(dist, 0)
    n_exact = REL_BUCKETS // 2
    nf = jnp.maximum(n, 1).astype(jnp.float32)
    large = n_exact + (jnp.log(nf / n_exact) / math.log(REL_MAX_DIST / n_exact)
                       * (REL_BUCKETS - n_exact)).astype(jnp.int32)
    return jnp.where(n < n_exact, n, jnp.minimum(large, REL_BUCKETS - 1))


def _nsa_prompt_core(proj, cmp_pos, cmp_w1, cmp_w2, tables):
    b, t, _ = proj.shape
    cmp = _nsa_compress_prompt(proj, cmp_pos, cmp_w1, cmp_w2)
    merged = _nsa_attn_prompt_t(proj, cmp, tables)
    o1 = NSA_Q_W
    o2 = o1 + 4 * NSA_KV_W
    o3 = o2 + 2 * NSA_KV_W
    kv_cs = proj[..., o1:o2].reshape(b, t, 4, NSA_KV_HEADS, NSA_HEAD_DIM)
    kv_win = proj[:, t - min(WINDOW, t):, o2:o3].reshape(b, min(WINDOW, t), 2, NSA_KV_HEADS, NSA_HEAD_DIM)
    return merged, kv_cs, kv_win


def _nsa_sample_core(proj, kv_pages, win_buf, page_table, cmp_pos, cmp_w1, cmp_w2, tables):
    b, t, _ = proj.shape
    assert t == 1 and win_buf.shape[1] == WINDOW and page_table.shape[1] % PAGE_GROUP == 0
    merged = _nsa_attn_sample(proj.reshape(b, -1), kv_pages, win_buf, page_table, cmp_pos, cmp_w1, cmp_w2, tables)
    o1 = NSA_Q_W
    o2 = o1 + 4 * NSA_KV_W
    o3 = o2 + 2 * NSA_KV_W
    kv_cs = proj[..., o1:o2].reshape(b, t, 4, NSA_KV_HEADS, NSA_HEAD_DIM)
    kv_win = proj[..., o2:o3].reshape(b, t, 2, NSA_KV_HEADS, NSA_HEAD_DIM)
    new_win = jnp.concatenate([win_buf[:, t:], kv_win], axis=1)
    return merged, kv_cs, new_win


def _pad_cols(w, n):
    return jnp.pad(w, ((0, 0), (0, n - w.shape[1])))


def kernel(x_prompt, x_sample, cache_nsa_kv, cache_nsa_win, state_hgrn, state_ssd, state_ssd_conv, page_table, c_prompt, c_sample, rel_bias, hgrn_lower_bounds, w_ada, b_ada, norm_gains, w_mlp_in, w_mlp_out, nsa_w_in, nsa_cmp_pos, nsa_cmp_w1, nsa_cmp_w2, nsa_w_out, hg_w_in, hg_norm, hg_w_out, ssd_w_in, ssd_conv_w, ssd_conv_b, ssd_dt_bias, ssd_a_log, ssd_d, ssd_norm, ssd_w_out):
    bf16 = jnp.bfloat16
    bp, tp, d = x_prompt.shape
    bs, ts, _ = x_sample.shape
    mp, ms = bp * tp, bs * ts
    lb_p = jax.nn.softmax(hgrn_lower_bounds, axis=0)
    lower_bounds = jnp.cumsum(lb_p, axis=0) - lb_p[0]

    mod = _ada_all(jnp.concatenate([c_prompt, c_sample], axis=0), w_ada, b_ada)
    mod = mod.reshape(DEPTH, bp + bs, ADA_CHUNKS, d)
    mod_p = mod[:, :bp].transpose(0, 2, 1, 3)[:, :, :, None, :]
    mod_s = mod[:, bp:].transpose(0, 2, 1, 3)[:, :, None, :, :]

    xp = x_prompt.reshape(mp, d)
    xs = x_sample.reshape(ms, d)
    tm_p, tm_s = PROMPT_ROW_TILE, ms
    nsa_tables = _nsa_prompt_tables_t(rel_bias, tp)
    nsa_tables_s = _nsa_sample_tables(rel_bias, page_table.shape[1] * PAGE_SIZE, cache_nsa_win.shape[2])
    n_phys = cache_nsa_kv.shape[1]
    kv_pages = cache_nsa_kv.reshape(cache_nsa_kv.shape[0] * n_phys, PAGE_SIZE, 4 * NSA_KV_W)

    kv_p, kv_s, win_p, win_s = [], [], [], []
    hg_p, hg_s, ssd_p, ssd_s, conv_p, conv_s = [], [], [], [], [], []
    for i in range(DEPTH):
        j = i // N_MIXERS
        kind = i % N_MIXERS
        g = norm_gains[i]
        shp_m, scp_m, gtp_m, shp_f, scp_f, gtp_f = [mod_p[i, c] for c in range(ADA_CHUNKS)]
        shs_m, scs_m, gts_m, shs_f, scs_f, gts_f = [mod_s[i, c] for c in range(ADA_CHUNKS)]
        if kind == 0:
            n_pad = NSA_PROJ_W
            w_in = _pad_cols(nsa_w_in[j], n_pad).astype(bf16)
            w_out = nsa_w_out[j].astype(bf16)
            pp = _norm_mod_matmul(xp, g[0], scp_m, shp_m, w_in, tp, tm_p).reshape(bp, tp, n_pad)
            ps = _norm_mod_matmul(xs, g[0], scs_m, shs_m, w_in, ts, tm_s).reshape(bs, ts, n_pad)
            ap, new_kv_p, new_win_p = _nsa_prompt_core(pp, nsa_cmp_pos[j], nsa_cmp_w1[j], nsa_cmp_w2[j], nsa_tables)
            as_, new_kv_s, new_win_s = _nsa_sample_core(ps, kv_pages, cache_nsa_win[j], page_table + j * n_phys,
                                                        nsa_cmp_pos[j], nsa_cmp_w1[j], nsa_cmp_w2[j], nsa_tables_s)
            kv_p.append(new_kv_p)
            kv_s.append(new_kv_s)
            win_p.append(new_win_p)
            win_s.append(new_win_s)
        elif kind == 1:
            w_in = hg_w_in[j].astype(bf16)
            w_out = hg_w_out[j].astype(bf16)
            pp = _norm_mod_matmul(xp, g[0], scp_m, shp_m, w_in, tp, tm_p).reshape(bp, tp, -1)
            ps = _norm_mod_matmul(xs, g[0], scs_m, shs_m, w_in, ts, tm_s).reshape(bs, ts, -1)
            ap, new_hp = _hgrn_prompt(pp, lower_bounds[i], hg_norm[j])
            as_, new_hs = _hgrn_step(ps.reshape(bs, -1), state_hgrn[j], lower_bounds[i], hg_norm[j])
            hg_p.append(new_hp)
            hg_s.append(new_hs)
        else:
            n_pad = SSD_PROJ_W
            w_in = _pad_cols(ssd_w_in[j], n_pad).astype(bf16)
            w_out = ssd_w_out[j].astype(bf16)
            pp = _norm_mod_matmul(xp, g[0], scp_m, shp_m, w_in, tp, tm_p).reshape(bp, tp, n_pad)
            ps = _norm_mod_matmul(xs, g[0], scs_m, shs_m, w_in, ts, tm_s).reshape(bs, ts, n_pad)
            ap, new_sp, new_cp = _ssd_prompt(pp, ssd_conv_w[j], ssd_conv_b[j], ssd_dt_bias[j],
                                             ssd_a_log[j], ssd_d[j], ssd_norm[j])
            as_, new_ss, new_cs = _ssd_step(ps.reshape(bs, -1), state_ssd[j], state_ssd_conv[j], ssd_conv_w[j],
                                            ssd_conv_b[j], ssd_dt_bias[j], ssd_a_log[j], ssd_d[j], ssd_norm[j])
            ssd_p.append(new_sp)
            ssd_s.append(new_ss)
            conv_p.append(new_cp)
            conv_s.append(new_cs)
        xp = _matmul_norm_res(ap, w_out, xp, g[1], gtp_m, tp, tm_p)
        xs = _matmul_norm_res(as_, w_out, xs, g[1], gts_m, ts, tm_s)
        w1 = w_mlp_in[i].astype(bf16)
        w2 = w_mlp_out[i].astype(bf16)
        xp = _mlp(xp, g[2], scp_f, shp_f, w1, w2, g[3], gtp_f, tp, tm_p)
        xs = _mlp(xs, g[2], scs_f, shs_f, w1, w2, g[3], gts_f, ts, tm_s)
    return (xp.reshape(bp, tp, d), xs.reshape(bs, ts, d),
            jnp.stack(kv_p), jnp.stack(kv_s), jnp.stack(win_p), jnp.stack(win_s),
            jnp.stack(hg_p), jnp.stack(hg_s), jnp.stack(ssd_p), jnp.stack(ssd_s),
            jnp.stack(conv_p), jnp.stack(conv_s))
```

```python
import functools
import math

import jax
import jax.numpy as jnp
import numpy as np
from jax import lax
from jax.experimental import pallas as pl
from jax.experimental.pallas import tpu as pltpu

D_MODEL = 1024
DEPTH = 4
PAGE_SIZE = 128
N_MIXERS = 3
ADA_CHUNKS = 6
NORM_EPS = 1e-6
D_FF = 4 * D_MODEL

NSA_HEADS = 16
NSA_HEAD_DIM = D_MODEL // NSA_HEADS
NSA_KV_HEADS = 4
NSA_GROUP = NSA_HEADS // NSA_KV_HEADS
CMP_STRIDE = 16
CMP_LEN = 2 * CMP_STRIDE
CMP_HIDDEN = 2 * NSA_HEAD_DIM
SEL_BLOCK = 64
SEL_TOP_N = 16
WINDOW = 512
WIN_Q_BLOCK = 128
SEL_Q_BLOCK = 16
NSA_Q_W = NSA_HEADS * NSA_HEAD_DIM
NSA_KV_W = NSA_KV_HEADS * NSA_HEAD_DIM
NSA_IN_W = NSA_Q_W + 6 * NSA_KV_W + 3 * NSA_HEADS

REL_BUCKETS = 32
REL_MAX_DIST = 128

HG_EXPAND = 128
HG_HEADS = D_MODEL // HG_EXPAND
HG_DK = HG_EXPAND
HG_DV = D_MODEL // HG_HEADS
HG_CHUNK = 64

SSD_D_INNER = 2 * D_MODEL
SSD_HEAD_DIM = 64
SSD_HEADS = SSD_D_INNER // SSD_HEAD_DIM
SSD_GROUPS = 8
SSD_HPG = SSD_HEADS // SSD_GROUPS
SSD_STATE = 128
SSD_CONV_W = 4
SSD_CONV_DIM = SSD_D_INNER + 2 * SSD_GROUPS * SSD_STATE
SSD_IN_W = SSD_D_INNER + SSD_CONV_DIM + SSD_HEADS
SSD_CHUNK = 128

NEG_INF = -1e30
FORCE_SCORE = 1e4

LANES = 128
VMEM_LIMIT_BYTES = 48 * 1024 * 1024
PROMPT_ROW_TILE = 512


def _round_up(n, m):
    return -(-n // m) * m


def _col_tile(n, cap=1536):
    best = LANES
    for t in range(LANES, cap + 1, LANES):
        if n % t == 0:
            best = t
    return best


def _rms(x, g):
    return x * lax.rsqrt(jnp.mean(x * x, axis=-1, keepdims=True) + NORM_EPS) * g


def _mod_spec(mod, rows_per_mod, tm, ngrid):
    r = mod.shape[1]
    if r == 1:
        per = rows_per_mod // tm
        if ngrid == 1:
            return pl.BlockSpec((None, 1, mod.shape[2]), lambda i: (i // per, 0, 0))
        return pl.BlockSpec((None, 1, mod.shape[2]), lambda i, j: (i // per, 0, 0))
    if ngrid == 1:
        return pl.BlockSpec((None, r, mod.shape[2]), lambda i: (0, 0, 0))
    return pl.BlockSpec((None, r, mod.shape[2]), lambda i, j: (0, 0, 0))


def _ada_kernel(c_ref, w_ref, b_ref, o_ref):
    c = c_ref[...]
    s = (c * jax.nn.sigmoid(c)).astype(jnp.bfloat16)
    o_ref[...] = jnp.dot(s, w_ref[...].astype(jnp.bfloat16),
                         preferred_element_type=jnp.float32) + b_ref[...]


def _ada_all(c_all, w_ada, b_ada):
    rows = c_all.shape[0]
    n = ADA_CHUNKS * D_MODEL
    tn = 1024
    return pl.pallas_call(
        _ada_kernel,
        grid=(DEPTH, n // tn),
        in_specs=[pl.BlockSpec((rows, D_MODEL), lambda l, j: (0, 0)),
                  pl.BlockSpec((None, D_MODEL, tn), lambda l, j: (l, 0, j)),
                  pl.BlockSpec((None, 1, tn), lambda l, j: (l, 0, j))],
        out_specs=pl.BlockSpec((None, rows, tn), lambda l, j: (l, 0, j)),
        out_shape=jax.ShapeDtypeStruct((DEPTH, rows, n), jnp.float32),
        compiler_params=pltpu.CompilerParams(
            dimension_semantics=("parallel", "parallel"), vmem_limit_bytes=VMEM_LIMIT_BYTES),
        name="ada",
    )(c_all, w_ada, b_ada.reshape(DEPTH, 1, n))


def _norm_mod_matmul_kernel(x_ref, g_ref, sc_ref, sh_ref, w_ref, o_ref, h_ref):
    @pl.when(pl.program_id(1) == 0)
    def _():
        h = _rms(x_ref[...], g_ref[...]) * (1.0 + sc_ref[...]) + sh_ref[...]
        h_ref[...] = h.astype(jnp.bfloat16)

    o_ref[...] = jnp.dot(h_ref[...], w_ref[...], preferred_element_type=jnp.float32)


def _norm_mod_matmul(x, g, sc, sh, w, rows_per_mod, tm):
    m, d = x.shape
    n = w.shape[1]
    tn = _col_tile(n)
    return pl.pallas_call(
        _norm_mod_matmul_kernel,
        grid=(m // tm, n // tn),
        in_specs=[pl.BlockSpec((tm, d), lambda i, j: (i, 0)),
                  pl.BlockSpec((1, d), lambda i, j: (0, 0)),
                  _mod_spec(sc, rows_per_mod, tm, 2),
                  _mod_spec(sh, rows_per_mod, tm, 2),
                  pl.BlockSpec((d, tn), lambda i, j: (0, j))],
        out_specs=pl.BlockSpec((tm, tn), lambda i, j: (i, j)),
        out_shape=jax.ShapeDtypeStruct((m, n), jnp.float32),
        scratch_shapes=[pltpu.VMEM((tm, d), jnp.bfloat16)],
        compiler_params=pltpu.CompilerParams(
            dimension_semantics=("parallel", "arbitrary"), vmem_limit_bytes=VMEM_LIMIT_BYTES),
        name="norm_mod_matmul",
    )(x, g.reshape(1, d), sc, sh, w)


def _matmul_norm_res_kernel(a_ref, w_ref, x_ref, g_ref, gt_ref, o_ref):
    y = jnp.dot(a_ref[...].astype(jnp.bfloat16), w_ref[...], preferred_element_type=jnp.float32)
    o_ref[...] = x_ref[...] + gt_ref[...] * _rms(y, g_ref[...])


def _matmul_norm_res(a, w, x, g, gate, rows_per_mod, tm):
    m, k = a.shape
    d = w.shape[1]
    return pl.pallas_call(
        _matmul_norm_res_kernel,
        grid=(m // tm,),
        in_specs=[pl.BlockSpec((tm, k), lambda i: (i, 0)),
                  pl.BlockSpec((k, d), lambda i: (0, 0)),
                  pl.BlockSpec((tm, d), lambda i: (i, 0)),
                  pl.BlockSpec((1, d), lambda i: (0, 0)),
                  _mod_spec(gate, rows_per_mod, tm, 1)],
        out_specs=pl.BlockSpec((tm, d), lambda i: (i, 0)),
        out_shape=jax.ShapeDtypeStruct((m, d), jnp.float32),
        compiler_params=pltpu.CompilerParams(
            dimension_semantics=("parallel",), vmem_limit_bytes=VMEM_LIMIT_BYTES),
        name="matmul_norm_res",
    )(a, w, x, g.reshape(1, d), gate)


def _mlp_kernel(x_ref, g2_ref, sc_ref, sh_ref, w1_ref, w2_ref, g3_ref, gt_ref, o_ref, h_ref, acc_ref):
    j = pl.program_id(1)

    @pl.when(j == 0)
    def _():
        h = _rms(x_ref[...], g2_ref[...]) * (1.0 + sc_ref[...]) + sh_ref[...]
        h_ref[...] = h.astype(jnp.bfloat16)

    u = jnp.dot(h_ref[...], w1_ref[...], preferred_element_type=jnp.float32)
    u = jnp.square(jnp.maximum(u, 0.0)).astype(jnp.bfloat16)
    part = jnp.dot(u, w2_ref[...], preferred_element_type=jnp.float32)

    @pl.when(j == 0)
    def _():
        acc_ref[...] = part

    @pl.when(j > 0)
    def _():
        acc_ref[...] += part

    @pl.when(j == pl.num_programs(1) - 1)
    def _():
        o_ref[...] = x_ref[...] + gt_ref[...] * _rms(acc_ref[...], g3_ref[...])


def _mlp(x, g2, sc, sh, w1, w2, g3, gate, rows_per_mod, tm):
    m, d = x.shape
    f = w1.shape[1]
    tf = 1024
    return pl.pallas_call(
        _mlp_kernel,
        grid=(m // tm, f // tf),
        in_specs=[pl.BlockSpec((tm, d), lambda i, j: (i, 0)),
                  pl.BlockSpec((1, d), lambda i, j: (0, 0)),
                  _mod_spec(sc, rows_per_mod, tm, 2),
                  _mod_spec(sh, rows_per_mod, tm, 2),
                  pl.BlockSpec((d, tf), lambda i, j: (0, j)),
                  pl.BlockSpec((tf, d), lambda i, j: (j, 0)),
                  pl.BlockSpec((1, d), lambda i, j: (0, 0)),
                  _mod_spec(gate, rows_per_mod, tm, 2)],
        out_specs=pl.BlockSpec((tm, d), lambda i, j: (i, 0)),
        out_shape=jax.ShapeDtypeStruct((m, d), jnp.float32),
        scratch_shapes=[pltpu.VMEM((tm, d), jnp.bfloat16), pltpu.VMEM((tm, d), jnp.float32)],
        compiler_params=pltpu.CompilerParams(
            dimension_semantics=("parallel", "arbitrary"), vmem_limit_bytes=VMEM_LIMIT_BYTES),
        name="mlp",
    )(x, g2.reshape(1, d), sc, sh, w1, w2, g3.reshape(1, d), gate)


NSA_COL_BLOCK = NSA_KV_W
NSA_PROJ_W = 11 * NSA_COL_BLOCK
NSA_GATE_BLOCK = (NSA_Q_W + 6 * NSA_KV_W) // NSA_COL_BLOCK
ATT_TILE = 128
ROWS = NSA_GROUP * ATT_TILE
ATT_TILE_GROUP = 4


def _dot_nt(a, b):
    return lax.dot_general(a, b, (((1,), (1,)), ((), ())), preferred_element_type=jnp.float32)


def _dot_tn(a, b):
    return lax.dot_general(a, b, (((0,), (0,)), ((), ())), preferred_element_type=jnp.float32)


def _gelu_tanh(x):
    return 0.5 * x * (1.0 + jnp.tanh(math.sqrt(2.0 / math.pi) * (x + 0.044715 * (x * x * x))))


def _split3_bf16(x):
    hi = x.astype(jnp.bfloat16)
    r1 = x - hi.astype(jnp.float32)
    mid = r1.astype(jnp.bfloat16)
    lo = (r1 - mid.astype(jnp.float32)).astype(jnp.bfloat16)
    return hi, mid, lo


def _nsa_compress_kernel(x0_ref, x1_ref, x2_ref, x3_ref, w1_ref, w1f_ref, pos_ref, w2_ref, o_ref):
    n = x0_ref.shape[0] // CMP_STRIDE
    hd = NSA_HEAD_DIM
    x_refs = ((x0_ref, x1_ref), (x2_ref, x3_ref))
    for r in range(2):
        pos_b = jnp.dot(pos_ref[r].astype(jnp.bfloat16), w1f_ref[r], preferred_element_type=jnp.float32)
        acc = [jnp.zeros((n, 2 * CMP_HIDDEN), jnp.float32) for _ in range(NSA_KV_HEADS)]
        for l in range(CMP_STRIDE):
            w = w1_ref[r, l]
            for pair in range(2):
                xl = x_refs[r][pair][pl.ds(l, n, stride=CMP_STRIDE), :].astype(jnp.bfloat16)
                for half in range(2):
                    k = 2 * pair + half
                    acc[k] = acc[k] + jnp.dot(xl[:, half * hd:(half + 1) * hd], w,
                                              preferred_element_type=jnp.float32)
        for k in range(NSA_KV_HEADS):
            pa = acc[k][:, :CMP_HIDDEN]
            pb_next = pltpu.roll(acc[k][:, CMP_HIDDEN:], n - 1, 0)
            hid = _gelu_tanh(pa + pb_next + pos_b)
            out = jnp.dot(hid.astype(jnp.bfloat16), w2_ref[r], preferred_element_type=jnp.float32)
            o_ref[r, :, k * hd:(k + 1) * hd] = out.astype(o_ref.dtype)


def _compress_weights(cmp_pos, w1, w2):
    bf16 = jnp.bfloat16
    w1r = w1.reshape(2, CMP_LEN, NSA_HEAD_DIM, CMP_HIDDEN)
    w1cat = jnp.concatenate([w1r[:, :CMP_STRIDE], w1r[:, CMP_STRIDE:]], axis=-1).astype(bf16)
    return w1cat, w1.astype(bf16), cmp_pos.reshape(2, 1, CMP_LEN * NSA_HEAD_DIM), w2.astype(bf16)


def _nsa_compress_prompt(proj, cmp_pos, w1, w2):
    b, t, _ = proj.shape
    n = t // CMP_STRIDE
    w1cat, w1f, pos, w2b = _compress_weights(cmp_pos, w1, w2)
    return pl.pallas_call(
        _nsa_compress_kernel,
        grid=(b,),
        in_specs=[pl.BlockSpec((None, t, LANES), lambda i, c=c: (i, 0, NSA_Q_W // LANES + c)) for c in range(4)]
                 + [pl.BlockSpec(w1cat.shape, lambda i: (0, 0, 0, 0)),
                  pl.BlockSpec(w1f.shape, lambda i: (0, 0, 0)),
                  pl.BlockSpec(pos.shape, lambda i: (0, 0, 0)),
                  pl.BlockSpec(w2b.shape, lambda i: (0, 0, 0))],
        out_specs=pl.BlockSpec((None, 2, n, NSA_KV_W), lambda i: (i, 0, 0, 0)),
        out_shape=jax.ShapeDtypeStruct((b, 2, n, NSA_KV_W), jnp.bfloat16),
        compiler_params=pltpu.CompilerParams(
            dimension_semantics=("parallel",), vmem_limit_bytes=VMEM_LIMIT_BYTES),
        name="nsa_compress",
    )(proj, proj, proj, proj, w1cat, w1f, pos, w2b)


def _bias_lookup(rel_bias, dist):
    onehot = jax.nn.one_hot(_rel_bucket(dist), REL_BUCKETS, dtype=jnp.float32)
    return jnp.einsum('...c,ch->...h', onehot, rel_bias, precision=lax.Precision.HIGHEST)


DEN_ROWS = 8


def _with_ones(v):
    return jnp.concatenate([v, jnp.ones((v.shape[0], DEN_ROWS), v.dtype)], axis=1)


def _key_softmax_step(s, v, m, acc):
    m_new = jnp.maximum(m, jnp.max(s, axis=0, keepdims=True))
    e = jnp.exp(s - m_new).astype(jnp.bfloat16)
    acc = jnp.exp(m - m_new) * acc + _dot_tn(_with_ones(v), e)
    return m_new, acc


def _softmax_out(acc):
    hd = acc.shape[0] - DEN_ROWS
    return acc[:hd] / acc[hd:hd + 1]


def _nsa_attn_t_kernel(q_ref, g_ref, c_ref, ks_ref, vs_ref, kw_ref, vw_ref, bc_ref, bt_ref,
                       mimp_ref, eg_ref, o_ref, oc_s, os_s, ow_s, sel_s, *, tile0, n_far):
    i = tile0 + pl.program_id(1)
    hd = NSA_HEAD_DIM
    tq = ATT_TILE
    bf16 = jnp.bfloat16
    n_cmp_pad = c_ref.shape[1]
    n_sel = mimp_ref.shape[0]
    kj = lax.broadcasted_iota(jnp.int32, (tq, ROWS), 0)
    qi = lax.broadcasted_iota(jnp.int32, (tq, ROWS), 1) % tq
    causal = kj <= qi
    win_edge = kj >= qi
    cmp_end = CMP_STRIDE * lax.broadcasted_iota(jnp.int32, (n_cmp_pad, ROWS), 0) + (CMP_LEN - 1)
    mask_c = cmp_end <= i * tq + lax.broadcasted_iota(jnp.int32, (n_cmp_pad, ROWS), 1) % tq
    blk = lax.broadcasted_iota(jnp.int32, (n_sel, tq), 0)
    cur = (i * tq + lax.broadcasted_iota(jnp.int32, (n_sel, tq), 1)) // SEL_BLOCK
    forced = (blk == 0) | (blk == cur) | (blk == cur - 1)
    valid = blk <= cur

    heads = range(NSA_KV_HEADS)
    lanes = [slice(k * hd, (k + 1) * hd) for k in heads]
    per_chunk = tq // SEL_BLOCK
    qk, o_cmp = [], []
    for k in heads:
        q = jnp.concatenate(
            [q_ref[:, (k * NSA_GROUP + g) * hd:(k * NSA_GROUP + g + 1) * hd] for g in range(NSA_GROUP)], axis=0)
        qk.append((q * (hd ** -0.5)).astype(bf16))

        s = jnp.where(mask_c, _dot_nt(c_ref[0, :, lanes[k]], qk[k]) + bc_ref[k], NEG_INF)
        m = jnp.max(s, axis=0, keepdims=True)
        e = jnp.where(mask_c, jnp.exp(s - m), 0.0)
        p = e * (1.0 / jnp.maximum(jnp.sum(e, axis=0, keepdims=True), 1e-30))
        o_cmp.append(_dot_tn(c_ref[1, :, lanes[k]], p.astype(bf16)))
        p_sum = sum(p[:, g * tq:(g + 1) * tq] for g in range(NSA_GROUP))
        imp = _dot3(p_sum, mimp_ref[...], lambda a, b, **kw: jnp.dot(b, a, **kw))
        score = jnp.where(valid, jnp.where(forced, FORCE_SCORE, imp), NEG_INF)
        rank = jnp.zeros((n_sel, tq), jnp.float32)
        for j in range(n_sel):
            row = score[j:j + 1, :]
            beats = (row > score) | ((row == score) & (blk > j))
            rank = rank + jnp.where(beats, 1.0, 0.0)
        sel = jnp.where((rank < SEL_TOP_N) & (score > 0.5 * NEG_INF), 1.0, 0.0)
        sel = jnp.concatenate([sel] * NSA_GROUP, axis=1)
        for c in range(n_sel // per_chunk):
            sel_s[k, c, 0:per_chunk, :] = sel[c * per_chunk:(c + 1) * per_chunk, :]

    def rows_of(c, n=1):
        return pl.ds(c * tq, n * tq) if isinstance(c, int) else pl.ds(pl.multiple_of(c * tq, tq), n * tq)

    def chunk(k_ref, v_ref, k, rows, carry, bias, mk):
        s = _dot_nt(k_ref[rows, lanes[k]].astype(bf16), qk[k])
        if bias is not None:
            s = s + bias
        return _key_softmax_step(jnp.where(mk, s, NEG_INF), v_ref[rows, lanes[k]].astype(bf16), *carry)

    def sel_mask(k, c, ok):
        pair = sel_s[k, c, 0:per_chunk, :]
        picked = jnp.concatenate([jnp.broadcast_to(pair[j:j + 1, :], (SEL_BLOCK, ROWS)) for j in range(per_chunk)],
                                 axis=0)
        return (picked > 0.5) & jnp.broadcast_to(ok, (tq, ROWS))

    init = (jnp.full((1, ROWS), NEG_INF, jnp.float32), jnp.zeros((hd + DEN_ROWS, ROWS), jnp.float32))
    carry = [init] * NSA_KV_HEADS
    c_prev = jnp.maximum(i - 1, 0)
    for k in heads:
        carry[k] = chunk(ks_ref, vs_ref, k, rows_of(i), carry[k], bt_ref[k, 0], sel_mask(k, i, True) & causal)
    for k in heads:
        carry[k] = chunk(ks_ref, vs_ref, k, rows_of(c_prev), carry[k], bt_ref[k, 1], sel_mask(k, c_prev, i >= 1))
    for c in range(0, n_far, 2):
        for k in heads:
            mk = jnp.concatenate([sel_mask(k, c, c < i - 1), sel_mask(k, c + 1, c + 1 < i - 1)], axis=0)
            carry[k] = chunk(ks_ref, vs_ref, k, rows_of(c, 2), carry[k], None, mk)
    o_sel = [_softmax_out(carry[k][1]) for k in heads]

    carry = [init] * NSA_KV_HEADS
    n_back = WINDOW // tq
    for back in range(n_back + 1):
        c = i - back
        mk = jnp.broadcast_to(c >= 0, (tq, ROWS))
        if back == 0:
            mk = mk & causal
        if back == n_back:
            mk = mk & win_edge
        for k in heads:
            bias = bt_ref[k, back] if back < 2 else None
            carry[k] = chunk(kw_ref, vw_ref, k, rows_of(jnp.maximum(c, 0)), carry[k], bias, mk)
    o_win = [_softmax_out(carry[k][1]) for k in heads]

    for k in heads:
        for g in range(NSA_GROUP):
            rows = slice((k * NSA_GROUP + g) * hd, (k * NSA_GROUP + g + 1) * hd)
            oc_s[rows, :] = o_cmp[k][:, g * tq:(g + 1) * tq]
            os_s[rows, :] = o_sel[k][:, g * tq:(g + 1) * tq]
            ow_s[rows, :] = o_win[k][:, g * tq:(g + 1) * tq]

    gate = jax.nn.sigmoid(g_ref[...])
    g_hi = gate.astype(bf16)
    g_lo = (gate - g_hi.astype(jnp.float32)).astype(bf16)
    out = jnp.zeros((NSA_Q_W, tq), jnp.float32)
    for br, o_s in enumerate((oc_s, os_s, ow_s)):
        out = out + (_dot_nt(eg_ref[br], g_hi) + _dot_nt(eg_ref[br], g_lo)) * o_s[...]
    for r in range(NSA_Q_W // tq):
        o_ref[:, r * tq:(r + 1) * tq] = out[r * tq:(r + 1) * tq, :].T.astype(o_ref.dtype)


def _keys_by_kv_head(tab):
    *lead, q, t, _ = tab.shape
    tab = tab.reshape(*lead, q, t, NSA_KV_HEADS, NSA_GROUP)
    nl = len(lead)
    tab = jnp.transpose(tab, (*range(nl), nl + 2, nl + 1, nl + 3, nl))
    return tab.reshape(*lead, NSA_KV_HEADS, t, NSA_GROUP * q)


def _nsa_prompt_tables_t(rel_bias, t):
    tq = ATT_TILE
    n_chunks = t // CMP_STRIDE
    n_sel = t // SEL_BLOCK
    far = rel_bias[REL_BUCKETS - 1]
    ar = jnp.arange(tq)
    d_tile = (jnp.arange(2) * tq)[:, None, None] + ar[None, :, None] - ar[None, None, :]
    bt = _keys_by_kv_head(_bias_lookup(rel_bias, d_tile) - far)
    bt = jnp.transpose(bt, (1, 0, 2, 3))
    q_pos = jnp.arange(t).reshape(t // tq, tq)
    cmp_end = jnp.arange(n_chunks) * CMP_STRIDE + CMP_LEN - 1
    bc = _keys_by_kv_head(_bias_lookup(rel_bias, q_pos[:, :, None] - cmp_end[None, None, :]) - far)
    n_idx = np.arange(n_chunks)
    j_idx = np.arange(n_sel)[:, None]
    per = SEL_BLOCK // CMP_STRIDE
    mimp = 0.5 * ((n_idx // per == j_idx).astype(np.float32) + ((n_idx + 1) // per == j_idx).astype(np.float32))
    mimp[:, n_chunks - 1] = 0.0
    col = np.arange(NSA_Q_W) // NSA_HEAD_DIM
    eg = np.zeros((3, NSA_Q_W, NSA_COL_BLOCK), np.float32)
    for br in range(3):
        eg[br, np.arange(NSA_Q_W), br * NSA_HEADS + col] = 1.0
    return bt, bc, jnp.asarray(mimp, jnp.bfloat16), jnp.asarray(eg, jnp.bfloat16)


def _nsa_attn_prompt_t(proj, cmp, tables):
    b, t, _ = proj.shape
    bt, bc, mimp, eg = tables
    tq = ATT_TILE
    cb = NSA_COL_BLOCK
    first_kv = NSA_Q_W // cb

    def kv_spec(slab):
        return pl.BlockSpec((None, t, cb), lambda bi, i: (bi, 0, first_kv + slab))

    def const_spec(a):
        nd = a.ndim
        return pl.BlockSpec(a.shape, lambda bi, i: (0,) * nd)

    def tile_group(tile0):
        n_far = max(tile0 + ATT_TILE_GROUP - 2, 0)
        return pl.pallas_call(
            functools.partial(_nsa_attn_t_kernel, tile0=tile0, n_far=n_far),
            grid=(b, ATT_TILE_GROUP),
            in_specs=[pl.BlockSpec((None, tq, NSA_Q_W), lambda bi, i: (bi, tile0 + i, 0)),
                      pl.BlockSpec((None, tq, cb), lambda bi, i: (bi, tile0 + i, NSA_GATE_BLOCK)),
                      pl.BlockSpec((None,) + cmp.shape[1:], lambda bi, i: (bi, 0, 0, 0)),
                      kv_spec(2), kv_spec(3), kv_spec(4), kv_spec(5),
                      pl.BlockSpec((None,) + bc.shape[1:], lambda bi, i: (tile0 + i, 0, 0, 0)),
                      const_spec(bt), const_spec(mimp), const_spec(eg)],
            out_specs=pl.BlockSpec((None, tq, NSA_Q_W), lambda bi, i: (bi, i, 0)),
            out_shape=jax.ShapeDtypeStruct((b, ATT_TILE_GROUP * tq, NSA_Q_W), jnp.bfloat16),
            scratch_shapes=[pltpu.VMEM((NSA_Q_W, tq), jnp.float32)] * 3
                           + [pltpu.VMEM((NSA_KV_HEADS, t // tq, 8, ROWS), jnp.float32)],
            compiler_params=pltpu.CompilerParams(
                dimension_semantics=("parallel", "arbitrary"), vmem_limit_bytes=VMEM_LIMIT_BYTES),
            name="nsa_attn",
        )(proj, proj, cmp, proj, proj, proj, proj, bc, bt, mimp, eg)

    parts = [tile_group(tile0) for tile0 in range(0, t // tq, ATT_TILE_GROUP)]
    return jnp.concatenate(parts, axis=1).reshape(b * t, NSA_Q_W)


PAGE_GROUP = 8
SEL_PAD = 8


def _col_softmax_step(kk, vv, qbd, bias, mask, state):
    m, l, acc = state
    s = jnp.dot(kk, qbd, preferred_element_type=jnp.float32)
    if bias is not None:
        s = s + bias
    s = jnp.where(mask, s, NEG_INF)
    m_new = jnp.maximum(m, jnp.max(s, axis=0, keepdims=True))
    alpha = jnp.exp(m - m_new)
    e = jnp.where(mask, jnp.exp(s - m_new), 0.0)
    l = alpha * l + jnp.sum(e, axis=0, keepdims=True)
    acc = alpha * acc + _dot_tn(vv, e.astype(jnp.bfloat16))
    return m_new, l, acc


def _nsa_sample_kernel(pt_ref, *refs):
    n_cmp_in = 4 * PAGE_GROUP
    n_sel_in = 2 * PAGE_GROUP
    cmp_pages = refs[:n_cmp_in]
    sel_pages = refs[n_cmp_in:n_cmp_in + n_sel_in]
    (qbd_ref, new_ref, win_ref, w1_ref, w1f_ref, pos_ref, w2_ref, bcmp_ref, bsel_ref, bwin_ref,
     mimp_ref, hk_ref, bd_ref, rep_ref, gsel_ref, o_ref,
     a_s, kc_s, vc_s, score_s, selh_s, kw_s, vw_s, kn_s, vn_s, ocmp_s, m_s, l_s, acc_s) = refs[n_cmp_in + n_sel_in:]
    del pt_ref
    ph = pl.program_id(1)
    g = pl.program_id(2)
    n_groups = pl.num_programs(2)
    bf16 = jnp.bfloat16
    hd = NSA_HEAD_DIM
    n_chunks = a_s.shape[1]
    n_sel = mimp_ref.shape[1]
    per_page = PAGE_SIZE // CMP_STRIDE
    qbd = qbd_ref[...]

    def heads_out(acc_t, l):
        o = (acc_t / l) * bd_ref[...]
        hi = o.astype(bf16)
        lo = (o - hi.astype(jnp.float32)).astype(bf16)
        return _dot_tn(hi, rep_ref[...]) + _dot_tn(lo, rep_ref[...])

    @pl.when(ph == 0)
    def _():
        for u in range(PAGE_GROUP):
            row0 = pl.multiple_of((g * PAGE_GROUP + u) * per_page, per_page)
            for c in range(4):
                for l in range(CMP_STRIDE):
                    a_s[c, pl.ds(row0, per_page), l * LANES:(l + 1) * LANES] = (
                        cmp_pages[u * 4 + c][pl.ds(l, per_page, stride=CMP_STRIDE), :])

    @pl.when((ph == 0) & (g == n_groups - 1))
    def _():
        for r in range(2):
            pos_b = jnp.dot(pos_ref[r].astype(bf16), w1f_ref[r], preferred_element_type=jnp.float32)
            acc = [jnp.dot(a_s[2 * r + pair].astype(bf16), w1_ref[r], preferred_element_type=jnp.float32)
                   for pair in range(2)]
            dst = kc_s if r == 0 else vc_s
            for k in range(NSA_KV_HEADS):
                cols = (k % 2) * 2 * CMP_HIDDEN
                pa = acc[k // 2][:, cols:cols + CMP_HIDDEN]
                pb_next = pltpu.roll(acc[k // 2][:, cols + CMP_HIDDEN:cols + 2 * CMP_HIDDEN], n_chunks - 1, 0)
                hid = _gelu_tanh(pa + pb_next + pos_b)
                out = jnp.dot(hid.astype(bf16), w2_ref[r], preferred_element_type=jnp.float32)
                dst[:, k * hd:(k + 1) * hd] = out.astype(bf16)
        rows = lax.broadcasted_iota(jnp.int32, (n_chunks, NSA_HEADS), 0)
        mask_c = rows <= n_chunks - 2
        s = jnp.dot(kc_s[...], qbd, preferred_element_type=jnp.float32) + bcmp_ref[...]
        s = jnp.where(mask_c, s, NEG_INF)
        m = jnp.max(s, axis=0, keepdims=True)
        e = jnp.where(mask_c, jnp.exp(s - m), 0.0)
        l = jnp.maximum(jnp.sum(e, axis=0, keepdims=True), 1e-30)
        p = e / l
        ocmp_s[...] = heads_out(_dot_tn(vc_s[...], p.astype(bf16)), jnp.ones_like(l))
        p_kv = sum(jnp.dot(part, hk_ref[...], preferred_element_type=jnp.float32) for part in _split3_bf16(p))
        imp = sum(jnp.dot(mimp_ref[...], part, preferred_element_type=jnp.float32) for part in _split3_bf16(p_kv))
        blk = lax.broadcasted_iota(jnp.int32, imp.shape, 0)
        cur = n_chunks * CMP_STRIDE // SEL_BLOCK
        forced = (blk == 0) | (blk == cur) | (blk == cur - 1)
        score = jnp.where(blk <= cur, jnp.where(forced, FORCE_SCORE, imp), NEG_INF)
        score_s[...] = score

        def rank_body(j, rank):
            row = score_s[pl.ds(j, 1), :]
            beats = (row > score) | ((row == score) & (blk > j))
            return rank + jnp.where(beats, 1.0, 0.0)

        rank = lax.fori_loop(0, cur + 1, rank_body, jnp.zeros(imp.shape, jnp.float32), unroll=4)
        sel = jnp.where((rank < SEL_TOP_N) & (score > 0.5 * NEG_INF), 1.0, 0.0)
        selh_s[...] = jnp.dot(sel.astype(bf16), gsel_ref[...], preferred_element_type=jnp.float32)
        m_s[...] = jnp.full(m_s.shape, NEG_INF, jnp.float32)
        l_s[...] = jnp.zeros(l_s.shape, jnp.float32)
        acc_s[...] = jnp.zeros(acc_s.shape, jnp.float32)

    @pl.when(ph == 1)
    def _():
        state = (m_s[...], l_s[...], acc_s[...])
        half = lax.broadcasted_iota(jnp.int32, (PAGE_SIZE, NSA_HEADS), 0) < SEL_BLOCK
        for u in range(PAGE_GROUP):
            page = g * PAGE_GROUP + u
            pair = selh_s[pl.ds(page * (PAGE_SIZE // SEL_BLOCK), PAGE_SIZE // SEL_BLOCK), :]
            mask = jnp.where(half, pair[0:1, :], pair[1:2, :]) > 0.5
            near = jnp.where(page == n_groups * PAGE_GROUP - 1, 1.0, 0.0)
            state = _col_softmax_step(sel_pages[2 * u][...].astype(bf16), sel_pages[2 * u + 1][...].astype(bf16),
                                      qbd, near * bsel_ref[0:PAGE_SIZE, :], mask, state)
        m_s[...], l_s[...], acc_s[...] = state

    @pl.when((ph == 1) & (g == n_groups - 1))
    def _():
        kn_s[...] = jnp.zeros(kn_s.shape, bf16)
        vn_s[...] = jnp.zeros(vn_s.shape, bf16)
        kv0 = NSA_Q_W
        kn_s[0:1, :] = new_ref[:, kv0 + 2 * NSA_KV_W:kv0 + 3 * NSA_KV_W].astype(bf16)
        vn_s[0:1, :] = new_ref[:, kv0 + 3 * NSA_KV_W:kv0 + 4 * NSA_KV_W].astype(bf16)
        first = lax.broadcasted_iota(jnp.int32, (SEL_PAD, NSA_HEADS), 0) < 1
        state = _col_softmax_step(kn_s[...], vn_s[...], qbd, bsel_ref[PAGE_SIZE:PAGE_SIZE + SEL_PAD, :], first,
                                  (m_s[...], l_s[...], acc_s[...]))
        o_sel = heads_out(state[2], state[1])
        n_buf = win_ref.shape[0]
        kw_s[...] = jnp.zeros(kw_s.shape, bf16)
        vw_s[...] = jnp.zeros(vw_s.shape, bf16)
        kw_s[0:n_buf, :] = win_ref[:, 0:NSA_KV_W].astype(bf16)
        vw_s[0:n_buf, :] = win_ref[:, NSA_KV_W:2 * NSA_KV_W].astype(bf16)
        kw_s[n_buf:n_buf + 1, :] = new_ref[:, kv0 + 4 * NSA_KV_W:kv0 + 5 * NSA_KV_W].astype(bf16)
        vw_s[n_buf:n_buf + 1, :] = new_ref[:, kv0 + 5 * NSA_KV_W:kv0 + 6 * NSA_KV_W].astype(bf16)
        mask_w = lax.broadcasted_iota(jnp.int32, (kw_s.shape[0], NSA_HEADS), 0) <= n_buf
        init = (jnp.full((1, NSA_HEADS), NEG_INF, jnp.float32), jnp.zeros((1, NSA_HEADS), jnp.float32),
                jnp.zeros((NSA_KV_W, NSA_HEADS), jnp.float32))
        state = _col_softmax_step(kw_s[...], vw_s[...], qbd, bwin_ref[...], mask_w, init)
        o_win = heads_out(state[2], state[1])
        gate = jax.nn.sigmoid(new_ref[:, kv0 + 6 * NSA_KV_W:kv0 + 7 * NSA_KV_W])
        out = jnp.zeros(o_ref.shape, jnp.float32)
        for br, o_b in enumerate((ocmp_s[...], o_sel, o_win)):
            onehot = lax.broadcasted_iota(jnp.int32, (NSA_HEADS, NSA_COL_BLOCK), 1) == (
                lax.broadcasted_iota(jnp.int32, (NSA_HEADS, NSA_COL_BLOCK), 0) + br * NSA_HEADS)
            g_col = jnp.sum(jnp.where(onehot, gate, 0.0), axis=1, keepdims=True)
            out = out + g_col * o_b
        o_ref[...] = out


def _nsa_sample_tables(rel_bias, past_len, n_buf):
    far = rel_bias[REL_BUCKETS - 1]
    n_chunks = past_len // CMP_STRIDE
    n_sel = past_len // SEL_BLOCK + 1
    n_sel_pad = _round_up(n_sel, 8)
    cmp_end = jnp.arange(n_chunks) * CMP_STRIDE + CMP_LEN - 1
    bcmp = _bias_lookup(rel_bias, past_len - cmp_end) - far
    k_last = past_len - PAGE_SIZE + jnp.arange(PAGE_SIZE + SEL_PAD)
    bsel = _bias_lookup(rel_bias, past_len - k_last) - far
    n_win = _round_up(n_buf + 1, 8)
    bwin = _bias_lookup(rel_bias, n_buf - jnp.arange(n_win)) - far
    per = SEL_BLOCK // CMP_STRIDE
    n_idx = np.arange(n_chunks)
    j_idx = np.arange(n_sel_pad)[:, None]
    mimp = 0.5 * ((n_idx // per == j_idx).astype(np.float32) + ((n_idx + 1) // per == j_idx).astype(np.float32))
    mimp[:, n_chunks - 1] = 0.0
    heads = np.arange(NSA_HEADS)
    hk = (heads[:, None] // NSA_GROUP == np.arange(NSA_KV_HEADS)[None, :]).astype(np.float32)
    rowk = np.arange(NSA_KV_W) // NSA_HEAD_DIM
    bd = (rowk[:, None] == heads[None, :] // NSA_GROUP).astype(np.float32)
    rep = (np.arange(NSA_KV_W)[:, None] % NSA_HEAD_DIM == np.arange(NSA_HEAD_DIM)[None, :]).astype(np.float32)
    bf16 = jnp.bfloat16
    return (bcmp, bsel, bwin, jnp.asarray(mimp, bf16), jnp.asarray(hk, bf16), jnp.asarray(bd, jnp.float32),
            jnp.asarray(rep, bf16), jnp.asarray(hk.T, bf16))


def _nsa_attn_sample(proj, pages, win_buf, page_table, cmp_pos, w1, w2, tables):
    b = proj.shape[0]
    n_pages = page_table.shape[1]
    past_len = n_pages * PAGE_SIZE
    n_buf = win_buf.shape[1]
    n_chunks = past_len // CMP_STRIDE
    n_groups = n_pages // PAGE_GROUP
    bcmp, bsel, bwin, mimp, hk, bd, rep, gsel = tables
    n_sel_pad = mimp.shape[0]
    n_win = bwin.shape[0]
    w1cat, w1f, pos, w2b = _compress_weights(cmp_pos, w1, w2)
    w1cat = jnp.einsum('rlde,hg->rlhdge', w1cat, jnp.eye(2, dtype=w1cat.dtype)).reshape(
        2, CMP_STRIDE * LANES, 4 * CMP_HIDDEN)
    q = proj[:, :NSA_Q_W].reshape(b, NSA_KV_HEADS, NSA_GROUP, NSA_HEAD_DIM) * (NSA_HEAD_DIM ** -0.5)
    eye = jnp.eye(NSA_KV_HEADS, dtype=q.dtype)
    qbd = jnp.einsum('bkgd,kc->bkdcg', q, eye).reshape(b, NSA_KV_W, NSA_HEADS).astype(jnp.bfloat16)
    proj3 = proj.reshape(b, 1, proj.shape[1])
    win = win_buf.reshape(b, n_buf, 2 * NSA_KV_W)

    def cmp_spec(u, c):
        def imap(i, ph, g, pt):
            gg = jnp.where(ph == 0, g, n_groups - 1)
            return (pt[i, gg * PAGE_GROUP + u], 0, c)
        return pl.BlockSpec((None, PAGE_SIZE, LANES), imap)

    def sel_spec(u, slab):
        def imap(i, ph, g, pt):
            gg = jnp.where(ph == 1, g, 0)
            return (pt[i, gg * PAGE_GROUP + u], 0, slab)
        return pl.BlockSpec((None, PAGE_SIZE, NSA_KV_W), imap)

    def const_spec(a):
        nd = a.ndim
        return pl.BlockSpec(a.shape, lambda i, ph, g, pt: (0,) * nd)

    in_specs = ([cmp_spec(u, c) for u in range(PAGE_GROUP) for c in range(4)]
                + [sel_spec(u, slab) for u in range(PAGE_GROUP) for slab in (2, 3)]
                + [pl.BlockSpec((None, NSA_KV_W, NSA_HEADS), lambda i, ph, g, pt: (i, 0, 0)),
                   pl.BlockSpec((None, 1, proj.shape[1]), lambda i, ph, g, pt: (i, 0, 0)),
                   pl.BlockSpec((None, n_buf, 2 * NSA_KV_W), lambda i, ph, g, pt: (i, 0, 0))]
                + [const_spec(a) for a in (w1cat, w1f, pos, w2b, bcmp, bsel, bwin, mimp, hk, bd, rep, gsel)])
    f32, bf16 = jnp.float32, jnp.bfloat16
    out = pl.pallas_call(
        _nsa_sample_kernel,
        grid_spec=pltpu.PrefetchScalarGridSpec(
            num_scalar_prefetch=1,
            grid=(b, 2, n_groups),
            in_specs=in_specs,
            out_specs=pl.BlockSpec((None, NSA_HEADS, NSA_HEAD_DIM), lambda i, ph, g, pt: (i, 0, 0)),
            scratch_shapes=[pltpu.VMEM((4, n_chunks, CMP_STRIDE * LANES), f32),
                            pltpu.VMEM((n_chunks, NSA_KV_W), bf16), pltpu.VMEM((n_chunks, NSA_KV_W), bf16),
                            pltpu.VMEM((n_sel_pad, NSA_KV_HEADS), f32), pltpu.VMEM((n_sel_pad, NSA_HEADS), f32),
                            pltpu.VMEM((n_win, NSA_KV_W), bf16), pltpu.VMEM((n_win, NSA_KV_W), bf16),
                            pltpu.VMEM((SEL_PAD, NSA_KV_W), bf16), pltpu.VMEM((SEL_PAD, NSA_KV_W), bf16),
                            pltpu.VMEM((NSA_HEADS, NSA_HEAD_DIM), f32),
                            pltpu.VMEM((1, NSA_HEADS), f32), pltpu.VMEM((1, NSA_HEADS), f32),
                            pltpu.VMEM((NSA_KV_W, NSA_HEADS), f32)]),
        out_shape=jax.ShapeDtypeStruct((b, NSA_HEADS, NSA_HEAD_DIM), f32),
        compiler_params=pltpu.CompilerParams(
            dimension_semantics=("parallel", "arbitrary", "arbitrary"), vmem_limit_bytes=VMEM_LIMIT_BYTES),
        name="nsa_sample",
    )(page_table, *([pages] * (6 * PAGE_GROUP)), qbd, proj3, win,
      w1cat, w1f, pos, w2b, bcmp, bsel, bwin, mimp, hk, bd, rep, gsel)
    return out.reshape(b, NSA_Q_W)


SSD_PROJ_W = _round_up(SSD_IN_W, 7 * LANES)
SSD_TILE = 128
SSD_GN = SSD_GROUPS * SSD_STATE
SSD_GW = SSD_HPG * SSD_HEAD_DIM
CONV_PAD = 8


def _silu(x):
    return x * jax.nn.sigmoid(x)


def _softplus(x):
    return jnp.maximum(x, 0.0) + jnp.log(1.0 + jnp.exp(-jnp.abs(x)))


def _dot3(x, table, dot=jnp.dot):
    return sum(dot(part, table, preferred_element_type=jnp.float32) for part in _split3_bf16(x))


def _dot3_tn(x, table):
    return sum(_dot_tn(part, table) for part in _split3_bf16(x))


def _ssd_prompt_kernel(z_ref, x_ref, bc_ref, dt_ref, cw_ref, cb_ref, dtb_ref, alog_ref, d_ref, nw_ref,
                       eh_ref, tril_ref, triu_ref, eye_ref, y_ref, st_ref, conv_ref, ux_s, ubc_s, st_s):
    c = pl.program_id(1)
    t = SSD_TILE
    di = SSD_D_INNER
    bf16 = jnp.bfloat16

    @pl.when(c == 0)
    def _():
        ux_s[0:CONV_PAD, :] = jnp.zeros((CONV_PAD, di), jnp.float32)
        ubc_s[0:CONV_PAD, :] = jnp.zeros((CONV_PAD, 2 * SSD_GN), jnp.float32)
        st_s[...] = jnp.zeros(st_s.shape, jnp.float32)

    ux_s[CONV_PAD:CONV_PAD + t, :] = x_ref[...]
    ubc_s[CONV_PAD:CONV_PAD + t, :] = bc_ref[...]

    def conv(buf, col0, width):
        y = cb_ref[:, col0:col0 + width]
        for i in range(SSD_CONV_W):
            y = y + buf[pl.ds(CONV_PAD - (SSD_CONV_W - 1 - i), t), :] * cw_ref[i:i + 1, col0:col0 + width]
        return _silu(y)

    xs = conv(ux_s, 0, di)
    bcs = conv(ubc_s, di, 2 * SSD_GN)
    dt = _softplus(dt_ref[:, 0:SSD_HEADS] + dtb_ref[...])
    dta = dt * (-jnp.exp(alog_ref[...]))
    cum = _dot3(dta, tril_ref[...], lambda a, b, **kw: jnp.dot(b, a, **kw))
    cum_t = _dot3_tn(dta, triu_ref[...])
    dt_t = _dot3_tn(dt, eye_ref[...])
    last = cum[t - 1:t, :]
    dec_in = _dot3(jnp.exp(cum), eh_ref[...])
    wgt = _dot3(jnp.exp(last - cum) * dt, eh_ref[...])
    st_scale = _dot3(jnp.broadcast_to(jnp.exp(last), (8, SSD_HEADS)), eh_ref[...])[0:1, :]
    causal = lax.broadcasted_iota(jnp.int32, (t, t), 1) <= lax.broadcasted_iota(jnp.int32, (t, t), 0)

    for g in range(SSD_GROUPS):
        gl = slice(g * SSD_GW, (g + 1) * SSD_GW)
        bg = bcs[:, g * SSD_STATE:(g + 1) * SSD_STATE].astype(bf16)
        cg = bcs[:, SSD_GN + g * SSD_STATE:SSD_GN + (g + 1) * SSD_STATE].astype(bf16)
        cb = _dot_nt(cg, bg)
        xg = xs[:, gl]
        y_heads = []
        for j in range(SSD_HPG):
            h = g * SSD_HPG + j
            decay = jnp.exp(jnp.minimum(cum[:, h:h + 1] - cum_t[h:h + 1, :], 0.0))
            w = jnp.where(causal, cb * decay * dt_t[h:h + 1, :], 0.0).astype(bf16)
            y_heads.append(jnp.dot(w, xg[:, j * SSD_HEAD_DIM:(j + 1) * SSD_HEAD_DIM].astype(bf16),
                                   preferred_element_type=jnp.float32))
        st = st_s[g]
        y = jnp.concatenate(y_heads, axis=1)
        y = y + jnp.dot(cg, st.astype(bf16), preferred_element_type=jnp.float32) * dec_in[:, gl]
        st_s[g] = st_scale[:, gl] * st + _dot_tn(bg, (xg * wgt[:, gl]).astype(bf16))
        y = (y + d_ref[:, gl] * xg) * _silu(z_ref[:, gl])
        y = y * lax.rsqrt(jnp.mean(y * y, axis=-1, keepdims=True) + NORM_EPS) * nw_ref[:, gl]
        y_ref[:, gl] = y.astype(y_ref.dtype)

    ux_s[0:CONV_PAD, :] = ux_s[t:t + CONV_PAD, :]
    ubc_s[0:CONV_PAD, :] = ubc_s[t:t + CONV_PAD, :]

    @pl.when(c == pl.num_programs(1) - 1)
    def _():
        st_ref[...] = st_s[...]
        keep = SSD_CONV_W - 1
        conv_ref[:, 0:di] = ux_s[CONV_PAD - keep:CONV_PAD, :]
        conv_ref[:, di:] = ubc_s[CONV_PAD - keep:CONV_PAD, :]


def _ssd_tables():
    eh = (np.arange(SSD_HEADS)[:, None] == np.arange(SSD_D_INNER)[None, :] // SSD_HEAD_DIM).astype(np.float32)
    tril = np.tril(np.ones((SSD_TILE, SSD_TILE), np.float32))
    bf16 = jnp.bfloat16
    return (jnp.asarray(eh, bf16), jnp.asarray(tril, bf16), jnp.asarray(tril.T, bf16),
            jnp.asarray(np.eye(SSD_TILE, dtype=np.float32), bf16))


def _state_from_transposed(st_t):
    b = st_t.shape[0]
    st = st_t.reshape(b, SSD_GROUPS, SSD_STATE, SSD_HPG, SSD_HEAD_DIM)
    return jnp.transpose(st, (0, 1, 3, 4, 2)).reshape(b, SSD_HEADS, SSD_HEAD_DIM, SSD_STATE)


def _ssd_prompt(proj, conv_w, conv_b, dt_bias, a_log, d_skip, norm_w):
    b, t, _ = proj.shape
    di = SSD_D_INNER
    tt = SSD_TILE
    d_exp = jnp.repeat(d_skip, SSD_HEAD_DIM).reshape(1, di)
    consts = (conv_w, conv_b.reshape(1, -1), dt_bias.reshape(1, -1), a_log.reshape(1, -1), d_exp,
              norm_w.reshape(1, di)) + _ssd_tables()

    def const_spec(a):
        nd = a.ndim
        return pl.BlockSpec(a.shape, lambda i, c: (0,) * nd)

    y, st_t, conv_new = pl.pallas_call(
        _ssd_prompt_kernel,
        grid=(b, t // tt),
        in_specs=[pl.BlockSpec((None, tt, di), lambda i, c: (i, c, 0)),
                  pl.BlockSpec((None, tt, di), lambda i, c: (i, c, 1)),
                  pl.BlockSpec((None, tt, 2 * SSD_GN), lambda i, c: (i, c, 2)),
                  pl.BlockSpec((None, tt, LANES), lambda i, c: (i, c, (di + SSD_CONV_DIM) // LANES))]
                 + [const_spec(a) for a in consts],
        out_specs=[pl.BlockSpec((None, tt, di), lambda i, c: (i, c, 0)),
                   pl.BlockSpec((None, SSD_GROUPS, SSD_STATE, SSD_GW), lambda i, c: (i, 0, 0, 0)),
                   pl.BlockSpec((None, SSD_CONV_W - 1, SSD_CONV_DIM), lambda i, c: (i, 0, 0))],
        out_shape=[jax.ShapeDtypeStruct((b, t, di), jnp.bfloat16),
                   jax.ShapeDtypeStruct((b, SSD_GROUPS, SSD_STATE, SSD_GW), jnp.float32),
                   jax.ShapeDtypeStruct((b, SSD_CONV_W - 1, SSD_CONV_DIM), jnp.float32)],
        scratch_shapes=[pltpu.VMEM((CONV_PAD + tt, di), jnp.float32),
                        pltpu.VMEM((CONV_PAD + tt, 2 * SSD_GN), jnp.float32),
                        pltpu.VMEM((SSD_GROUPS, SSD_STATE, SSD_GW), jnp.float32)],
        compiler_params=pltpu.CompilerParams(
            dimension_semantics=("parallel", "arbitrary"), vmem_limit_bytes=VMEM_LIMIT_BYTES),
        name="ssd_prompt",
    )(proj, proj, proj, proj, *consts)
    return y.reshape(b * t, di), _state_from_transposed(st_t), conv_new


ROW_PAD = 8


def _row8(x):
    return jnp.concatenate([x, jnp.zeros((ROW_PAD - 1, x.shape[1]), x.dtype)], axis=0)


def _ssd_step_kernel(p_ref, conv_ref, st_ref, cw_ref, cb_ref, dtb_ref, alog_ref, d_ref, nw_ref,
                     y_ref, st_out, conv_out, y_s):
    di = SSD_D_INNER
    bf16 = jnp.bfloat16
    u = p_ref[:, di:di + SSD_CONV_DIM]
    keep = SSD_CONV_W - 1
    y = cb_ref[...] + u * cw_ref[keep:keep + 1, :]
    for i in range(keep):
        y = y + conv_ref[i:i + 1, :] * cw_ref[i:i + 1, :]
    conv_out[0:keep - 1, :] = conv_ref[1:keep, :]
    conv_out[keep - 1:keep, :] = u
    xbc = _silu(y)
    xs = xbc[:, :di]
    dt = _softplus(p_ref[:, di + SSD_CONV_DIM:di + SSD_CONV_DIM + SSD_HEADS] + dtb_ref[...])
    decay = jnp.exp(dt * (-jnp.exp(alog_ref[...])))
    for g in range(SSD_GROUPS):
        bg = _row8(xbc[:, di + g * SSD_STATE:di + (g + 1) * SSD_STATE]).astype(bf16)
        cg = _row8(xbc[:, di + SSD_GN + g * SSD_STATE:di + SSD_GN + (g + 1) * SSD_STATE]).astype(bf16)
        for j in range(SSD_HPG):
            h = g * SSD_HPG + j
            cols = slice(h * SSD_HEAD_DIM, (h + 1) * SSD_HEAD_DIM)
            xh = _row8(xs[:, cols] * dt[:, h:h + 1]).astype(bf16)
            st = decay[:, h:h + 1] * st_ref[h] + _dot_tn(xh, bg)
            st_out[h] = st
            y_s[:, cols] = _dot_nt(cg, st.astype(bf16))
    yv = y_s[0:1, :]
    yv = (yv + d_ref[...] * xs) * _silu(p_ref[:, 0:di])
    for g in range(SSD_GROUPS):
        gl = slice(g * SSD_GW, (g + 1) * SSD_GW)
        yg = yv[:, gl]
        y_ref[:, gl] = yg * lax.rsqrt(jnp.mean(yg * yg, axis=-1, keepdims=True) + NORM_EPS) * nw_ref[:, gl]


def _ssd_step(proj, ssm0, conv0, conv_w, conv_b, dt_bias, a_log, d_skip, norm_w):
    b = proj.shape[0]
    di = SSD_D_INNER
    consts = (conv_w, conv_b.reshape(1, -1), dt_bias.reshape(1, -1), a_log.reshape(1, -1),
              jnp.repeat(d_skip, SSD_HEAD_DIM).reshape(1, di), norm_w.reshape(1, di))

    def const_spec(a):
        nd = a.ndim
        return pl.BlockSpec(a.shape, lambda i: (0,) * nd)

    y, st, conv_new = pl.pallas_call(
        _ssd_step_kernel,
        grid=(b,),
        in_specs=[pl.BlockSpec((None, 1, proj.shape[1]), lambda i: (i, 0, 0)),
                  pl.BlockSpec((None,) + conv0.shape[1:], lambda i: (i, 0, 0)),
                  pl.BlockSpec((None,) + ssm0.shape[1:], lambda i: (i, 0, 0, 0))]
                 + [const_spec(a) for a in consts],
        out_specs=[pl.BlockSpec((None, 1, di), lambda i: (i, 0, 0)),
                   pl.BlockSpec((None,) + ssm0.shape[1:], lambda i: (i, 0, 0, 0)),
                   pl.BlockSpec((None,) + conv0.shape[1:], lambda i: (i, 0, 0))],
        out_shape=[jax.ShapeDtypeStruct((b, 1, di), jnp.float32),
                   jax.ShapeDtypeStruct(ssm0.shape, jnp.float32),
                   jax.ShapeDtypeStruct(conv0.shape, jnp.float32)],
        scratch_shapes=[pltpu.VMEM((ROW_PAD, di), jnp.float32)],
        compiler_params=pltpu.CompilerParams(
            dimension_semantics=("parallel",), vmem_limit_bytes=VMEM_LIMIT_BYTES),
        name="ssd_step",
    )(proj.reshape(b, 1, -1), conv0, ssm0, *consts)
    return y.reshape(b, di), st, conv_new


def _hgrn_gates(p_ref, lb_ref):
    wk = HG_HEADS * HG_DK
    q = _silu(p_ref[:, 0:wk])
    f = lb_ref[...] + (1.0 - lb_ref[...]) * jax.nn.sigmoid(p_ref[:, wk:2 * wk])
    return q, f


HG_TILE = 128
HG_SUB = 16


def _hgrn_prompt_kernel(p_ref, lb_ref, gn_ref, tril_ref, subend_ref, ones_ref, y_ref, st_ref, st_s):
    c = pl.program_id(1)
    t = HG_TILE
    wk = HG_HEADS * HG_DK
    wv = HG_HEADS * HG_DV
    bf16 = jnp.bfloat16
    n_sub = t // HG_SUB

    @pl.when(c == 0)
    def _():
        st_s[...] = jnp.zeros(st_s.shape, jnp.float32)

    row = lax.broadcasted_iota(jnp.int32, (t, HG_DK), 0)
    sub_pos = row % HG_SUB
    row_sub = lax.broadcasted_iota(jnp.int32, (t, t), 0) // HG_SUB
    col_sub = lax.broadcasted_iota(jnp.int32, (t, t), 1) // HG_SUB
    left = lambda a, b, **kw: jnp.dot(b, a, **kw)

    def head(h, carry):
        kc = pl.ds(pl.multiple_of(h * HG_DK, HG_DK), HG_DK)
        q = _silu(p_ref[:, kc])
        lb = lb_ref[:, kc]
        f = lb + (1.0 - lb) * jax.nn.sigmoid(p_ref[:, pl.ds(pl.multiple_of(wk + h * HG_DK, HG_DK), HG_DK)])
        k = 1.0 - f
        v = p_ref[:, pl.ds(pl.multiple_of(2 * wk + h * HG_DV, HG_DV), HG_DV)]
        gate = p_ref[:, pl.ds(pl.multiple_of(2 * wk + wv + h * HG_DV, HG_DV), HG_DV)]
        cum = _dot3(jnp.log(f), tril_ref[...], left)
        sub_end = _dot3(cum, subend_ref[...], left)
        k_hat = k * jnp.exp(sub_end - cum)
        a_off = jnp.zeros((t, t), jnp.float32)
        for j in range(n_sub - 1):
            end_j = cum[(j + 1) * HG_SUB - 1:(j + 1) * HG_SUB, :]
            q_j = (q * jnp.exp(jnp.minimum(cum - end_j, 0.0))).astype(bf16)
            k_j = jnp.where(row // HG_SUB == j, k_hat, 0.0).astype(bf16)
            a_off = a_off + _dot_nt(q_j, k_j)
        a_off = jnp.where(col_sub < row_sub, a_off, 0.0)
        v16 = v.astype(bf16)
        o = jnp.dot(a_off.astype(bf16), v16, preferred_element_type=jnp.float32)
        for d in range(HG_SUB):
            k_d, cum_d, v_d = (k, cum, v) if d == 0 else (pltpu.roll(k, d, 0), pltpu.roll(cum, d, 0), pltpu.roll(v, d, 0))
            e = (q * k_d * jnp.exp(jnp.minimum(cum - cum_d, 0.0))).astype(bf16)
            a_d = jnp.dot(e, ones_ref[...], preferred_element_type=jnp.float32)
            o = o + jnp.where(sub_pos >= d, a_d, 0.0) * v_d
        st = st_s[h]
        o = o + _dot_nt((q * jnp.exp(cum)).astype(bf16), st.astype(bf16))
        last = cum[t - 1:t, :]
        st_s[h] = st * jnp.exp(last) + _dot_tn(v16, (k * jnp.exp(last - cum)).astype(bf16))
        o = o * lax.rsqrt(jnp.mean(o * o, axis=-1, keepdims=True) + NORM_EPS) * gn_ref[...]
        y_ref[:, pl.ds(pl.multiple_of(h * HG_DV, HG_DV), HG_DV)] = (o * _silu(gate)).astype(y_ref.dtype)
        return carry

    lax.fori_loop(0, HG_HEADS, head, 0)

    @pl.when(c == pl.num_programs(1) - 1)
    def _():
        st_ref[...] = st_s[...]


def _hgrn_prompt(proj, lb, g_norm):
    b, t, w = proj.shape
    tt = HG_TILE
    wv = HG_HEADS * HG_DV
    idx = np.arange(tt)
    tril = np.tril(np.ones((tt, tt), np.float32))
    subend = (idx[None, :] == (idx[:, None] // HG_SUB) * HG_SUB + HG_SUB - 1).astype(np.float32)
    bf16 = jnp.bfloat16
    consts = (lb.reshape(1, -1), g_norm.reshape(1, -1), jnp.asarray(tril, bf16), jnp.asarray(subend, bf16),
              jnp.ones((HG_DK, HG_DK), bf16))

    def const_spec(a):
        nd = a.ndim
        return pl.BlockSpec(a.shape, lambda i, c: (0,) * nd)

    y, st_t = pl.pallas_call(
        _hgrn_prompt_kernel,
        grid=(b, t // tt),
        in_specs=[pl.BlockSpec((None, tt, w), lambda i, c: (i, c, 0))] + [const_spec(a) for a in consts],
        out_specs=[pl.BlockSpec((None, tt, wv), lambda i, c: (i, c, 0)),
                   pl.BlockSpec((None, HG_HEADS, HG_DV, HG_DK), lambda i, c: (i, 0, 0, 0))],
        out_shape=[jax.ShapeDtypeStruct((b, t, wv), bf16),
                   jax.ShapeDtypeStruct((b, HG_HEADS, HG_DV, HG_DK), jnp.float32)],
        scratch_shapes=[pltpu.VMEM((HG_HEADS, HG_DV, HG_DK), jnp.float32)],
        compiler_params=pltpu.CompilerParams(
            dimension_semantics=("parallel", "arbitrary"), vmem_limit_bytes=VMEM_LIMIT_BYTES),
        name="hgrn_prompt",
    )(proj, *consts)
    return y.reshape(b * t, wv), jnp.swapaxes(st_t, 2, 3)


def _hgrn_step_kernel(p_ref, st_ref, lb_ref, gn_ref, y_ref, st_out):
    wk = HG_HEADS * HG_DK
    wv = HG_HEADS * HG_DV
    bf16 = jnp.bfloat16
    q, f = _hgrn_gates(p_ref, lb_ref)
    eye = lax.broadcasted_iota(jnp.int32, (HG_DK, HG_DK), 0) == lax.broadcasted_iota(jnp.int32, (HG_DK, HG_DK), 1)
    for h in range(HG_HEADS):
        kc = slice(h * HG_DK, (h + 1) * HG_DK)
        vc = slice(2 * wk + h * HG_DV, 2 * wk + (h + 1) * HG_DV)
        gc = slice(2 * wk + wv + h * HG_DV, 2 * wk + wv + (h + 1) * HG_DV)
        fh = f[:, kc]
        f_col = jnp.sum(jnp.where(eye, fh, 0.0), axis=1, keepdims=True)
        kv = _dot_tn(_row8(1.0 - fh).astype(bf16), _row8(p_ref[:, vc]).astype(bf16))
        st = f_col * st_ref[h] + kv
        st_out[h] = st
        o = jnp.dot(_row8(q[:, kc]).astype(bf16), st.astype(bf16), preferred_element_type=jnp.float32)[0:1, :]
        o = o * lax.rsqrt(jnp.mean(o * o, axis=-1, keepdims=True) + NORM_EPS) * gn_ref[...]
        y_ref[:, h * HG_DV:(h + 1) * HG_DV] = o * _silu(p_ref[:, gc])


def _hgrn_step(proj, s0, lb, g_norm):
    b = proj.shape[0]
    wv = HG_HEADS * HG_DV
    y, st = pl.pallas_call(
        _hgrn_step_kernel,
        grid=(b,),
        in_specs=[pl.BlockSpec((None, 1, proj.shape[1]), lambda i: (i, 0, 0)),
                  pl.BlockSpec((None,) + s0.shape[1:], lambda i: (i, 0, 0, 0)),
                  pl.BlockSpec((1, HG_HEADS * HG_DK), lambda i: (0, 0)),
                  pl.BlockSpec((1, HG_DV), lambda i: (0, 0))],
        out_specs=[pl.BlockSpec((None, 1, wv), lambda i: (i, 0, 0)),
                   pl.BlockSpec((None,) + s0.shape[1:], lambda i: (i, 0, 0, 0))],
        out_shape=[jax.ShapeDtypeStruct((b, 1, wv), jnp.float32), jax.ShapeDtypeStruct(s0.shape, jnp.float32)],
        compiler_params=pltpu.CompilerParams(
            dimension_semantics=("parallel",), vmem_limit_bytes=VMEM_LIMIT_BYTES),
        name="hgrn_step",
    )(proj.reshape(b, 1, -1), s0, lb.reshape(1, -1), g_norm.reshape(1, -1))
    return y.reshape(b, wv), st


def _rel_bucket(dist):
    n = jnp.maximum(dist, 0)
    n_exact = REL_BUCKETS // 2
    nf = jnp.maximum(n, 1).astype(jnp.float32)
    large = n_exact + (jnp.log(nf / n_exact) / math.log(REL_MAX_DIST / n_exact)
                       * (REL_BUCKETS - n_exact)).astype(jnp.int32)
    return jnp.where(n < n_exact, n, jnp.minimum(large, REL_BUCKETS - 1))


def _nsa_prompt_core(proj, cmp_pos, cmp_w1, cmp_w2, tables):
    b, t, _ = proj.shape
    cmp = _nsa_compress_prompt(proj, cmp_pos, cmp_w1, cmp_w2)
    merged = _nsa_attn_prompt_t(proj, cmp, tables)
    o1 = NSA_Q_W
    o2 = o1 + 4 * NSA_KV_W
    o3 = o2 + 2 * NSA_KV_W
    kv_cs = proj[..., o1:o2].reshape(b, t, 4, NSA_KV_HEADS, NSA_HEAD_DIM)
    kv_win = proj[:, t - min(WINDOW, t):, o2:o3].reshape(b, min(WINDOW, t), 2, NSA_KV_HEADS, NSA_HEAD_DIM)
    return merged, kv_cs, kv_win


def _nsa_sample_core(proj, kv_pages, win_buf, page_table, cmp_pos, cmp_w1, cmp_w2, tables):
    b, t, _ = proj.shape
    assert t == 1 and win_buf.shape[1] == WINDOW and page_table.shape[1] % PAGE_GROUP == 0
    merged = _nsa_attn_sample(proj.reshape(b, -1), kv_pages, win_buf, page_table, cmp_pos, cmp_w1, cmp_w2, tables)
    o1 = NSA_Q_W
    o2 = o1 + 4 * NSA_KV_W
    o3 = o2 + 2 * NSA_KV_W
    kv_cs = proj[..., o1:o2].reshape(b, t, 4, NSA_KV_HEADS, NSA_HEAD_DIM)
    kv_win = proj[..., o2:o3].reshape(b, t, 2, NSA_KV_HEADS, NSA_HEAD_DIM)
    new_win = jnp.concatenate([win_buf[:, t:], kv_win], axis=1)
    return merged, kv_cs, new_win


def _pad_cols(w, n):
    return jnp.pad(w, ((0, 0), (0, n - w.shape[1])))


def kernel(x_prompt, x_sample, cache_nsa_kv, cache_nsa_win, state_hgrn, state_ssd, state_ssd_conv, page_table, c_prompt, c_sample, rel_bias, hgrn_lower_bounds, w_ada, b_ada, norm_gains, w_mlp_in, w_mlp_out, nsa_w_in, nsa_cmp_pos, nsa_cmp_w1, nsa_cmp_w2, nsa_w_out, hg_w_in, hg_norm, hg_w_out, ssd_w_in, ssd_conv_w, ssd_conv_b, ssd_dt_bias, ssd_a_log, ssd_d, ssd_norm, ssd_w_out):
    bf16 = jnp.bfloat16
    bp, tp, d = x_prompt.shape
    bs, ts, _ = x_sample.shape
    mp, ms = bp * tp, bs * ts
    lb_p = jax.nn.softmax(hgrn_lower_bounds, axis=0)
    lower_bounds = jnp.cumsum(lb_p, axis=0) - lb_p[0]

    mod = _ada_all(jnp.concatenate([c_prompt, c_sample], axis=0), w_ada, b_ada)
    mod = mod.reshape(DEPTH, bp + bs, ADA_CHUNKS, d)
    mod_p = mod[:, :bp].transpose(0, 2, 1, 3)[:, :, :, None, :]
    mod_s = mod[:, bp:].transpose(0, 2, 1, 3)[:, :, None, :, :]

    xp = x_prompt.reshape(mp, d)
    xs = x_sample.reshape(ms, d)
    tm_p, tm_s = PROMPT_ROW_TILE, ms
    nsa_tables = _nsa_prompt_tables_t(rel_bias, tp)
    nsa_tables_s = _nsa_sample_tables(rel_bias, page_table.shape[1] * PAGE_SIZE, cache_nsa_win.shape[2])
    n_phys = cache_nsa_kv.shape[1]
    kv_pages = cache_nsa_kv.reshape(cache_nsa_kv.shape[0] * n_phys, PAGE_SIZE, 4 * NSA_KV_W)

    kv_p, kv_s, win_p, win_s = [], [], [], []
    hg_p, hg_s, ssd_p, ssd_s, conv_p, conv_s = [], [], [], [], [], []
    for i in range(DEPTH):
        j = i // N_MIXERS
        kind = i % N_MIXERS
        g = norm_gains[i]
        shp_m, scp_m, gtp_m, shp_f, scp_f, gtp_f = [mod_p[i, c] for c in range(ADA_CHUNKS)]
        shs_m, scs_m, gts_m, shs_f, scs_f, gts_f = [mod_s[i, c] for c in range(ADA_CHUNKS)]
        if kind == 0:
            n_pad = NSA_PROJ_W
            w_in = _pad_cols(nsa_w_in[j], n_pad).astype(bf16)
            w_out = nsa_w_out[j].astype(bf16)
            pp = _norm_mod_matmul(xp, g[0], scp_m, shp_m, w_in, tp, tm_p).reshape(bp, tp, n_pad)
            ps = _norm_mod_matmul(xs, g[0], scs_m, shs_m, w_in, ts, tm_s).reshape(bs, ts, n_pad)
            ap, new_kv_p, new_win_p = _nsa_prompt_core(pp, nsa_cmp_pos[j], nsa_cmp_w1[j], nsa_cmp_w2[j], nsa_tables)
            as_, new_kv_s, new_win_s = _nsa_sample_core(ps, kv_pages, cache_nsa_win[j], page_table + j * n_phys,
                                                        nsa_cmp_pos[j], nsa_cmp_w1[j], nsa_cmp_w2[j], nsa_tables_s)
            kv_p.append(new_kv_p)
            kv_s.append(new_kv_s)
            win_p.append(new_win_p)
            win_s.append(new_win_s)
        elif kind == 1:
            w_in = hg_w_in[j].astype(bf16)
            w_out = hg_w_out[j].astype(bf16)
            pp = _norm_mod_matmul(xp, g[0], scp_m, shp_m, w_in, tp, tm_p).reshape(bp, tp, -1)
            ps = _norm_mod_matmul(xs, g[0], scs_m, shs_m, w_in, ts, tm_s).reshape(bs, ts, -1)
            ap, new_hp = _hgrn_prompt(pp, lower_bounds[i], hg_norm[j])
            as_, new_hs = _hgrn_step(ps.reshape(bs, -1), state_hgrn[j], lower_bounds[i], hg_norm[j])
            hg_p.append(new_hp)
            hg_s.append(new_hs)
        else:
            n_pad = SSD_PROJ_W
            w_in = _pad_cols(ssd_w_in[j], n_pad).astype(bf16)
            w_out = ssd_w_out[j].astype(bf16)
            pp = _norm_mod_matmul(xp, g[0], scp_m, shp_m, w_in, tp, tm_p).reshape(bp, tp, n_pad)
            ps = _norm_mod_matmul(xs, g[0], scs_m, shs_m, w_in, ts, tm_s).reshape(bs, ts, n_pad)
            ap, new_sp, new_cp = _ssd_prompt(pp, ssd_conv_w[j], ssd_conv_b[j], ssd_dt_bias[j],
                                             ssd_a_log[j], ssd_d[j], ssd_norm[j])
            as_, new_ss, new_cs = _ssd_step(ps.reshape(bs, -1), state_ssd[j], state_ssd_conv[j], ssd_conv_w[j],
                                            ssd_conv_b[j], ssd_dt_bias[j], ssd_a_log[j], ssd_d[j], ssd_norm[j])
            ssd_p.append(new_sp)
            ssd_s.append(new_ss)
            conv_p.append(new_cp)
            conv_s.append(new_cs)
        xp = _matmul_norm_res(ap, w_out, xp, g[1], gtp_m, tp, tm_p)
        xs = _matmul_norm_res(as_, w_out, xs, g[1], gts_m, ts, tm_s)
        w1 = w_mlp_in[i].astype(bf16)
        w2 = w_mlp_out[i].astype(bf16)
        xp = _mlp(xp, g[2], scp_f, shp_f, w1, w2, g[3], gtp_f, tp, tm_p)
        xs = _mlp(xs, g[2], scs_f, shs_f, w1, w2, g[3], gts_f, ts, tm_s)
    return (xp.reshape(bp, tp, d), xs.reshape(bs, ts, d),
            jnp.stack(kv_p), jnp.stack(kv_s), jnp.stack(win_p), jnp.stack(win_s),
            jnp.stack(hg_p), jnp.stack(hg_s), jnp.stack(ssd_p), jnp.stack(ssd_s),
            jnp.stack(conv_p), jnp.stack(conv_s))
```

```python
import functools
import math

import jax
import jax.numpy as jnp
import numpy as np
from jax import lax
from jax.experimental import pallas as pl
from jax.experimental.pallas import tpu as pltpu

D_MODEL = 1024
DEPTH = 4
PAGE_SIZE = 128
N_MIXERS = 3
ADA_CHUNKS = 6
NORM_EPS = 1e-6
D_FF = 4 * D_MODEL

NSA_HEADS = 16
NSA_HEAD_DIM = D_MODEL // NSA_HEADS
NSA_KV_HEADS = 4
NSA_GROUP = NSA_HEADS // NSA_KV_HEADS
CMP_STRIDE = 16
CMP_LEN = 2 * CMP_STRIDE
CMP_HIDDEN = 2 * NSA_HEAD_DIM
SEL_BLOCK = 64
SEL_TOP_N = 16
WINDOW = 512
WIN_Q_BLOCK = 128
SEL_Q_BLOCK = 16
NSA_Q_W = NSA_HEADS * NSA_HEAD_DIM
NSA_KV_W = NSA_KV_HEADS * NSA_HEAD_DIM
NSA_IN_W = NSA_Q_W + 6 * NSA_KV_W + 3 * NSA_HEADS

REL_BUCKETS = 32
REL_MAX_DIST = 128

HG_EXPAND = 128
HG_HEADS = D_MODEL // HG_EXPAND
HG_DK = HG_EXPAND
HG_DV = D_MODEL // HG_HEADS
HG_CHUNK = 64

SSD_D_INNER = 2 * D_MODEL
SSD_HEAD_DIM = 64
SSD_HEADS = SSD_D_INNER // SSD_HEAD_DIM
SSD_GROUPS = 8
SSD_HPG = SSD_HEADS // SSD_GROUPS
SSD_STATE = 128
SSD_CONV_W = 4
SSD_CONV_DIM = SSD_D_INNER + 2 * SSD_GROUPS * SSD_STATE
SSD_IN_W = SSD_D_INNER + SSD_CONV_DIM + SSD_HEADS
SSD_CHUNK = 128

NEG_INF = -1e30
FORCE_SCORE = 1e4

LANES = 128
VMEM_LIMIT_BYTES = 48 * 1024 * 1024
PROMPT_ROW_TILE = 512


def _round_up(n, m):
    return -(-n // m) * m


def _col_tile(n, cap=1536):
    best = LANES
    for t in range(LANES, cap + 1, LANES):
        if n % t == 0:
            best = t
    return best


def _rms(x, g):
    return x * lax.rsqrt(jnp.mean(x * x, axis=-1, keepdims=True) + NORM_EPS) * g


def _mod_spec(mod, rows_per_mod, tm, ngrid):
    r = mod.shape[1]
    if r == 1:
        per = rows_per_mod // tm
        if ngrid == 1:
            return pl.BlockSpec((None, 1, mod.shape[2]), lambda i: (i // per, 0, 0))
        return pl.BlockSpec((None, 1, mod.shape[2]), lambda i, j: (i // per, 0, 0))
    if ngrid == 1:
        return pl.BlockSpec((None, r, mod.shape[2]), lambda i: (0, 0, 0))
    return pl.BlockSpec((None, r, mod.shape[2]), lambda i, j: (0, 0, 0))


def _ada_kernel(c_ref, w_ref, b_ref, o_ref):
    c = c_ref[...]
    s = (c * jax.nn.sigmoid(c)).astype(jnp.bfloat16)
    o_ref[...] = jnp.dot(s, w_ref[...].astype(jnp.bfloat16),
                         preferred_element_type=jnp.float32) + b_ref[...]


def _ada_all(c_all, w_ada, b_ada):
    rows = c_all.shape[0]
    n = ADA_CHUNKS * D_MODEL
    tn = 1024
    return pl.pallas_call(
        _ada_kernel,
        grid=(DEPTH, n // tn),
        in_specs=[pl.BlockSpec((rows, D_MODEL), lambda l, j: (0, 0)),
                  pl.BlockSpec((None, D_MODEL, tn), lambda l, j: (l, 0, j)),
                  pl.BlockSpec((None, 1, tn), lambda l, j: (l, 0, j))],
        out_specs=pl.BlockSpec((None, rows, tn), lambda l, j: (l, 0, j)),
        out_shape=jax.ShapeDtypeStruct((DEPTH, rows, n), jnp.float32),
        compiler_params=pltpu.CompilerParams(
            dimension_semantics=("parallel", "parallel"), vmem_limit_bytes=VMEM_LIMIT_BYTES),
        name="ada",
    )(c_all, w_ada, b_ada.reshape(DEPTH, 1, n))


def _norm_mod_matmul_kernel(x_ref, g_ref, sc_ref, sh_ref, w_ref, o_ref, h_ref):
    @pl.when(pl.program_id(1) == 0)
    def _():
        h = _rms(x_ref[...], g_ref[...]) * (1.0 + sc_ref[...]) + sh_ref[...]
        h_ref[...] = h.astype(jnp.bfloat16)

    o_ref[...] = jnp.dot(h_ref[...], w_ref[...], preferred_element_type=jnp.float32)


def _norm_mod_matmul(x, g, sc, sh, w, rows_per_mod, tm):
    m, d = x.shape
    n = w.shape[1]
    tn = _col_tile(n)
    return pl.pallas_call(
        _norm_mod_matmul_kernel,
        grid=(m // tm, n // tn),
        in_specs=[pl.BlockSpec((tm, d), lambda i, j: (i, 0)),
                  pl.BlockSpec((1, d), lambda i, j: (0, 0)),
                  _mod_spec(sc, rows_per_mod, tm, 2),
                  _mod_spec(sh, rows_per_mod, tm, 2),
                  pl.BlockSpec((d, tn), lambda i, j: (0, j))],
        out_specs=pl.BlockSpec((tm, tn), lambda i, j: (i, j)),
        out_shape=jax.ShapeDtypeStruct((m, n), jnp.float32),
        scratch_shapes=[pltpu.VMEM((tm, d), jnp.bfloat16)],
        compiler_params=pltpu.CompilerParams(
            dimension_semantics=("parallel", "arbitrary"), vmem_limit_bytes=VMEM_LIMIT_BYTES),
        name="norm_mod_matmul",
    )(x, g.reshape(1, d), sc, sh, w)


def _matmul_norm_res_kernel(a_ref, w_ref, x_ref, g_ref, gt_ref, o_ref):
    y = jnp.dot(a_ref[...].astype(jnp.bfloat16), w_ref[...], preferred_element_type=jnp.float32)
    o_ref[...] = x_ref[...] + gt_ref[...] * _rms(y, g_ref[...])


def _matmul_norm_res(a, w, x, g, gate, rows_per_mod, tm):
    m, k = a.shape
    d = w.shape[1]
    return pl.pallas_call(
        _matmul_norm_res_kernel,
        grid=(m // tm,),
        in_specs=[pl.BlockSpec((tm, k), lambda i: (i, 0)),
                  pl.BlockSpec((k, d), lambda i: (0, 0)),
                  pl.BlockSpec((tm, d), lambda i: (i, 0)),
                  pl.BlockSpec((1, d), lambda i: (0, 0)),
                  _mod_spec(gate, rows_per_mod, tm, 1)],
        out_specs=pl.BlockSpec((tm, d), lambda i: (i, 0)),
        out_shape=jax.ShapeDtypeStruct((m, d), jnp.float32),
        compiler_params=pltpu.CompilerParams(
            dimension_semantics=("parallel",), vmem_limit_bytes=VMEM_LIMIT_BYTES),
        name="matmul_norm_res",
    )(a, w, x, g.reshape(1, d), gate)


def _mlp_kernel(x_ref, g2_ref, sc_ref, sh_ref, w1_ref, w2_ref, g3_ref, gt_ref, o_ref, h_ref, acc_ref):
    j = pl.program_id(1)

    @pl.when(j == 0)
    def _():
        h = _rms(x_ref[...], g2_ref[...]) * (1.0 + sc_ref[...]) + sh_ref[...]
        h_ref[...] = h.astype(jnp.bfloat16)

    u = jnp.dot(h_ref[...], w1_ref[...], preferred_element_type=jnp.float32)
    u = jnp.square(jnp.maximum(u, 0.0)).astype(jnp.bfloat16)
    part = jnp.dot(u, w2_ref[...], preferred_element_type=jnp.float32)

    @pl.when(j == 0)
    def _():
        acc_ref[...] = part

    @pl.when(j > 0)
    def _():
        acc_ref[...] += part

    @pl.when(j == pl.num_programs(1) - 1)
    def _():
        o_ref[...] = x_ref[...] + gt_ref[...] * _rms(acc_ref[...], g3_ref[...])


def _mlp(x, g2, sc, sh, w1, w2, g3, gate, rows_per_mod, tm):
    m, d = x.shape
    f = w1.shape[1]
    tf = 1024
    return pl.pallas_call(
        _mlp_kernel,
        grid=(m // tm, f // tf),
        in_specs=[pl.BlockSpec((tm, d), lambda i, j: (i, 0)),
                  pl.BlockSpec((1, d), lambda i, j: (0, 0)),
                  _mod_spec(sc, rows_per_mod, tm, 2),
                  _mod_spec(sh, rows_per_mod, tm, 2),
                  pl.BlockSpec((d, tf), lambda i, j: (0, j)),
                  pl.BlockSpec((tf, d), lambda i, j: (j, 0)),
                  pl.BlockSpec((1, d), lambda i, j: (0, 0)),
                  _mod_spec(gate, rows_per_mod, tm, 2)],
        out_specs=pl.BlockSpec((tm, d), lambda i, j: (i, 0)),
        out_shape=jax.ShapeDtypeStruct((m, d), jnp.float32),
        scratch_shapes=[pltpu.VMEM((tm, d), jnp.bfloat16), pltpu.VMEM((tm, d), jnp.float32)],
        compiler_params=pltpu.CompilerParams(
            dimension_semantics=("parallel", "arbitrary"), vmem_limit_bytes=VMEM_LIMIT_BYTES),
        name="mlp",
    )(x, g2.reshape(1, d), sc, sh, w1, w2, g3.reshape(1, d), gate)


NSA_COL_BLOCK = NSA_KV_W
NSA_PROJ_W = 11 * NSA_COL_BLOCK
NSA_GATE_BLOCK = (NSA_Q_W + 6 * NSA_KV_W) // NSA_COL_BLOCK
ATT_TILE = 128
ROWS = NSA_GROUP * ATT_TILE
ATT_TILE_GROUP = 4


def _dot_nt(a, b):
    return lax.dot_general(a, b, (((1,), (1,)), ((), ())), preferred_element_type=jnp.float32)


def _dot_tn(a, b):
    return lax.dot_general(a, b, (((0,), (0,)), ((), ())), preferred_element_type=jnp.float32)


def _gelu_tanh(x):
    return 0.5 * x * (1.0 + jnp.tanh(math.sqrt(2.0 / math.pi) * (x + 0.044715 * (x * x * x))))


def _split3_bf16(x):
    hi = x.astype(jnp.bfloat16)
    r1 = x - hi.astype(jnp.float32)
    mid = r1.astype(jnp.bfloat16)
    lo = (r1 - mid.astype(jnp.float32)).astype(jnp.bfloat16)
    return hi, mid, lo


def _nsa_compress_kernel(x0_ref, x1_ref, x2_ref, x3_ref, w1_ref, w1f_ref, pos_ref, w2_ref, o_ref):
    n = x0_ref.shape[0] // CMP_STRIDE
    hd = NSA_HEAD_DIM
    x_refs = ((x0_ref, x1_ref), (x2_ref, x3_ref))
    for r in range(2):
        pos_b = jnp.dot(pos_ref[r].astype(jnp.bfloat16), w1f_ref[r], preferred_element_type=jnp.float32)
        acc = [jnp.zeros((n, 2 * CMP_HIDDEN), jnp.float32) for _ in range(NSA_KV_HEADS)]
        for l in range(CMP_STRIDE):
            w = w1_ref[r, l]
            for pair in range(2):
                xl = x_refs[r][pair][pl.ds(l, n, stride=CMP_STRIDE), :].astype(jnp.bfloat16)
                for half in range(2):
                    k = 2 * pair + half
                    acc[k] = acc[k] + jnp.dot(xl[:, half * hd:(half + 1) * hd], w,
                                              preferred_element_type=jnp.float32)
        for k in range(NSA_KV_HEADS):
            pa = acc[k][:, :CMP_HIDDEN]
            pb_next = pltpu.roll(acc[k][:, CMP_HIDDEN:], n - 1, 0)
            hid = _gelu_tanh(pa + pb_next + pos_b)
            out = jnp.dot(hid.astype(jnp.bfloat16), w2_ref[r], preferred_element_type=jnp.float32)
            o_ref[r, :, k * hd:(k + 1) * hd] = out.astype(o_ref.dtype)


def _compress_weights(cmp_pos, w1, w2):
    bf16 = jnp.bfloat16
    w1r = w1.reshape(2, CMP_LEN, NSA_HEAD_DIM, CMP_HIDDEN)
    w1cat = jnp.concatenate([w1r[:, :CMP_STRIDE], w1r[:, CMP_STRIDE:]], axis=-1).astype(bf16)
    return w1cat, w1.astype(bf16), cmp_pos.reshape(2, 1, CMP_LEN * NSA_HEAD_DIM), w2.astype(bf16)


def _nsa_compress_prompt(proj, cmp_pos, w1, w2):
    b, t, _ = proj.shape
    n = t // CMP_STRIDE
    w1cat, w1f, pos, w2b = _compress_weights(cmp_pos, w1, w2)
    return pl.pallas_call(
        _nsa_compress_kernel,
        grid=(b,),
        in_specs=[pl.BlockSpec((None, t, LANES), lambda i, c=c: (i, 0, NSA_Q_W // LANES + c)) for c in range(4)]
                 + [pl.BlockSpec(w1cat.shape, lambda i: (0, 0, 0, 0)),
                  pl.BlockSpec(w1f.shape, lambda i: (0, 0, 0)),
                  pl.BlockSpec(pos.shape, lambda i: (0, 0, 0)),
                  pl.BlockSpec(w2b.shape, lambda i: (0, 0, 0))],
        out_specs=pl.BlockSpec((None, 2, n, NSA_KV_W), lambda i: (i, 0, 0, 0)),
        out_shape=jax.ShapeDtypeStruct((b, 2, n, NSA_KV_W), jnp.bfloat16),
        compiler_params=pltpu.CompilerParams(
            dimension_semantics=("parallel",), vmem_limit_bytes=VMEM_LIMIT_BYTES),
        name="nsa_compress",
    )(proj, proj, proj, proj, w1cat, w1f, pos, w2b)


def _bias_lookup(rel_bias, dist):
    onehot = jax.nn.one_hot(_rel_bucket(dist), REL_BUCKETS, dtype=jnp.float32)
    return jnp.einsum('...c,ch->...h', onehot, rel_bias, precision=lax.Precision.HIGHEST)


DEN_ROWS = 8


def _with_ones(v):
    return jnp.concatenate([v, jnp.ones((v.shape[0], DEN_ROWS), v.dtype)], axis=1)


def _key_softmax_step(s, v, m, acc):
    m_new = jnp.maximum(m, jnp.max(s, axis=0, keepdims=True))
    e = jnp.exp(s - m_new).astype(jnp.bfloat16)
    acc = jnp.exp(m - m_new) * acc + _dot_tn(_with_ones(v), e)
    return m_new, acc


def _softmax_out(acc):
    hd = acc.shape[0] - DEN_ROWS
    return acc[:hd] / acc[hd:hd + 1]


def _nsa_attn_t_kernel(q_ref, g_ref, c_ref, ks_ref, vs_ref, kw_ref, vw_ref, bc_ref, bt_ref,
                       mimp_ref, eg_ref, o_ref, oc_s, os_s, ow_s, sel_s, *, tile0, n_far):
    i = tile0 + pl.program_id(1)
    hd = NSA_HEAD_DIM
    tq = ATT_TILE
    bf16 = jnp.bfloat16
    n_cmp_pad = c_ref.shape[1]
    n_sel = mimp_ref.shape[0]
    kj = lax.broadcasted_iota(jnp.int32, (tq, ROWS), 0)
    qi = lax.broadcasted_iota(jnp.int32, (tq, ROWS), 1) % tq
    causal = kj <= qi
    win_edge = kj >= qi
    cmp_end = CMP_STRIDE * lax.broadcasted_iota(jnp.int32, (n_cmp_pad, ROWS), 0) + (CMP_LEN - 1)
    mask_c = cmp_end <= i * tq + lax.broadcasted_iota(jnp.int32, (n_cmp_pad, ROWS), 1) % tq
    blk = lax.broadcasted_iota(jnp.int32, (n_sel, tq), 0)
    cur = (i * tq + lax.broadcasted_iota(jnp.int32, (n_sel, tq), 1)) // SEL_BLOCK
    forced = (blk == 0) | (blk == cur) | (blk == cur - 1)
    valid = blk <= cur

    heads = range(NSA_KV_HEADS)
    lanes = [slice(k * hd, (k + 1) * hd) for k in heads]
    per_chunk = tq // SEL_BLOCK
    qk, o_cmp = [], []
    for k in heads:
        q = jnp.concatenate(
            [q_ref[:, (k * NSA_GROUP + g) * hd:(k * NSA_GROUP + g + 1) * hd] for g in range(NSA_GROUP)], axis=0)
        qk.append((q * (hd ** -0.5)).astype(bf16))

        s = jnp.where(mask_c, _dot_nt(c_ref[0, :, lanes[k]], qk[k]) + bc_ref[k], NEG_INF)
        m = jnp.max(s, axis=0, keepdims=True)
        e = jnp.where(mask_c, jnp.exp(s - m), 0.0)
        p = e * (1.0 / jnp.maximum(jnp.sum(e, axis=0, keepdims=True), 1e-30))
        o_cmp.append(_dot_tn(c_ref[1, :, lanes[k]], p.astype(bf16)))
        p_sum = sum(p[:, g * tq:(g + 1) * tq] for g in range(NSA_GROUP))
        imp = _dot3(p_sum, mimp_ref[...], lambda a, b, **kw: jnp.dot(b, a, **kw))
        score = jnp.where(valid, jnp.where(forced, FORCE_SCORE, imp), NEG_INF)
        rank = jnp.zeros((n_sel, tq), jnp.float32)
        for j in range(n_sel):
            row = score[j:j + 1, :]
            beats = (row > score) | ((row == score) & (blk > j))
            rank = rank + jnp.where(beats, 1.0, 0.0)
        sel = jnp.where((rank < SEL_TOP_N) & (score > 0.5 * NEG_INF), 1.0, 0.0)
        sel = jnp.concatenate([sel] * NSA_GROUP, axis=1)
        for c in range(n_sel // per_chunk):
            sel_s[k, c, 0:per_chunk, :] = sel[c * per_chunk:(c + 1) * per_chunk, :]

    def rows_of(c, n=1):
        return pl.ds(c * tq, n * tq) if isinstance(c, int) else pl.ds(pl.multiple_of(c * tq, tq), n * tq)

    def chunk(k_ref, v_ref, k, rows, carry, bias, mk):
        s = _dot_nt(k_ref[rows, lanes[k]].astype(bf16), qk[k])
        if bias is not None:
            s = s + bias
        return _key_softmax_step(jnp.where(mk, s, NEG_INF), v_ref[rows, lanes[k]].astype(bf16), *carry)

    def sel_mask(k, c, ok):
        pair = sel_s[k, c, 0:per_chunk, :]
        picked = jnp.concatenate([jnp.broadcast_to(pair[j:j + 1, :], (SEL_BLOCK, ROWS)) for j in range(per_chunk)],
                                 axis=0)
        return (picked > 0.5) & jnp.broadcast_to(ok, (tq, ROWS))

    init = (jnp.full((1, ROWS), NEG_INF, jnp.float32), jnp.zeros((hd + DEN_ROWS, ROWS), jnp.float32))
    carry = [init] * NSA_KV_HEADS
    c_prev = jnp.maximum(i - 1, 0)
    for k in heads:
        carry[k] = chunk(ks_ref, vs_ref, k, rows_of(i), carry[k], bt_ref[k, 0], sel_mask(k, i, True) & causal)
    for k in heads:
        carry[k] = chunk(ks_ref, vs_ref, k, rows_of(c_prev), carry[k], bt_ref[k, 1], sel_mask(k, c_prev, i >= 1))
    for c in range(0, n_far, 2):
        for k in heads:
            mk = jnp.concatenate([sel_mask(k, c, c < i - 1), sel_mask(k, c + 1, c + 1 < i - 1)], axis=0)
            carry[k] = chunk(ks_ref, vs_ref, k, rows_of(c, 2), carry[k], None, mk)
    o_sel = [_softmax_out(carry[k][1]) for k in heads]

    carry = [init] * NSA_KV_HEADS
    n_back = WINDOW // tq
    for back in range(n_back + 1):
        c = i - back
        mk = jnp.broadcast_to(c >= 0, (tq, ROWS))
        if back == 0:
            mk = mk & causal
        if back == n_back:
            mk = mk & win_edge
        for k in heads:
            bias = bt_ref[k, back] if back < 2 else None
            carry[k] = chunk(kw_ref, vw_ref, k, rows_of(jnp.maximum(c, 0)), carry[k], bias, mk)
    o_win = [_softmax_out(carry[k][1]) for k in heads]

    for k in heads:
        for g in range(NSA_GROUP):
            rows = slice((k * NSA_GROUP + g) * hd, (k * NSA_GROUP + g + 1) * hd)
            oc_s[rows, :] = o_cmp[k][:, g * tq:(g + 1) * tq]
            os_s[rows, :] = o_sel[k][:, g * tq:(g + 1) * tq]
            ow_s[rows, :] = o_win[k][:, g * tq:(g + 1) * tq]

    gate = jax.nn.sigmoid(g_ref[...])
    g_hi = gate.astype(bf16)
    g_lo = (gate - g_hi.astype(jnp.float32)).astype(bf16)
    out = jnp.zeros((NSA_Q_W, tq), jnp.float32)
    for br, o_s in enumerate((oc_s, os_s, ow_s)):
        out = out + (_dot_nt(eg_ref[br], g_hi) + _dot_nt(eg_ref[br], g_lo)) * o_s[...]
    for r in range(NSA_Q_W // tq):
        o_ref[:, r * tq:(r + 1) * tq] = out[r * tq:(r + 1) * tq, :].T.astype(o_ref.dtype)


def _keys_by_kv_head(tab):
    *lead, q, t, _ = tab.shape
    tab = tab.reshape(*lead, q, t, NSA_KV_HEADS, NSA_GROUP)
    nl = len(lead)
    tab = jnp.transpose(tab, (*range(nl), nl + 2, nl + 1, nl + 3, nl))
    return tab.reshape(*lead, NSA_KV_HEADS, t, NSA_GROUP * q)


def _nsa_prompt_tables_t(rel_bias, t):
    tq = ATT_TILE
    n_chunks = t // CMP_STRIDE
    n_sel = t // SEL_BLOCK
    far = rel_bias[REL_BUCKETS - 1]
    ar = jnp.arange(tq)
    d_tile = (jnp.arange(2) * tq)[:, None, None] + ar[None, :, None] - ar[None, None, :]
    bt = _keys_by_kv_head(_bias_lookup(rel_bias, d_tile) - far)
    bt = jnp.transpose(bt, (1, 0, 2, 3))
    q_pos = jnp.arange(t).reshape(t // tq, tq)
    cmp_end = jnp.arange(n_chunks) * CMP_STRIDE + CMP_LEN - 1
    bc = _keys_by_kv_head(_bias_lookup(rel_bias, q_pos[:, :, None] - cmp_end[None, None, :]) - far)
    n_idx = np.arange(n_chunks)
    j_idx = np.arange(n_sel)[:, None]
    per = SEL_BLOCK // CMP_STRIDE
    mimp = 0.5 * ((n_idx // per == j_idx).astype(np.float32) + ((n_idx + 1) // per == j_idx).astype(np.float32))
    mimp[:, n_chunks - 1] = 0.0
    col = np.arange(NSA_Q_W) // NSA_HEAD_DIM
    eg = np.zeros((3, NSA_Q_W, NSA_COL_BLOCK), np.float32)
    for br in range(3):
        eg[br, np.arange(NSA_Q_W), br * NSA_HEADS + col] = 1.0
    return bt, bc, jnp.asarray(mimp, jnp.bfloat16), jnp.asarray(eg, jnp.bfloat16)


def _nsa_attn_prompt_t(proj, cmp, tables):
    b, t, _ = proj.shape
    bt, bc, mimp, eg = tables
    tq = ATT_TILE
    cb = NSA_COL_BLOCK
    first_kv = NSA_Q_W // cb

    def kv_spec(slab):
        return pl.BlockSpec((None, t, cb), lambda bi, i: (bi, 0, first_kv + slab))

    def const_spec(a):
        nd = a.ndim
        return pl.BlockSpec(a.shape, lambda bi, i: (0,) * nd)

    def tile_group(tile0):
        n_far = max(tile0 + ATT_TILE_GROUP - 2, 0)
        return pl.pallas_call(
            functools.partial(_nsa_attn_t_kernel, tile0=tile0, n_far=n_far),
            grid=(b, ATT_TILE_GROUP),
            in_specs=[pl.BlockSpec((None, tq, NSA_Q_W), lambda bi, i: (bi, tile0 + i, 0)),
                      pl.BlockSpec((None, tq, cb), lambda bi, i: (bi, tile0 + i, NSA_GATE_BLOCK)),
                      pl.BlockSpec((None,) + cmp.shape[1:], lambda bi, i: (bi, 0, 0, 0)),
                      kv_spec(2), kv_spec(3), kv_spec(4), kv_spec(5),
                      pl.BlockSpec((None,) + bc.shape[1:], lambda bi, i: (tile0 + i, 0, 0, 0)),
                      const_spec(bt), const_spec(mimp), const_spec(eg)],
            out_specs=pl.BlockSpec((None, tq, NSA_Q_W), lambda bi, i: (bi, i, 0)),
            out_shape=jax.ShapeDtypeStruct((b, ATT_TILE_GROUP * tq, NSA_Q_W), jnp.bfloat16),
            scratch_shapes=[pltpu.VMEM((NSA_Q_W, tq), jnp.float32)] * 3
                           + [pltpu.VMEM((NSA_KV_HEADS, t // tq, 8, ROWS), jnp.float32)],
            compiler_params=pltpu.CompilerParams(
                dimension_semantics=("parallel", "arbitrary"), vmem_limit_bytes=VMEM_LIMIT_BYTES),
            name="nsa_attn",
        )(proj, proj, cmp, proj, proj, proj, proj, bc, bt, mimp, eg)

    parts = [tile_group(tile0) for tile0 in range(0, t // tq, ATT_TILE_GROUP)]
    return jnp.concatenate(parts, axis=1).reshape(b * t, NSA_Q_W)


PAGE_GROUP = 8
SEL_PAD = 8


def _nsa_sample_kernel(pt_ref, *refs):
    n_cmp_in = 2 * PAGE_GROUP
    n_sel_in = 2 * PAGE_GROUP
    cmp_pages = refs[:n_cmp_in]
    sel_pages = refs[n_cmp_in:n_cmp_in + n_sel_in]
    (qbd_ref, qbdt_ref, new_ref, win_ref, w1_ref, w1f_ref, pos_ref, w2_ref, bcmp_ref, bsel_ref, bwin_ref,
     mimp_ref, hk_ref, bd_ref, rep_ref, bdt_ref, gsel_ref, eexp_ref, o_ref,
     a_s, stage_s, kc_s, vc_s, score_s, mask_s, kn_s, vn_s, ocmp_s, m_s, l_s, acc_s) = refs[n_cmp_in + n_sel_in:]
    del pt_ref
    ph = pl.program_id(1)
    g = pl.program_id(2)
    n_groups = pl.num_programs(2)
    bf16 = jnp.bfloat16
    hd = NSA_HEAD_DIM
    n_chunks = a_s.shape[1]
    n_sel = mimp_ref.shape[1]
    per_page = PAGE_SIZE // CMP_STRIDE
    qbd = qbd_ref[...]

    def heads_out(acc_t, l):
        o = (acc_t / l) * bd_ref[...]
        hi = o.astype(bf16)
        lo = (o - hi.astype(jnp.float32)).astype(bf16)
        return _dot_tn(hi, rep_ref[...]) + _dot_tn(lo, rep_ref[...])

    @pl.when(ph == 0)
    def _():
        for u in range(PAGE_GROUP):
            row0 = pl.multiple_of((g * PAGE_GROUP + u) * per_page, per_page)
            for r in range(2):
                for pair in range(2):
                    c = 2 * r + pair
                    stage_s[c] = cmp_pages[u * 2 + r][2 * pair:2 * pair + 2].reshape(LANES, PAGE_SIZE).T
                    for l in range(CMP_STRIDE):
                        a_s[c, pl.ds(row0, per_page), l * LANES:(l + 1) * LANES] = (
                            stage_s[c, pl.ds(l, per_page, stride=CMP_STRIDE), :])

    @pl.when((ph == 0) & (g == n_groups - 1))
    def _():
        for r in range(2):
            pos_b = jnp.dot(pos_ref[r].astype(bf16), w1f_ref[r], preferred_element_type=jnp.float32)
            acc = [jnp.dot(a_s[2 * r + pair].astype(bf16), w1_ref[r], preferred_element_type=jnp.float32)
                   for pair in range(2)]
            dst = kc_s if r == 0 else vc_s
            for k in range(NSA_KV_HEADS):
                cols = (k % 2) * 2 * CMP_HIDDEN
                pa = acc[k // 2][:, cols:cols + CMP_HIDDEN]
                pb_next = pltpu.roll(acc[k // 2][:, cols + CMP_HIDDEN:cols + 2 * CMP_HIDDEN], n_chunks - 1, 0)
                hid = _gelu_tanh(pa + pb_next + pos_b)
                out = jnp.dot(hid.astype(bf16), w2_ref[r], preferred_element_type=jnp.float32)
                dst[:, k * hd:(k + 1) * hd] = out.astype(bf16)
        rows = lax.broadcasted_iota(jnp.int32, (n_chunks, NSA_HEADS), 0)
        mask_c = rows <= n_chunks - 2
        s = jnp.dot(kc_s[...], qbd, preferred_element_type=jnp.float32) + bcmp_ref[...]
        s = jnp.where(mask_c, s, NEG_INF)
        m = jnp.max(s, axis=0, keepdims=True)
        e = jnp.where(mask_c, jnp.exp(s - m), 0.0)
        l = jnp.maximum(jnp.sum(e, axis=0, keepdims=True), 1e-30)
        p = e / l
        ocmp_s[...] = heads_out(_dot_tn(vc_s[...], p.astype(bf16)), jnp.ones_like(l))
        p_kv = sum(jnp.dot(part, hk_ref[...], preferred_element_type=jnp.float32) for part in _split3_bf16(p))
        imp = sum(jnp.dot(mimp_ref[...], part, preferred_element_type=jnp.float32) for part in _split3_bf16(p_kv))
        blk = lax.broadcasted_iota(jnp.int32, imp.shape, 0)
        cur = n_chunks * CMP_STRIDE // SEL_BLOCK
        forced = (blk == 0) | (blk == cur) | (blk == cur - 1)
        score = jnp.where(blk <= cur, jnp.where(forced, FORCE_SCORE, imp), NEG_INF)
        score_s[...] = score

        def rank_body(j, rank):
            row = score_s[pl.ds(j, 1), :]
            beats = (row > score) | ((row == score) & (blk > j))
            return rank + jnp.where(beats, 1.0, 0.0)

        rank = lax.fori_loop(0, cur + 1, rank_body, jnp.zeros(imp.shape, jnp.float32), unroll=4)
        sel = jnp.where((rank < SEL_TOP_N) & (score > 0.5 * NEG_INF), 1.0, 0.0)
        sel_h = jnp.dot(sel.astype(bf16), gsel_ref[...], preferred_element_type=jnp.float32)
        mask_s[...] = _dot_tn(sel_h.astype(bf16), eexp_ref[...])
        m_s[...] = jnp.full(m_s.shape, NEG_INF, jnp.float32)
        l_s[...] = jnp.zeros(l_s.shape, jnp.float32)
        acc_s[...] = jnp.zeros(acc_s.shape, jnp.float32)

    qbd_t = qbdt_ref[...]

    def lane_step(s, mask, state, pv):
        m, l, acc = state
        s = jnp.where(mask, s, NEG_INF)
        m_new = jnp.maximum(m, jnp.max(s, axis=1, keepdims=True))
        alpha = jnp.exp(m - m_new)
        e = jnp.where(mask, jnp.exp(s - m_new), 0.0)
        return m_new, alpha * l + jnp.sum(e, axis=1, keepdims=True), alpha * acc + pv(e.astype(bf16))

    def heads_out_t(acc, l):
        o = (acc / l) * bdt_ref[...]
        return sum(o[:, k * hd:(k + 1) * hd] for k in range(NSA_KV_HEADS))

    @pl.when(ph == 1)
    def _():
        state = (m_s[...], l_s[...], acc_s[...])
        for u in range(PAGE_GROUP):
            page = g * PAGE_GROUP + u
            kt = sel_pages[2 * u][...].reshape(NSA_KV_W, PAGE_SIZE).astype(bf16)
            vt = sel_pages[2 * u + 1][...].reshape(NSA_KV_W, PAGE_SIZE).astype(bf16)
            mask = mask_s[:, pl.ds(pl.multiple_of(page * PAGE_SIZE, PAGE_SIZE), PAGE_SIZE)] > 0.5
            near = jnp.where(page == n_groups * PAGE_GROUP - 1, 1.0, 0.0)
            s = jnp.dot(qbd_t, kt, preferred_element_type=jnp.float32) + near * bsel_ref[:, 0:PAGE_SIZE]
            state = lane_step(s, mask, state, lambda e, vt=vt: _dot_nt(e, vt))
        m_s[...], l_s[...], acc_s[...] = state

    @pl.when((ph == 1) & (g == n_groups - 1))
    def _():
        kv0 = NSA_Q_W
        first = lax.broadcasted_iota(jnp.int32, (NSA_HEADS, SEL_PAD), 1) < 1

        def new_token_step(k_col, v_col, bias, state):
            kn_s[...] = jnp.zeros(kn_s.shape, bf16)
            vn_s[...] = jnp.zeros(vn_s.shape, bf16)
            kn_s[0:1, :] = new_ref[:, k_col:k_col + NSA_KV_W].astype(bf16)
            vn_s[0:1, :] = new_ref[:, v_col:v_col + NSA_KV_W].astype(bf16)
            s = _dot_nt(qbd_t, kn_s[...]) + bias
            vn = vn_s[...]
            return lane_step(s, first, state, lambda e: jnp.dot(e, vn, preferred_element_type=jnp.float32))

        state = new_token_step(kv0 + 2 * NSA_KV_W, kv0 + 3 * NSA_KV_W, bsel_ref[:, PAGE_SIZE:PAGE_SIZE + SEL_PAD],
                               (m_s[...], l_s[...], acc_s[...]))
        o_sel = heads_out_t(state[2], state[1])
        n_buf = win_ref.shape[2]
        kt = win_ref[0].astype(bf16)
        vt = win_ref[1].astype(bf16)
        init = (jnp.full((NSA_HEADS, 1), NEG_INF, jnp.float32), jnp.zeros((NSA_HEADS, 1), jnp.float32),
                jnp.zeros((NSA_HEADS, NSA_KV_W), jnp.float32))
        s = jnp.dot(qbd_t, kt, preferred_element_type=jnp.float32) + bwin_ref[:, 0:n_buf]
        state = lane_step(s, jnp.full(s.shape, True), init, lambda e: _dot_nt(e, vt))
        state = new_token_step(kv0 + 4 * NSA_KV_W, kv0 + 5 * NSA_KV_W, bwin_ref[:, n_buf:n_buf + SEL_PAD], state)
        o_win = heads_out_t(state[2], state[1])
        gate = jax.nn.sigmoid(new_ref[:, kv0 + 6 * NSA_KV_W:kv0 + 7 * NSA_KV_W])
        out = jnp.zeros(o_ref.shape, jnp.float32)
        for br, o_b in enumerate((ocmp_s[...], o_sel, o_win)):
            onehot = lax.broadcasted_iota(jnp.int32, (NSA_HEADS, NSA_COL_BLOCK), 1) == (
                lax.broadcasted_iota(jnp.int32, (NSA_HEADS, NSA_COL_BLOCK), 0) + br * NSA_HEADS)
            g_col = jnp.sum(jnp.where(onehot, gate, 0.0), axis=1, keepdims=True)
            out = out + g_col * o_b
        o_ref[...] = out


def _nsa_sample_tables(rel_bias, past_len, n_buf):
    far = rel_bias[REL_BUCKETS - 1]
    n_chunks = past_len // CMP_STRIDE
    n_sel = past_len // SEL_BLOCK + 1
    n_sel_pad = _round_up(n_sel, 8)
    cmp_end = jnp.arange(n_chunks) * CMP_STRIDE + CMP_LEN - 1
    bcmp = _bias_lookup(rel_bias, past_len - cmp_end) - far
    k_last = past_len - PAGE_SIZE + jnp.arange(PAGE_SIZE + SEL_PAD)
    bsel = (_bias_lookup(rel_bias, past_len - k_last) - far).T
    bwin = (_bias_lookup(rel_bias, n_buf - jnp.arange(n_buf + SEL_PAD)) - far).T
    per = SEL_BLOCK // CMP_STRIDE
    n_idx = np.arange(n_chunks)
    j_idx = np.arange(n_sel_pad)[:, None]
    mimp = 0.5 * ((n_idx // per == j_idx).astype(np.float32) + ((n_idx + 1) // per == j_idx).astype(np.float32))
    mimp[:, n_chunks - 1] = 0.0
    heads = np.arange(NSA_HEADS)
    hk = (heads[:, None] // NSA_GROUP == np.arange(NSA_KV_HEADS)[None, :]).astype(np.float32)
    rowk = np.arange(NSA_KV_W) // NSA_HEAD_DIM
    bd = (rowk[:, None] == heads[None, :] // NSA_GROUP).astype(np.float32)
    rep = (np.arange(NSA_KV_W)[:, None] % NSA_HEAD_DIM == np.arange(NSA_HEAD_DIM)[None, :]).astype(np.float32)
    eexp = (np.arange(past_len)[None, :] // SEL_BLOCK == np.arange(n_sel_pad)[:, None]).astype(np.float32)
    bf16 = jnp.bfloat16
    return (bcmp, bsel, bwin, jnp.asarray(mimp, bf16), jnp.asarray(hk, bf16), jnp.asarray(bd, jnp.float32),
            jnp.asarray(rep, bf16), jnp.asarray(bd.T, jnp.float32), jnp.asarray(hk.T, bf16), jnp.asarray(eexp, bf16))


def _nsa_attn_sample(proj, kt_pages, win_t, win_row0, page_table, cmp_pos, w1, w2, tables):
    b = proj.shape[0]
    n_pages = page_table.shape[1]
    past_len = n_pages * PAGE_SIZE
    n_buf = win_t.shape[3]
    n_chunks = past_len // CMP_STRIDE
    n_groups = n_pages // PAGE_GROUP
    bcmp, bsel, bwin, mimp, hk, bd, rep, bdt, gsel, eexp = tables
    n_sel_pad = mimp.shape[0]
    w1cat, w1f, pos, w2b = _compress_weights(cmp_pos, w1, w2)
    w1cat = jnp.einsum('rlde,hg->rlhdge', w1cat, jnp.eye(2, dtype=w1cat.dtype)).reshape(
        2, CMP_STRIDE * LANES, 4 * CMP_HIDDEN)
    q = proj[:, :NSA_Q_W].reshape(b, NSA_KV_HEADS, NSA_GROUP, NSA_HEAD_DIM) * (NSA_HEAD_DIM ** -0.5)
    eye = jnp.eye(NSA_KV_HEADS, dtype=q.dtype)
    qbd = jnp.einsum('bkgd,kc->bkdcg', q, eye).reshape(b, NSA_KV_W, NSA_HEADS).astype(jnp.bfloat16)
    qbd_t = jnp.swapaxes(qbd, 1, 2)
    proj3 = proj.reshape(b, 1, proj.shape[1])

    def cmp_spec(u, r):
        def imap(i, ph, g, pt):
            gg = jnp.where(ph == 0, g, n_groups - 1)
            return (pt[i, gg * PAGE_GROUP + u], r, 0, 0)
        return pl.BlockSpec((None, NSA_KV_HEADS, NSA_HEAD_DIM, PAGE_SIZE), imap)

    def sel_spec(u, r):
        def imap(i, ph, g, pt):
            gg = jnp.where(ph == 1, g, 0)
            return (pt[i, gg * PAGE_GROUP + u], r, 0, 0)
        return pl.BlockSpec((None, NSA_KV_HEADS, NSA_HEAD_DIM, PAGE_SIZE), imap)

    def const_spec(a):
        nd = a.ndim
        return pl.BlockSpec(a.shape, lambda i, ph, g, pt: (0,) * nd)

    consts = (w1cat, w1f, pos, w2b, bcmp, bsel, bwin, mimp, hk, bd, rep, bdt, gsel, eexp)
    in_specs = ([cmp_spec(u, r) for u in range(PAGE_GROUP) for r in (0, 1)]
                + [sel_spec(u, r) for u in range(PAGE_GROUP) for r in (2, 3)]
                + [pl.BlockSpec((None, NSA_KV_W, NSA_HEADS), lambda i, ph, g, pt: (i, 0, 0)),
                   pl.BlockSpec((None, NSA_HEADS, NSA_KV_W), lambda i, ph, g, pt: (i, 0, 0)),
                   pl.BlockSpec((None, 1, proj.shape[1]), lambda i, ph, g, pt: (i, 0, 0)),
                   pl.BlockSpec((None, 2, NSA_KV_W, n_buf), lambda i, ph, g, pt: (win_row0 + i, 0, 0, 0))]
                + [const_spec(a) for a in consts])
    f32, bf16 = jnp.float32, jnp.bfloat16
    out = pl.pallas_call(
        _nsa_sample_kernel,
        grid_spec=pltpu.PrefetchScalarGridSpec(
            num_scalar_prefetch=1,
            grid=(b, 2, n_groups),
            in_specs=in_specs,
            out_specs=pl.BlockSpec((None, NSA_HEADS, NSA_HEAD_DIM), lambda i, ph, g, pt: (i, 0, 0)),
            scratch_shapes=[pltpu.VMEM((4, n_chunks, CMP_STRIDE * LANES), f32), pltpu.VMEM((4, PAGE_SIZE, LANES), f32),
                            pltpu.VMEM((n_chunks, NSA_KV_W), bf16), pltpu.VMEM((n_chunks, NSA_KV_W), bf16),
                            pltpu.VMEM((n_sel_pad, NSA_KV_HEADS), f32), pltpu.VMEM((NSA_HEADS, past_len), f32),
                            pltpu.VMEM((SEL_PAD, NSA_KV_W), bf16), pltpu.VMEM((SEL_PAD, NSA_KV_W), bf16),
                            pltpu.VMEM((NSA_HEADS, NSA_HEAD_DIM), f32),
                            pltpu.VMEM((NSA_HEADS, 1), f32), pltpu.VMEM((NSA_HEADS, 1), f32),
                            pltpu.VMEM((NSA_HEADS, NSA_KV_W), f32)]),
        out_shape=jax.ShapeDtypeStruct((b, NSA_HEADS, NSA_HEAD_DIM), f32),
        compiler_params=pltpu.CompilerParams(
            dimension_semantics=("parallel", "arbitrary", "arbitrary"), vmem_limit_bytes=VMEM_LIMIT_BYTES),
        name="nsa_sample",
    )(page_table, *([kt_pages] * (4 * PAGE_GROUP)), qbd, qbd_t, proj3, win_t,
      *consts)
    return out.reshape(b, NSA_Q_W)


SSD_PROJ_W = _round_up(SSD_IN_W, 7 * LANES)
SSD_TILE = 128
SSD_GN = SSD_GROUPS * SSD_STATE
SSD_GW = SSD_HPG * SSD_HEAD_DIM
CONV_PAD = 8


def _silu(x):
    return x * jax.nn.sigmoid(x)


def _softplus(x):
    return jnp.maximum(x, 0.0) + jnp.log(1.0 + jnp.exp(-jnp.abs(x)))


def _dot3(x, table, dot=jnp.dot):
    return sum(dot(part, table, preferred_element_type=jnp.float32) for part in _split3_bf16(x))


def _dot3_tn(x, table):
    return sum(_dot_tn(part, table) for part in _split3_bf16(x))


def _ssd_prompt_kernel(z_ref, x_ref, bc_ref, dt_ref, cw_ref, cb_ref, dtb_ref, alog_ref, d_ref, nw_ref,
                       eh_ref, tril_ref, triu_ref, eye_ref, y_ref, st_ref, conv_ref, ux_s, ubc_s, st_s):
    c = pl.program_id(1)
    t = SSD_TILE
    di = SSD_D_INNER
    bf16 = jnp.bfloat16

    @pl.when(c == 0)
    def _():
        ux_s[0:CONV_PAD, :] = jnp.zeros((CONV_PAD, di), jnp.float32)
        ubc_s[0:CONV_PAD, :] = jnp.zeros((CONV_PAD, 2 * SSD_GN), jnp.float32)
        st_s[...] = jnp.zeros(st_s.shape, jnp.float32)

    ux_s[CONV_PAD:CONV_PAD + t, :] = x_ref[...]
    ubc_s[CONV_PAD:CONV_PAD + t, :] = bc_ref[...]

    def conv(buf, col0, width):
        y = cb_ref[:, col0:col0 + width]
        for i in range(SSD_CONV_W):
            y = y + buf[pl.ds(CONV_PAD - (SSD_CONV_W - 1 - i), t), :] * cw_ref[i:i + 1, col0:col0 + width]
        return _silu(y)

    xs = conv(ux_s, 0, di)
    bcs = conv(ubc_s, di, 2 * SSD_GN)
    dt = _softplus(dt_ref[:, 0:SSD_HEADS] + dtb_ref[...])
    dta = dt * (-jnp.exp(alog_ref[...]))
    cum = _dot3(dta, tril_ref[...], lambda a, b, **kw: jnp.dot(b, a, **kw))
    cum_t = _dot3_tn(dta, triu_ref[...])
    dt_t = _dot3_tn(dt, eye_ref[...])
    last = cum[t - 1:t, :]
    dec_in = _dot3(jnp.exp(cum), eh_ref[...])
    wgt = _dot3(jnp.exp(last - cum) * dt, eh_ref[...])
    st_scale = _dot3(jnp.broadcast_to(jnp.exp(last), (8, SSD_HEADS)), eh_ref[...])[0:1, :]
    causal = lax.broadcasted_iota(jnp.int32, (t, t), 1) <= lax.broadcasted_iota(jnp.int32, (t, t), 0)

    for g in range(SSD_GROUPS):
        gl = slice(g * SSD_GW, (g + 1) * SSD_GW)
        bg = bcs[:, g * SSD_STATE:(g + 1) * SSD_STATE].astype(bf16)
        cg = bcs[:, SSD_GN + g * SSD_STATE:SSD_GN + (g + 1) * SSD_STATE].astype(bf16)
        cb = _dot_nt(cg, bg)
        xg = xs[:, gl]
        y_heads = []
        for j in range(SSD_HPG):
            h = g * SSD_HPG + j
            decay = jnp.exp(jnp.minimum(cum[:, h:h + 1] - cum_t[h:h + 1, :], 0.0))
            w = jnp.where(causal, cb * decay * dt_t[h:h + 1, :], 0.0).astype(bf16)
            y_heads.append(jnp.dot(w, xg[:, j * SSD_HEAD_DIM:(j + 1) * SSD_HEAD_DIM].astype(bf16),
                                   preferred_element_type=jnp.float32))
        st = st_s[g]
        y = jnp.concatenate(y_heads, axis=1)
        y = y + jnp.dot(cg, st.astype(bf16), preferred_element_type=jnp.float32) * dec_in[:, gl]
        st_s[g] = st_scale[:, gl] * st + _dot_tn(bg, (xg * wgt[:, gl]).astype(bf16))
        y = (y + d_ref[:, gl] * xg) * _silu(z_ref[:, gl])
        y = y * lax.rsqrt(jnp.mean(y * y, axis=-1, keepdims=True) + NORM_EPS) * nw_ref[:, gl]
        y_ref[:, gl] = y.astype(y_ref.dtype)

    ux_s[0:CONV_PAD, :] = ux_s[t:t + CONV_PAD, :]
    ubc_s[0:CONV_PAD, :] = ubc_s[t:t + CONV_PAD, :]

    @pl.when(c == pl.num_programs(1) - 1)
    def _():
        st_ref[...] = st_s[...]
        keep = SSD_CONV_W - 1
        conv_ref[:, 0:di] = ux_s[CONV_PAD - keep:CONV_PAD, :]
        conv_ref[:, di:] = ubc_s[CONV_PAD - keep:CONV_PAD, :]


def _ssd_tables():
    eh = (np.arange(SSD_HEADS)[:, None] == np.arange(SSD_D_INNER)[None, :] // SSD_HEAD_DIM).astype(np.float32)
    tril = np.tril(np.ones((SSD_TILE, SSD_TILE), np.float32))
    bf16 = jnp.bfloat16
    return (jnp.asarray(eh, bf16), jnp.asarray(tril, bf16), jnp.asarray(tril.T, bf16),
            jnp.asarray(np.eye(SSD_TILE, dtype=np.float32), bf16))


def _state_from_transposed(st_t):
    b = st_t.shape[0]
    st = st_t.reshape(b, SSD_GROUPS, SSD_STATE, SSD_HPG, SSD_HEAD_DIM)
    return jnp.transpose(st, (0, 1, 3, 4, 2)).reshape(b, SSD_HEADS, SSD_HEAD_DIM, SSD_STATE)


def _ssd_prompt(proj, conv_w, conv_b, dt_bias, a_log, d_skip, norm_w):
    b, t, _ = proj.shape
    di = SSD_D_INNER
    tt = SSD_TILE
    d_exp = jnp.repeat(d_skip, SSD_HEAD_DIM).reshape(1, di)
    consts = (conv_w, conv_b.reshape(1, -1), dt_bias.reshape(1, -1), a_log.reshape(1, -1), d_exp,
              norm_w.reshape(1, di)) + _ssd_tables()

    def const_spec(a):
        nd = a.ndim
        return pl.BlockSpec(a.shape, lambda i, c: (0,) * nd)

    y, st_t, conv_new = pl.pallas_call(
        _ssd_prompt_kernel,
        grid=(b, t // tt),
        in_specs=[pl.BlockSpec((None, tt, di), lambda i, c: (i, c, 0)),
                  pl.BlockSpec((None, tt, di), lambda i, c: (i, c, 1)),
                  pl.BlockSpec((None, tt, 2 * SSD_GN), lambda i, c: (i, c, 2)),
                  pl.BlockSpec((None, tt, LANES), lambda i, c: (i, c, (di + SSD_CONV_DIM) // LANES))]
                 + [const_spec(a) for a in consts],
        out_specs=[pl.BlockSpec((None, tt, di), lambda i, c: (i, c, 0)),
                   pl.BlockSpec((None, SSD_GROUPS, SSD_STATE, SSD_GW), lambda i, c: (i, 0, 0, 0)),
                   pl.BlockSpec((None, SSD_CONV_W - 1, SSD_CONV_DIM), lambda i, c: (i, 0, 0))],
        out_shape=[jax.ShapeDtypeStruct((b, t, di), jnp.bfloat16),
                   jax.ShapeDtypeStruct((b, SSD_GROUPS, SSD_STATE, SSD_GW), jnp.float32),
                   jax.ShapeDtypeStruct((b, SSD_CONV_W - 1, SSD_CONV_DIM), jnp.float32)],
        scratch_shapes=[pltpu.VMEM((CONV_PAD + tt, di), jnp.float32),
                        pltpu.VMEM((CONV_PAD + tt, 2 * SSD_GN), jnp.float32),
                        pltpu.VMEM((SSD_GROUPS, SSD_STATE, SSD_GW), jnp.float32)],
        compiler_params=pltpu.CompilerParams(
            dimension_semantics=("parallel", "arbitrary"), vmem_limit_bytes=VMEM_LIMIT_BYTES),
        name="ssd_prompt",
    )(proj, proj, proj, proj, *consts)
    return y.reshape(b * t, di), _state_from_transposed(st_t), conv_new


ROW_PAD = 8


def _row8(x):
    return jnp.concatenate([x, jnp.zeros((ROW_PAD - 1, x.shape[1]), x.dtype)], axis=0)


def _ssd_step_kernel(p_ref, conv_ref, st_ref, cw_ref, cb_ref, dtb_ref, alog_ref, d_ref, nw_ref,
                     y_ref, st_out, conv_out, y_s):
    di = SSD_D_INNER
    bf16 = jnp.bfloat16
    u = p_ref[:, di:di + SSD_CONV_DIM]
    keep = SSD_CONV_W - 1
    y = cb_ref[...] + u * cw_ref[keep:keep + 1, :]
    for i in range(keep):
        y = y + conv_ref[i:i + 1, :] * cw_ref[i:i + 1, :]
    conv_out[0:keep - 1, :] = conv_ref[1:keep, :]
    conv_out[keep - 1:keep, :] = u
    xbc = _silu(y)
    xs = xbc[:, :di]
    dt = _softplus(p_ref[:, di + SSD_CONV_DIM:di + SSD_CONV_DIM + SSD_HEADS] + dtb_ref[...])
    decay = jnp.exp(dt * (-jnp.exp(alog_ref[...])))
    for g in range(SSD_GROUPS):
        bg = _row8(xbc[:, di + g * SSD_STATE:di + (g + 1) * SSD_STATE]).astype(bf16)
        cg = _row8(xbc[:, di + SSD_GN + g * SSD_STATE:di + SSD_GN + (g + 1) * SSD_STATE]).astype(bf16)
        for j in range(SSD_HPG):
            h = g * SSD_HPG + j
            cols = slice(h * SSD_HEAD_DIM, (h + 1) * SSD_HEAD_DIM)
            xh = _row8(xs[:, cols] * dt[:, h:h + 1]).astype(bf16)
            st = decay[:, h:h + 1] * st_ref[h] + _dot_tn(xh, bg)
            st_out[h] = st
            y_s[:, cols] = _dot_nt(cg, st.astype(bf16))
    yv = y_s[0:1, :]
    yv = (yv + d_ref[...] * xs) * _silu(p_ref[:, 0:di])
    for g in range(SSD_GROUPS):
        gl = slice(g * SSD_GW, (g + 1) * SSD_GW)
        yg = yv[:, gl]
        y_ref[:, gl] = yg * lax.rsqrt(jnp.mean(yg * yg, axis=-1, keepdims=True) + NORM_EPS) * nw_ref[:, gl]


def _ssd_step(proj, ssm0, conv0, conv_w, conv_b, dt_bias, a_log, d_skip, norm_w):
    b = proj.shape[0]
    di = SSD_D_INNER
    consts = (conv_w, conv_b.reshape(1, -1), dt_bias.reshape(1, -1), a_log.reshape(1, -1),
              jnp.repeat(d_skip, SSD_HEAD_DIM).reshape(1, di), norm_w.reshape(1, di))

    def const_spec(a):
        nd = a.ndim
        return pl.BlockSpec(a.shape, lambda i: (0,) * nd)

    y, st, conv_new = pl.pallas_call(
        _ssd_step_kernel,
        grid=(b,),
        in_specs=[pl.BlockSpec((None, 1, proj.shape[1]), lambda i: (i, 0, 0)),
                  pl.BlockSpec((None,) + conv0.shape[1:], lambda i: (i, 0, 0)),
                  pl.BlockSpec((None,) + ssm0.shape[1:], lambda i: (i, 0, 0, 0))]
                 + [const_spec(a) for a in consts],
        out_specs=[pl.BlockSpec((None, 1, di), lambda i: (i, 0, 0)),
                   pl.BlockSpec((None,) + ssm0.shape[1:], lambda i: (i, 0, 0, 0)),
                   pl.BlockSpec((None,) + conv0.shape[1:], lambda i: (i, 0, 0))],
        out_shape=[jax.ShapeDtypeStruct((b, 1, di), jnp.float32),
                   jax.ShapeDtypeStruct(ssm0.shape, jnp.float32),
                   jax.ShapeDtypeStruct(conv0.shape, jnp.float32)],
        scratch_shapes=[pltpu.VMEM((ROW_PAD, di), jnp.float32)],
        compiler_params=pltpu.CompilerParams(
            dimension_semantics=("parallel",), vmem_limit_bytes=VMEM_LIMIT_BYTES),
        name="ssd_step",
    )(proj.reshape(b, 1, -1), conv0, ssm0, *consts)
    return y.reshape(b, di), st, conv_new


def _hgrn_gates(p_ref, lb_ref):
    wk = HG_HEADS * HG_DK
    q = _silu(p_ref[:, 0:wk])
    f = lb_ref[...] + (1.0 - lb_ref[...]) * jax.nn.sigmoid(p_ref[:, wk:2 * wk])
    return q, f


HG_TILE = 128
HG_SUB = 16


def _hgrn_prompt_kernel(p_ref, lb_ref, gn_ref, tril_ref, subend_ref, ones_ref, y_ref, st_ref, st_s):
    c = pl.program_id(1)
    t = HG_TILE
    wk = HG_HEADS * HG_DK
    wv = HG_HEADS * HG_DV
    bf16 = jnp.bfloat16
    n_sub = t // HG_SUB

    @pl.when(c == 0)
    def _():
        st_s[...] = jnp.zeros(st_s.shape, jnp.float32)

    row = lax.broadcasted_iota(jnp.int32, (t, HG_DK), 0)
    sub_pos = row % HG_SUB
    row_sub = lax.broadcasted_iota(jnp.int32, (t, t), 0) // HG_SUB
    col_sub = lax.broadcasted_iota(jnp.int32, (t, t), 1) // HG_SUB
    left = lambda a, b, **kw: jnp.dot(b, a, **kw)

    def head(h, carry):
        kc = pl.ds(pl.multiple_of(h * HG_DK, HG_DK), HG_DK)
        q = _silu(p_ref[:, kc])
        lb = lb_ref[:, kc]
        f = lb + (1.0 - lb) * jax.nn.sigmoid(p_ref[:, pl.ds(pl.multiple_of(wk + h * HG_DK, HG_DK), HG_DK)])
        k = 1.0 - f
        v = p_ref[:, pl.ds(pl.multiple_of(2 * wk + h * HG_DV, HG_DV), HG_DV)]
        gate = p_ref[:, pl.ds(pl.multiple_of(2 * wk + wv + h * HG_DV, HG_DV), HG_DV)]
        cum = _dot3(jnp.log(f), tril_ref[...], left)
        sub_end = _dot3(cum, subend_ref[...], left)
        k_hat = k * jnp.exp(sub_end - cum)
        a_off = jnp.zeros((t, t), jnp.float32)
        for j in range(n_sub - 1):
            end_j = cum[(j + 1) * HG_SUB - 1:(j + 1) * HG_SUB, :]
            q_j = (q * jnp.exp(jnp.minimum(cum - end_j, 0.0))).astype(bf16)
            k_j = jnp.where(row // HG_SUB == j, k_hat, 0.0).astype(bf16)
            a_off = a_off + _dot_nt(q_j, k_j)
        a_off = jnp.where(col_sub < row_sub, a_off, 0.0)
        v16 = v.astype(bf16)
        o = jnp.dot(a_off.astype(bf16), v16, preferred_element_type=jnp.float32)
        for d in range(HG_SUB):
            k_d, cum_d, v_d = (k, cum, v) if d == 0 else (pltpu.roll(k, d, 0), pltpu.roll(cum, d, 0), pltpu.roll(v, d, 0))
            e = (q * k_d * jnp.exp(jnp.minimum(cum - cum_d, 0.0))).astype(bf16)
            a_d = jnp.dot(e, ones_ref[...], preferred_element_type=jnp.float32)
            o = o + jnp.where(sub_pos >= d, a_d, 0.0) * v_d
        st = st_s[h]
        o = o + _dot_nt((q * jnp.exp(cum)).astype(bf16), st.astype(bf16))
        last = cum[t - 1:t, :]
        st_s[h] = st * jnp.exp(last) + _dot_tn(v16, (k * jnp.exp(last - cum)).astype(bf16))
        o = o * lax.rsqrt(jnp.mean(o * o, axis=-1, keepdims=True) + NORM_EPS) * gn_ref[...]
        y_ref[:, pl.ds(pl.multiple_of(h * HG_DV, HG_DV), HG_DV)] = (o * _silu(gate)).astype(y_ref.dtype)
        return carry

    lax.fori_loop(0, HG_HEADS, head, 0)

    @pl.when(c == pl.num_programs(1) - 1)
    def _():
        st_ref[...] = st_s[...]


def _hgrn_prompt(proj, lb, g_norm):
    b, t, w = proj.shape
    tt = HG_TILE
    wv = HG_HEADS * HG_DV
    idx = np.arange(tt)
    tril = np.tril(np.ones((tt, tt), np.float32))
    subend = (idx[None, :] == (idx[:, None] // HG_SUB) * HG_SUB + HG_SUB - 1).astype(np.float32)
    bf16 = jnp.bfloat16
    consts = (lb.reshape(1, -1), g_norm.reshape(1, -1), jnp.asarray(tril, bf16), jnp.asarray(subend, bf16),
              jnp.ones((HG_DK, HG_DK), bf16))

    def const_spec(a):
        nd = a.ndim
        return pl.BlockSpec(a.shape, lambda i, c: (0,) * nd)

    y, st_t = pl.pallas_call(
        _hgrn_prompt_kernel,
        grid=(b, t // tt),
        in_specs=[pl.BlockSpec((None, tt, w), lambda i, c: (i, c, 0))] + [const_spec(a) for a in consts],
        out_specs=[pl.BlockSpec((None, tt, wv), lambda i, c: (i, c, 0)),
                   pl.BlockSpec((None, HG_HEADS, HG_DV, HG_DK), lambda i, c: (i, 0, 0, 0))],
        out_shape=[jax.ShapeDtypeStruct((b, t, wv), bf16),
                   jax.ShapeDtypeStruct((b, HG_HEADS, HG_DV, HG_DK), jnp.float32)],
        scratch_shapes=[pltpu.VMEM((HG_HEADS, HG_DV, HG_DK), jnp.float32)],
        compiler_params=pltpu.CompilerParams(
            dimension_semantics=("parallel", "arbitrary"), vmem_limit_bytes=VMEM_LIMIT_BYTES),
        name="hgrn_prompt",
    )(proj, *consts)
    return y.reshape(b * t, wv), jnp.swapaxes(st_t, 2, 3)


def _hgrn_step_kernel(p_ref, st_ref, lb_ref, gn_ref, y_ref, st_out):
    wk = HG_HEADS * HG_DK
    wv = HG_HEADS * HG_DV
    bf16 = jnp.bfloat16
    q, f = _hgrn_gates(p_ref, lb_ref)
    eye = lax.broadcasted_iota(jnp.int32, (HG_DK, HG_DK), 0) == lax.broadcasted_iota(jnp.int32, (HG_DK, HG_DK), 1)
    for h in range(HG_HEADS):
        kc = slice(h * HG_DK, (h + 1) * HG_DK)
        vc = slice(2 * wk + h * HG_DV, 2 * wk + (h + 1) * HG_DV)
        gc = slice(2 * wk + wv + h * HG_DV, 2 * wk + wv + (h + 1) * HG_DV)
        fh = f[:, kc]
        f_col = jnp.sum(jnp.where(eye, fh, 0.0), axis=1, keepdims=True)
        kv = _dot_tn(_row8(1.0 - fh).astype(bf16), _row8(p_ref[:, vc]).astype(bf16))
        st = f_col * st_ref[h] + kv
        st_out[h] = st
        o = jnp.dot(_row8(q[:, kc]).astype(bf16), st.astype(bf16), preferred_element_type=jnp.float32)[0:1, :]
        o = o * lax.rsqrt(jnp.mean(o * o, axis=-1, keepdims=True) + NORM_EPS) * gn_ref[...]
        y_ref[:, h * HG_DV:(h + 1) * HG_DV] = o * _silu(p_ref[:, gc])


def _hgrn_step(proj, s0, lb, g_norm):
    b = proj.shape[0]
    wv = HG_HEADS * HG_DV
    y, st = pl.pallas_call(
        _hgrn_step_kernel,
        grid=(b,),
        in_specs=[pl.BlockSpec((None, 1, proj.shape[1]), lambda i: (i, 0, 0)),
                  pl.BlockSpec((None,) + s0.shape[1:], lambda i: (i, 0, 0, 0)),
                  pl.BlockSpec((1, HG_HEADS * HG_DK), lambda i: (0, 0)),
                  pl.BlockSpec((1, HG_DV), lambda i: (0, 0))],
        out_specs=[pl.BlockSpec((None, 1, wv), lambda i: (i, 0, 0)),
                   pl.BlockSpec((None,) + s0.shape[1:], lambda i: (i, 0, 0, 0))],
        out_shape=[jax.ShapeDtypeStruct((b, 1, wv), jnp.float32), jax.ShapeDtypeStruct(s0.shape, jnp.float32)],
        compiler_params=pltpu.CompilerParams(
            dimension_semantics=("parallel",), vmem_limit_bytes=VMEM_LIMIT_BYTES),
        name="hgrn_step",
    )(proj.reshape(b, 1, -1), s0, lb.reshape(1, -1), g_norm.reshape(1, -1))
    return y.reshape(b, wv), st


def _rel_bucket(dist):
    n = jnp.maximum(dist, 0)
    n_exact = REL_BUCKETS // 2
    nf = jnp.maximum(n, 1).astype(jnp.float32)
    large = n_exact + (jnp.log(nf / n_exact) / math.log(REL_MAX_DIST / n_exact)
                       * (REL_BUCKETS - n_exact)).astype(jnp.int32)
    return jnp.where(n < n_exact, n, jnp.minimum(large, REL_BUCKETS - 1))


def _nsa_prompt_core(proj, cmp_pos, cmp_w1, cmp_w2, tables):
    b, t, _ = proj.shape
    cmp = _nsa_compress_prompt(proj, cmp_pos, cmp_w1, cmp_w2)
    merged = _nsa_attn_prompt_t(proj, cmp, tables)
    o1 = NSA_Q_W
    o2 = o1 + 4 * NSA_KV_W
    o3 = o2 + 2 * NSA_KV_W
    kv_cs = proj[..., o1:o2].reshape(b, t, 4, NSA_KV_HEADS, NSA_HEAD_DIM)
    kv_win = proj[:, t - min(WINDOW, t):, o2:o3].reshape(b, min(WINDOW, t), 2, NSA_KV_HEADS, NSA_HEAD_DIM)
    return merged, kv_cs, kv_win


def _nsa_sample_core(proj, caches, layer, win_buf, page_table, cmp_pos, cmp_w1, cmp_w2, tables):
    b, t, _ = proj.shape
    assert t == 1 and win_buf.shape[1] == WINDOW and page_table.shape[1] % PAGE_GROUP == 0
    kt_pages, win_t, n_phys = caches
    merged = _nsa_attn_sample(proj.reshape(b, -1), kt_pages, win_t, layer * b,
                              page_table + layer * n_phys, cmp_pos, cmp_w1, cmp_w2, tables)
    o1 = NSA_Q_W
    o2 = o1 + 4 * NSA_KV_W
    o3 = o2 + 2 * NSA_KV_W
    kv_cs = proj[..., o1:o2].reshape(b, t, 4, NSA_KV_HEADS, NSA_HEAD_DIM)
    kv_win = proj[..., o2:o3].reshape(b, t, 2, NSA_KV_HEADS, NSA_HEAD_DIM)
    new_win = jnp.concatenate([win_buf[:, t:], kv_win], axis=1)
    return merged, kv_cs, new_win


def _pad_cols(w, n):
    return jnp.pad(w, ((0, 0), (0, n - w.shape[1])))


def kernel(x_prompt, x_sample, cache_nsa_kv, cache_nsa_win, state_hgrn, state_ssd, state_ssd_conv, page_table, c_prompt, c_sample, rel_bias, hgrn_lower_bounds, w_ada, b_ada, norm_gains, w_mlp_in, w_mlp_out, nsa_w_in, nsa_cmp_pos, nsa_cmp_w1, nsa_cmp_w2, nsa_w_out, hg_w_in, hg_norm, hg_w_out, ssd_w_in, ssd_conv_w, ssd_conv_b, ssd_dt_bias, ssd_a_log, ssd_d, ssd_norm, ssd_w_out):
    bf16 = jnp.bfloat16
    bp, tp, d = x_prompt.shape
    bs, ts, _ = x_sample.shape
    mp, ms = bp * tp, bs * ts
    lb_p = jax.nn.softmax(hgrn_lower_bounds, axis=0)
    lower_bounds = jnp.cumsum(lb_p, axis=0) - lb_p[0]

    mod = _ada_all(jnp.concatenate([c_prompt, c_sample], axis=0), w_ada, b_ada)
    mod = mod.reshape(DEPTH, bp + bs, ADA_CHUNKS, d)
    mod_p = mod[:, :bp].transpose(0, 2, 1, 3)[:, :, :, None, :]
    mod_s = mod[:, bp:].transpose(0, 2, 1, 3)[:, :, None, :, :]

    xp = x_prompt.reshape(mp, d)
    xs = x_sample.reshape(ms, d)
    tm_p, tm_s = PROMPT_ROW_TILE, ms
    nsa_tables = _nsa_prompt_tables_t(rel_bias, tp)
    nsa_tables_s = _nsa_sample_tables(rel_bias, page_table.shape[1] * PAGE_SIZE, cache_nsa_win.shape[2])
    n_phys = cache_nsa_kv.shape[1]
    n_all = cache_nsa_kv.shape[0] * n_phys
    kt_pages = jnp.transpose(cache_nsa_kv, (0, 1, 3, 4, 5, 2)).reshape(n_all, 4 * NSA_KV_HEADS, NSA_HEAD_DIM, PAGE_SIZE)
    win_t = jnp.transpose(cache_nsa_win, (0, 1, 3, 4, 5, 2)).reshape(-1, 2, NSA_KV_W, cache_nsa_win.shape[2])
    nsa_caches = (kt_pages, win_t, n_phys)

    kv_p, kv_s, win_p, win_s = [], [], [], []
    hg_p, hg_s, ssd_p, ssd_s, conv_p, conv_s = [], [], [], [], [], []
    for i in range(DEPTH):
        j = i // N_MIXERS
        kind = i % N_MIXERS
        g = norm_gains[i]
        shp_m, scp_m, gtp_m, shp_f, scp_f, gtp_f = [mod_p[i, c] for c in range(ADA_CHUNKS)]
        shs_m, scs_m, gts_m, shs_f, scs_f, gts_f = [mod_s[i, c] for c in range(ADA_CHUNKS)]
        if kind == 0:
            n_pad = NSA_PROJ_W
            w_in = _pad_cols(nsa_w_in[j], n_pad).astype(bf16)
            w_out = nsa_w_out[j].astype(bf16)
            pp = _norm_mod_matmul(xp, g[0], scp_m, shp_m, w_in, tp, tm_p).reshape(bp, tp, n_pad)
            ps = _norm_mod_matmul(xs, g[0], scs_m, shs_m, w_in, ts, tm_s).reshape(bs, ts, n_pad)
            ap, new_kv_p, new_win_p = _nsa_prompt_core(pp, nsa_cmp_pos[j], nsa_cmp_w1[j], nsa_cmp_w2[j], nsa_tables)
            as_, new_kv_s, new_win_s = _nsa_sample_core(ps, nsa_caches, j, cache_nsa_win[j], page_table,
                                                        nsa_cmp_pos[j], nsa_cmp_w1[j], nsa_cmp_w2[j], nsa_tables_s)
            kv_p.append(new_kv_p)
            kv_s.append(new_kv_s)
            win_p.append(new_win_p)
            win_s.append(new_win_s)
        elif kind == 1:
            w_in = hg_w_in[j].astype(bf16)
            w_out = hg_w_out[j].astype(bf16)
            pp = _norm_mod_matmul(xp, g[0], scp_m, shp_m, w_in, tp, tm_p).reshape(bp, tp, -1)
            ps = _norm_mod_matmul(xs, g[0], scs_m, shs_m, w_in, ts, tm_s).reshape(bs, ts, -1)
            ap, new_hp = _hgrn_prompt(pp, lower_bounds[i], hg_norm[j])
            as_, new_hs = _hgrn_step(ps.reshape(bs, -1), state_hgrn[j], lower_bounds[i], hg_norm[j])
            hg_p.append(new_hp)
            hg_s.append(new_hs)
        else:
            n_pad = SSD_PROJ_W
            w_in = _pad_cols(ssd_w_in[j], n_pad).astype(bf16)
            w_out = ssd_w_out[j].astype(bf16)
            pp = _norm_mod_matmul(xp, g[0], scp_m, shp_m, w_in, tp, tm_p).reshape(bp, tp, n_pad)
            ps = _norm_mod_matmul(xs, g[0], scs_m, shs_m, w_in, ts, tm_s).reshape(bs, ts, n_pad)
            ap, new_sp, new_cp = _ssd_prompt(pp, ssd_conv_w[j], ssd_conv_b[j], ssd_dt_bias[j],
                                             ssd_a_log[j], ssd_d[j], ssd_norm[j])
            as_, new_ss, new_cs = _ssd_step(ps.reshape(bs, -1), state_ssd[j], state_ssd_conv[j], ssd_conv_w[j],
                                            ssd_conv_b[j], ssd_dt_bias[j], ssd_a_log[j], ssd_d[j], ssd_norm[j])
            ssd_p.append(new_sp)
            ssd_s.append(new_ss)
            conv_p.append(new_cp)
            conv_s.append(new_cs)
        xp = _matmul_norm_res(ap, w_out, xp, g[1], gtp_m, tp, tm_p)
        xs = _matmul_norm_res(as_, w_out, xs, g[1], gts_m, ts, tm_s)
        w1 = w_mlp_in[i].astype(bf16)
        w2 = w_mlp_out[i].astype(bf16)
        xp = _mlp(xp, g[2], scp_f, shp_f, w1, w2, g[3], gtp_f, tp, tm_p)
        xs = _mlp(xs, g[2], scs_f, shs_f, w1, w2, g[3], gts_f, ts, tm_s)
    return (xp.reshape(bp, tp, d), xs.reshape(bs, ts, d),
            jnp.stack(kv_p), jnp.stack(kv_s), jnp.stack(win_p), jnp.stack(win_s),
            jnp.stack(hg_p), jnp.stack(hg_s), jnp.stack(ssd_p), jnp.stack(ssd_s),
            jnp.stack(conv_p), jnp.stack(conv_s))
```

```python
import functools
import math

import jax
import jax.numpy as jnp
import numpy as np
from jax import lax
from jax.experimental import pallas as pl
from jax.experimental.pallas import tpu as pltpu

D_MODEL = 1024
DEPTH = 4
PAGE_SIZE = 128
N_MIXERS = 3
ADA_CHUNKS = 6
NORM_EPS = 1e-6
D_FF = 4 * D_MODEL

NSA_HEADS = 16
NSA_HEAD_DIM = D_MODEL // NSA_HEADS
NSA_KV_HEADS = 4
NSA_GROUP = NSA_HEADS // NSA_KV_HEADS
CMP_STRIDE = 16
CMP_LEN = 2 * CMP_STRIDE
CMP_HIDDEN = 2 * NSA_HEAD_DIM
SEL_BLOCK = 64
SEL_TOP_N = 16
WINDOW = 512
WIN_Q_BLOCK = 128
SEL_Q_BLOCK = 16
NSA_Q_W = NSA_HEADS * NSA_HEAD_DIM
NSA_KV_W = NSA_KV_HEADS * NSA_HEAD_DIM
NSA_IN_W = NSA_Q_W + 6 * NSA_KV_W + 3 * NSA_HEADS

REL_BUCKETS = 32
REL_MAX_DIST = 128

HG_EXPAND = 128
HG_HEADS = D_MODEL // HG_EXPAND
HG_DK = HG_EXPAND
HG_DV = D_MODEL // HG_HEADS
HG_CHUNK = 64

SSD_D_INNER = 2 * D_MODEL
SSD_HEAD_DIM = 64
SSD_HEADS = SSD_D_INNER // SSD_HEAD_DIM
SSD_GROUPS = 8
SSD_HPG = SSD_HEADS // SSD_GROUPS
SSD_STATE = 128
SSD_CONV_W = 4
SSD_CONV_DIM = SSD_D_INNER + 2 * SSD_GROUPS * SSD_STATE
SSD_IN_W = SSD_D_INNER + SSD_CONV_DIM + SSD_HEADS
SSD_CHUNK = 128

NEG_INF = -1e30
FORCE_SCORE = 1e4

LANES = 128
VMEM_LIMIT_BYTES = 48 * 1024 * 1024
PROMPT_ROW_TILE = 512


def _round_up(n, m):
    return -(-n // m) * m


def _col_tile(n, cap=1536):
    best = LANES
    for t in range(LANES, cap + 1, LANES):
        if n % t == 0:
            best = t
    return best


def _rms(x, g):
    return x * lax.rsqrt(jnp.mean(x * x, axis=-1, keepdims=True) + NORM_EPS) * g


def _mod_spec(mod, rows_per_mod, tm, ngrid):
    r = mod.shape[1]
    if r == 1:
        per = rows_per_mod // tm
        if ngrid == 1:
            return pl.BlockSpec((None, 1, mod.shape[2]), lambda i: (i // per, 0, 0))
        return pl.BlockSpec((None, 1, mod.shape[2]), lambda i, j: (i // per, 0, 0))
    if ngrid == 1:
        return pl.BlockSpec((None, r, mod.shape[2]), lambda i: (0, 0, 0))
    return pl.BlockSpec((None, r, mod.shape[2]), lambda i, j: (0, 0, 0))


def _ada_kernel(c_ref, w_ref, b_ref, o_ref):
    c = c_ref[...]
    s = (c * jax.nn.sigmoid(c)).astype(jnp.bfloat16)
    o_ref[...] = jnp.dot(s, w_ref[...].astype(jnp.bfloat16),
                         preferred_element_type=jnp.float32) + b_ref[...]


def _ada_all(c_all, w_ada, b_ada):
    rows = c_all.shape[0]
    n = ADA_CHUNKS * D_MODEL
    tn = 1024
    return pl.pallas_call(
        _ada_kernel,
        grid=(DEPTH, n // tn),
        in_specs=[pl.BlockSpec((rows, D_MODEL), lambda l, j: (0, 0)),
                  pl.BlockSpec((None, D_MODEL, tn), lambda l, j: (l, 0, j)),
                  pl.BlockSpec((None, 1, tn), lambda l, j: (l, 0, j))],
        out_specs=pl.BlockSpec((None, rows, tn), lambda l, j: (l, 0, j)),
        out_shape=jax.ShapeDtypeStruct((DEPTH, rows, n), jnp.float32),
        compiler_params=pltpu.CompilerParams(
            dimension_semantics=("parallel", "parallel"), vmem_limit_bytes=VMEM_LIMIT_BYTES),
        name="ada",
    )(c_all, w_ada, b_ada.reshape(DEPTH, 1, n))


def _norm_mod_matmul_kernel(x_ref, g_ref, sc_ref, sh_ref, w_ref, o_ref, h_ref):
    @pl.when(pl.program_id(1) == 0)
    def _():
        h = _rms(x_ref[...], g_ref[...]) * (1.0 + sc_ref[...]) + sh_ref[...]
        h_ref[...] = h.astype(jnp.bfloat16)

    o_ref[...] = jnp.dot(h_ref[...], w_ref[...], preferred_element_type=jnp.float32)


def _norm_mod_matmul(x, g, sc, sh, w, rows_per_mod, tm):
    m, d = x.shape
    n = w.shape[1]
    tn = _col_tile(n)
    return pl.pallas_call(
        _norm_mod_matmul_kernel,
        grid=(m // tm, n // tn),
        in_specs=[pl.BlockSpec((tm, d), lambda i, j: (i, 0)),
                  pl.BlockSpec((1, d), lambda i, j: (0, 0)),
                  _mod_spec(sc, rows_per_mod, tm, 2),
                  _mod_spec(sh, rows_per_mod, tm, 2),
                  pl.BlockSpec((d, tn), lambda i, j: (0, j))],
        out_specs=pl.BlockSpec((tm, tn), lambda i, j: (i, j)),
        out_shape=jax.ShapeDtypeStruct((m, n), jnp.float32),
        scratch_shapes=[pltpu.VMEM((tm, d), jnp.bfloat16)],
        compiler_params=pltpu.CompilerParams(
            dimension_semantics=("parallel", "arbitrary"), vmem_limit_bytes=VMEM_LIMIT_BYTES),
        name="norm_mod_matmul",
    )(x, g.reshape(1, d), sc, sh, w)


def _matmul_norm_res_kernel(a_ref, w_ref, x_ref, g_ref, gt_ref, o_ref):
    y = jnp.dot(a_ref[...].astype(jnp.bfloat16), w_ref[...], preferred_element_type=jnp.float32)
    o_ref[...] = x_ref[...] + gt_ref[...] * _rms(y, g_ref[...])


def _matmul_norm_res(a, w, x, g, gate, rows_per_mod, tm):
    m, k = a.shape
    d = w.shape[1]
    return pl.pallas_call(
        _matmul_norm_res_kernel,
        grid=(m // tm,),
        in_specs=[pl.BlockSpec((tm, k), lambda i: (i, 0)),
                  pl.BlockSpec((k, d), lambda i: (0, 0)),
                  pl.BlockSpec((tm, d), lambda i: (i, 0)),
                  pl.BlockSpec((1, d), lambda i: (0, 0)),
                  _mod_spec(gate, rows_per_mod, tm, 1)],
        out_specs=pl.BlockSpec((tm, d), lambda i: (i, 0)),
        out_shape=jax.ShapeDtypeStruct((m, d), jnp.float32),
        compiler_params=pltpu.CompilerParams(
            dimension_semantics=("parallel",), vmem_limit_bytes=VMEM_LIMIT_BYTES),
        name="matmul_norm_res",
    )(a, w, x, g.reshape(1, d), gate)


def _mlp_kernel(x_ref, g2_ref, sc_ref, sh_ref, w1_ref, w2_ref, g3_ref, gt_ref, o_ref, h_ref, acc_ref):
    j = pl.program_id(1)

    @pl.when(j == 0)
    def _():
        h = _rms(x_ref[...], g2_ref[...]) * (1.0 + sc_ref[...]) + sh_ref[...]
        h_ref[...] = h.astype(jnp.bfloat16)

    u = jnp.dot(h_ref[...], w1_ref[...], preferred_element_type=jnp.float32)
    u = jnp.square(jnp.maximum(u, 0.0)).astype(jnp.bfloat16)
    part = jnp.dot(u, w2_ref[...], preferred_element_type=jnp.float32)

    @pl.when(j == 0)
    def _():
        acc_ref[...] = part

    @pl.when(j > 0)
    def _():
        acc_ref[...] += part

    @pl.when(j == pl.num_programs(1) - 1)
    def _():
        o_ref[...] = x_ref[...] + gt_ref[...] * _rms(acc_ref[...], g3_ref[...])


def _mlp(x, g2, sc, sh, w1, w2, g3, gate, rows_per_mod, tm):
    m, d = x.shape
    f = w1.shape[1]
    tf = 1024
    return pl.pallas_call(
        _mlp_kernel,
        grid=(m // tm, f // tf),
        in_specs=[pl.BlockSpec((tm, d), lambda i, j: (i, 0)),
                  pl.BlockSpec((1, d), lambda i, j: (0, 0)),
                  _mod_spec(sc, rows_per_mod, tm, 2),
                  _mod_spec(sh, rows_per_mod, tm, 2),
                  pl.BlockSpec((d, tf), lambda i, j: (0, j)),
                  pl.BlockSpec((tf, d), lambda i, j: (j, 0)),
                  pl.BlockSpec((1, d), lambda i, j: (0, 0)),
                  _mod_spec(gate, rows_per_mod, tm, 2)],
        out_specs=pl.BlockSpec((tm, d), lambda i, j: (i, 0)),
        out_shape=jax.ShapeDtypeStruct((m, d), jnp.float32),
        scratch_shapes=[pltpu.VMEM((tm, d), jnp.bfloat16), pltpu.VMEM((tm, d), jnp.float32)],
        compiler_params=pltpu.CompilerParams(
            dimension_semantics=("parallel", "arbitrary"), vmem_limit_bytes=VMEM_LIMIT_BYTES),
        name="mlp",
    )(x, g2.reshape(1, d), sc, sh, w1, w2, g3.reshape(1, d), gate)


NSA_COL_BLOCK = NSA_KV_W
NSA_PROJ_W = 11 * NSA_COL_BLOCK
NSA_GATE_BLOCK = (NSA_Q_W + 6 * NSA_KV_W) // NSA_COL_BLOCK
ATT_TILE = 128
ROWS = NSA_GROUP * ATT_TILE
ATT_TILE_GROUP = 4


def _dot_nt(a, b):
    return lax.dot_general(a, b, (((1,), (1,)), ((), ())), preferred_element_type=jnp.float32)


def _dot_tn(a, b):
    return lax.dot_general(a, b, (((0,), (0,)), ((), ())), preferred_element_type=jnp.float32)


def _gelu_tanh(x):
    return 0.5 * x * (1.0 + jnp.tanh(math.sqrt(2.0 / math.pi) * (x + 0.044715 * (x * x * x))))


def _split3_bf16(x):
    hi = x.astype(jnp.bfloat16)
    r1 = x - hi.astype(jnp.float32)
    mid = r1.astype(jnp.bfloat16)
    lo = (r1 - mid.astype(jnp.float32)).astype(jnp.bfloat16)
    return hi, mid, lo


def _nsa_compress_kernel(x0_ref, x1_ref, x2_ref, x3_ref, w1_ref, w1f_ref, pos_ref, w2_ref, o_ref):
    n = x0_ref.shape[0] // CMP_STRIDE
    hd = NSA_HEAD_DIM
    x_refs = ((x0_ref, x1_ref), (x2_ref, x3_ref))
    for r in range(2):
        pos_b = jnp.dot(pos_ref[r].astype(jnp.bfloat16), w1f_ref[r], preferred_element_type=jnp.float32)
        acc = [jnp.zeros((n, 2 * CMP_HIDDEN), jnp.float32) for _ in range(NSA_KV_HEADS)]
        for l in range(CMP_STRIDE):
            w = w1_ref[r, l]
            for pair in range(2):
                xl = x_refs[r][pair][pl.ds(l, n, stride=CMP_STRIDE), :].astype(jnp.bfloat16)
                for half in range(2):
                    k = 2 * pair + half
                    acc[k] = acc[k] + jnp.dot(xl[:, half * hd:(half + 1) * hd], w,
                                              preferred_element_type=jnp.float32)
        for k in range(NSA_KV_HEADS):
            pa = acc[k][:, :CMP_HIDDEN]
            pb_next = pltpu.roll(acc[k][:, CMP_HIDDEN:], n - 1, 0)
            hid = _gelu_tanh(pa + pb_next + pos_b)
            out = jnp.dot(hid.astype(jnp.bfloat16), w2_ref[r], preferred_element_type=jnp.float32)
            o_ref[r, :, k * hd:(k + 1) * hd] = out.astype(o_ref.dtype)


def _compress_weights(cmp_pos, w1, w2):
    bf16 = jnp.bfloat16
    w1r = w1.reshape(2, CMP_LEN, NSA_HEAD_DIM, CMP_HIDDEN)
    w1cat = jnp.concatenate([w1r[:, :CMP_STRIDE], w1r[:, CMP_STRIDE:]], axis=-1).astype(bf16)
    return w1cat, w1.astype(bf16), cmp_pos.reshape(2, 1, CMP_LEN * NSA_HEAD_DIM), w2.astype(bf16)


def _nsa_compress_prompt(proj, cmp_pos, w1, w2):
    b, t, _ = proj.shape
    n = t // CMP_STRIDE
    w1cat, w1f, pos, w2b = _compress_weights(cmp_pos, w1, w2)
    return pl.pallas_call(
        _nsa_compress_kernel,
        grid=(b,),
        in_specs=[pl.BlockSpec((None, t, LANES), lambda i, c=c: (i, 0, NSA_Q_W // LANES + c)) for c in range(4)]
                 + [pl.BlockSpec(w1cat.shape, lambda i: (0, 0, 0, 0)),
                  pl.BlockSpec(w1f.shape, lambda i: (0, 0, 0)),
                  pl.BlockSpec(pos.shape, lambda i: (0, 0, 0)),
                  pl.BlockSpec(w2b.shape, lambda i: (0, 0, 0))],
        out_specs=pl.BlockSpec((None, 2, n, NSA_KV_W), lambda i: (i, 0, 0, 0)),
        out_shape=jax.ShapeDtypeStruct((b, 2, n, NSA_KV_W), jnp.bfloat16),
        compiler_params=pltpu.CompilerParams(
            dimension_semantics=("parallel",), vmem_limit_bytes=VMEM_LIMIT_BYTES),
        name="nsa_compress",
    )(proj, proj, proj, proj, w1cat, w1f, pos, w2b)


def _bias_lookup(rel_bias, dist):
    onehot = jax.nn.one_hot(_rel_bucket(dist), REL_BUCKETS, dtype=jnp.float32)
    return jnp.einsum('...c,ch->...h', onehot, rel_bias, precision=lax.Precision.HIGHEST)


DEN_ROWS = 8


def _with_ones(v):
    return jnp.concatenate([v, jnp.ones((v.shape[0], DEN_ROWS), v.dtype)], axis=1)


def _key_softmax_step(s, v, m, acc):
    m_new = jnp.maximum(m, jnp.max(s, axis=0, keepdims=True))
    e = jnp.exp(s - m_new).astype(jnp.bfloat16)
    acc = jnp.exp(m - m_new) * acc + _dot_tn(_with_ones(v), e)
    return m_new, acc


def _softmax_out(acc):
    hd = acc.shape[0] - DEN_ROWS
    return acc[:hd] / acc[hd:hd + 1]


def _nsa_attn_t_kernel(q_ref, g_ref, c_ref, ks_ref, vs_ref, kw_ref, vw_ref, bc_ref, bt_ref,
                       mimp_ref, eg_ref, o_ref, oc_s, os_s, ow_s, sel_s, *, tile0, n_far):
    i = tile0 + pl.program_id(1)
    hd = NSA_HEAD_DIM
    tq = ATT_TILE
    bf16 = jnp.bfloat16
    n_cmp_pad = c_ref.shape[1]
    n_sel = mimp_ref.shape[0]
    kj = lax.broadcasted_iota(jnp.int32, (tq, ROWS), 0)
    qi = lax.broadcasted_iota(jnp.int32, (tq, ROWS), 1) % tq
    causal = kj <= qi
    win_edge = kj >= qi
    cmp_end = CMP_STRIDE * lax.broadcasted_iota(jnp.int32, (n_cmp_pad, ROWS), 0) + (CMP_LEN - 1)
    mask_c = cmp_end <= i * tq + lax.broadcasted_iota(jnp.int32, (n_cmp_pad, ROWS), 1) % tq
    blk = lax.broadcasted_iota(jnp.int32, (n_sel, tq), 0)
    cur = (i * tq + lax.broadcasted_iota(jnp.int32, (n_sel, tq), 1)) // SEL_BLOCK
    forced = (blk == 0) | (blk == cur) | (blk == cur - 1)
    valid = blk <= cur

    heads = range(NSA_KV_HEADS)
    lanes = [slice(k * hd, (k + 1) * hd) for k in heads]
    per_chunk = tq // SEL_BLOCK
    qk, o_cmp = [], []
    for k in heads:
        q = jnp.concatenate(
            [q_ref[:, (k * NSA_GROUP + g) * hd:(k * NSA_GROUP + g + 1) * hd] for g in range(NSA_GROUP)], axis=0)
        qk.append((q * (hd ** -0.5)).astype(bf16))

        s = jnp.where(mask_c, _dot_nt(c_ref[0, :, lanes[k]], qk[k]) + bc_ref[k], NEG_INF)
        m = jnp.max(s, axis=0, keepdims=True)
        e = jnp.where(mask_c, jnp.exp(s - m), 0.0)
        p = e * (1.0 / jnp.maximum(jnp.sum(e, axis=0, keepdims=True), 1e-30))
        o_cmp.append(_dot_tn(c_ref[1, :, lanes[k]], p.astype(bf16)))
        p_sum = sum(p[:, g * tq:(g + 1) * tq] for g in range(NSA_GROUP))
        imp = _dot3(p_sum, mimp_ref[...], lambda a, b, **kw: jnp.dot(b, a, **kw))
        score = jnp.where(valid, jnp.where(forced, FORCE_SCORE, imp), NEG_INF)
        rank = jnp.zeros((n_sel, tq), jnp.float32)
        for j in range(n_sel):
            row = score[j:j + 1, :]
            beats = (row > score) | ((row == score) & (blk > j))
            rank = rank + jnp.where(beats, 1.0, 0.0)
        sel = jnp.where((rank < SEL_TOP_N) & (score > 0.5 * NEG_INF), 1.0, 0.0)
        sel = jnp.concatenate([sel] * NSA_GROUP, axis=1)
        for c in range(n_sel // per_chunk):
            sel_s[k, c, 0:per_chunk, :] = sel[c * per_chunk:(c + 1) * per_chunk, :]

    def rows_of(c, n=1):
        return pl.ds(c * tq, n * tq) if isinstance(c, int) else pl.ds(pl.multiple_of(c * tq, tq), n * tq)

    def chunk(k_ref, v_ref, k, rows, carry, bias, mk):
        s = _dot_nt(k_ref[rows, lanes[k]].astype(bf16), qk[k])
        if bias is not None:
            s = s + bias
        return _key_softmax_step(jnp.where(mk, s, NEG_INF), v_ref[rows, lanes[k]].astype(bf16), *carry)

    def sel_mask(k, c, ok):
        pair = sel_s[k, c, 0:per_chunk, :]
        picked = jnp.concatenate([jnp.broadcast_to(pair[j:j + 1, :], (SEL_BLOCK, ROWS)) for j in range(per_chunk)],
                                 axis=0)
        return (picked > 0.5) & jnp.broadcast_to(ok, (tq, ROWS))

    init = (jnp.full((1, ROWS), NEG_INF, jnp.float32), jnp.zeros((hd + DEN_ROWS, ROWS), jnp.float32))
    carry = [init] * NSA_KV_HEADS
    c_prev = jnp.maximum(i - 1, 0)
    for k in heads:
        carry[k] = chunk(ks_ref, vs_ref, k, rows_of(i), carry[k], bt_ref[k, 0], sel_mask(k, i, True) & causal)
    for k in heads:
        carry[k] = chunk(ks_ref, vs_ref, k, rows_of(c_prev), carry[k], bt_ref[k, 1], sel_mask(k, c_prev, i >= 1))
    for c in range(0, n_far, 2):
        for k in heads:
            mk = jnp.concatenate([sel_mask(k, c, c < i - 1), sel_mask(k, c + 1, c + 1 < i - 1)], axis=0)
            carry[k] = chunk(ks_ref, vs_ref, k, rows_of(c, 2), carry[k], None, mk)
    o_sel = [_softmax_out(carry[k][1]) for k in heads]

    carry = [init] * NSA_KV_HEADS
    n_back = WINDOW // tq

    def tile_ok(c):
        return jnp.broadcast_to(c >= 0, (tq, ROWS))

    for k in heads:
        carry[k] = chunk(kw_ref, vw_ref, k, rows_of(i + n_back), carry[k], bt_ref[k, 0], causal)
    mk = jnp.concatenate([tile_ok(i - 2), tile_ok(i - 1)], axis=0)
    for k in heads:
        bias = jnp.concatenate([jnp.zeros((tq, ROWS), jnp.float32), bt_ref[k, 1]], axis=0)
        carry[k] = chunk(kw_ref, vw_ref, k, rows_of(i + n_back - 2, 2), carry[k], bias, mk)
    mk = jnp.concatenate([tile_ok(i - 4) & win_edge, tile_ok(i - 3)], axis=0)
    for k in heads:
        carry[k] = chunk(kw_ref, vw_ref, k, rows_of(i + n_back - 4, 2), carry[k], None, mk)
    o_win = [_softmax_out(carry[k][1]) for k in heads]

    for k in heads:
        for g in range(NSA_GROUP):
            rows = slice((k * NSA_GROUP + g) * hd, (k * NSA_GROUP + g + 1) * hd)
            oc_s[rows, :] = o_cmp[k][:, g * tq:(g + 1) * tq]
            os_s[rows, :] = o_sel[k][:, g * tq:(g + 1) * tq]
            ow_s[rows, :] = o_win[k][:, g * tq:(g + 1) * tq]

    gate = jax.nn.sigmoid(g_ref[...])
    g_hi = gate.astype(bf16)
    g_lo = (gate - g_hi.astype(jnp.float32)).astype(bf16)
    out = jnp.zeros((NSA_Q_W, tq), jnp.float32)
    for br, o_s in enumerate((oc_s, os_s, ow_s)):
        out = out + (_dot_nt(eg_ref[br], g_hi) + _dot_nt(eg_ref[br], g_lo)) * o_s[...]
    for r in range(NSA_Q_W // tq):
        o_ref[:, r * tq:(r + 1) * tq] = out[r * tq:(r + 1) * tq, :].T.astype(o_ref.dtype)


def _keys_by_kv_head(tab):
    *lead, q, t, _ = tab.shape
    tab = tab.reshape(*lead, q, t, NSA_KV_HEADS, NSA_GROUP)
    nl = len(lead)
    tab = jnp.transpose(tab, (*range(nl), nl + 2, nl + 1, nl + 3, nl))
    return tab.reshape(*lead, NSA_KV_HEADS, t, NSA_GROUP * q)


def _nsa_prompt_tables_t(rel_bias, t):
    tq = ATT_TILE
    n_chunks = t // CMP_STRIDE
    n_sel = t // SEL_BLOCK
    far = rel_bias[REL_BUCKETS - 1]
    ar = jnp.arange(tq)
    d_tile = (jnp.arange(2) * tq)[:, None, None] + ar[None, :, None] - ar[None, None, :]
    bt = _keys_by_kv_head(_bias_lookup(rel_bias, d_tile) - far)
    bt = jnp.transpose(bt, (1, 0, 2, 3))
    q_pos = jnp.arange(t).reshape(t // tq, tq)
    cmp_end = jnp.arange(n_chunks) * CMP_STRIDE + CMP_LEN - 1
    bc = _keys_by_kv_head(_bias_lookup(rel_bias, q_pos[:, :, None] - cmp_end[None, None, :]) - far)
    n_idx = np.arange(n_chunks)
    j_idx = np.arange(n_sel)[:, None]
    per = SEL_BLOCK // CMP_STRIDE
    mimp = 0.5 * ((n_idx // per == j_idx).astype(np.float32) + ((n_idx + 1) // per == j_idx).astype(np.float32))
    mimp[:, n_chunks - 1] = 0.0
    col = np.arange(NSA_Q_W) // NSA_HEAD_DIM
    eg = np.zeros((3, NSA_Q_W, NSA_COL_BLOCK), np.float32)
    for br in range(3):
        eg[br, np.arange(NSA_Q_W), br * NSA_HEADS + col] = 1.0
    return bt, bc, jnp.asarray(mimp, jnp.bfloat16), jnp.asarray(eg, jnp.bfloat16)


def _nsa_attn_prompt_t(proj, cmp, tables):
    b, t, _ = proj.shape
    bt, bc, mimp, eg = tables
    tq = ATT_TILE
    cb = NSA_COL_BLOCK
    first_kv = NSA_Q_W // cb

    def kv_spec(slab):
        return pl.BlockSpec((None, t, cb), lambda bi, i: (bi, 0, first_kv + slab))

    def const_spec(a):
        nd = a.ndim
        return pl.BlockSpec(a.shape, lambda bi, i: (0,) * nd)

    assert WINDOW == 4 * tq
    win0 = NSA_Q_W + 4 * NSA_KV_W
    kw, vw = (jnp.pad(proj[:, :, c0:c0 + cb].astype(jnp.bfloat16), ((0, 0), (WINDOW, 0), (0, 0)))
              for c0 in (win0, win0 + cb))
    win_spec = pl.BlockSpec((None, t + WINDOW, cb), lambda bi, i: (bi, 0, 0))

    def tile_group(tile0):
        n_far = max(tile0 + ATT_TILE_GROUP - 2, 0)
        return pl.pallas_call(
            functools.partial(_nsa_attn_t_kernel, tile0=tile0, n_far=n_far),
            grid=(b, ATT_TILE_GROUP),
            in_specs=[pl.BlockSpec((None, tq, NSA_Q_W), lambda bi, i: (bi, tile0 + i, 0)),
                      pl.BlockSpec((None, tq, cb), lambda bi, i: (bi, tile0 + i, NSA_GATE_BLOCK)),
                      pl.BlockSpec((None,) + cmp.shape[1:], lambda bi, i: (bi, 0, 0, 0)),
                      kv_spec(2), kv_spec(3), win_spec, win_spec,
                      pl.BlockSpec((None,) + bc.shape[1:], lambda bi, i: (tile0 + i, 0, 0, 0)),
                      const_spec(bt), const_spec(mimp), const_spec(eg)],
            out_specs=pl.BlockSpec((None, tq, NSA_Q_W), lambda bi, i: (bi, i, 0)),
            out_shape=jax.ShapeDtypeStruct((b, ATT_TILE_GROUP * tq, NSA_Q_W), jnp.bfloat16),
            scratch_shapes=[pltpu.VMEM((NSA_Q_W, tq), jnp.float32)] * 3
                           + [pltpu.VMEM((NSA_KV_HEADS, t // tq, 8, ROWS), jnp.float32)],
            compiler_params=pltpu.CompilerParams(
                dimension_semantics=("parallel", "arbitrary"), vmem_limit_bytes=VMEM_LIMIT_BYTES),
            name="nsa_attn",
        )(proj, proj, cmp, proj, proj, kw, vw, bc, bt, mimp, eg)

    parts = [tile_group(tile0) for tile0 in range(0, t // tq, ATT_TILE_GROUP)]
    return jnp.concatenate(parts, axis=1).reshape(b * t, NSA_Q_W)


PAGE_GROUP = 8
SEL_PAD = 8


def _nsa_sample_kernel(pt_ref, *refs):
    n_cmp_in = 2 * PAGE_GROUP
    n_sel_in = 2 * PAGE_GROUP
    cmp_pages = refs[:n_cmp_in]
    sel_pages = refs[n_cmp_in:n_cmp_in + n_sel_in]
    (qbd_ref, qbdt_ref, new_ref, win_ref, w1_ref, w1f_ref, pos_ref, w2_ref, bcmp_ref, bsel_ref, bwin_ref,
     mimp_ref, hk_ref, bd_ref, rep_ref, bdt_ref, gsel_ref, eexp_ref, o_ref,
     a_s, stage_s, kc_s, vc_s, score_s, mask_s, kn_s, vn_s, ocmp_s, m_s, l_s, acc_s) = refs[n_cmp_in + n_sel_in:]
    del pt_ref
    ph = pl.program_id(1)
    g = pl.program_id(2)
    n_groups = pl.num_programs(2)
    bf16 = jnp.bfloat16
    hd = NSA_HEAD_DIM
    n_chunks = a_s.shape[1]
    n_sel = mimp_ref.shape[1]
    per_page = PAGE_SIZE // CMP_STRIDE
    qbd = qbd_ref[...]

    def heads_out(acc_t, l):
        o = (acc_t / l) * bd_ref[...]
        hi = o.astype(bf16)
        lo = (o - hi.astype(jnp.float32)).astype(bf16)
        return _dot_tn(hi, rep_ref[...]) + _dot_tn(lo, rep_ref[...])

    @pl.when(ph == 0)
    def _():
        for u in range(PAGE_GROUP):
            row0 = pl.multiple_of((g * PAGE_GROUP + u) * per_page, per_page)
            for r in range(2):
                for pair in range(2):
                    c = 2 * r + pair
                    stage_s[c] = cmp_pages[u * 2 + r][2 * pair:2 * pair + 2].reshape(LANES, PAGE_SIZE).T
                    for l in range(CMP_STRIDE):
                        a_s[c, pl.ds(row0, per_page), l * LANES:(l + 1) * LANES] = (
                            stage_s[c, pl.ds(l, per_page, stride=CMP_STRIDE), :])

    @pl.when((ph == 0) & (g == n_groups - 1))
    def _():
        for r in range(2):
            pos_b = jnp.dot(pos_ref[r].astype(bf16), w1f_ref[r], preferred_element_type=jnp.float32)
            acc = [jnp.dot(a_s[2 * r + pair].astype(bf16), w1_ref[r], preferred_element_type=jnp.float32)
                   for pair in range(2)]
            dst = kc_s if r == 0 else vc_s
            for k in range(NSA_KV_HEADS):
                cols = (k % 2) * 2 * CMP_HIDDEN
                pa = acc[k // 2][:, cols:cols + CMP_HIDDEN]
                pb_next = pltpu.roll(acc[k // 2][:, cols + CMP_HIDDEN:cols + 2 * CMP_HIDDEN], n_chunks - 1, 0)
                hid = _gelu_tanh(pa + pb_next + pos_b)
                out = jnp.dot(hid.astype(bf16), w2_ref[r], preferred_element_type=jnp.float32)
                dst[:, k * hd:(k + 1) * hd] = out.astype(bf16)
        rows = lax.broadcasted_iota(jnp.int32, (n_chunks, NSA_HEADS), 0)
        mask_c = rows <= n_chunks - 2
        s = jnp.dot(kc_s[...], qbd, preferred_element_type=jnp.float32) + bcmp_ref[...]
        s = jnp.where(mask_c, s, NEG_INF)
        m = jnp.max(s, axis=0, keepdims=True)
        e = jnp.where(mask_c, jnp.exp(s - m), 0.0)
        l = jnp.maximum(jnp.sum(e, axis=0, keepdims=True), 1e-30)
        p = e / l
        ocmp_s[...] = heads_out(_dot_tn(vc_s[...], p.astype(bf16)), jnp.ones_like(l))
        p_kv = sum(jnp.dot(part, hk_ref[...], preferred_element_type=jnp.float32) for part in _split3_bf16(p))
        imp = sum(jnp.dot(mimp_ref[...], part, preferred_element_type=jnp.float32) for part in _split3_bf16(p_kv))
        blk = lax.broadcasted_iota(jnp.int32, imp.shape, 0)
        cur = n_chunks * CMP_STRIDE // SEL_BLOCK
        forced = (blk == 0) | (blk == cur) | (blk == cur - 1)
        score = jnp.where(blk <= cur, jnp.where(forced, FORCE_SCORE, imp), NEG_INF)
        score_s[...] = score

        def rank_body(j, rank):
            row = score_s[pl.ds(j, 1), :]
            beats = (row > score) | ((row == score) & (blk > j))
            return rank + jnp.where(beats, 1.0, 0.0)

        rank = lax.fori_loop(0, cur + 1, rank_body, jnp.zeros(imp.shape, jnp.float32), unroll=4)
        sel = jnp.where((rank < SEL_TOP_N) & (score > 0.5 * NEG_INF), 1.0, 0.0)
        sel_h = jnp.dot(sel.astype(bf16), gsel_ref[...], preferred_element_type=jnp.float32)
        mask_s[...] = _dot_tn(sel_h.astype(bf16), eexp_ref[...])
        m_s[...] = jnp.full(m_s.shape, NEG_INF, jnp.float32)
        l_s[...] = jnp.zeros(l_s.shape, jnp.float32)
        acc_s[...] = jnp.zeros(acc_s.shape, jnp.float32)

    qbd_t = qbdt_ref[...]

    def lane_step(s, mask, state, pv):
        m, l, acc = state
        s = jnp.where(mask, s, NEG_INF)
        m_new = jnp.maximum(m, jnp.max(s, axis=1, keepdims=True))
        alpha = jnp.exp(m - m_new)
        e = jnp.where(mask, jnp.exp(s - m_new), 0.0)
        return m_new, alpha * l + jnp.sum(e, axis=1, keepdims=True), alpha * acc + pv(e.astype(bf16))

    def heads_out_t(acc, l):
        o = (acc / l) * bdt_ref[...]
        return sum(o[:, k * hd:(k + 1) * hd] for k in range(NSA_KV_HEADS))

    @pl.when(ph == 1)
    def _():
        span = PAGE_GROUP * PAGE_SIZE
        kt = jnp.concatenate([sel_pages[2 * u][...].reshape(NSA_KV_W, PAGE_SIZE).astype(bf16)
                              for u in range(PAGE_GROUP)], axis=1)
        vt = jnp.concatenate([sel_pages[2 * u + 1][...].reshape(NSA_KV_W, PAGE_SIZE).astype(bf16)
                              for u in range(PAGE_GROUP)], axis=1)
        mask = mask_s[:, pl.ds(pl.multiple_of(g * span, span), span)] > 0.5
        s = jnp.dot(qbd_t, kt, preferred_element_type=jnp.float32)
        near = jnp.where(g == n_groups - 1, 1.0, 0.0) * bsel_ref[:, 0:PAGE_SIZE]
        s = jnp.concatenate([s[:, :span - PAGE_SIZE], s[:, span - PAGE_SIZE:] + near], axis=1)
        m_s[...], l_s[...], acc_s[...] = lane_step(s, mask, (m_s[...], l_s[...], acc_s[...]),
                                                   lambda e: _dot_nt(e, vt))

    @pl.when((ph == 1) & (g == n_groups - 1))
    def _():
        kv0 = NSA_Q_W
        first = lax.broadcasted_iota(jnp.int32, (NSA_HEADS, SEL_PAD), 1) < 1

        def new_token_step(k_col, v_col, bias, state):
            kn_s[...] = jnp.zeros(kn_s.shape, bf16)
            vn_s[...] = jnp.zeros(vn_s.shape, bf16)
            kn_s[0:1, :] = new_ref[:, k_col:k_col + NSA_KV_W].astype(bf16)
            vn_s[0:1, :] = new_ref[:, v_col:v_col + NSA_KV_W].astype(bf16)
            s = _dot_nt(qbd_t, kn_s[...]) + bias
            vn = vn_s[...]
            return lane_step(s, first, state, lambda e: jnp.dot(e, vn, preferred_element_type=jnp.float32))

        state = new_token_step(kv0 + 2 * NSA_KV_W, kv0 + 3 * NSA_KV_W, bsel_ref[:, PAGE_SIZE:PAGE_SIZE + SEL_PAD],
                               (m_s[...], l_s[...], acc_s[...]))
        o_sel = heads_out_t(state[2], state[1])
        n_buf = win_ref.shape[2]
        kt = win_ref[0].astype(bf16)
        vt = win_ref[1].astype(bf16)
        init = (jnp.full((NSA_HEADS, 1), NEG_INF, jnp.float32), jnp.zeros((NSA_HEADS, 1), jnp.float32),
                jnp.zeros((NSA_HEADS, NSA_KV_W), jnp.float32))
        s = jnp.dot(qbd_t, kt, preferred_element_type=jnp.float32) + bwin_ref[:, 0:n_buf]
        state = lane_step(s, jnp.full(s.shape, True), init, lambda e: _dot_nt(e, vt))
        state = new_token_step(kv0 + 4 * NSA_KV_W, kv0 + 5 * NSA_KV_W, bwin_ref[:, n_buf:n_buf + SEL_PAD], state)
        o_win = heads_out_t(state[2], state[1])
        gate = jax.nn.sigmoid(new_ref[:, kv0 + 6 * NSA_KV_W:kv0 + 7 * NSA_KV_W])
        out = jnp.zeros(o_ref.shape, jnp.float32)
        for br, o_b in enumerate((ocmp_s[...], o_sel, o_win)):
            onehot = lax.broadcasted_iota(jnp.int32, (NSA_HEADS, NSA_COL_BLOCK), 1) == (
                lax.broadcasted_iota(jnp.int32, (NSA_HEADS, NSA_COL_BLOCK), 0) + br * NSA_HEADS)
            g_col = jnp.sum(jnp.where(onehot, gate, 0.0), axis=1, keepdims=True)
            out = out + g_col * o_b
        o_ref[...] = out


def _nsa_sample_tables(rel_bias, past_len, n_buf):
    far = rel_bias[REL_BUCKETS - 1]
    n_chunks = past_len // CMP_STRIDE
    n_sel = past_len // SEL_BLOCK + 1
    n_sel_pad = _round_up(n_sel, 8)
    cmp_end = jnp.arange(n_chunks) * CMP_STRIDE + CMP_LEN - 1
    bcmp = _bias_lookup(rel_bias, past_len - cmp_end) - far
    k_last = past_len - PAGE_SIZE + jnp.arange(PAGE_SIZE + SEL_PAD)
    bsel = (_bias_lookup(rel_bias, past_len - k_last) - far).T
    bwin = (_bias_lookup(rel_bias, n_buf - jnp.arange(n_buf + SEL_PAD)) - far).T
    per = SEL_BLOCK // CMP_STRIDE
    n_idx = np.arange(n_chunks)
    j_idx = np.arange(n_sel_pad)[:, None]
    mimp = 0.5 * ((n_idx // per == j_idx).astype(np.float32) + ((n_idx + 1) // per == j_idx).astype(np.float32))
    mimp[:, n_chunks - 1] = 0.0
    heads = np.arange(NSA_HEADS)
    hk = (heads[:, None] // NSA_GROUP == np.arange(NSA_KV_HEADS)[None, :]).astype(np.float32)
    rowk = np.arange(NSA_KV_W) // NSA_HEAD_DIM
    bd = (rowk[:, None] == heads[None, :] // NSA_GROUP).astype(np.float32)
    rep = (np.arange(NSA_KV_W)[:, None] % NSA_HEAD_DIM == np.arange(NSA_HEAD_DIM)[None, :]).astype(np.float32)
    eexp = (np.arange(past_len)[None, :] // SEL_BLOCK == np.arange(n_sel_pad)[:, None]).astype(np.float32)
    bf16 = jnp.bfloat16
    return (bcmp, bsel, bwin, jnp.asarray(mimp, bf16), jnp.asarray(hk, bf16), jnp.asarray(bd, jnp.float32),
            jnp.asarray(rep, bf16), jnp.asarray(bd.T, jnp.float32), jnp.asarray(hk.T, bf16), jnp.asarray(eexp, bf16))


def _nsa_attn_sample(proj, kt_pages, win_t, win_row0, page_table, cmp_pos, w1, w2, tables):
    b = proj.shape[0]
    n_pages = page_table.shape[1]
    past_len = n_pages * PAGE_SIZE
    n_buf = win_t.shape[3]
    n_chunks = past_len // CMP_STRIDE
    n_groups = n_pages // PAGE_GROUP
    bcmp, bsel, bwin, mimp, hk, bd, rep, bdt, gsel, eexp = tables
    n_sel_pad = mimp.shape[0]
    w1cat, w1f, pos, w2b = _compress_weights(cmp_pos, w1, w2)
    w1cat = jnp.einsum('rlde,hg->rlhdge', w1cat, jnp.eye(2, dtype=w1cat.dtype)).reshape(
        2, CMP_STRIDE * LANES, 4 * CMP_HIDDEN)
    q = proj[:, :NSA_Q_W].reshape(b, NSA_KV_HEADS, NSA_GROUP, NSA_HEAD_DIM) * (NSA_HEAD_DIM ** -0.5)
    eye = jnp.eye(NSA_KV_HEADS, dtype=q.dtype)
    qbd = jnp.einsum('bkgd,kc->bkdcg', q, eye).reshape(b, NSA_KV_W, NSA_HEADS).astype(jnp.bfloat16)
    qbd_t = jnp.swapaxes(qbd, 1, 2)
    proj3 = proj.reshape(b, 1, proj.shape[1])

    def cmp_spec(u, r):
        def imap(i, ph, g, pt):
            gg = jnp.where(ph == 0, g, n_groups - 1)
            return (pt[i, gg * PAGE_GROUP + u], r, 0, 0)
        return pl.BlockSpec((None, NSA_KV_HEADS, NSA_HEAD_DIM, PAGE_SIZE), imap)

    def sel_spec(u, r):
        def imap(i, ph, g, pt):
            gg = jnp.where(ph == 1, g, 0)
            return (pt[i, gg * PAGE_GROUP + u], r, 0, 0)
        return pl.BlockSpec((None, NSA_KV_HEADS, NSA_HEAD_DIM, PAGE_SIZE), imap)

    def const_spec(a):
        nd = a.ndim
        return pl.BlockSpec(a.shape, lambda i, ph, g, pt: (0,) * nd)

    consts = (w1cat, w1f, pos, w2b, bcmp, bsel, bwin, mimp, hk, bd, rep, bdt, gsel, eexp)
    in_specs = ([cmp_spec(u, r) for u in range(PAGE_GROUP) for r in (0, 1)]
                + [sel_spec(u, r) for u in range(PAGE_GROUP) for r in (2, 3)]
                + [pl.BlockSpec((None, NSA_KV_W, NSA_HEADS), lambda i, ph, g, pt: (i, 0, 0)),
                   pl.BlockSpec((None, NSA_HEADS, NSA_KV_W), lambda i, ph, g, pt: (i, 0, 0)),
                   pl.BlockSpec((None, 1, proj.shape[1]), lambda i, ph, g, pt: (i, 0, 0)),
                   pl.BlockSpec((None, 2, NSA_KV_W, n_buf), lambda i, ph, g, pt: (win_row0 + i, 0, 0, 0))]
                + [const_spec(a) for a in consts])
    f32, bf16 = jnp.float32, jnp.bfloat16
    out = pl.pallas_call(
        _nsa_sample_kernel,
        grid_spec=pltpu.PrefetchScalarGridSpec(
            num_scalar_prefetch=1,
            grid=(b, 2, n_groups),
            in_specs=in_specs,
            out_specs=pl.BlockSpec((None, NSA_HEADS, NSA_HEAD_DIM), lambda i, ph, g, pt: (i, 0, 0)),
            scratch_shapes=[pltpu.VMEM((4, n_chunks, CMP_STRIDE * LANES), f32), pltpu.VMEM((4, PAGE_SIZE, LANES), f32),
                            pltpu.VMEM((n_chunks, NSA_KV_W), bf16), pltpu.VMEM((n_chunks, NSA_KV_W), bf16),
                            pltpu.VMEM((n_sel_pad, NSA_KV_HEADS), f32), pltpu.VMEM((NSA_HEADS, past_len), f32),
                            pltpu.VMEM((SEL_PAD, NSA_KV_W), bf16), pltpu.VMEM((SEL_PAD, NSA_KV_W), bf16),
                            pltpu.VMEM((NSA_HEADS, NSA_HEAD_DIM), f32),
                            pltpu.VMEM((NSA_HEADS, 1), f32), pltpu.VMEM((NSA_HEADS, 1), f32),
                            pltpu.VMEM((NSA_HEADS, NSA_KV_W), f32)]),
        out_shape=jax.ShapeDtypeStruct((b, NSA_HEADS, NSA_HEAD_DIM), f32),
        compiler_params=pltpu.CompilerParams(
            dimension_semantics=("parallel", "arbitrary", "arbitrary"), vmem_limit_bytes=VMEM_LIMIT_BYTES),
        name="nsa_sample",
    )(page_table, *([kt_pages] * (4 * PAGE_GROUP)), qbd, qbd_t, proj3, win_t,
      *consts)
    return out.reshape(b, NSA_Q_W)


SSD_PROJ_W = _round_up(SSD_IN_W, 7 * LANES)
SSD_TILE = 128
SSD_GN = SSD_GROUPS * SSD_STATE
SSD_GW = SSD_HPG * SSD_HEAD_DIM
CONV_PAD = 8


def _silu(x):
    return x * jax.nn.sigmoid(x)


def _softplus(x):
    return jnp.maximum(x, 0.0) + jnp.log(1.0 + jnp.exp(-jnp.abs(x)))


def _dot3(x, table, dot=jnp.dot):
    return sum(dot(part, table, preferred_element_type=jnp.float32) for part in _split3_bf16(x))


def _dot3_tn(x, table):
    return sum(_dot_tn(part, table) for part in _split3_bf16(x))


def _ssd_prompt_kernel(z_ref, x_ref, bc_ref, dt_ref, cw_ref, cb_ref, dtb_ref, alog_ref, d_ref, nw_ref,
                       eh_ref, tril_ref, triu_ref, eye_ref, y_ref, st_ref, conv_ref, ux_s, ubc_s, st_s):
    c = pl.program_id(1)
    t = SSD_TILE
    di = SSD_D_INNER
    bf16 = jnp.bfloat16

    @pl.when(c == 0)
    def _():
        ux_s[0:CONV_PAD, :] = jnp.zeros((CONV_PAD, di), jnp.float32)
        ubc_s[0:CONV_PAD, :] = jnp.zeros((CONV_PAD, 2 * SSD_GN), jnp.float32)
        st_s[...] = jnp.zeros(st_s.shape, jnp.float32)

    ux_s[CONV_PAD:CONV_PAD + t, :] = x_ref[...]
    ubc_s[CONV_PAD:CONV_PAD + t, :] = bc_ref[...]

    def conv(buf, col0, width):
        y = cb_ref[:, col0:col0 + width]
        for i in range(SSD_CONV_W):
            y = y + buf[pl.ds(CONV_PAD - (SSD_CONV_W - 1 - i), t), :] * cw_ref[i:i + 1, col0:col0 + width]
        return _silu(y)

    xs = conv(ux_s, 0, di)
    bcs = conv(ubc_s, di, 2 * SSD_GN)
    dt = _softplus(dt_ref[:, 0:SSD_HEADS] + dtb_ref[...])
    dta = dt * (-jnp.exp(alog_ref[...]))
    cum = _dot3(dta, tril_ref[...], lambda a, b, **kw: jnp.dot(b, a, **kw))
    cum_t = _dot3_tn(dta, triu_ref[...])
    dt_t = _dot3_tn(dt, eye_ref[...])
    last = cum[t - 1:t, :]
    dec_in = _dot3(jnp.exp(cum), eh_ref[...])
    wgt = _dot3(jnp.exp(last - cum) * dt, eh_ref[...])
    st_scale = _dot3(jnp.broadcast_to(jnp.exp(last), (8, SSD_HEADS)), eh_ref[...])[0:1, :]
    causal = lax.broadcasted_iota(jnp.int32, (t, t), 1) <= lax.broadcasted_iota(jnp.int32, (t, t), 0)

    for g in range(SSD_GROUPS):
        gl = slice(g * SSD_GW, (g + 1) * SSD_GW)
        bg = bcs[:, g * SSD_STATE:(g + 1) * SSD_STATE].astype(bf16)
        cg = bcs[:, SSD_GN + g * SSD_STATE:SSD_GN + (g + 1) * SSD_STATE].astype(bf16)
        cb = _dot_nt(cg, bg)
        xg = xs[:, gl]
        y_heads = []
        for j in range(SSD_HPG):
            h = g * SSD_HPG + j
            decay = jnp.exp(jnp.minimum(cum[:, h:h + 1] - cum_t[h:h + 1, :], 0.0))
            w = jnp.where(causal, cb * decay * dt_t[h:h + 1, :], 0.0).astype(bf16)
            y_heads.append(jnp.dot(w, xg[:, j * SSD_HEAD_DIM:(j + 1) * SSD_HEAD_DIM].astype(bf16),
                                   preferred_element_type=jnp.float32))
        st = st_s[g]
        y = jnp.concatenate(y_heads, axis=1)
        y = y + jnp.dot(cg, st.astype(bf16), preferred_element_type=jnp.float32) * dec_in[:, gl]
        st_s[g] = st_scale[:, gl] * st + _dot_tn(bg, (xg * wgt[:, gl]).astype(bf16))
        y = (y + d_ref[:, gl] * xg) * _silu(z_ref[:, gl])
        y = y * lax.rsqrt(jnp.mean(y * y, axis=-1, keepdims=True) + NORM_EPS) * nw_ref[:, gl]
        y_ref[:, gl] = y.astype(y_ref.dtype)

    ux_s[0:CONV_PAD, :] = ux_s[t:t + CONV_PAD, :]
    ubc_s[0:CONV_PAD, :] = ubc_s[t:t + CONV_PAD, :]

    @pl.when(c == pl.num_programs(1) - 1)
    def _():
        st_ref[...] = st_s[...]
        keep = SSD_CONV_W - 1
        conv_ref[:, 0:di] = ux_s[CONV_PAD - keep:CONV_PAD, :]
        conv_ref[:, di:] = ubc_s[CONV_PAD - keep:CONV_PAD, :]


def _ssd_tables():
    eh = (np.arange(SSD_HEADS)[:, None] == np.arange(SSD_D_INNER)[None, :] // SSD_HEAD_DIM).astype(np.float32)
    tril = np.tril(np.ones((SSD_TILE, SSD_TILE), np.float32))
    bf16 = jnp.bfloat16
    return (jnp.asarray(eh, bf16), jnp.asarray(tril, bf16), jnp.asarray(tril.T, bf16),
            jnp.asarray(np.eye(SSD_TILE, dtype=np.float32), bf16))


def _state_from_transposed(st_t):
    b = st_t.shape[0]
    st = st_t.reshape(b, SSD_GROUPS, SSD_STATE, SSD_HPG, SSD_HEAD_DIM)
    return jnp.transpose(st, (0, 1, 3, 4, 2)).reshape(b, SSD_HEADS, SSD_HEAD_DIM, SSD_STATE)


def _ssd_prompt(proj, conv_w, conv_b, dt_bias, a_log, d_skip, norm_w):
    b, t, _ = proj.shape
    di = SSD_D_INNER
    tt = SSD_TILE
    d_exp = jnp.repeat(d_skip, SSD_HEAD_DIM).reshape(1, di)
    consts = (conv_w, conv_b.reshape(1, -1), dt_bias.reshape(1, -1), a_log.reshape(1, -1), d_exp,
              norm_w.reshape(1, di)) + _ssd_tables()

    def const_spec(a):
        nd = a.ndim
        return pl.BlockSpec(a.shape, lambda i, c: (0,) * nd)

    y, st_t, conv_new = pl.pallas_call(
        _ssd_prompt_kernel,
        grid=(b, t // tt),
        in_specs=[pl.BlockSpec((None, tt, di), lambda i, c: (i, c, 0)),
                  pl.BlockSpec((None, tt, di), lambda i, c: (i, c, 1)),
                  pl.BlockSpec((None, tt, 2 * SSD_GN), lambda i, c: (i, c, 2)),
                  pl.BlockSpec((None, tt, LANES), lambda i, c: (i, c, (di + SSD_CONV_DIM) // LANES))]
                 + [const_spec(a) for a in consts],
        out_specs=[pl.BlockSpec((None, tt, di), lambda i, c: (i, c, 0)),
                   pl.BlockSpec((None, SSD_GROUPS, SSD_STATE, SSD_GW), lambda i, c: (i, 0, 0, 0)),
                   pl.BlockSpec((None, SSD_CONV_W - 1, SSD_CONV_DIM), lambda i, c: (i, 0, 0))],
        out_shape=[jax.ShapeDtypeStruct((b, t, di), jnp.bfloat16),
                   jax.ShapeDtypeStruct((b, SSD_GROUPS, SSD_STATE, SSD_GW), jnp.float32),
                   jax.ShapeDtypeStruct((b, SSD_CONV_W - 1, SSD_CONV_DIM), jnp.float32)],
        scratch_shapes=[pltpu.VMEM((CONV_PAD + tt, di), jnp.float32),
                        pltpu.VMEM((CONV_PAD + tt, 2 * SSD_GN), jnp.float32),
                        pltpu.VMEM((SSD_GROUPS, SSD_STATE, SSD_GW), jnp.float32)],
        compiler_params=pltpu.CompilerParams(
            dimension_semantics=("parallel", "arbitrary"), vmem_limit_bytes=VMEM_LIMIT_BYTES),
        name="ssd_prompt",
    )(proj, proj, proj, proj, *consts)
    return y.reshape(b * t, di), _state_from_transposed(st_t), conv_new


ROW_PAD = 8


def _row8(x):
    return jnp.concatenate([x, jnp.zeros((ROW_PAD - 1, x.shape[1]), x.dtype)], axis=0)


def _ssd_step_kernel(p_ref, conv_ref, st_ref, cw_ref, cb_ref, dtb_ref, alog_ref, d_ref, nw_ref,
                     y_ref, st_out, conv_out, y_s):
    di = SSD_D_INNER
    bf16 = jnp.bfloat16
    u = p_ref[:, di:di + SSD_CONV_DIM]
    keep = SSD_CONV_W - 1
    y = cb_ref[...] + u * cw_ref[keep:keep + 1, :]
    for i in range(keep):
        y = y + conv_ref[i:i + 1, :] * cw_ref[i:i + 1, :]
    conv_out[0:keep - 1, :] = conv_ref[1:keep, :]
    conv_out[keep - 1:keep, :] = u
    xbc = _silu(y)
    xs = xbc[:, :di]
    dt = _softplus(p_ref[:, di + SSD_CONV_DIM:di + SSD_CONV_DIM + SSD_HEADS] + dtb_ref[...])
    decay = jnp.exp(dt * (-jnp.exp(alog_ref[...])))
    for g in range(SSD_GROUPS):
        bg = _row8(xbc[:, di + g * SSD_STATE:di + (g + 1) * SSD_STATE]).astype(bf16)
        cg = _row8(xbc[:, di + SSD_GN + g * SSD_STATE:di + SSD_GN + (g + 1) * SSD_STATE]).astype(bf16)
        for j in range(SSD_HPG):
            h = g * SSD_HPG + j
            cols = slice(h * SSD_HEAD_DIM, (h + 1) * SSD_HEAD_DIM)
            xh = _row8(xs[:, cols] * dt[:, h:h + 1]).astype(bf16)
            st = decay[:, h:h + 1] * st_ref[h] + _dot_tn(xh, bg)
            st_out[h] = st
            y_s[:, cols] = _dot_nt(cg, st.astype(bf16))
    yv = y_s[0:1, :]
    yv = (yv + d_ref[...] * xs) * _silu(p_ref[:, 0:di])
    for g in range(SSD_GROUPS):
        gl = slice(g * SSD_GW, (g + 1) * SSD_GW)
        yg = yv[:, gl]
        y_ref[:, gl] = yg * lax.rsqrt(jnp.mean(yg * yg, axis=-1, keepdims=True) + NORM_EPS) * nw_ref[:, gl]


def _ssd_step(proj, ssm0, conv0, conv_w, conv_b, dt_bias, a_log, d_skip, norm_w):
    b = proj.shape[0]
    di = SSD_D_INNER
    consts = (conv_w, conv_b.reshape(1, -1), dt_bias.reshape(1, -1), a_log.reshape(1, -1),
              jnp.repeat(d_skip, SSD_HEAD_DIM).reshape(1, di), norm_w.reshape(1, di))

    def const_spec(a):
        nd = a.ndim
        return pl.BlockSpec(a.shape, lambda i: (0,) * nd)

    y, st, conv_new = pl.pallas_call(
        _ssd_step_kernel,
        grid=(b,),
        in_specs=[pl.BlockSpec((None, 1, proj.shape[1]), lambda i: (i, 0, 0)),
                  pl.BlockSpec((None,) + conv0.shape[1:], lambda i: (i, 0, 0)),
                  pl.BlockSpec((None,) + ssm0.shape[1:], lambda i: (i, 0, 0, 0))]
                 + [const_spec(a) for a in consts],
        out_specs=[pl.BlockSpec((None, 1, di), lambda i: (i, 0, 0)),
                   pl.BlockSpec((None,) + ssm0.shape[1:], lambda i: (i, 0, 0, 0)),
                   pl.BlockSpec((None,) + conv0.shape[1:], lambda i: (i, 0, 0))],
        out_shape=[jax.ShapeDtypeStruct((b, 1, di), jnp.float32),
                   jax.ShapeDtypeStruct(ssm0.shape, jnp.float32),
                   jax.ShapeDtypeStruct(conv0.shape, jnp.float32)],
        scratch_shapes=[pltpu.VMEM((ROW_PAD, di), jnp.float32)],
        compiler_params=pltpu.CompilerParams(
            dimension_semantics=("parallel",), vmem_limit_bytes=VMEM_LIMIT_BYTES),
        name="ssd_step",
    )(proj.reshape(b, 1, -1), conv0, ssm0, *consts)
    return y.reshape(b, di), st, conv_new


def _hgrn_gates(p_ref, lb_ref):
    wk = HG_HEADS * HG_DK
    q = _silu(p_ref[:, 0:wk])
    f = lb_ref[...] + (1.0 - lb_ref[...]) * jax.nn.sigmoid(p_ref[:, wk:2 * wk])
    return q, f


HG_TILE = 128
HG_SUB = 16


def _hgrn_prompt_kernel(p_ref, lb_ref, gn_ref, tril_ref, subend_ref, ones_ref, y_ref, st_ref, st_s):
    c = pl.program_id(1)
    t = HG_TILE
    wk = HG_HEADS * HG_DK
    wv = HG_HEADS * HG_DV
    bf16 = jnp.bfloat16
    n_sub = t // HG_SUB

    @pl.when(c == 0)
    def _():
        st_s[...] = jnp.zeros(st_s.shape, jnp.float32)

    row = lax.broadcasted_iota(jnp.int32, (t, HG_DK), 0)
    sub_pos = row % HG_SUB
    row_sub = lax.broadcasted_iota(jnp.int32, (t, t), 0) // HG_SUB
    col_sub = lax.broadcasted_iota(jnp.int32, (t, t), 1) // HG_SUB
    left = lambda a, b, **kw: jnp.dot(b, a, **kw)

    def head(h, carry):
        kc = pl.ds(pl.multiple_of(h * HG_DK, HG_DK), HG_DK)
        q = _silu(p_ref[:, kc])
        lb = lb_ref[:, kc]
        f = lb + (1.0 - lb) * jax.nn.sigmoid(p_ref[:, pl.ds(pl.multiple_of(wk + h * HG_DK, HG_DK), HG_DK)])
        k = 1.0 - f
        v = p_ref[:, pl.ds(pl.multiple_of(2 * wk + h * HG_DV, HG_DV), HG_DV)]
        gate = p_ref[:, pl.ds(pl.multiple_of(2 * wk + wv + h * HG_DV, HG_DV), HG_DV)]
        cum = _dot3(jnp.log(f), tril_ref[...], left)
        sub_end = _dot3(cum, subend_ref[...], left)
        k_hat = k * jnp.exp(sub_end - cum)
        a_off = jnp.zeros((t, t), jnp.float32)
        for j in range(n_sub - 1):
            end_j = cum[(j + 1) * HG_SUB - 1:(j + 1) * HG_SUB, :]
            q_j = (q * jnp.exp(jnp.minimum(cum - end_j, 0.0))).astype(bf16)
            k_j = jnp.where(row // HG_SUB == j, k_hat, 0.0).astype(bf16)
            a_off = a_off + _dot_nt(q_j, k_j)
        a_off = jnp.where(col_sub < row_sub, a_off, 0.0)
        v16 = v.astype(bf16)
        o = jnp.dot(a_off.astype(bf16), v16, preferred_element_type=jnp.float32)
        decay = None
        for d in range(HG_SUB):
            k_d, v_d = (k, v) if d == 0 else (pltpu.roll(k, d, 0), pltpu.roll(v, d, 0))
            if d == 1:
                decay = f
            elif d > 1:
                decay = decay * pltpu.roll(f, d - 1, 0)
            e = (q * k_d if d == 0 else q * k_d * decay).astype(bf16)
            a_d = jnp.dot(e, ones_ref[...], preferred_element_type=jnp.float32)
            o = o + jnp.where(sub_pos >= d, a_d, 0.0) * v_d
        st = st_s[h]
        o = o + _dot_nt((q * jnp.exp(cum)).astype(bf16), st.astype(bf16))
        last = cum[t - 1:t, :]
        st_s[h] = st * jnp.exp(last) + _dot_tn(v16, (k * jnp.exp(last - cum)).astype(bf16))
        o = o * lax.rsqrt(jnp.mean(o * o, axis=-1, keepdims=True) + NORM_EPS) * gn_ref[...]
        y_ref[:, pl.ds(pl.multiple_of(h * HG_DV, HG_DV), HG_DV)] = (o * _silu(gate)).astype(y_ref.dtype)
        return carry

    lax.fori_loop(0, HG_HEADS, head, 0)

    @pl.when(c == pl.num_programs(1) - 1)
    def _():
        st_ref[...] = st_s[...]


def _hgrn_prompt(proj, lb, g_norm):
    b, t, w = proj.shape
    tt = HG_TILE
    wv = HG_HEADS * HG_DV
    idx = np.arange(tt)
    tril = np.tril(np.ones((tt, tt), np.float32))
    subend = (idx[None, :] == (idx[:, None] // HG_SUB) * HG_SUB + HG_SUB - 1).astype(np.float32)
    bf16 = jnp.bfloat16
    consts = (lb.reshape(1, -1), g_norm.reshape(1, -1), jnp.asarray(tril, bf16), jnp.asarray(subend, bf16),
              jnp.ones((HG_DK, HG_DK), bf16))

    def const_spec(a):
        nd = a.ndim
        return pl.BlockSpec(a.shape, lambda i, c: (0,) * nd)

    y, st_t = pl.pallas_call(
        _hgrn_prompt_kernel,
        grid=(b, t // tt),
        in_specs=[pl.BlockSpec((None, tt, w), lambda i, c: (i, c, 0))] + [const_spec(a) for a in consts],
        out_specs=[pl.BlockSpec((None, tt, wv), lambda i, c: (i, c, 0)),
                   pl.BlockSpec((None, HG_HEADS, HG_DV, HG_DK), lambda i, c: (i, 0, 0, 0))],
        out_shape=[jax.ShapeDtypeStruct((b, t, wv), bf16),
                   jax.ShapeDtypeStruct((b, HG_HEADS, HG_DV, HG_DK), jnp.float32)],
        scratch_shapes=[pltpu.VMEM((HG_HEADS, HG_DV, HG_DK), jnp.float32)],
        compiler_params=pltpu.CompilerParams(
            dimension_semantics=("parallel", "arbitrary"), vmem_limit_bytes=VMEM_LIMIT_BYTES),
        name="hgrn_prompt",
    )(proj, *consts)
    return y.reshape(b * t, wv), jnp.swapaxes(st_t, 2, 3)


def _hgrn_step_kernel(p_ref, st_ref, lb_ref, gn_ref, y_ref, st_out):
    wk = HG_HEADS * HG_DK
    wv = HG_HEADS * HG_DV
    bf16 = jnp.bfloat16
    q, f = _hgrn_gates(p_ref, lb_ref)
    eye = lax.broadcasted_iota(jnp.int32, (HG_DK, HG_DK), 0) == lax.broadcasted_iota(jnp.int32, (HG_DK, HG_DK), 1)
    for h in range(HG_HEADS):
        kc = slice(h * HG_DK, (h + 1) * HG_DK)
        vc = slice(2 * wk + h * HG_DV, 2 * wk + (h + 1) * HG_DV)
        gc = slice(2 * wk + wv + h * HG_DV, 2 * wk + wv + (h + 1) * HG_DV)
        fh = f[:, kc]
        f_col = jnp.sum(jnp.where(eye, fh, 0.0), axis=1, keepdims=True)
        kv = _dot_tn(_row8(1.0 - fh).astype(bf16), _row8(p_ref[:, vc]).astype(bf16))
        st = f_col * st_ref[h] + kv
        st_out[h] = st
        o = jnp.dot(_row8(q[:, kc]).astype(bf16), st.astype(bf16), preferred_element_type=jnp.float32)[0:1, :]
        o = o * lax.rsqrt(jnp.mean(o * o, axis=-1, keepdims=True) + NORM_EPS) * gn_ref[...]
        y_ref[:, h * HG_DV:(h + 1) * HG_DV] = o * _silu(p_ref[:, gc])


def _hgrn_step(proj, s0, lb, g_norm):
    b = proj.shape[0]
    wv = HG_HEADS * HG_DV
    y, st = pl.pallas_call(
        _hgrn_step_kernel,
        grid=(b,),
        in_specs=[pl.BlockSpec((None, 1, proj.shape[1]), lambda i: (i, 0, 0)),
                  pl.BlockSpec((None,) + s0.shape[1:], lambda i: (i, 0, 0, 0)),
                  pl.BlockSpec((1, HG_HEADS * HG_DK), lambda i: (0, 0)),
                  pl.BlockSpec((1, HG_DV), lambda i: (0, 0))],
        out_specs=[pl.BlockSpec((None, 1, wv), lambda i: (i, 0, 0)),
                   pl.BlockSpec((None,) + s0.shape[1:], lambda i: (i, 0, 0, 0))],
        out_shape=[jax.ShapeDtypeStruct((b, 1, wv), jnp.float32), jax.ShapeDtypeStruct(s0.shape, jnp.float32)],
        compiler_params=pltpu.CompilerParams(
            dimension_semantics=("parallel",), vmem_limit_bytes=VMEM_LIMIT_BYTES),
        name="hgrn_step",
    )(proj.reshape(b, 1, -1), s0, lb.reshape(1, -1), g_norm.reshape(1, -1))
    return y.reshape(b, wv), st


def _rel_bucket(dist):
    n = jnp.maximum(dist, 0)
    n_exact = REL_BUCKETS // 2
    nf = jnp.maximum(n, 1).astype(jnp.float32)
    large = n_exact + (jnp.log(nf / n_exact) / math.log(REL_MAX_DIST / n_exact)
                       * (REL_BUCKETS - n_exact)).astype(jnp.int32)
    return jnp.where(n < n_exact, n, jnp.minimum(large, REL_BUCKETS - 1))


def _nsa_prompt_core(proj, cmp_pos, cmp_w1, cmp_w2, tables):
    b, t, _ = proj.shape
    cmp = _nsa_compress_prompt(proj, cmp_pos, cmp_w1, cmp_w2)
    merged = _nsa_attn_prompt_t(proj, cmp, tables)
    o1 = NSA_Q_W
    o2 = o1 + 4 * NSA_KV_W
    o3 = o2 + 2 * NSA_KV_W
    kv_cs = proj[..., o1:o2].reshape(b, t, 4, NSA_KV_HEADS, NSA_HEAD_DIM)
    kv_win = proj[:, t - min(WINDOW, t):, o2:o3].reshape(b, min(WINDOW, t), 2, NSA_KV_HEADS, NSA_HEAD_DIM)
    return merged, kv_cs, kv_win


def _nsa_sample_core(proj, caches, layer, win_buf, page_table, cmp_pos, cmp_w1, cmp_w2, tables):
    b, t, _ = proj.shape
    assert t == 1 and win_buf.shape[1] == WINDOW and page_table.shape[1] % PAGE_GROUP == 0
    kt_pages, win_t, n_phys = caches
    merged = _nsa_attn_sample(proj.reshape(b, -1), kt_pages, win_t, layer * b,
                              page_table + layer * n_phys, cmp_pos, cmp_w1, cmp_w2, tables)
    o1 = NSA_Q_W
    o2 = o1 + 4 * NSA_KV_W
    o3 = o2 + 2 * NSA_KV_W
    kv_cs = proj[..., o1:o2].reshape(b, t, 4, NSA_KV_HEADS, NSA_HEAD_DIM)
    kv_win = proj[..., o2:o3].reshape(b, t, 2, NSA_KV_HEADS, NSA_HEAD_DIM)
    new_win = jnp.concatenate([win_buf[:, t:], kv_win], axis=1)
    return merged, kv_cs, new_win


def _pad_cols(w, n):
    return jnp.pad(w, ((0, 0), (0, n - w.shape[1])))


def kernel(x_prompt, x_sample, cache_nsa_kv, cache_nsa_win, state_hgrn, state_ssd, state_ssd_conv, page_table, c_prompt, c_sample, rel_bias, hgrn_lower_bounds, w_ada, b_ada, norm_gains, w_mlp_in, w_mlp_out, nsa_w_in, nsa_cmp_pos, nsa_cmp_w1, nsa_cmp_w2, nsa_w_out, hg_w_in, hg_norm, hg_w_out, ssd_w_in, ssd_conv_w, ssd_conv_b, ssd_dt_bias, ssd_a_log, ssd_d, ssd_norm, ssd_w_out):
    bf16 = jnp.bfloat16
    bp, tp, d = x_prompt.shape
    bs, ts, _ = x_sample.shape
    mp, ms = bp * tp, bs * ts
    lb_p = jax.nn.softmax(hgrn_lower_bounds, axis=0)
    lower_bounds = jnp.cumsum(lb_p, axis=0) - lb_p[0]

    mod = _ada_all(jnp.concatenate([c_prompt, c_sample], axis=0), w_ada, b_ada)
    mod = mod.reshape(DEPTH, bp + bs, ADA_CHUNKS, d)
    mod_p = mod[:, :bp].transpose(0, 2, 1, 3)[:, :, :, None, :]
    mod_s = mod[:, bp:].transpose(0, 2, 1, 3)[:, :, None, :, :]

    xp = x_prompt.reshape(mp, d)
    xs = x_sample.reshape(ms, d)
    tm_p, tm_s = PROMPT_ROW_TILE, ms
    nsa_tables = _nsa_prompt_tables_t(rel_bias, tp)
    nsa_tables_s = _nsa_sample_tables(rel_bias, page_table.shape[1] * PAGE_SIZE, cache_nsa_win.shape[2])
    n_phys = cache_nsa_kv.shape[1]
    n_all = cache_nsa_kv.shape[0] * n_phys
    kt_pages = jnp.transpose(cache_nsa_kv, (0, 1, 3, 4, 5, 2)).reshape(n_all, 4 * NSA_KV_HEADS, NSA_HEAD_DIM, PAGE_SIZE)
    win_t = jnp.transpose(cache_nsa_win, (0, 1, 3, 4, 5, 2)).reshape(-1, 2, NSA_KV_W, cache_nsa_win.shape[2])
    nsa_caches = (kt_pages, win_t, n_phys)

    kv_p, kv_s, win_p, win_s = [], [], [], []
    hg_p, hg_s, ssd_p, ssd_s, conv_p, conv_s = [], [], [], [], [], []
    for i in range(DEPTH):
        j = i // N_MIXERS
        kind = i % N_MIXERS
        g = norm_gains[i]
        shp_m, scp_m, gtp_m, shp_f, scp_f, gtp_f = [mod_p[i, c] for c in range(ADA_CHUNKS)]
        shs_m, scs_m, gts_m, shs_f, scs_f, gts_f = [mod_s[i, c] for c in range(ADA_CHUNKS)]
        if kind == 0:
            n_pad = NSA_PROJ_W
            w_in = _pad_cols(nsa_w_in[j], n_pad).astype(bf16)
            w_out = nsa_w_out[j].astype(bf16)
            pp = _norm_mod_matmul(xp, g[0], scp_m, shp_m, w_in, tp, tm_p).reshape(bp, tp, n_pad)
            ps = _norm_mod_matmul(xs, g[0], scs_m, shs_m, w_in, ts, tm_s).reshape(bs, ts, n_pad)
            ap, new_kv_p, new_win_p = _nsa_prompt_core(pp, nsa_cmp_pos[j], nsa_cmp_w1[j], nsa_cmp_w2[j], nsa_tables)
            as_, new_kv_s, new_win_s = _nsa_sample_core(ps, nsa_caches, j, cache_nsa_win[j], page_table,
                                                        nsa_cmp_pos[j], nsa_cmp_w1[j], nsa_cmp_w2[j], nsa_tables_s)
            kv_p.append(new_kv_p)
            kv_s.append(new_kv_s)
            win_p.append(new_win_p)
            win_s.append(new_win_s)
        elif kind == 1:
            w_in = hg_w_in[j].astype(bf16)
            w_out = hg_w_out[j].astype(bf16)
            pp = _norm_mod_matmul(xp, g[0], scp_m, shp_m, w_in, tp, tm_p).reshape(bp, tp, -1)
            ps = _norm_mod_matmul(xs, g[0], scs_m, shs_m, w_in, ts, tm_s).reshape(bs, ts, -1)
            ap, new_hp = _hgrn_prompt(pp, lower_bounds[i], hg_norm[j])
            as_, new_hs = _hgrn_step(ps.reshape(bs, -1), state_hgrn[j], lower_bounds[i], hg_norm[j])
            hg_p.append(new_hp)
            hg_s.append(new_hs)
        else:
            n_pad = SSD_PROJ_W
            w_in = _pad_cols(ssd_w_in[j], n_pad).astype(bf16)
            w_out = ssd_w_out[j].astype(bf16)
            pp = _norm_mod_matmul(xp, g[0], scp_m, shp_m, w_in, tp, tm_p).reshape(bp, tp, n_pad)
            ps = _norm_mod_matmul(xs, g[0], scs_m, shs_m, w_in, ts, tm_s).reshape(bs, ts, n_pad)
            ap, new_sp, new_cp = _ssd_prompt(pp, ssd_conv_w[j], ssd_conv_b[j], ssd_dt_bias[j],
                                             ssd_a_log[j], ssd_d[j], ssd_norm[j])
            as_, new_ss, new_cs = _ssd_step(ps.reshape(bs, -1), state_ssd[j], state_ssd_conv[j], ssd_conv_w[j],
                                            ssd_conv_b[j], ssd_dt_bias[j], ssd_a_log[j], ssd_d[j], ssd_norm[j])
            ssd_p.append(new_sp)
            ssd_s.append(new_ss)
            conv_p.append(new_cp)
            conv_s.append(new_cs)
        xp = _matmul_norm_res(ap, w_out, xp, g[1], gtp_m, tp, tm_p)
        xs = _matmul_norm_res(as_, w_out, xs, g[1], gts_m, ts, tm_s)
        w1 = w_mlp_in[i].astype(bf16)
        w2 = w_mlp_out[i].astype(bf16)
        xp = _mlp(xp, g[2], scp_f, shp_f, w1, w2, g[3], gtp_f, tp, tm_p)
        xs = _mlp(xs, g[2], scs_f, shs_f, w1, w2, g[3], gts_f, ts, tm_s)
    return (xp.reshape(bp, tp, d), xs.reshape(bs, ts, d),
            jnp.stack(kv_p), jnp.stack(kv_s), jnp.stack(win_p), jnp.stack(win_s),
            jnp.stack(hg_p), jnp.stack(hg_s), jnp.stack(ssd_p), jnp.stack(ssd_s),
            jnp.stack(conv_p), jnp.stack(conv_s))
```

```python
import functools
import math

import jax
import jax.numpy as jnp
import numpy as np
from jax import lax
from jax.experimental import pallas as pl
from jax.experimental.pallas import tpu as pltpu

D_MODEL = 1024
DEPTH = 4
PAGE_SIZE = 128
N_MIXERS = 3
ADA_CHUNKS = 6
NORM_EPS = 1e-6
D_FF = 4 * D_MODEL

NSA_HEADS = 16
NSA_HEAD_DIM = D_MODEL // NSA_HEADS
NSA_KV_HEADS = 4
NSA_GROUP = NSA_HEADS // NSA_KV_HEADS
CMP_STRIDE = 16
CMP_LEN = 2 * CMP_STRIDE
CMP_HIDDEN = 2 * NSA_HEAD_DIM
SEL_BLOCK = 64
SEL_TOP_N = 16
WINDOW = 512
NSA_Q_W = NSA_HEADS * NSA_HEAD_DIM
NSA_KV_W = NSA_KV_HEADS * NSA_HEAD_DIM
NSA_IN_W = NSA_Q_W + 6 * NSA_KV_W + 3 * NSA_HEADS

REL_BUCKETS = 32
REL_MAX_DIST = 128

HG_EXPAND = 128
HG_HEADS = D_MODEL // HG_EXPAND
HG_DK = HG_EXPAND
HG_DV = D_MODEL // HG_HEADS

SSD_D_INNER = 2 * D_MODEL
SSD_HEAD_DIM = 64
SSD_HEADS = SSD_D_INNER // SSD_HEAD_DIM
SSD_GROUPS = 8
SSD_HPG = SSD_HEADS // SSD_GROUPS
SSD_STATE = 128
SSD_CONV_W = 4
SSD_CONV_DIM = SSD_D_INNER + 2 * SSD_GROUPS * SSD_STATE
SSD_IN_W = SSD_D_INNER + SSD_CONV_DIM + SSD_HEADS

NEG_INF = -1e30
FORCE_SCORE = 1e4

LANES = 128
VMEM_LIMIT_BYTES = 48 * 1024 * 1024
PROMPT_ROW_TILE = 1024


def _round_up(n, m):
    return -(-n // m) * m


def _col_tile(n, cap=1536):
    best = LANES
    for t in range(LANES, cap + 1, LANES):
        if n % t == 0:
            best = t
    return best


def _rms(x, g):
    return x * lax.rsqrt(jnp.mean(x * x, axis=-1, keepdims=True) + NORM_EPS) * g


def _mod_spec(mod, rows_per_mod, tm, ngrid):
    r = mod.shape[1]
    if r == 1:
        per = rows_per_mod // tm
        if ngrid == 1:
            return pl.BlockSpec((None, 1, mod.shape[2]), lambda i: (i // per, 0, 0))
        return pl.BlockSpec((None, 1, mod.shape[2]), lambda i, j: (i // per, 0, 0))
    if ngrid == 1:
        return pl.BlockSpec((None, r, mod.shape[2]), lambda i: (0, 0, 0))
    return pl.BlockSpec((None, r, mod.shape[2]), lambda i, j: (0, 0, 0))


def _ada_kernel(c_ref, w_ref, b_ref, o_ref):
    c = c_ref[...]
    s = (c * jax.nn.sigmoid(c)).astype(jnp.bfloat16)
    o_ref[...] = jnp.dot(s, w_ref[...].astype(jnp.bfloat16),
                         preferred_element_type=jnp.float32) + b_ref[...]


def _ada_all(c_all, w_ada, b_ada):
    rows = c_all.shape[0]
    n = ADA_CHUNKS * D_MODEL
    tn = 1024
    return pl.pallas_call(
        _ada_kernel,
        grid=(DEPTH, n // tn),
        in_specs=[pl.BlockSpec((rows, D_MODEL), lambda l, j: (0, 0)),
                  pl.BlockSpec((None, D_MODEL, tn), lambda l, j: (l, 0, j)),
                  pl.BlockSpec((None, 1, tn), lambda l, j: (l, 0, j))],
        out_specs=pl.BlockSpec((None, rows, tn), lambda l, j: (l, 0, j)),
        out_shape=jax.ShapeDtypeStruct((DEPTH, rows, n), jnp.float32),
        compiler_params=pltpu.CompilerParams(
            dimension_semantics=("parallel", "parallel"), vmem_limit_bytes=VMEM_LIMIT_BYTES),
        name="ada",
    )(c_all, w_ada, b_ada.reshape(DEPTH, 1, n))


def _norm_mod_matmul_kernel(x_ref, g_ref, sc_ref, sh_ref, w_ref, o_ref, h_ref):
    @pl.when(pl.program_id(1) == 0)
    def _():
        h = _rms(x_ref[...], g_ref[...]) * (1.0 + sc_ref[...]) + sh_ref[...]
        h_ref[...] = h.astype(jnp.bfloat16)

    o_ref[...] = jnp.dot(h_ref[...], w_ref[...], preferred_element_type=jnp.float32)


def _norm_mod_matmul(x, g, sc, sh, w, rows_per_mod, tm):
    m, d = x.shape
    n = w.shape[1]
    tn = _col_tile(n)
    return pl.pallas_call(
        _norm_mod_matmul_kernel,
        grid=(m // tm, n // tn),
        in_specs=[pl.BlockSpec((tm, d), lambda i, j: (i, 0)),
                  pl.BlockSpec((1, d), lambda i, j: (0, 0)),
                  _mod_spec(sc, rows_per_mod, tm, 2),
                  _mod_spec(sh, rows_per_mod, tm, 2),
                  pl.BlockSpec((d, tn), lambda i, j: (0, j))],
        out_specs=pl.BlockSpec((tm, tn), lambda i, j: (i, j)),
        out_shape=jax.ShapeDtypeStruct((m, n), jnp.float32),
        scratch_shapes=[pltpu.VMEM((tm, d), jnp.bfloat16)],
        compiler_params=pltpu.CompilerParams(
            dimension_semantics=("parallel", "arbitrary"), vmem_limit_bytes=VMEM_LIMIT_BYTES),
        name="norm_mod_matmul",
    )(x, g.reshape(1, d), sc, sh, w)


def _matmul_norm_res_kernel(a_ref, w_ref, x_ref, g_ref, gt_ref, o_ref):
    y = jnp.dot(a_ref[...].astype(jnp.bfloat16), w_ref[...], preferred_element_type=jnp.float32)
    o_ref[...] = x_ref[...] + gt_ref[...] * _rms(y, g_ref[...])


def _matmul_norm_res(a, w, x, g, gate, rows_per_mod, tm):
    m, k = a.shape
    d = w.shape[1]
    return pl.pallas_call(
        _matmul_norm_res_kernel,
        grid=(m // tm,),
        in_specs=[pl.BlockSpec((tm, k), lambda i: (i, 0)),
                  pl.BlockSpec((k, d), lambda i: (0, 0)),
                  pl.BlockSpec((tm, d), lambda i: (i, 0)),
                  pl.BlockSpec((1, d), lambda i: (0, 0)),
                  _mod_spec(gate, rows_per_mod, tm, 1)],
        out_specs=pl.BlockSpec((tm, d), lambda i: (i, 0)),
        out_shape=jax.ShapeDtypeStruct((m, d), jnp.float32),
        compiler_params=pltpu.CompilerParams(
            dimension_semantics=("parallel",), vmem_limit_bytes=VMEM_LIMIT_BYTES),
        name="matmul_norm_res",
    )(a, w, x, g.reshape(1, d), gate)


def _mlp_kernel(x_ref, g2_ref, sc_ref, sh_ref, w1_ref, w2_ref, g3_ref, gt_ref, o_ref, h_ref, acc_ref):
    j = pl.program_id(1)

    @pl.when(j == 0)
    def _():
        h = _rms(x_ref[...], g2_ref[...]) * (1.0 + sc_ref[...]) + sh_ref[...]
        h_ref[...] = h.astype(jnp.bfloat16)

    u = jnp.dot(h_ref[...], w1_ref[...], preferred_element_type=jnp.float32)
    u = jnp.square(jnp.maximum(u, 0.0)).astype(jnp.bfloat16)
    part = jnp.dot(u, w2_ref[...], preferred_element_type=jnp.float32)

    @pl.when(j == 0)
    def _():
        acc_ref[...] = part

    @pl.when(j > 0)
    def _():
        acc_ref[...] += part

    @pl.when(j == pl.num_programs(1) - 1)
    def _():
        o_ref[...] = x_ref[...] + gt_ref[...] * _rms(acc_ref[...], g3_ref[...])


def _mlp(x, g2, sc, sh, w1, w2, g3, gate, rows_per_mod, tm):
    m, d = x.shape
    f = w1.shape[1]
    tf = 1024
    return pl.pallas_call(
        _mlp_kernel,
        grid=(m // tm, f // tf),
        in_specs=[pl.BlockSpec((tm, d), lambda i, j: (i, 0)),
                  pl.BlockSpec((1, d), lambda i, j: (0, 0)),
                  _mod_spec(sc, rows_per_mod, tm, 2),
                  _mod_spec(sh, rows_per_mod, tm, 2),
                  pl.BlockSpec((d, tf), lambda i, j: (0, j)),
                  pl.BlockSpec((tf, d), lambda i, j: (j, 0)),
                  pl.BlockSpec((1, d), lambda i, j: (0, 0)),
                  _mod_spec(gate, rows_per_mod, tm, 2)],
        out_specs=pl.BlockSpec((tm, d), lambda i, j: (i, 0)),
        out_shape=jax.ShapeDtypeStruct((m, d), jnp.float32),
        scratch_shapes=[pltpu.VMEM((tm, d), jnp.bfloat16), pltpu.VMEM((tm, d), jnp.float32)],
        compiler_params=pltpu.CompilerParams(
            dimension_semantics=("parallel", "arbitrary"), vmem_limit_bytes=VMEM_LIMIT_BYTES),
        name="mlp",
    )(x, g2.reshape(1, d), sc, sh, w1, w2, g3.reshape(1, d), gate)


NSA_COL_BLOCK = NSA_KV_W
NSA_PROJ_W = 11 * NSA_COL_BLOCK
NSA_GATE_BLOCK = (NSA_Q_W + 6 * NSA_KV_W) // NSA_COL_BLOCK
ATT_TILE = 128
ROWS = NSA_GROUP * ATT_TILE
ATT_TILE_GROUP = 4


def _dot_nt(a, b):
    return lax.dot_general(a, b, (((1,), (1,)), ((), ())), preferred_element_type=jnp.float32)


def _dot_tn(a, b):
    return lax.dot_general(a, b, (((0,), (0,)), ((), ())), preferred_element_type=jnp.float32)


def _gelu_tanh(x):
    return 0.5 * x * (1.0 + jnp.tanh(math.sqrt(2.0 / math.pi) * (x + 0.044715 * (x * x * x))))


def _split3_bf16(x):
    hi = x.astype(jnp.bfloat16)
    r1 = x - hi.astype(jnp.float32)
    mid = r1.astype(jnp.bfloat16)
    lo = (r1 - mid.astype(jnp.float32)).astype(jnp.bfloat16)
    return hi, mid, lo


def _nsa_compress_kernel(x0_ref, x1_ref, x2_ref, x3_ref, w1_ref, w1f_ref, pos_ref, w2_ref, o_ref):
    n = x0_ref.shape[0] // CMP_STRIDE
    hd = NSA_HEAD_DIM
    x_refs = ((x0_ref, x1_ref), (x2_ref, x3_ref))
    for r in range(2):
        pos_b = jnp.dot(pos_ref[r].astype(jnp.bfloat16), w1f_ref[r], preferred_element_type=jnp.float32)
        acc = [jnp.zeros((n, 2 * CMP_HIDDEN), jnp.float32) for _ in range(NSA_KV_HEADS)]
        for l in range(CMP_STRIDE):
            w = w1_ref[r, l]
            for pair in range(2):
                xl = x_refs[r][pair][pl.ds(l, n, stride=CMP_STRIDE), :].astype(jnp.bfloat16)
                for half in range(2):
                    k = 2 * pair + half
                    acc[k] = acc[k] + jnp.dot(xl[:, half * hd:(half + 1) * hd], w,
                                              preferred_element_type=jnp.float32)
        for k in range(NSA_KV_HEADS):
            pa = acc[k][:, :CMP_HIDDEN]
            pb_next = pltpu.roll(acc[k][:, CMP_HIDDEN:], n - 1, 0)
            hid = _gelu_tanh(pa + pb_next + pos_b)
            out = jnp.dot(hid.astype(jnp.bfloat16), w2_ref[r], preferred_element_type=jnp.float32)
            o_ref[r, :, k * hd:(k + 1) * hd] = out.astype(o_ref.dtype)


def _compress_weights(cmp_pos, w1, w2):
    bf16 = jnp.bfloat16
    w1r = w1.reshape(2, CMP_LEN, NSA_HEAD_DIM, CMP_HIDDEN)
    w1cat = jnp.concatenate([w1r[:, :CMP_STRIDE], w1r[:, CMP_STRIDE:]], axis=-1).astype(bf16)
    return w1cat, w1.astype(bf16), cmp_pos.reshape(2, 1, CMP_LEN * NSA_HEAD_DIM), w2.astype(bf16)


def _nsa_compress_prompt(proj, cmp_pos, w1, w2):
    b, t, _ = proj.shape
    n = t // CMP_STRIDE
    w1cat, w1f, pos, w2b = _compress_weights(cmp_pos, w1, w2)
    return pl.pallas_call(
        _nsa_compress_kernel,
        grid=(b,),
        in_specs=[pl.BlockSpec((None, t, LANES), lambda i, c=c: (i, 0, NSA_Q_W // LANES + c)) for c in range(4)]
                 + [pl.BlockSpec(w1cat.shape, lambda i: (0, 0, 0, 0)),
                  pl.BlockSpec(w1f.shape, lambda i: (0, 0, 0)),
                  pl.BlockSpec(pos.shape, lambda i: (0, 0, 0)),
                  pl.BlockSpec(w2b.shape, lambda i: (0, 0, 0))],
        out_specs=pl.BlockSpec((None, 2, n, NSA_KV_W), lambda i: (i, 0, 0, 0)),
        out_shape=jax.ShapeDtypeStruct((b, 2, n, NSA_KV_W), jnp.bfloat16),
        compiler_params=pltpu.CompilerParams(
            dimension_semantics=("parallel",), vmem_limit_bytes=VMEM_LIMIT_BYTES),
        name="nsa_compress",
    )(proj, proj, proj, proj, w1cat, w1f, pos, w2b)


def _bias_lookup(rel_bias, dist):
    onehot = jax.nn.one_hot(_rel_bucket(dist), REL_BUCKETS, dtype=jnp.float32)
    return jnp.einsum('...c,ch->...h', onehot, rel_bias, precision=lax.Precision.HIGHEST)


DEN_ROWS = 8


def _with_ones(v):
    return jnp.concatenate([v, jnp.ones((v.shape[0], DEN_ROWS), v.dtype)], axis=1)


def _key_softmax_step(s, v, m, acc):
    m_new = jnp.maximum(m, jnp.max(s, axis=0, keepdims=True))
    e = jnp.exp(s - m_new).astype(jnp.bfloat16)
    acc = jnp.exp(m - m_new) * acc + _dot_tn(_with_ones(v), e)
    return m_new, acc


def _softmax_out(acc):
    hd = acc.shape[0] - DEN_ROWS
    return acc[:hd] / acc[hd:hd + 1]


def _nsa_attn_t_kernel(q_ref, g_ref, c_ref, ks_ref, vs_ref, kw_ref, vw_ref, bc_ref, bt_ref,
                       mimp_ref, eg_ref, o_ref, oc_s, os_s, ow_s, sel_s, *, tile0, n_far):
    i = tile0 + pl.program_id(1)
    hd = NSA_HEAD_DIM
    tq = ATT_TILE
    bf16 = jnp.bfloat16
    n_cmp_pad = c_ref.shape[1]
    n_sel = mimp_ref.shape[0]
    kj = lax.broadcasted_iota(jnp.int32, (tq, ROWS), 0)
    qi = lax.broadcasted_iota(jnp.int32, (tq, ROWS), 1) % tq
    causal = kj <= qi
    win_edge = kj >= qi
    cmp_end = CMP_STRIDE * lax.broadcasted_iota(jnp.int32, (n_cmp_pad, ROWS), 0) + (CMP_LEN - 1)
    mask_c = cmp_end <= i * tq + lax.broadcasted_iota(jnp.int32, (n_cmp_pad, ROWS), 1) % tq
    blk = lax.broadcasted_iota(jnp.int32, (n_sel, tq), 0)
    cur = (i * tq + lax.broadcasted_iota(jnp.int32, (n_sel, tq), 1)) // SEL_BLOCK
    forced = (blk == 0) | (blk == cur) | (blk == cur - 1)
    valid = blk <= cur

    heads = range(NSA_KV_HEADS)
    lanes = [slice(k * hd, (k + 1) * hd) for k in heads]
    per_chunk = tq // SEL_BLOCK
    qk, o_cmp = [], []
    for k in heads:
        q = jnp.concatenate(
            [q_ref[:, (k * NSA_GROUP + g) * hd:(k * NSA_GROUP + g + 1) * hd] for g in range(NSA_GROUP)], axis=0)
        qk.append((q * (hd ** -0.5)).astype(bf16))

        s = jnp.where(mask_c, _dot_nt(c_ref[0, :, lanes[k]], qk[k]) + bc_ref[k], NEG_INF)
        m = jnp.max(s, axis=0, keepdims=True)
        e = jnp.where(mask_c, jnp.exp(s - m), 0.0)
        p = e * (1.0 / jnp.maximum(jnp.sum(e, axis=0, keepdims=True), 1e-30))
        o_cmp.append(_dot_tn(c_ref[1, :, lanes[k]], p.astype(bf16)))
        p_sum = sum(p[:, g * tq:(g + 1) * tq] for g in range(NSA_GROUP))
        imp = _dot3(p_sum, mimp_ref[...], lambda a, b, **kw: jnp.dot(b, a, **kw))
        score = jnp.where(valid, jnp.where(forced, FORCE_SCORE, imp), NEG_INF)
        rank = jnp.zeros((n_sel, tq), jnp.float32)
        for j in range(n_sel):
            row = score[j:j + 1, :]
            beats = (row > score) | ((row == score) & (blk > j))
            rank = rank + jnp.where(beats, 1.0, 0.0)
        sel = jnp.where((rank < SEL_TOP_N) & (score > 0.5 * NEG_INF), 1.0, 0.0)
        sel = jnp.concatenate([sel] * NSA_GROUP, axis=1)
        for c in range(n_sel // per_chunk):
            sel_s[k, c, 0:per_chunk, :] = sel[c * per_chunk:(c + 1) * per_chunk, :]

    def rows_of(c, n=1):
        return pl.ds(c * tq, n * tq) if isinstance(c, int) else pl.ds(pl.multiple_of(c * tq, tq), n * tq)

    def chunk(k_ref, v_ref, k, rows, carry, bias, mk):
        s = _dot_nt(k_ref[rows, lanes[k]].astype(bf16), qk[k])
        if bias is not None:
            s = s + bias
        return _key_softmax_step(jnp.where(mk, s, NEG_INF), v_ref[rows, lanes[k]].astype(bf16), *carry)

    def sel_mask(k, c, ok):
        pair = sel_s[k, c, 0:per_chunk, :]
        picked = jnp.concatenate([jnp.broadcast_to(pair[j:j + 1, :], (SEL_BLOCK, ROWS)) for j in range(per_chunk)],
                                 axis=0)
        return (picked > 0.5) & jnp.broadcast_to(ok, (tq, ROWS))

    init = (jnp.full((1, ROWS), NEG_INF, jnp.float32), jnp.zeros((hd + DEN_ROWS, ROWS), jnp.float32))
    carry = [init] * NSA_KV_HEADS
    c_prev = jnp.maximum(i - 1, 0)
    for k in heads:
        carry[k] = chunk(ks_ref, vs_ref, k, rows_of(i), carry[k], bt_ref[k, 0], sel_mask(k, i, True) & causal)
    for k in heads:
        carry[k] = chunk(ks_ref, vs_ref, k, rows_of(c_prev), carry[k], bt_ref[k, 1], sel_mask(k, c_prev, i >= 1))
    for c in range(0, n_far, 2):
        for k in heads:
            mk = jnp.concatenate([sel_mask(k, c, c < i - 1), sel_mask(k, c + 1, c + 1 < i - 1)], axis=0)
            carry[k] = chunk(ks_ref, vs_ref, k, rows_of(c, 2), carry[k], None, mk)
    o_sel = [_softmax_out(carry[k][1]) for k in heads]

    carry = [init] * NSA_KV_HEADS
    n_back = WINDOW // tq

    def tile_ok(c):
        return jnp.broadcast_to(c >= 0, (tq, ROWS))

    for k in heads:
        carry[k] = chunk(kw_ref, vw_ref, k, rows_of(i + n_back), carry[k], bt_ref[k, 0], causal)
    mk = jnp.concatenate([tile_ok(i - 2), tile_ok(i - 1)], axis=0)
    for k in heads:
        bias = jnp.concatenate([jnp.zeros((tq, ROWS), jnp.float32), bt_ref[k, 1]], axis=0)
        carry[k] = chunk(kw_ref, vw_ref, k, rows_of(i + n_back - 2, 2), carry[k], bias, mk)
    mk = jnp.concatenate([tile_ok(i - 4) & win_edge, tile_ok(i - 3)], axis=0)
    for k in heads:
        carry[k] = chunk(kw_ref, vw_ref, k, rows_of(i + n_back - 4, 2), carry[k], None, mk)
    o_win = [_softmax_out(carry[k][1]) for k in heads]

    for k in heads:
        for g in range(NSA_GROUP):
            rows = slice((k * NSA_GROUP + g) * hd, (k * NSA_GROUP + g + 1) * hd)
            oc_s[rows, :] = o_cmp[k][:, g * tq:(g + 1) * tq]
            os_s[rows, :] = o_sel[k][:, g * tq:(g + 1) * tq]
            ow_s[rows, :] = o_win[k][:, g * tq:(g + 1) * tq]

    gate = jax.nn.sigmoid(g_ref[...])
    g_hi = gate.astype(bf16)
    g_lo = (gate - g_hi.astype(jnp.float32)).astype(bf16)
    out = jnp.zeros((NSA_Q_W, tq), jnp.float32)
    for br, o_s in enumerate((oc_s, os_s, ow_s)):
        out = out + (_dot_nt(eg_ref[br], g_hi) + _dot_nt(eg_ref[br], g_lo)) * o_s[...]
    for r in range(NSA_Q_W // tq):
        o_ref[:, r * tq:(r + 1) * tq] = out[r * tq:(r + 1) * tq, :].T.astype(o_ref.dtype)


def _keys_by_kv_head(tab):
    *lead, q, t, _ = tab.shape
    tab = tab.reshape(*lead, q, t, NSA_KV_HEADS, NSA_GROUP)
    nl = len(lead)
    tab = jnp.transpose(tab, (*range(nl), nl + 2, nl + 1, nl + 3, nl))
    return tab.reshape(*lead, NSA_KV_HEADS, t, NSA_GROUP * q)


def _nsa_prompt_tables_t(rel_bias, t):
    tq = ATT_TILE
    n_chunks = t // CMP_STRIDE
    n_sel = t // SEL_BLOCK
    far = rel_bias[REL_BUCKETS - 1]
    ar = jnp.arange(tq)
    d_tile = (jnp.arange(2) * tq)[:, None, None] + ar[None, :, None] - ar[None, None, :]
    bt = _keys_by_kv_head(_bias_lookup(rel_bias, d_tile) - far)
    bt = jnp.transpose(bt, (1, 0, 2, 3))
    q_pos = jnp.arange(t).reshape(t // tq, tq)
    cmp_end = jnp.arange(n_chunks) * CMP_STRIDE + CMP_LEN - 1
    bc = _keys_by_kv_head(_bias_lookup(rel_bias, q_pos[:, :, None] - cmp_end[None, None, :]) - far)
    n_idx = np.arange(n_chunks)
    j_idx = np.arange(n_sel)[:, None]
    per = SEL_BLOCK // CMP_STRIDE
    mimp = 0.5 * ((n_idx // per == j_idx).astype(np.float32) + ((n_idx + 1) // per == j_idx).astype(np.float32))
    mimp[:, n_chunks - 1] = 0.0
    col = np.arange(NSA_Q_W) // NSA_HEAD_DIM
    eg = np.zeros((3, NSA_Q_W, NSA_COL_BLOCK), np.float32)
    for br in range(3):
        eg[br, np.arange(NSA_Q_W), br * NSA_HEADS + col] = 1.0
    return bt, bc, jnp.asarray(mimp, jnp.bfloat16), jnp.asarray(eg, jnp.bfloat16)


def _nsa_attn_prompt_t(proj, cmp, tables):
    b, t, _ = proj.shape
    bt, bc, mimp, eg = tables
    tq = ATT_TILE
    cb = NSA_COL_BLOCK
    first_kv = NSA_Q_W // cb

    def kv_spec(slab):
        return pl.BlockSpec((None, t, cb), lambda bi, i: (bi, 0, first_kv + slab))

    def const_spec(a):
        nd = a.ndim
        return pl.BlockSpec(a.shape, lambda bi, i: (0,) * nd)

    assert WINDOW == 4 * tq
    win0 = NSA_Q_W + 4 * NSA_KV_W
    kw, vw = (jnp.pad(proj[:, :, c0:c0 + cb].astype(jnp.bfloat16), ((0, 0), (WINDOW, 0), (0, 0)))
              for c0 in (win0, win0 + cb))
    win_spec = pl.BlockSpec((None, t + WINDOW, cb), lambda bi, i: (bi, 0, 0))

    def tile_group(tile0):
        n_far = max(tile0 + ATT_TILE_GROUP - 2, 0)
        return pl.pallas_call(
            functools.partial(_nsa_attn_t_kernel, tile0=tile0, n_far=n_far),
            grid=(b, ATT_TILE_GROUP),
            in_specs=[pl.BlockSpec((None, tq, NSA_Q_W), lambda bi, i: (bi, tile0 + i, 0)),
                      pl.BlockSpec((None, tq, cb), lambda bi, i: (bi, tile0 + i, NSA_GATE_BLOCK)),
                      pl.BlockSpec((None,) + cmp.shape[1:], lambda bi, i: (bi, 0, 0, 0)),
                      kv_spec(2), kv_spec(3), win_spec, win_spec,
                      pl.BlockSpec((None,) + bc.shape[1:], lambda bi, i: (tile0 + i, 0, 0, 0)),
                      const_spec(bt), const_spec(mimp), const_spec(eg)],
            out_specs=pl.BlockSpec((None, tq, NSA_Q_W), lambda bi, i: (bi, i, 0)),
            out_shape=jax.ShapeDtypeStruct((b, ATT_TILE_GROUP * tq, NSA_Q_W), jnp.bfloat16),
            scratch_shapes=[pltpu.VMEM((NSA_Q_W, tq), jnp.float32)] * 3
                           + [pltpu.VMEM((NSA_KV_HEADS, t // tq, 8, ROWS), jnp.float32)],
            compiler_params=pltpu.CompilerParams(
                dimension_semantics=("parallel", "arbitrary"), vmem_limit_bytes=VMEM_LIMIT_BYTES),
            name="nsa_attn",
        )(proj, proj, cmp, proj, proj, kw, vw, bc, bt, mimp, eg)

    parts = [tile_group(tile0) for tile0 in range(0, t // tq, ATT_TILE_GROUP)]
    return jnp.concatenate(parts, axis=1).reshape(b * t, NSA_Q_W)


PAGE_GROUP = 8
SEL_PAD = 8


def _nsa_sample_kernel(pt_ref, *refs):
    n_cmp_in = 2 * PAGE_GROUP
    n_sel_in = 2 * PAGE_GROUP
    cmp_pages = refs[:n_cmp_in]
    sel_pages = refs[n_cmp_in:n_cmp_in + n_sel_in]
    (qbd_ref, qbdt_ref, new_ref, win_ref, w1_ref, w1f_ref, pos_ref, w2_ref, bcmp_ref, bsel_ref, bwin_ref,
     mimp_ref, hk_ref, bd_ref, rep_ref, bdt_ref, gsel_ref, eexp_ref, o_ref,
     a_s, stage_s, kc_s, vc_s, score_s, mask_s, kn_s, vn_s, ocmp_s, m_s, l_s, acc_s) = refs[n_cmp_in + n_sel_in:]
    del pt_ref
    ph = pl.program_id(1)
    g = pl.program_id(2)
    n_groups = pl.num_programs(2)
    bf16 = jnp.bfloat16
    hd = NSA_HEAD_DIM
    n_chunks = a_s.shape[1]
    n_sel = mimp_ref.shape[1]
    per_page = PAGE_SIZE // CMP_STRIDE
    qbd = qbd_ref[...]

    def heads_out(acc_t, l):
        o = (acc_t / l) * bd_ref[...]
        hi = o.astype(bf16)
        lo = (o - hi.astype(jnp.float32)).astype(bf16)
        return _dot_tn(hi, rep_ref[...]) + _dot_tn(lo, rep_ref[...])

    @pl.when(ph == 0)
    def _():
        for u in range(PAGE_GROUP):
            row0 = pl.multiple_of((g * PAGE_GROUP + u) * per_page, per_page)
            for r in range(2):
                for pair in range(2):
                    c = 2 * r + pair
                    stage_s[c] = cmp_pages[u * 2 + r][2 * pair:2 * pair + 2].reshape(LANES, PAGE_SIZE).T
                    for l in range(CMP_STRIDE):
                        a_s[c, pl.ds(row0, per_page), l * LANES:(l + 1) * LANES] = (
                            stage_s[c, pl.ds(l, per_page, stride=CMP_STRIDE), :])

    @pl.when((ph == 0) & (g == n_groups - 1))
    def _():
        for r in range(2):
            pos_b = jnp.dot(pos_ref[r].astype(bf16), w1f_ref[r], preferred_element_type=jnp.float32)
            acc = [jnp.dot(a_s[2 * r + pair].astype(bf16), w1_ref[r], preferred_element_type=jnp.float32)
                   for pair in range(2)]
            dst = kc_s if r == 0 else vc_s
            for k in range(NSA_KV_HEADS):
                cols = (k % 2) * 2 * CMP_HIDDEN
                pa = acc[k // 2][:, cols:cols + CMP_HIDDEN]
                pb_next = pltpu.roll(acc[k // 2][:, cols + CMP_HIDDEN:cols + 2 * CMP_HIDDEN], n_chunks - 1, 0)
                hid = _gelu_tanh(pa + pb_next + pos_b)
                out = jnp.dot(hid.astype(bf16), w2_ref[r], preferred_element_type=jnp.float32)
                dst[:, k * hd:(k + 1) * hd] = out.astype(bf16)
        rows = lax.broadcasted_iota(jnp.int32, (n_chunks, NSA_HEADS), 0)
        mask_c = rows <= n_chunks - 2
        s = jnp.dot(kc_s[...], qbd, preferred_element_type=jnp.float32) + bcmp_ref[...]
        s = jnp.where(mask_c, s, NEG_INF)
        m = jnp.max(s, axis=0, keepdims=True)
        e = jnp.where(mask_c, jnp.exp(s - m), 0.0)
        l = jnp.maximum(jnp.sum(e, axis=0, keepdims=True), 1e-30)
        p = e / l
        ocmp_s[...] = heads_out(_dot_tn(vc_s[...], p.astype(bf16)), jnp.ones_like(l))
        p_kv = sum(jnp.dot(part, hk_ref[...], preferred_element_type=jnp.float32) for part in _split3_bf16(p))
        imp = sum(jnp.dot(mimp_ref[...], part, preferred_element_type=jnp.float32) for part in _split3_bf16(p_kv))
        blk = lax.broadcasted_iota(jnp.int32, imp.shape, 0)
        cur = n_chunks * CMP_STRIDE // SEL_BLOCK
        forced = (blk == 0) | (blk == cur) | (blk == cur - 1)
        score = jnp.where(blk <= cur, jnp.where(forced, FORCE_SCORE, imp), NEG_INF)
        score_s[...] = score

        def rank_body(j, rank):
            row = score_s[pl.ds(j, 1), :]
            beats = (row > score) | ((row == score) & (blk > j))
            return rank + jnp.where(beats, 1.0, 0.0)

        rank = lax.fori_loop(0, cur + 1, rank_body, jnp.zeros(imp.shape, jnp.float32), unroll=4)
        sel = jnp.where((rank < SEL_TOP_N) & (score > 0.5 * NEG_INF), 1.0, 0.0)
        sel_h = jnp.dot(sel.astype(bf16), gsel_ref[...], preferred_element_type=jnp.float32)
        mask_s[...] = _dot_tn(sel_h.astype(bf16), eexp_ref[...])
        m_s[...] = jnp.full(m_s.shape, NEG_INF, jnp.float32)
        l_s[...] = jnp.zeros(l_s.shape, jnp.float32)
        acc_s[...] = jnp.zeros(acc_s.shape, jnp.float32)

    qbd_t = qbdt_ref[...]

    def lane_step(s, mask, state, pv):
        m, l, acc = state
        s = jnp.where(mask, s, NEG_INF)
        m_new = jnp.maximum(m, jnp.max(s, axis=1, keepdims=True))
        alpha = jnp.exp(m - m_new)
        e = jnp.where(mask, jnp.exp(s - m_new), 0.0)
        return m_new, alpha * l + jnp.sum(e, axis=1, keepdims=True), alpha * acc + pv(e.astype(bf16))

    def heads_out_t(acc, l):
        o = (acc / l) * bdt_ref[...]
        return sum(o[:, k * hd:(k + 1) * hd] for k in range(NSA_KV_HEADS))

    @pl.when(ph == 1)
    def _():
        span = PAGE_GROUP * PAGE_SIZE
        kt = jnp.concatenate([sel_pages[2 * u][...].reshape(NSA_KV_W, PAGE_SIZE).astype(bf16)
                              for u in range(PAGE_GROUP)], axis=1)
        vt = jnp.concatenate([sel_pages[2 * u + 1][...].reshape(NSA_KV_W, PAGE_SIZE).astype(bf16)
                              for u in range(PAGE_GROUP)], axis=1)
        mask = mask_s[:, pl.ds(pl.multiple_of(g * span, span), span)] > 0.5
        s = jnp.dot(qbd_t, kt, preferred_element_type=jnp.float32)
        near = jnp.where(g == n_groups - 1, 1.0, 0.0) * bsel_ref[:, 0:PAGE_SIZE]
        s = jnp.concatenate([s[:, :span - PAGE_SIZE], s[:, span - PAGE_SIZE:] + near], axis=1)
        m_s[...], l_s[...], acc_s[...] = lane_step(s, mask, (m_s[...], l_s[...], acc_s[...]),
                                                   lambda e: _dot_nt(e, vt))

    @pl.when((ph == 1) & (g == n_groups - 1))
    def _():
        kv0 = NSA_Q_W
        first = lax.broadcasted_iota(jnp.int32, (NSA_HEADS, SEL_PAD), 1) < 1

        def new_token_step(k_col, v_col, bias, state):
            kn_s[...] = jnp.zeros(kn_s.shape, bf16)
            vn_s[...] = jnp.zeros(vn_s.shape, bf16)
            kn_s[0:1, :] = new_ref[:, k_col:k_col + NSA_KV_W].astype(bf16)
            vn_s[0:1, :] = new_ref[:, v_col:v_col + NSA_KV_W].astype(bf16)
            s = _dot_nt(qbd_t, kn_s[...]) + bias
            vn = vn_s[...]
            return lane_step(s, first, state, lambda e: jnp.dot(e, vn, preferred_element_type=jnp.float32))

        state = new_token_step(kv0 + 2 * NSA_KV_W, kv0 + 3 * NSA_KV_W, bsel_ref[:, PAGE_SIZE:PAGE_SIZE + SEL_PAD],
                               (m_s[...], l_s[...], acc_s[...]))
        o_sel = heads_out_t(state[2], state[1])
        n_buf = win_ref.shape[2]
        kt = win_ref[0].astype(bf16)
        vt = win_ref[1].astype(bf16)
        init = (jnp.full((NSA_HEADS, 1), NEG_INF, jnp.float32), jnp.zeros((NSA_HEADS, 1), jnp.float32),
                jnp.zeros((NSA_HEADS, NSA_KV_W), jnp.float32))
        s = jnp.dot(qbd_t, kt, preferred_element_type=jnp.float32) + bwin_ref[:, 0:n_buf]
        state = lane_step(s, jnp.full(s.shape, True), init, lambda e: _dot_nt(e, vt))
        state = new_token_step(kv0 + 4 * NSA_KV_W, kv0 + 5 * NSA_KV_W, bwin_ref[:, n_buf:n_buf + SEL_PAD], state)
        o_win = heads_out_t(state[2], state[1])
        gate = jax.nn.sigmoid(new_ref[:, kv0 + 6 * NSA_KV_W:kv0 + 7 * NSA_KV_W])
        out = jnp.zeros(o_ref.shape, jnp.float32)
        for br, o_b in enumerate((ocmp_s[...], o_sel, o_win)):
            onehot = lax.broadcasted_iota(jnp.int32, (NSA_HEADS, NSA_COL_BLOCK), 1) == (
                lax.broadcasted_iota(jnp.int32, (NSA_HEADS, NSA_COL_BLOCK), 0) + br * NSA_HEADS)
            g_col = jnp.sum(jnp.where(onehot, gate, 0.0), axis=1, keepdims=True)
            out = out + g_col * o_b
        o_ref[...] = out


def _nsa_sample_tables(rel_bias, past_len, n_buf):
    far = rel_bias[REL_BUCKETS - 1]
    n_chunks = past_len // CMP_STRIDE
    n_sel = past_len // SEL_BLOCK + 1
    n_sel_pad = _round_up(n_sel, 8)
    cmp_end = jnp.arange(n_chunks) * CMP_STRIDE + CMP_LEN - 1
    bcmp = _bias_lookup(rel_bias, past_len - cmp_end) - far
    k_last = past_len - PAGE_SIZE + jnp.arange(PAGE_SIZE + SEL_PAD)
    bsel = (_bias_lookup(rel_bias, past_len - k_last) - far).T
    bwin = (_bias_lookup(rel_bias, n_buf - jnp.arange(n_buf + SEL_PAD)) - far).T
    per = SEL_BLOCK // CMP_STRIDE
    n_idx = np.arange(n_chunks)
    j_idx = np.arange(n_sel_pad)[:, None]
    mimp = 0.5 * ((n_idx // per == j_idx).astype(np.float32) + ((n_idx + 1) // per == j_idx).astype(np.float32))
    mimp[:, n_chunks - 1] = 0.0
    heads = np.arange(NSA_HEADS)
    hk = (heads[:, None] // NSA_GROUP == np.arange(NSA_KV_HEADS)[None, :]).astype(np.float32)
    rowk = np.arange(NSA_KV_W) // NSA_HEAD_DIM
    bd = (rowk[:, None] == heads[None, :] // NSA_GROUP).astype(np.float32)
    rep = (np.arange(NSA_KV_W)[:, None] % NSA_HEAD_DIM == np.arange(NSA_HEAD_DIM)[None, :]).astype(np.float32)
    eexp = (np.arange(past_len)[None, :] // SEL_BLOCK == np.arange(n_sel_pad)[:, None]).astype(np.float32)
    bf16 = jnp.bfloat16
    return (bcmp, bsel, bwin, jnp.asarray(mimp, bf16), jnp.asarray(hk, bf16), jnp.asarray(bd, jnp.float32),
            jnp.asarray(rep, bf16), jnp.asarray(bd.T, jnp.float32), jnp.asarray(hk.T, bf16), jnp.asarray(eexp, bf16))


def _nsa_attn_sample(proj, kt_pages, win_t, win_row0, page_table, cmp_pos, w1, w2, tables):
    b = proj.shape[0]
    n_pages = page_table.shape[1]
    past_len = n_pages * PAGE_SIZE
    n_buf = win_t.shape[3]
    n_chunks = past_len // CMP_STRIDE
    n_groups = n_pages // PAGE_GROUP
    bcmp, bsel, bwin, mimp, hk, bd, rep, bdt, gsel, eexp = tables
    n_sel_pad = mimp.shape[0]
    w1cat, w1f, pos, w2b = _compress_weights(cmp_pos, w1, w2)
    w1cat = jnp.einsum('rlde,hg->rlhdge', w1cat, jnp.eye(2, dtype=w1cat.dtype)).reshape(
        2, CMP_STRIDE * LANES, 4 * CMP_HIDDEN)
    q = proj[:, :NSA_Q_W].reshape(b, NSA_KV_HEADS, NSA_GROUP, NSA_HEAD_DIM) * (NSA_HEAD_DIM ** -0.5)
    eye = jnp.eye(NSA_KV_HEADS, dtype=q.dtype)
    qbd = jnp.einsum('bkgd,kc->bkdcg', q, eye).reshape(b, NSA_KV_W, NSA_HEADS).astype(jnp.bfloat16)
    qbd_t = jnp.swapaxes(qbd, 1, 2)
    proj3 = proj.reshape(b, 1, proj.shape[1])

    def cmp_spec(u, r):
        def imap(i, ph, g, pt):
            gg = jnp.where(ph == 0, g, n_groups - 1)
            return (pt[i, gg * PAGE_GROUP + u], r, 0, 0)
        return pl.BlockSpec((None, NSA_KV_HEADS, NSA_HEAD_DIM, PAGE_SIZE), imap)

    def sel_spec(u, r):
        def imap(i, ph, g, pt):
            gg = jnp.where(ph == 1, g, 0)
            return (pt[i, gg * PAGE_GROUP + u], r, 0, 0)
        return pl.BlockSpec((None, NSA_KV_HEADS, NSA_HEAD_DIM, PAGE_SIZE), imap)

    def const_spec(a):
        nd = a.ndim
        return pl.BlockSpec(a.shape, lambda i, ph, g, pt: (0,) * nd)

    consts = (w1cat, w1f, pos, w2b, bcmp, bsel, bwin, mimp, hk, bd, rep, bdt, gsel, eexp)
    in_specs = ([cmp_spec(u, r) for u in range(PAGE_GROUP) for r in (0, 1)]
                + [sel_spec(u, r) for u in range(PAGE_GROUP) for r in (2, 3)]
                + [pl.BlockSpec((None, NSA_KV_W, NSA_HEADS), lambda i, ph, g, pt: (i, 0, 0)),
                   pl.BlockSpec((None, NSA_HEADS, NSA_KV_W), lambda i, ph, g, pt: (i, 0, 0)),
                   pl.BlockSpec((None, 1, proj.shape[1]), lambda i, ph, g, pt: (i, 0, 0)),
                   pl.BlockSpec((None, 2, NSA_KV_W, n_buf), lambda i, ph, g, pt: (win_row0 + i, 0, 0, 0))]
                + [const_spec(a) for a in consts])
    f32, bf16 = jnp.float32, jnp.bfloat16
    out = pl.pallas_call(
        _nsa_sample_kernel,
        grid_spec=pltpu.PrefetchScalarGridSpec(
            num_scalar_prefetch=1,
            grid=(b, 2, n_groups),
            in_specs=in_specs,
            out_specs=pl.BlockSpec((None, NSA_HEADS, NSA_HEAD_DIM), lambda i, ph, g, pt: (i, 0, 0)),
            scratch_shapes=[pltpu.VMEM((4, n_chunks, CMP_STRIDE * LANES), f32), pltpu.VMEM((4, PAGE_SIZE, LANES), f32),
                            pltpu.VMEM((n_chunks, NSA_KV_W), bf16), pltpu.VMEM((n_chunks, NSA_KV_W), bf16),
                            pltpu.VMEM((n_sel_pad, NSA_KV_HEADS), f32), pltpu.VMEM((NSA_HEADS, past_len), f32),
                            pltpu.VMEM((SEL_PAD, NSA_KV_W), bf16), pltpu.VMEM((SEL_PAD, NSA_KV_W), bf16),
                            pltpu.VMEM((NSA_HEADS, NSA_HEAD_DIM), f32),
                            pltpu.VMEM((NSA_HEADS, 1), f32), pltpu.VMEM((NSA_HEADS, 1), f32),
                            pltpu.VMEM((NSA_HEADS, NSA_KV_W), f32)]),
        out_shape=jax.ShapeDtypeStruct((b, NSA_HEADS, NSA_HEAD_DIM), f32),
        compiler_params=pltpu.CompilerParams(
            dimension_semantics=("parallel", "arbitrary", "arbitrary"), vmem_limit_bytes=VMEM_LIMIT_BYTES),
        name="nsa_sample",
    )(page_table, *([kt_pages] * (4 * PAGE_GROUP)), qbd, qbd_t, proj3, win_t,
      *consts)
    return out.reshape(b, NSA_Q_W)


SSD_PROJ_W = _round_up(SSD_IN_W, 7 * LANES)
SSD_TILE = 128
SSD_GN = SSD_GROUPS * SSD_STATE
SSD_GW = SSD_HPG * SSD_HEAD_DIM
CONV_PAD = 8


def _silu(x):
    return x * jax.nn.sigmoid(x)


def _softplus(x):
    return jnp.maximum(x, 0.0) + jnp.log(1.0 + jnp.exp(-jnp.abs(x)))


def _dot3(x, table, dot=jnp.dot):
    return sum(dot(part, table, preferred_element_type=jnp.float32) for part in _split3_bf16(x))


def _dot3_tn(x, table):
    return sum(_dot_tn(part, table) for part in _split3_bf16(x))


def _ssd_prompt_kernel(z_ref, x_ref, bc_ref, dt_ref, cw_ref, cb_ref, dtb_ref, alog_ref, d_ref, nw_ref,
                       eh_ref, tril_ref, triu_ref, eye_ref, y_ref, st_ref, conv_ref, ux_s, ubc_s, st_s):
    c = pl.program_id(1)
    t = SSD_TILE
    di = SSD_D_INNER
    bf16 = jnp.bfloat16

    @pl.when(c == 0)
    def _():
        ux_s[0:CONV_PAD, :] = jnp.zeros((CONV_PAD, di), jnp.float32)
        ubc_s[0:CONV_PAD, :] = jnp.zeros((CONV_PAD, 2 * SSD_GN), jnp.float32)
        st_s[...] = jnp.zeros(st_s.shape, jnp.float32)

    ux_s[CONV_PAD:CONV_PAD + t, :] = x_ref[...]
    ubc_s[CONV_PAD:CONV_PAD + t, :] = bc_ref[...]

    def conv(buf, col0, width):
        y = cb_ref[:, col0:col0 + width]
        for i in range(SSD_CONV_W):
            y = y + buf[pl.ds(CONV_PAD - (SSD_CONV_W - 1 - i), t), :] * cw_ref[i:i + 1, col0:col0 + width]
        return _silu(y)

    xs = conv(ux_s, 0, di)
    bcs = conv(ubc_s, di, 2 * SSD_GN)
    dt = _softplus(dt_ref[:, 0:SSD_HEADS] + dtb_ref[...])
    dta = dt * (-jnp.exp(alog_ref[...]))
    cum = _dot3(dta, tril_ref[...], lambda a, b, **kw: jnp.dot(b, a, **kw))
    cum_t = _dot3_tn(dta, triu_ref[...])
    dt_t = _dot3_tn(dt, eye_ref[...])
    last = cum[t - 1:t, :]
    dec_in = _dot3(jnp.exp(cum), eh_ref[...])
    wgt = _dot3(jnp.exp(last - cum) * dt, eh_ref[...])
    st_scale = _dot3(jnp.broadcast_to(jnp.exp(last), (8, SSD_HEADS)), eh_ref[...])[0:1, :]
    causal = lax.broadcasted_iota(jnp.int32, (t, t), 1) <= lax.broadcasted_iota(jnp.int32, (t, t), 0)

    for g in range(SSD_GROUPS):
        gl = slice(g * SSD_GW, (g + 1) * SSD_GW)
        bg = bcs[:, g * SSD_STATE:(g + 1) * SSD_STATE].astype(bf16)
        cg = bcs[:, SSD_GN + g * SSD_STATE:SSD_GN + (g + 1) * SSD_STATE].astype(bf16)
        cb = _dot_nt(cg, bg)
        xg = xs[:, gl]
        y_heads = []
        for j in range(SSD_HPG):
            h = g * SSD_HPG + j
            decay = jnp.exp(jnp.minimum(cum[:, h:h + 1] - cum_t[h:h + 1, :], 0.0))
            w = jnp.where(causal, cb * decay * dt_t[h:h + 1, :], 0.0).astype(bf16)
            y_heads.append(jnp.dot(w, xg[:, j * SSD_HEAD_DIM:(j + 1) * SSD_HEAD_DIM].astype(bf16),
                                   preferred_element_type=jnp.float32))
        st = st_s[g]
        y = jnp.concatenate(y_heads, axis=1)
        y = y + jnp.dot(cg, st.astype(bf16), preferred_element_type=jnp.float32) * dec_in[:, gl]
        st_s[g] = st_scale[:, gl] * st + _dot_tn(bg, (xg * wgt[:, gl]).astype(bf16))
        y = (y + d_ref[:, gl] * xg) * _silu(z_ref[:, gl])
        y = y * lax.rsqrt(jnp.mean(y * y, axis=-1, keepdims=True) + NORM_EPS) * nw_ref[:, gl]
        y_ref[:, gl] = y.astype(y_ref.dtype)

    ux_s[0:CONV_PAD, :] = ux_s[t:t + CONV_PAD, :]
    ubc_s[0:CONV_PAD, :] = ubc_s[t:t + CONV_PAD, :]

    @pl.when(c == pl.num_programs(1) - 1)
    def _():
        st_ref[...] = st_s[...]
        keep = SSD_CONV_W - 1
        conv_ref[:, 0:di] = ux_s[CONV_PAD - keep:CONV_PAD, :]
        conv_ref[:, di:] = ubc_s[CONV_PAD - keep:CONV_PAD, :]


def _ssd_tables():
    eh = (np.arange(SSD_HEADS)[:, None] == np.arange(SSD_D_INNER)[None, :] // SSD_HEAD_DIM).astype(np.float32)
    tril = np.tril(np.ones((SSD_TILE, SSD_TILE), np.float32))
    bf16 = jnp.bfloat16
    return (jnp.asarray(eh, bf16), jnp.asarray(tril, bf16), jnp.asarray(tril.T, bf16),
            jnp.asarray(np.eye(SSD_TILE, dtype=np.float32), bf16))


def _state_from_transposed(st_t):
    b = st_t.shape[0]
    st = st_t.reshape(b, SSD_GROUPS, SSD_STATE, SSD_HPG, SSD_HEAD_DIM)
    return jnp.transpose(st, (0, 1, 3, 4, 2)).reshape(b, SSD_HEADS, SSD_HEAD_DIM, SSD_STATE)


def _ssd_prompt(proj, conv_w, conv_b, dt_bias, a_log, d_skip, norm_w):
    b, t, _ = proj.shape
    di = SSD_D_INNER
    tt = SSD_TILE
    d_exp = jnp.repeat(d_skip, SSD_HEAD_DIM).reshape(1, di)
    consts = (conv_w, conv_b.reshape(1, -1), dt_bias.reshape(1, -1), a_log.reshape(1, -1), d_exp,
              norm_w.reshape(1, di)) + _ssd_tables()

    def const_spec(a):
        nd = a.ndim
        return pl.BlockSpec(a.shape, lambda i, c: (0,) * nd)

    y, st_t, conv_new = pl.pallas_call(
        _ssd_prompt_kernel,
        grid=(b, t // tt),
        in_specs=[pl.BlockSpec((None, tt, di), lambda i, c: (i, c, 0)),
                  pl.BlockSpec((None, tt, di), lambda i, c: (i, c, 1)),
                  pl.BlockSpec((None, tt, 2 * SSD_GN), lambda i, c: (i, c, 2)),
                  pl.BlockSpec((None, tt, LANES), lambda i, c: (i, c, (di + SSD_CONV_DIM) // LANES))]
                 + [const_spec(a) for a in consts],
        out_specs=[pl.BlockSpec((None, tt, di), lambda i, c: (i, c, 0)),
                   pl.BlockSpec((None, SSD_GROUPS, SSD_STATE, SSD_GW), lambda i, c: (i, 0, 0, 0)),
                   pl.BlockSpec((None, SSD_CONV_W - 1, SSD_CONV_DIM), lambda i, c: (i, 0, 0))],
        out_shape=[jax.ShapeDtypeStruct((b, t, di), jnp.bfloat16),
                   jax.ShapeDtypeStruct((b, SSD_GROUPS, SSD_STATE, SSD_GW), jnp.float32),
                   jax.ShapeDtypeStruct((b, SSD_CONV_W - 1, SSD_CONV_DIM), jnp.float32)],
        scratch_shapes=[pltpu.VMEM((CONV_PAD + tt, di), jnp.float32),
                        pltpu.VMEM((CONV_PAD + tt, 2 * SSD_GN), jnp.float32),
                        pltpu.VMEM((SSD_GROUPS, SSD_STATE, SSD_GW), jnp.float32)],
        compiler_params=pltpu.CompilerParams(
            dimension_semantics=("parallel", "arbitrary"), vmem_limit_bytes=VMEM_LIMIT_BYTES),
        name="ssd_prompt",
    )(proj, proj, proj, proj, *consts)
    return y.reshape(b * t, di), _state_from_transposed(st_t), conv_new


ROW_PAD = 8


def _row8(x):
    return jnp.concatenate([x, jnp.zeros((ROW_PAD - 1, x.shape[1]), x.dtype)], axis=0)


def _ssd_step_kernel(p_ref, conv_ref, st_ref, cw_ref, cb_ref, dtb_ref, alog_ref, d_ref, nw_ref,
                     y_ref, st_out, conv_out, y_s):
    di = SSD_D_INNER
    bf16 = jnp.bfloat16
    u = p_ref[:, di:di + SSD_CONV_DIM]
    keep = SSD_CONV_W - 1
    y = cb_ref[...] + u * cw_ref[keep:keep + 1, :]
    for i in range(keep):
        y = y + conv_ref[i:i + 1, :] * cw_ref[i:i + 1, :]
    conv_out[0:keep - 1, :] = conv_ref[1:keep, :]
    conv_out[keep - 1:keep, :] = u
    xbc = _silu(y)
    xs = xbc[:, :di]
    dt = _softplus(p_ref[:, di + SSD_CONV_DIM:di + SSD_CONV_DIM + SSD_HEADS] + dtb_ref[...])
    decay = jnp.exp(dt * (-jnp.exp(alog_ref[...])))
    for g in range(SSD_GROUPS):
        bg = _row8(xbc[:, di + g * SSD_STATE:di + (g + 1) * SSD_STATE]).astype(bf16)
        cg = _row8(xbc[:, di + SSD_GN + g * SSD_STATE:di + SSD_GN + (g + 1) * SSD_STATE]).astype(bf16)
        for j in range(SSD_HPG):
            h = g * SSD_HPG + j
            cols = slice(h * SSD_HEAD_DIM, (h + 1) * SSD_HEAD_DIM)
            xh = _row8(xs[:, cols] * dt[:, h:h + 1]).astype(bf16)
            st = decay[:, h:h + 1] * st_ref[h] + _dot_tn(xh, bg)
            st_out[h] = st
            y_s[:, cols] = _dot_nt(cg, st.astype(bf16))
    yv = y_s[0:1, :]
    yv = (yv + d_ref[...] * xs) * _silu(p_ref[:, 0:di])
    for g in range(SSD_GROUPS):
        gl = slice(g * SSD_GW, (g + 1) * SSD_GW)
        yg = yv[:, gl]
        y_ref[:, gl] = yg * lax.rsqrt(jnp.mean(yg * yg, axis=-1, keepdims=True) + NORM_EPS) * nw_ref[:, gl]


def _ssd_step(proj, ssm0, conv0, conv_w, conv_b, dt_bias, a_log, d_skip, norm_w):
    b = proj.shape[0]
    di = SSD_D_INNER
    consts = (conv_w, conv_b.reshape(1, -1), dt_bias.reshape(1, -1), a_log.reshape(1, -1),
              jnp.repeat(d_skip, SSD_HEAD_DIM).reshape(1, di), norm_w.reshape(1, di))

    def const_spec(a):
        nd = a.ndim
        return pl.BlockSpec(a.shape, lambda i: (0,) * nd)

    y, st, conv_new = pl.pallas_call(
        _ssd_step_kernel,
        grid=(b,),
        in_specs=[pl.BlockSpec((None, 1, proj.shape[1]), lambda i: (i, 0, 0)),
                  pl.BlockSpec((None,) + conv0.shape[1:], lambda i: (i, 0, 0)),
                  pl.BlockSpec((None,) + ssm0.shape[1:], lambda i: (i, 0, 0, 0))]
                 + [const_spec(a) for a in consts],
        out_specs=[pl.BlockSpec((None, 1, di), lambda i: (i, 0, 0)),
                   pl.BlockSpec((None,) + ssm0.shape[1:], lambda i: (i, 0, 0, 0)),
                   pl.BlockSpec((None,) + conv0.shape[1:], lambda i: (i, 0, 0))],
        out_shape=[jax.ShapeDtypeStruct((b, 1, di), jnp.float32),
                   jax.ShapeDtypeStruct(ssm0.shape, jnp.float32),
                   jax.ShapeDtypeStruct(conv0.shape, jnp.float32)],
        scratch_shapes=[pltpu.VMEM((ROW_PAD, di), jnp.float32)],
        compiler_params=pltpu.CompilerParams(
            dimension_semantics=("parallel",), vmem_limit_bytes=VMEM_LIMIT_BYTES),
        name="ssd_step",
    )(proj.reshape(b, 1, -1), conv0, ssm0, *consts)
    return y.reshape(b, di), st, conv_new


def _hgrn_gates(p_ref, lb_ref):
    wk = HG_HEADS * HG_DK
    q = _silu(p_ref[:, 0:wk])
    f = lb_ref[...] + (1.0 - lb_ref[...]) * jax.nn.sigmoid(p_ref[:, wk:2 * wk])
    return q, f


HG_TILE = 128
HG_SUB = 16


def _hgrn_prompt_kernel(p_ref, lb_ref, gn_ref, tril_ref, subend_ref, ones_ref, y_ref, st_ref, st_s):
    c = pl.program_id(1)
    t = HG_TILE
    wk = HG_HEADS * HG_DK
    wv = HG_HEADS * HG_DV
    bf16 = jnp.bfloat16
    n_sub = t // HG_SUB

    @pl.when(c == 0)
    def _():
        st_s[...] = jnp.zeros(st_s.shape, jnp.float32)

    row = lax.broadcasted_iota(jnp.int32, (t, HG_DK), 0)
    sub_pos = row % HG_SUB
    row_sub = lax.broadcasted_iota(jnp.int32, (t, t), 0) // HG_SUB
    col_sub = lax.broadcasted_iota(jnp.int32, (t, t), 1) // HG_SUB
    left = lambda a, b, **kw: jnp.dot(b, a, **kw)

    def head(h, carry):
        kc = pl.ds(pl.multiple_of(h * HG_DK, HG_DK), HG_DK)
        q = _silu(p_ref[:, kc])
        lb = lb_ref[:, kc]
        f = lb + (1.0 - lb) * jax.nn.sigmoid(p_ref[:, pl.ds(pl.multiple_of(wk + h * HG_DK, HG_DK), HG_DK)])
        k = 1.0 - f
        v = p_ref[:, pl.ds(pl.multiple_of(2 * wk + h * HG_DV, HG_DV), HG_DV)]
        gate = p_ref[:, pl.ds(pl.multiple_of(2 * wk + wv + h * HG_DV, HG_DV), HG_DV)]
        cum = _dot3(jnp.log(f), tril_ref[...], left)
        sub_end = _dot3(cum, subend_ref[...], left)
        k_hat = k * jnp.exp(sub_end - cum)
        a_off = jnp.zeros((t, t), jnp.float32)
        for j in range(n_sub - 1):
            end_j = cum[(j + 1) * HG_SUB - 1:(j + 1) * HG_SUB, :]
            q_j = (q * jnp.exp(jnp.minimum(cum - end_j, 0.0))).astype(bf16)
            k_j = jnp.where(row // HG_SUB == j, k_hat, 0.0).astype(bf16)
            a_off = a_off + _dot_nt(q_j, k_j)
        a_off = jnp.where(col_sub < row_sub, a_off, 0.0)
        v16 = v.astype(bf16)
        o = jnp.dot(a_off.astype(bf16), v16, preferred_element_type=jnp.float32)
        decay = None
        for d in range(HG_SUB):
            k_d, v_d = (k, v) if d == 0 else (pltpu.roll(k, d, 0), pltpu.roll(v, d, 0))
            if d == 1:
                decay = f
            elif d > 1:
                decay = decay * pltpu.roll(f, d - 1, 0)
            e = (q * k_d if d == 0 else q * k_d * decay).astype(bf16)
            a_d = jnp.dot(e, ones_ref[...], preferred_element_type=jnp.float32)
            o = o + jnp.where(sub_pos >= d, a_d, 0.0) * v_d
        st = st_s[h]
        o = o + _dot_nt((q * jnp.exp(cum)).astype(bf16), st.astype(bf16))
        last = cum[t - 1:t, :]
        st_s[h] = st * jnp.exp(last) + _dot_tn(v16, (k * jnp.exp(last - cum)).astype(bf16))
        o = o * lax.rsqrt(jnp.mean(o * o, axis=-1, keepdims=True) + NORM_EPS) * gn_ref[...]
        y_ref[:, pl.ds(pl.multiple_of(h * HG_DV, HG_DV), HG_DV)] = (o * _silu(gate)).astype(y_ref.dtype)
        return carry

    lax.fori_loop(0, HG_HEADS, head, 0)

    @pl.when(c == pl.num_programs(1) - 1)
    def _():
        st_ref[...] = st_s[...]


def _hgrn_prompt(proj, lb, g_norm):
    b, t, w = proj.shape
    tt = HG_TILE
    wv = HG_HEADS * HG_DV
    idx = np.arange(tt)
    tril = np.tril(np.ones((tt, tt), np.float32))
    subend = (idx[None, :] == (idx[:, None] // HG_SUB) * HG_SUB + HG_SUB - 1).astype(np.float32)
    bf16 = jnp.bfloat16
    consts = (lb.reshape(1, -1), g_norm.reshape(1, -1), jnp.asarray(tril, bf16), jnp.asarray(subend, bf16),
              jnp.ones((HG_DK, HG_DK), bf16))

    def const_spec(a):
        nd = a.ndim
        return pl.BlockSpec(a.shape, lambda i, c: (0,) * nd)

    y, st_t = pl.pallas_call(
        _hgrn_prompt_kernel,
        grid=(b, t // tt),
        in_specs=[pl.BlockSpec((None, tt, w), lambda i, c: (i, c, 0))] + [const_spec(a) for a in consts],
        out_specs=[pl.BlockSpec((None, tt, wv), lambda i, c: (i, c, 0)),
                   pl.BlockSpec((None, HG_HEADS, HG_DV, HG_DK), lambda i, c: (i, 0, 0, 0))],
        out_shape=[jax.ShapeDtypeStruct((b, t, wv), bf16),
                   jax.ShapeDtypeStruct((b, HG_HEADS, HG_DV, HG_DK), jnp.float32)],
        scratch_shapes=[pltpu.VMEM((HG_HEADS, HG_DV, HG_DK), jnp.float32)],
        compiler_params=pltpu.CompilerParams(
            dimension_semantics=("parallel", "arbitrary"), vmem_limit_bytes=VMEM_LIMIT_BYTES),
        name="hgrn_prompt",
    )(proj, *consts)
    return y.reshape(b * t, wv), jnp.swapaxes(st_t, 2, 3)


def _hgrn_step_kernel(p_ref, st_ref, lb_ref, gn_ref, y_ref, st_out):
    wk = HG_HEADS * HG_DK
    wv = HG_HEADS * HG_DV
    bf16 = jnp.bfloat16
    q, f = _hgrn_gates(p_ref, lb_ref)
    eye = lax.broadcasted_iota(jnp.int32, (HG_DK, HG_DK), 0) == lax.broadcasted_iota(jnp.int32, (HG_DK, HG_DK), 1)
    for h in range(HG_HEADS):
        kc = slice(h * HG_DK, (h + 1) * HG_DK)
        vc = slice(2 * wk + h * HG_DV, 2 * wk + (h + 1) * HG_DV)
        gc = slice(2 * wk + wv + h * HG_DV, 2 * wk + wv + (h + 1) * HG_DV)
        fh = f[:, kc]
        f_col = jnp.sum(jnp.where(eye, fh, 0.0), axis=1, keepdims=True)
        kv = _dot_tn(_row8(1.0 - fh).astype(bf16), _row8(p_ref[:, vc]).astype(bf16))
        st = f_col * st_ref[h] + kv
        st_out[h] = st
        o = jnp.dot(_row8(q[:, kc]).astype(bf16), st.astype(bf16), preferred_element_type=jnp.float32)[0:1, :]
        o = o * lax.rsqrt(jnp.mean(o * o, axis=-1, keepdims=True) + NORM_EPS) * gn_ref[...]
        y_ref[:, h * HG_DV:(h + 1) * HG_DV] = o * _silu(p_ref[:, gc])


def _hgrn_step(proj, s0, lb, g_norm):
    b = proj.shape[0]
    wv = HG_HEADS * HG_DV
    y, st = pl.pallas_call(
        _hgrn_step_kernel,
        grid=(b,),
        in_specs=[pl.BlockSpec((None, 1, proj.shape[1]), lambda i: (i, 0, 0)),
                  pl.BlockSpec((None,) + s0.shape[1:], lambda i: (i, 0, 0, 0)),
                  pl.BlockSpec((1, HG_HEADS * HG_DK), lambda i: (0, 0)),
                  pl.BlockSpec((1, HG_DV), lambda i: (0, 0))],
        out_specs=[pl.BlockSpec((None, 1, wv), lambda i: (i, 0, 0)),
                   pl.BlockSpec((None,) + s0.shape[1:], lambda i: (i, 0, 0, 0))],
        out_shape=[jax.ShapeDtypeStruct((b, 1, wv), jnp.float32), jax.ShapeDtypeStruct(s0.shape, jnp.float32)],
        compiler_params=pltpu.CompilerParams(
            dimension_semantics=("parallel",), vmem_limit_bytes=VMEM_LIMIT_BYTES),
        name="hgrn_step",
    )(proj.reshape(b, 1, -1), s0, lb.reshape(1, -1), g_norm.reshape(1, -1))
    return y.reshape(b, wv), st


def _rel_bucket(dist):
    n = jnp.maximum(dist, 0)
    n_exact = REL_BUCKETS // 2
    nf = jnp.maximum(n, 1).astype(jnp.float32)
    large = n_exact + (jnp.log(nf / n_exact) / math.log(REL_MAX_DIST / n_exact)
                       * (REL_BUCKETS - n_exact)).astype(jnp.int32)
    return jnp.where(n < n_exact, n, jnp.minimum(large, REL_BUCKETS - 1))


def _nsa_prompt_core(proj, cmp_pos, cmp_w1, cmp_w2, tables):
    b, t, _ = proj.shape
    cmp = _nsa_compress_prompt(proj, cmp_pos, cmp_w1, cmp_w2)
    merged = _nsa_attn_prompt_t(proj, cmp, tables)
    o1 = NSA_Q_W
    o2 = o1 + 4 * NSA_KV_W
    o3 = o2 + 2 * NSA_KV_W
    kv_cs = proj[..., o1:o2].reshape(b, t, 4, NSA_KV_HEADS, NSA_HEAD_DIM)
    kv_win = proj[:, t - min(WINDOW, t):, o2:o3].reshape(b, min(WINDOW, t), 2, NSA_KV_HEADS, NSA_HEAD_DIM)
    return merged, kv_cs, kv_win


def _nsa_sample_core(proj, caches, layer, win_buf, page_table, cmp_pos, cmp_w1, cmp_w2, tables):
    b, t, _ = proj.shape
    assert t == 1 and win_buf.shape[1] == WINDOW and page_table.shape[1] % PAGE_GROUP == 0
    kt_pages, win_t, n_phys = caches
    merged = _nsa_attn_sample(proj.reshape(b, -1), kt_pages, win_t, layer * b,
                              page_table + layer * n_phys, cmp_pos, cmp_w1, cmp_w2, tables)
    o1 = NSA_Q_W
    o2 = o1 + 4 * NSA_KV_W
    o3 = o2 + 2 * NSA_KV_W
    kv_cs = proj[..., o1:o2].reshape(b, t, 4, NSA_KV_HEADS, NSA_HEAD_DIM)
    kv_win = proj[..., o2:o3].reshape(b, t, 2, NSA_KV_HEADS, NSA_HEAD_DIM)
    new_win = jnp.concatenate([win_buf[:, t:], kv_win], axis=1)
    return merged, kv_cs, new_win


def _pad_cols(w, n):
    return jnp.pad(w, ((0, 0), (0, n - w.shape[1])))


def kernel(x_prompt, x_sample, cache_nsa_kv, cache_nsa_win, state_hgrn, state_ssd, state_ssd_conv, page_table, c_prompt, c_sample, rel_bias, hgrn_lower_bounds, w_ada, b_ada, norm_gains, w_mlp_in, w_mlp_out, nsa_w_in, nsa_cmp_pos, nsa_cmp_w1, nsa_cmp_w2, nsa_w_out, hg_w_in, hg_norm, hg_w_out, ssd_w_in, ssd_conv_w, ssd_conv_b, ssd_dt_bias, ssd_a_log, ssd_d, ssd_norm, ssd_w_out):
    bf16 = jnp.bfloat16
    bp, tp, d = x_prompt.shape
    bs, ts, _ = x_sample.shape
    mp, ms = bp * tp, bs * ts
    lb_p = jax.nn.softmax(hgrn_lower_bounds, axis=0)
    lower_bounds = jnp.cumsum(lb_p, axis=0) - lb_p[0]

    mod = _ada_all(jnp.concatenate([c_prompt, c_sample], axis=0), w_ada, b_ada)
    mod = mod.reshape(DEPTH, bp + bs, ADA_CHUNKS, d)
    mod_p = mod[:, :bp].transpose(0, 2, 1, 3)[:, :, :, None, :]
    mod_s = mod[:, bp:].transpose(0, 2, 1, 3)[:, :, None, :, :]

    xp = x_prompt.reshape(mp, d)
    xs = x_sample.reshape(ms, d)
    tm_p, tm_s = PROMPT_ROW_TILE, ms
    nsa_tables = _nsa_prompt_tables_t(rel_bias, tp)
    nsa_tables_s = _nsa_sample_tables(rel_bias, page_table.shape[1] * PAGE_SIZE, cache_nsa_win.shape[2])
    n_phys = cache_nsa_kv.shape[1]
    n_all = cache_nsa_kv.shape[0] * n_phys
    kt_pages = jnp.transpose(cache_nsa_kv, (0, 1, 3, 4, 5, 2)).reshape(n_all, 4 * NSA_KV_HEADS, NSA_HEAD_DIM, PAGE_SIZE)
    win_t = jnp.transpose(cache_nsa_win, (0, 1, 3, 4, 5, 2)).reshape(-1, 2, NSA_KV_W, cache_nsa_win.shape[2])
    nsa_caches = (kt_pages, win_t, n_phys)

    kv_p, kv_s, win_p, win_s = [], [], [], []
    hg_p, hg_s, ssd_p, ssd_s, conv_p, conv_s = [], [], [], [], [], []
    for i in range(DEPTH):
        j = i // N_MIXERS
        kind = i % N_MIXERS
        g = norm_gains[i]
        shp_m, scp_m, gtp_m, shp_f, scp_f, gtp_f = [mod_p[i, c] for c in range(ADA_CHUNKS)]
        shs_m, scs_m, gts_m, shs_f, scs_f, gts_f = [mod_s[i, c] for c in range(ADA_CHUNKS)]
        if kind == 0:
            n_pad = NSA_PROJ_W
            w_in = _pad_cols(nsa_w_in[j], n_pad).astype(bf16)
            w_out = nsa_w_out[j].astype(bf16)
            pp = _norm_mod_matmul(xp, g[0], scp_m, shp_m, w_in, tp, tm_p).reshape(bp, tp, n_pad)
            ps = _norm_mod_matmul(xs, g[0], scs_m, shs_m, w_in, ts, tm_s).reshape(bs, ts, n_pad)
            ap, new_kv_p, new_win_p = _nsa_prompt_core(pp, nsa_cmp_pos[j], nsa_cmp_w1[j], nsa_cmp_w2[j], nsa_tables)
            as_, new_kv_s, new_win_s = _nsa_sample_core(ps, nsa_caches, j, cache_nsa_win[j], page_table,
                                                        nsa_cmp_pos[j], nsa_cmp_w1[j], nsa_cmp_w2[j], nsa_tables_s)
            kv_p.append(new_kv_p)
            kv_s.append(new_kv_s)
            win_p.append(new_win_p)
            win_s.append(new_win_s)
        elif kind == 1:
            w_in = hg_w_in[j].astype(bf16)
            w_out = hg_w_out[j].astype(bf16)
            pp = _norm_mod_matmul(xp, g[0], scp_m, shp_m, w_in, tp, tm_p).reshape(bp, tp, -1)
            ps = _norm_mod_matmul(xs, g[0], scs_m, shs_m, w_in, ts, tm_s).reshape(bs, ts, -1)
            ap, new_hp = _hgrn_prompt(pp, lower_bounds[i], hg_norm[j])
            as_, new_hs = _hgrn_step(ps.reshape(bs, -1), state_hgrn[j], lower_bounds[i], hg_norm[j])
            hg_p.append(new_hp)
            hg_s.append(new_hs)
        else:
            n_pad = SSD_PROJ_W
            w_in = _pad_cols(ssd_w_in[j], n_pad).astype(bf16)
            w_out = ssd_w_out[j].astype(bf16)
            pp = _norm_mod_matmul(xp, g[0], scp_m, shp_m, w_in, tp, tm_p).reshape(bp, tp, n_pad)
            ps = _norm_mod_matmul(xs, g[0], scs_m, shs_m, w_in, ts, tm_s).reshape(bs, ts, n_pad)
            ap, new_sp, new_cp = _ssd_prompt(pp, ssd_conv_w[j], ssd_conv_b[j], ssd_dt_bias[j],
                                             ssd_a_log[j], ssd_d[j], ssd_norm[j])
            as_, new_ss, new_cs = _ssd_step(ps.reshape(bs, -1), state_ssd[j], state_ssd_conv[j], ssd_conv_w[j],
                                            ssd_conv_b[j], ssd_dt_bias[j], ssd_a_log[j], ssd_d[j], ssd_norm[j])
            ssd_p.append(new_sp)
            ssd_s.append(new_ss)
            conv_p.append(new_cp)
            conv_s.append(new_cs)
        xp = _matmul_norm_res(ap, w_out, xp, g[1], gtp_m, tp, tm_p)
        xs = _matmul_norm_res(as_, w_out, xs, g[1], gts_m, ts, tm_s)
        w1 = w_mlp_in[i].astype(bf16)
        w2 = w_mlp_out[i].astype(bf16)
        xp = _mlp(xp, g[2], scp_f, shp_f, w1, w2, g[3], gtp_f, tp, tm_p)
        xs = _mlp(xs, g[2], scs_f, shs_f, w1, w2, g[3], gts_f, ts, tm_s)
    return (xp.reshape(bp, tp, d), xs.reshape(bs, ts, d),
            jnp.stack(kv_p), jnp.stack(kv_s), jnp.stack(win_p), jnp.stack(win_s),
            jnp.stack(hg_p), jnp.stack(hg_s), jnp.stack(ssd_p), jnp.stack(ssd_s),
            jnp.stack(conv_p), jnp.stack(conv_s))
```

```python
import functools
import math

import jax
import jax.numpy as jnp
import numpy as np
from jax import lax
from jax.experimental import pallas as pl
from jax.experimental.pallas import tpu as pltpu

D_MODEL = 1024
DEPTH = 4
PAGE_SIZE = 128
N_MIXERS = 3
ADA_CHUNKS = 6
NORM_EPS = 1e-6
D_FF = 4 * D_MODEL

NSA_HEADS = 16
NSA_HEAD_DIM = D_MODEL // NSA_HEADS
NSA_KV_HEADS = 4
NSA_GROUP = NSA_HEADS // NSA_KV_HEADS
CMP_STRIDE = 16
CMP_LEN = 2 * CMP_STRIDE
CMP_HIDDEN = 2 * NSA_HEAD_DIM
SEL_BLOCK = 64
SEL_TOP_N = 16
WINDOW = 512
NSA_Q_W = NSA_HEADS * NSA_HEAD_DIM
NSA_KV_W = NSA_KV_HEADS * NSA_HEAD_DIM
NSA_IN_W = NSA_Q_W + 6 * NSA_KV_W + 3 * NSA_HEADS

REL_BUCKETS = 32
REL_MAX_DIST = 128

HG_EXPAND = 128
HG_HEADS = D_MODEL // HG_EXPAND
HG_DK = HG_EXPAND
HG_DV = D_MODEL // HG_HEADS

SSD_D_INNER = 2 * D_MODEL
SSD_HEAD_DIM = 64
SSD_HEADS = SSD_D_INNER // SSD_HEAD_DIM
SSD_GROUPS = 8
SSD_HPG = SSD_HEADS // SSD_GROUPS
SSD_STATE = 128
SSD_CONV_W = 4
SSD_CONV_DIM = SSD_D_INNER + 2 * SSD_GROUPS * SSD_STATE
SSD_IN_W = SSD_D_INNER + SSD_CONV_DIM + SSD_HEADS

NEG_INF = -1e30
FORCE_SCORE = 1e4

LANES = 128
VMEM_LIMIT_BYTES = 48 * 1024 * 1024
PROMPT_ROW_TILE = 1024


def _round_up(n, m):
    return -(-n // m) * m


def _col_tile(n, cap=1536):
    best = LANES
    for t in range(LANES, cap + 1, LANES):
        if n % t == 0:
            best = t
    return best


def _rms(x, g):
    return x * lax.rsqrt(jnp.mean(x * x, axis=-1, keepdims=True) + NORM_EPS) * g


def _mod_spec(mod, rows_per_mod, tm, ngrid):
    r = mod.shape[1]
    if r == 1:
        per = rows_per_mod // tm
        if ngrid == 1:
            return pl.BlockSpec((None, 1, mod.shape[2]), lambda i: (i // per, 0, 0))
        return pl.BlockSpec((None, 1, mod.shape[2]), lambda i, j: (i // per, 0, 0))
    if ngrid == 1:
        return pl.BlockSpec((None, r, mod.shape[2]), lambda i: (0, 0, 0))
    return pl.BlockSpec((None, r, mod.shape[2]), lambda i, j: (0, 0, 0))


def _ada_kernel(c_ref, w_ref, b_ref, o_ref):
    c = c_ref[...]
    s = (c * jax.nn.sigmoid(c)).astype(jnp.bfloat16)
    o_ref[...] = jnp.dot(s, w_ref[...].astype(jnp.bfloat16),
                         preferred_element_type=jnp.float32) + b_ref[...]


def _ada_all(c_all, w_ada, b_ada):
    rows = c_all.shape[0]
    n = ADA_CHUNKS * D_MODEL
    tn = 1024
    return pl.pallas_call(
        _ada_kernel,
        grid=(DEPTH, n // tn),
        in_specs=[pl.BlockSpec((rows, D_MODEL), lambda l, j: (0, 0)),
                  pl.BlockSpec((None, D_MODEL, tn), lambda l, j: (l, 0, j)),
                  pl.BlockSpec((None, 1, tn), lambda l, j: (l, 0, j))],
        out_specs=pl.BlockSpec((None, rows, tn), lambda l, j: (l, 0, j)),
        out_shape=jax.ShapeDtypeStruct((DEPTH, rows, n), jnp.float32),
        compiler_params=pltpu.CompilerParams(
            dimension_semantics=("parallel", "parallel"), vmem_limit_bytes=VMEM_LIMIT_BYTES),
        name="ada",
    )(c_all, w_ada, b_ada.reshape(DEPTH, 1, n))


def _norm_mod_matmul_kernel(x_ref, g_ref, sc_ref, sh_ref, w_ref, o_ref, h_ref):
    @pl.when(pl.program_id(1) == 0)
    def _():
        h = _rms(x_ref[...], g_ref[...]) * (1.0 + sc_ref[...]) + sh_ref[...]
        h_ref[...] = h.astype(jnp.bfloat16)

    o_ref[...] = jnp.dot(h_ref[...], w_ref[...], preferred_element_type=jnp.float32)


def _norm_mod_matmul(x, g, sc, sh, w, rows_per_mod, tm):
    m, d = x.shape
    n = w.shape[1]
    tn = _col_tile(n)
    return pl.pallas_call(
        _norm_mod_matmul_kernel,
        grid=(m // tm, n // tn),
        in_specs=[pl.BlockSpec((tm, d), lambda i, j: (i, 0)),
                  pl.BlockSpec((1, d), lambda i, j: (0, 0)),
                  _mod_spec(sc, rows_per_mod, tm, 2),
                  _mod_spec(sh, rows_per_mod, tm, 2),
                  pl.BlockSpec((d, tn), lambda i, j: (0, j))],
        out_specs=pl.BlockSpec((tm, tn), lambda i, j: (i, j)),
        out_shape=jax.ShapeDtypeStruct((m, n), jnp.float32),
        scratch_shapes=[pltpu.VMEM((tm, d), jnp.bfloat16)],
        compiler_params=pltpu.CompilerParams(
            dimension_semantics=("parallel", "arbitrary"), vmem_limit_bytes=VMEM_LIMIT_BYTES),
        name="norm_mod_matmul",
    )(x, g.reshape(1, d), sc, sh, w)


def _matmul_norm_res_kernel(a_ref, w_ref, x_ref, g_ref, gt_ref, o_ref):
    y = jnp.dot(a_ref[...].astype(jnp.bfloat16), w_ref[...], preferred_element_type=jnp.float32)
    o_ref[...] = x_ref[...] + gt_ref[...] * _rms(y, g_ref[...])


def _matmul_norm_res(a, w, x, g, gate, rows_per_mod, tm):
    m, k = a.shape
    d = w.shape[1]
    return pl.pallas_call(
        _matmul_norm_res_kernel,
        grid=(m // tm,),
        in_specs=[pl.BlockSpec((tm, k), lambda i: (i, 0)),
                  pl.BlockSpec((k, d), lambda i: (0, 0)),
                  pl.BlockSpec((tm, d), lambda i: (i, 0)),
                  pl.BlockSpec((1, d), lambda i: (0, 0)),
                  _mod_spec(gate, rows_per_mod, tm, 1)],
        out_specs=pl.BlockSpec((tm, d), lambda i: (i, 0)),
        out_shape=jax.ShapeDtypeStruct((m, d), jnp.float32),
        compiler_params=pltpu.CompilerParams(
            dimension_semantics=("parallel",), vmem_limit_bytes=VMEM_LIMIT_BYTES),
        name="matmul_norm_res",
    )(a, w, x, g.reshape(1, d), gate)


def _mlp_kernel(x_ref, g2_ref, sc_ref, sh_ref, w1_ref, w2_ref, g3_ref, gt_ref, o_ref, h_ref, acc_ref):
    j = pl.program_id(1)

    @pl.when(j == 0)
    def _():
        h = _rms(x_ref[...], g2_ref[...]) * (1.0 + sc_ref[...]) + sh_ref[...]
        h_ref[...] = h.astype(jnp.bfloat16)

    u = jnp.dot(h_ref[...], w1_ref[...], preferred_element_type=jnp.float32)
    u = jnp.square(jnp.maximum(u, 0.0)).astype(jnp.bfloat16)
    part = jnp.dot(u, w2_ref[...], preferred_element_type=jnp.float32)

    @pl.when(j == 0)
    def _():
        acc_ref[...] = part

    @pl.when(j > 0)
    def _():
        acc_ref[...] += part

    @pl.when(j == pl.num_programs(1) - 1)
    def _():
        o_ref[...] = x_ref[...] + gt_ref[...] * _rms(acc_ref[...], g3_ref[...])


def _mlp(x, g2, sc, sh, w1, w2, g3, gate, rows_per_mod, tm):
    m, d = x.shape
    f = w1.shape[1]
    tf = 1024
    return pl.pallas_call(
        _mlp_kernel,
        grid=(m // tm, f // tf),
        in_specs=[pl.BlockSpec((tm, d), lambda i, j: (i, 0)),
                  pl.BlockSpec((1, d), lambda i, j: (0, 0)),
                  _mod_spec(sc, rows_per_mod, tm, 2),
                  _mod_spec(sh, rows_per_mod, tm, 2),
                  pl.BlockSpec((d, tf), lambda i, j: (0, j)),
                  pl.BlockSpec((tf, d), lambda i, j: (j, 0)),
                  pl.BlockSpec((1, d), lambda i, j: (0, 0)),
                  _mod_spec(gate, rows_per_mod, tm, 2)],
        out_specs=pl.BlockSpec((tm, d), lambda i, j: (i, 0)),
        out_shape=jax.ShapeDtypeStruct((m, d), jnp.float32),
        scratch_shapes=[pltpu.VMEM((tm, d), jnp.bfloat16), pltpu.VMEM((tm, d), jnp.float32)],
        compiler_params=pltpu.CompilerParams(
            dimension_semantics=("parallel", "arbitrary"), vmem_limit_bytes=VMEM_LIMIT_BYTES),
        name="mlp",
    )(x, g2.reshape(1, d), sc, sh, w1, w2, g3.reshape(1, d), gate)


NSA_COL_BLOCK = NSA_KV_W
NSA_PROJ_W = 11 * NSA_COL_BLOCK
NSA_GATE_BLOCK = (NSA_Q_W + 6 * NSA_KV_W) // NSA_COL_BLOCK
ATT_TILE = 128
ROWS = NSA_GROUP * ATT_TILE
ATT_TILE_GROUP = 4


def _dot_nt(a, b):
    return lax.dot_general(a, b, (((1,), (1,)), ((), ())), preferred_element_type=jnp.float32)


def _dot_tn(a, b):
    return lax.dot_general(a, b, (((0,), (0,)), ((), ())), preferred_element_type=jnp.float32)


def _gelu_tanh(x):
    return 0.5 * x * (1.0 + jnp.tanh(math.sqrt(2.0 / math.pi) * (x + 0.044715 * (x * x * x))))


def _split3_bf16(x):
    hi = x.astype(jnp.bfloat16)
    r1 = x - hi.astype(jnp.float32)
    mid = r1.astype(jnp.bfloat16)
    lo = (r1 - mid.astype(jnp.float32)).astype(jnp.bfloat16)
    return hi, mid, lo


def _nsa_compress_kernel(x0_ref, x1_ref, x2_ref, x3_ref, w1_ref, w1f_ref, pos_ref, w2_ref, o_ref):
    n = x0_ref.shape[0] // CMP_STRIDE
    hd = NSA_HEAD_DIM
    x_refs = ((x0_ref, x1_ref), (x2_ref, x3_ref))
    for r in range(2):
        pos_b = jnp.dot(pos_ref[r].astype(jnp.bfloat16), w1f_ref[r], preferred_element_type=jnp.float32)
        acc = [jnp.zeros((n, 2 * CMP_HIDDEN), jnp.float32) for _ in range(NSA_KV_HEADS)]
        for l in range(CMP_STRIDE):
            w = w1_ref[r, l]
            for pair in range(2):
                xl = x_refs[r][pair][pl.ds(l, n, stride=CMP_STRIDE), :].astype(jnp.bfloat16)
                for half in range(2):
                    k = 2 * pair + half
                    acc[k] = acc[k] + jnp.dot(xl[:, half * hd:(half + 1) * hd], w,
                                              preferred_element_type=jnp.float32)
        for k in range(NSA_KV_HEADS):
            pa = acc[k][:, :CMP_HIDDEN]
            pb_next = pltpu.roll(acc[k][:, CMP_HIDDEN:], n - 1, 0)
            hid = _gelu_tanh(pa + pb_next + pos_b)
            out = jnp.dot(hid.astype(jnp.bfloat16), w2_ref[r], preferred_element_type=jnp.float32)
            o_ref[r, :, k * hd:(k + 1) * hd] = out.astype(o_ref.dtype)


def _compress_weights(cmp_pos, w1, w2):
    bf16 = jnp.bfloat16
    w1r = w1.reshape(2, CMP_LEN, NSA_HEAD_DIM, CMP_HIDDEN)
    w1cat = jnp.concatenate([w1r[:, :CMP_STRIDE], w1r[:, CMP_STRIDE:]], axis=-1).astype(bf16)
    return w1cat, w1.astype(bf16), cmp_pos.reshape(2, 1, CMP_LEN * NSA_HEAD_DIM), w2.astype(bf16)


def _nsa_compress_prompt(proj, cmp_pos, w1, w2):
    b, t, _ = proj.shape
    n = t // CMP_STRIDE
    w1cat, w1f, pos, w2b = _compress_weights(cmp_pos, w1, w2)
    return pl.pallas_call(
        _nsa_compress_kernel,
        grid=(b,),
        in_specs=[pl.BlockSpec((None, t, LANES), lambda i, c=c: (i, 0, NSA_Q_W // LANES + c)) for c in range(4)]
                 + [pl.BlockSpec(w1cat.shape, lambda i: (0, 0, 0, 0)),
                  pl.BlockSpec(w1f.shape, lambda i: (0, 0, 0)),
                  pl.BlockSpec(pos.shape, lambda i: (0, 0, 0)),
                  pl.BlockSpec(w2b.shape, lambda i: (0, 0, 0))],
        out_specs=pl.BlockSpec((None, 2, n, NSA_KV_W), lambda i: (i, 0, 0, 0)),
        out_shape=jax.ShapeDtypeStruct((b, 2, n, NSA_KV_W), jnp.bfloat16),
        compiler_params=pltpu.CompilerParams(
            dimension_semantics=("parallel",), vmem_limit_bytes=VMEM_LIMIT_BYTES),
        name="nsa_compress",
    )(proj, proj, proj, proj, w1cat, w1f, pos, w2b)


def _bias_lookup(rel_bias, dist):
    onehot = jax.nn.one_hot(_rel_bucket(dist), REL_BUCKETS, dtype=jnp.float32)
    return jnp.einsum('...c,ch->...h', onehot, rel_bias, precision=lax.Precision.HIGHEST)


DEN_ROWS = 8


def _with_ones(v):
    return jnp.concatenate([v, jnp.ones((v.shape[0], DEN_ROWS), v.dtype)], axis=1)


def _key_softmax_step(s, v, m, acc):
    m_new = jnp.maximum(m, jnp.max(s, axis=0, keepdims=True))
    e = jnp.exp(s - m_new).astype(jnp.bfloat16)
    acc = jnp.exp(m - m_new) * acc + _dot_tn(_with_ones(v), e)
    return m_new, acc


def _softmax_out(acc):
    hd = acc.shape[0] - DEN_ROWS
    return acc[:hd] / acc[hd:hd + 1]


def _nsa_attn_t_kernel(q_ref, g_ref, c_ref, ks_ref, vs_ref, kw_ref, vw_ref, bc_ref, bt_ref,
                       mimp_ref, eg_ref, o_ref, oc_s, os_s, ow_s, sel_s, *, tile0, n_far):
    i = tile0 + pl.program_id(1)
    hd = NSA_HEAD_DIM
    tq = ATT_TILE
    bf16 = jnp.bfloat16
    n_cmp_pad = c_ref.shape[1]
    n_sel = mimp_ref.shape[0]
    kj = lax.broadcasted_iota(jnp.int32, (tq, ROWS), 0)
    qi = lax.broadcasted_iota(jnp.int32, (tq, ROWS), 1) % tq
    causal = kj <= qi
    win_edge = kj >= qi
    cmp_end = CMP_STRIDE * lax.broadcasted_iota(jnp.int32, (n_cmp_pad, ROWS), 0) + (CMP_LEN - 1)
    mask_c = cmp_end <= i * tq + lax.broadcasted_iota(jnp.int32, (n_cmp_pad, ROWS), 1) % tq
    blk = lax.broadcasted_iota(jnp.int32, (n_sel, tq), 0)
    cur = (i * tq + lax.broadcasted_iota(jnp.int32, (n_sel, tq), 1)) // SEL_BLOCK
    forced = (blk == 0) | (blk == cur) | (blk == cur - 1)
    valid = blk <= cur

    heads = range(NSA_KV_HEADS)
    lanes = [slice(k * hd, (k + 1) * hd) for k in heads]
    per_chunk = tq // SEL_BLOCK
    qk, o_cmp = [], []
    for k in heads:
        q = jnp.concatenate(
            [q_ref[:, (k * NSA_GROUP + g) * hd:(k * NSA_GROUP + g + 1) * hd] for g in range(NSA_GROUP)], axis=0)
        qk.append((q * (hd ** -0.5)).astype(bf16))

        s = jnp.where(mask_c, _dot_nt(c_ref[0, :, lanes[k]], qk[k]) + bc_ref[k], NEG_INF)
        m = jnp.max(s, axis=0, keepdims=True)
        e = jnp.where(mask_c, jnp.exp(s - m), 0.0)
        p = e * (1.0 / jnp.maximum(jnp.sum(e, axis=0, keepdims=True), 1e-30))
        o_cmp.append(_dot_tn(c_ref[1, :, lanes[k]], p.astype(bf16)))
        p_sum = sum(p[:, g * tq:(g + 1) * tq] for g in range(NSA_GROUP))
        imp = _dot3(p_sum, mimp_ref[...], lambda a, b, **kw: jnp.dot(b, a, **kw))
        score = jnp.where(valid, jnp.where(forced, FORCE_SCORE, imp), NEG_INF)
        rank = jnp.zeros((n_sel, tq), jnp.float32)
        for j in range(n_sel):
            row = score[j:j + 1, :]
            beats = (row > score) | ((row == score) & (blk > j))
            rank = rank + jnp.where(beats, 1.0, 0.0)
        sel = jnp.where((rank < SEL_TOP_N) & (score > 0.5 * NEG_INF), 1.0, 0.0)
        sel = jnp.concatenate([sel] * NSA_GROUP, axis=1)
        for c in range(n_sel // per_chunk):
            sel_s[k, c, 0:per_chunk, :] = sel[c * per_chunk:(c + 1) * per_chunk, :]

    def rows_of(c, n=1):
        return pl.ds(c * tq, n * tq) if isinstance(c, int) else pl.ds(pl.multiple_of(c * tq, tq), n * tq)

    def chunk(k_ref, v_ref, k, rows, carry, bias, mk):
        s = _dot_nt(k_ref[rows, lanes[k]].astype(bf16), qk[k])
        if bias is not None:
            s = s + bias
        return _key_softmax_step(jnp.where(mk, s, NEG_INF), v_ref[rows, lanes[k]].astype(bf16), *carry)

    def sel_mask(k, c, ok):
        pair = sel_s[k, c, 0:per_chunk, :]
        picked = jnp.concatenate([jnp.broadcast_to(pair[j:j + 1, :], (SEL_BLOCK, ROWS)) for j in range(per_chunk)],
                                 axis=0)
        return (picked > 0.5) & jnp.broadcast_to(ok, (tq, ROWS))

    init = (jnp.full((1, ROWS), NEG_INF, jnp.float32), jnp.zeros((hd + DEN_ROWS, ROWS), jnp.float32))
    carry = [init] * NSA_KV_HEADS
    c_prev = jnp.maximum(i - 1, 0)
    for k in heads:
        bias = jnp.concatenate([bt_ref[k, 1], bt_ref[k, 0]], axis=0)
        mk = jnp.concatenate([sel_mask(k, c_prev, i >= 1), sel_mask(k, i, True) & causal], axis=0)
        carry[k] = chunk(ks_ref, vs_ref, k, rows_of(i, 2), carry[k], bias, mk)
    for c in range(0, n_far, 2):
        for k in heads:
            mk = jnp.concatenate([sel_mask(k, c, c < i - 1), sel_mask(k, c + 1, c + 1 < i - 1)], axis=0)
            carry[k] = chunk(ks_ref, vs_ref, k, rows_of(c + 1, 2), carry[k], None, mk)
    o_sel = [_softmax_out(carry[k][1]) for k in heads]

    carry = [init] * NSA_KV_HEADS
    n_back = WINDOW // tq

    def tile_ok(c):
        return jnp.broadcast_to(c >= 0, (tq, ROWS))

    for k in heads:
        carry[k] = chunk(kw_ref, vw_ref, k, rows_of(i + n_back), carry[k], bt_ref[k, 0], causal)
    mk = jnp.concatenate([tile_ok(i - 2), tile_ok(i - 1)], axis=0)
    for k in heads:
        bias = jnp.concatenate([jnp.zeros((tq, ROWS), jnp.float32), bt_ref[k, 1]], axis=0)
        carry[k] = chunk(kw_ref, vw_ref, k, rows_of(i + n_back - 2, 2), carry[k], bias, mk)
    mk = jnp.concatenate([tile_ok(i - 4) & win_edge, tile_ok(i - 3)], axis=0)
    for k in heads:
        carry[k] = chunk(kw_ref, vw_ref, k, rows_of(i + n_back - 4, 2), carry[k], None, mk)
    o_win = [_softmax_out(carry[k][1]) for k in heads]

    for k in heads:
        for g in range(NSA_GROUP):
            rows = slice((k * NSA_GROUP + g) * hd, (k * NSA_GROUP + g + 1) * hd)
            oc_s[rows, :] = o_cmp[k][:, g * tq:(g + 1) * tq]
            os_s[rows, :] = o_sel[k][:, g * tq:(g + 1) * tq]
            ow_s[rows, :] = o_win[k][:, g * tq:(g + 1) * tq]

    gate = jax.nn.sigmoid(g_ref[...])
    g_hi = gate.astype(bf16)
    g_lo = (gate - g_hi.astype(jnp.float32)).astype(bf16)
    out = jnp.zeros((NSA_Q_W, tq), jnp.float32)
    for br, o_s in enumerate((oc_s, os_s, ow_s)):
        out = out + (_dot_nt(eg_ref[br], g_hi) + _dot_nt(eg_ref[br], g_lo)) * o_s[...]
    for r in range(NSA_Q_W // tq):
        o_ref[:, r * tq:(r + 1) * tq] = out[r * tq:(r + 1) * tq, :].T.astype(o_ref.dtype)


def _keys_by_kv_head(tab):
    *lead, q, t, _ = tab.shape
    tab = tab.reshape(*lead, q, t, NSA_KV_HEADS, NSA_GROUP)
    nl = len(lead)
    tab = jnp.transpose(tab, (*range(nl), nl + 2, nl + 1, nl + 3, nl))
    return tab.reshape(*lead, NSA_KV_HEADS, t, NSA_GROUP * q)


def _nsa_prompt_tables_t(rel_bias, t):
    tq = ATT_TILE
    n_chunks = t // CMP_STRIDE
    n_sel = t // SEL_BLOCK
    far = rel_bias[REL_BUCKETS - 1]
    ar = jnp.arange(tq)
    d_tile = (jnp.arange(2) * tq)[:, None, None] + ar[None, :, None] - ar[None, None, :]
    bt = _keys_by_kv_head(_bias_lookup(rel_bias, d_tile) - far)
    bt = jnp.transpose(bt, (1, 0, 2, 3))
    q_pos = jnp.arange(t).reshape(t // tq, tq)
    cmp_end = jnp.arange(n_chunks) * CMP_STRIDE + CMP_LEN - 1
    bc = _keys_by_kv_head(_bias_lookup(rel_bias, q_pos[:, :, None] - cmp_end[None, None, :]) - far)
    n_idx = np.arange(n_chunks)
    j_idx = np.arange(n_sel)[:, None]
    per = SEL_BLOCK // CMP_STRIDE
    mimp = 0.5 * ((n_idx // per == j_idx).astype(np.float32) + ((n_idx + 1) // per == j_idx).astype(np.float32))
    mimp[:, n_chunks - 1] = 0.0
    col = np.arange(NSA_Q_W) // NSA_HEAD_DIM
    eg = np.zeros((3, NSA_Q_W, NSA_COL_BLOCK), np.float32)
    for br in range(3):
        eg[br, np.arange(NSA_Q_W), br * NSA_HEADS + col] = 1.0
    return bt, bc, jnp.asarray(mimp, jnp.bfloat16), jnp.asarray(eg, jnp.bfloat16)


def _nsa_attn_prompt_t(proj, cmp, tables):
    b, t, _ = proj.shape
    bt, bc, mimp, eg = tables
    tq = ATT_TILE
    cb = NSA_COL_BLOCK

    def const_spec(a):
        nd = a.ndim
        return pl.BlockSpec(a.shape, lambda bi, i: (0,) * nd)

    assert WINDOW == 4 * tq

    def front_padded(col0, rows):
        a = jnp.pad(proj[:, :, col0:col0 + cb].astype(jnp.bfloat16), ((0, 0), (rows, 0), (0, 0)))
        return a, pl.BlockSpec((None, t + rows, cb), lambda bi, i: (bi, 0, 0))

    sel0 = NSA_Q_W + 2 * NSA_KV_W
    (ks, sel_spec), (vs, _) = front_padded(sel0, tq), front_padded(sel0 + cb, tq)
    (kw, win_spec), (vw, _) = front_padded(sel0 + 2 * cb, WINDOW), front_padded(sel0 + 3 * cb, WINDOW)

    def tile_group(tile0):
        n_far = max(tile0 + ATT_TILE_GROUP - 2, 0)
        return pl.pallas_call(
            functools.partial(_nsa_attn_t_kernel, tile0=tile0, n_far=n_far),
            grid=(b, ATT_TILE_GROUP),
            in_specs=[pl.BlockSpec((None, tq, NSA_Q_W), lambda bi, i: (bi, tile0 + i, 0)),
                      pl.BlockSpec((None, tq, cb), lambda bi, i: (bi, tile0 + i, NSA_GATE_BLOCK)),
                      pl.BlockSpec((None,) + cmp.shape[1:], lambda bi, i: (bi, 0, 0, 0)),
                      sel_spec, sel_spec, win_spec, win_spec,
                      pl.BlockSpec((None,) + bc.shape[1:], lambda bi, i: (tile0 + i, 0, 0, 0)),
                      const_spec(bt), const_spec(mimp), const_spec(eg)],
            out_specs=pl.BlockSpec((None, tq, NSA_Q_W), lambda bi, i: (bi, i, 0)),
            out_shape=jax.ShapeDtypeStruct((b, ATT_TILE_GROUP * tq, NSA_Q_W), jnp.bfloat16),
            scratch_shapes=[pltpu.VMEM((NSA_Q_W, tq), jnp.float32)] * 3
                           + [pltpu.VMEM((NSA_KV_HEADS, t // tq, 8, ROWS), jnp.float32)],
            compiler_params=pltpu.CompilerParams(
                dimension_semantics=("parallel", "arbitrary"), vmem_limit_bytes=VMEM_LIMIT_BYTES),
            name="nsa_attn",
        )(proj, proj, cmp, ks, vs, kw, vw, bc, bt, mimp, eg)

    parts = [tile_group(tile0) for tile0 in range(0, t // tq, ATT_TILE_GROUP)]
    return jnp.concatenate(parts, axis=1).reshape(b * t, NSA_Q_W)


PAGE_GROUP = 8
SEL_PAD = 8


def _nsa_sample_kernel(pt_ref, *refs):
    n_cmp_in = 2 * PAGE_GROUP
    n_sel_in = 2 * PAGE_GROUP
    cmp_pages = refs[:n_cmp_in]
    sel_pages = refs[n_cmp_in:n_cmp_in + n_sel_in]
    (qbd_ref, qbdt_ref, new_ref, win_ref, w1_ref, w1f_ref, pos_ref, w2_ref, bcmp_ref, bsel_ref, bwin_ref,
     mimp_ref, hk_ref, bd_ref, rep_ref, bdt_ref, gsel_ref, eexp_ref, eyes_ref, o_ref,
     a_s, stage_s, kc_s, vc_s, mask_s, kn_s, vn_s, ocmp_s, m_s, l_s, acc_s) = refs[n_cmp_in + n_sel_in:]
    del pt_ref
    ph = pl.program_id(1)
    g = pl.program_id(2)
    n_groups = pl.num_programs(2)
    bf16 = jnp.bfloat16
    hd = NSA_HEAD_DIM
    n_chunks = a_s.shape[1]
    per_page = PAGE_SIZE // CMP_STRIDE
    qbd = qbd_ref[...]

    def heads_out(acc_t, l):
        o = (acc_t / l) * bd_ref[...]
        hi = o.astype(bf16)
        lo = (o - hi.astype(jnp.float32)).astype(bf16)
        return _dot_tn(hi, rep_ref[...]) + _dot_tn(lo, rep_ref[...])

    @pl.when(ph == 0)
    def _():
        for u in range(PAGE_GROUP):
            row0 = pl.multiple_of((g * PAGE_GROUP + u) * per_page, per_page)
            for r in range(2):
                for pair in range(2):
                    c = 2 * r + pair
                    stage_s[c] = cmp_pages[u * 2 + r][2 * pair:2 * pair + 2].reshape(LANES, PAGE_SIZE).T
                    for l in range(CMP_STRIDE):
                        a_s[c, pl.ds(row0, per_page), l * LANES:(l + 1) * LANES] = (
                            stage_s[c, pl.ds(l, per_page, stride=CMP_STRIDE), :])

    @pl.when((ph == 0) & (g == n_groups - 1))
    def _():
        for r in range(2):
            pos_b = jnp.dot(pos_ref[r].astype(bf16), w1f_ref[r], preferred_element_type=jnp.float32)
            acc = [jnp.dot(a_s[2 * r + pair].astype(bf16), w1_ref[r], preferred_element_type=jnp.float32)
                   for pair in range(2)]
            dst = kc_s if r == 0 else vc_s
            for k in range(NSA_KV_HEADS):
                cols = (k % 2) * 2 * CMP_HIDDEN
                pa = acc[k // 2][:, cols:cols + CMP_HIDDEN]
                pb_next = pltpu.roll(acc[k // 2][:, cols + CMP_HIDDEN:cols + 2 * CMP_HIDDEN], n_chunks - 1, 0)
                hid = _gelu_tanh(pa + pb_next + pos_b)
                out = jnp.dot(hid.astype(bf16), w2_ref[r], preferred_element_type=jnp.float32)
                dst[:, k * hd:(k + 1) * hd] = out.astype(bf16)
        rows = lax.broadcasted_iota(jnp.int32, (n_chunks, NSA_HEADS), 0)
        mask_c = rows <= n_chunks - 2
        s = jnp.dot(kc_s[...], qbd, preferred_element_type=jnp.float32) + bcmp_ref[...]
        s = jnp.where(mask_c, s, NEG_INF)
        m = jnp.max(s, axis=0, keepdims=True)
        e = jnp.where(mask_c, jnp.exp(s - m), 0.0)
        l = jnp.maximum(jnp.sum(e, axis=0, keepdims=True), 1e-30)
        p = e / l
        ocmp_s[...] = heads_out(_dot_tn(vc_s[...], p.astype(bf16)), jnp.ones_like(l))
        p_kv = sum(jnp.dot(part, hk_ref[...], preferred_element_type=jnp.float32) for part in _split3_bf16(p))
        imp = sum(jnp.dot(mimp_ref[...], part, preferred_element_type=jnp.float32) for part in _split3_bf16(p_kv))
        blk = lax.broadcasted_iota(jnp.int32, imp.shape, 0)
        cur = n_chunks * CMP_STRIDE // SEL_BLOCK
        forced = (blk == 0) | (blk == cur) | (blk == cur - 1)
        score = jnp.where(blk <= cur, jnp.where(forced, FORCE_SCORE, imp), NEG_INF)
        n_blk = imp.shape[0]
        score_t = _dot3_tn(score, eyes_ref[...])
        other = lax.broadcasted_iota(jnp.int32, (n_blk, n_blk), 1)
        mine = lax.broadcasted_iota(jnp.int32, (n_blk, n_blk), 0)
        head = lax.broadcasted_iota(jnp.int32, imp.shape, 1)
        rank = jnp.zeros(imp.shape, jnp.float32)
        for k in range(NSA_KV_HEADS):
            row = score_t[k:k + 1, :]
            col = score[:, k:k + 1]
            beats = (row > col) | ((row == col) & (other < mine))
            rank = jnp.where(head == k, jnp.sum(jnp.where(beats, 1.0, 0.0), axis=1, keepdims=True), rank)
        sel = jnp.where((rank < SEL_TOP_N) & (score > 0.5 * NEG_INF), 1.0, 0.0)
        sel_h = jnp.dot(sel.astype(bf16), gsel_ref[...], preferred_element_type=jnp.float32)
        mask_s[...] = _dot_tn(sel_h.astype(bf16), eexp_ref[...])
        m_s[...] = jnp.full(m_s.shape, NEG_INF, jnp.float32)
        l_s[...] = jnp.zeros(l_s.shape, jnp.float32)
        acc_s[...] = jnp.zeros(acc_s.shape, jnp.float32)

    qbd_t = qbdt_ref[...]

    def lane_step(s, mask, state, pv):
        m, l, acc = state
        s = jnp.where(mask, s, NEG_INF)
        m_new = jnp.maximum(m, jnp.max(s, axis=1, keepdims=True))
        alpha = jnp.exp(m - m_new)
        e = jnp.where(mask, jnp.exp(s - m_new), 0.0)
        return m_new, alpha * l + jnp.sum(e, axis=1, keepdims=True), alpha * acc + pv(e.astype(bf16))

    def heads_out_t(acc, l):
        o = (acc / l) * bdt_ref[...]
        return sum(o[:, k * hd:(k + 1) * hd] for k in range(NSA_KV_HEADS))

    @pl.when(ph == 1)
    def _():
        span = PAGE_GROUP * PAGE_SIZE
        kt = jnp.concatenate([sel_pages[2 * u][...].reshape(NSA_KV_W, PAGE_SIZE).astype(bf16)
                              for u in range(PAGE_GROUP)], axis=1)
        vt = jnp.concatenate([sel_pages[2 * u + 1][...].reshape(NSA_KV_W, PAGE_SIZE).astype(bf16)
                              for u in range(PAGE_GROUP)], axis=1)
        mask = mask_s[:, pl.ds(pl.multiple_of(g * span, span), span)] > 0.5
        s = jnp.dot(qbd_t, kt, preferred_element_type=jnp.float32)
        near = jnp.where(g == n_groups - 1, 1.0, 0.0) * bsel_ref[:, 0:PAGE_SIZE]
        s = jnp.concatenate([s[:, :span - PAGE_SIZE], s[:, span - PAGE_SIZE:] + near], axis=1)
        m_s[...], l_s[...], acc_s[...] = lane_step(s, mask, (m_s[...], l_s[...], acc_s[...]),
                                                   lambda e: _dot_nt(e, vt))

    @pl.when((ph == 1) & (g == n_groups - 1))
    def _():
        kv0 = NSA_Q_W
        first = lax.broadcasted_iota(jnp.int32, (NSA_HEADS, SEL_PAD), 1) < 1

        def new_token_step(k_col, v_col, bias, state):
            kn_s[...] = jnp.zeros(kn_s.shape, bf16)
            vn_s[...] = jnp.zeros(vn_s.shape, bf16)
            kn_s[0:1, :] = new_ref[:, k_col:k_col + NSA_KV_W].astype(bf16)
            vn_s[0:1, :] = new_ref[:, v_col:v_col + NSA_KV_W].astype(bf16)
            s = _dot_nt(qbd_t, kn_s[...]) + bias
            vn = vn_s[...]
            return lane_step(s, first, state, lambda e: jnp.dot(e, vn, preferred_element_type=jnp.float32))

        state = new_token_step(kv0 + 2 * NSA_KV_W, kv0 + 3 * NSA_KV_W, bsel_ref[:, PAGE_SIZE:PAGE_SIZE + SEL_PAD],
                               (m_s[...], l_s[...], acc_s[...]))
        o_sel = heads_out_t(state[2], state[1])
        n_buf = win_ref.shape[2]
        kt = win_ref[0].astype(bf16)
        vt = win_ref[1].astype(bf16)
        init = (jnp.full((NSA_HEADS, 1), NEG_INF, jnp.float32), jnp.zeros((NSA_HEADS, 1), jnp.float32),
                jnp.zeros((NSA_HEADS, NSA_KV_W), jnp.float32))
        s = jnp.dot(qbd_t, kt, preferred_element_type=jnp.float32) + bwin_ref[:, 0:n_buf]
        state = lane_step(s, jnp.full(s.shape, True), init, lambda e: _dot_nt(e, vt))
        state = new_token_step(kv0 + 4 * NSA_KV_W, kv0 + 5 * NSA_KV_W, bwin_ref[:, n_buf:n_buf + SEL_PAD], state)
        o_win = heads_out_t(state[2], state[1])
        gate = jax.nn.sigmoid(new_ref[:, kv0 + 6 * NSA_KV_W:kv0 + 7 * NSA_KV_W])
        out = jnp.zeros(o_ref.shape, jnp.float32)
        for br, o_b in enumerate((ocmp_s[...], o_sel, o_win)):
            onehot = lax.broadcasted_iota(jnp.int32, (NSA_HEADS, NSA_COL_BLOCK), 1) == (
                lax.broadcasted_iota(jnp.int32, (NSA_HEADS, NSA_COL_BLOCK), 0) + br * NSA_HEADS)
            g_col = jnp.sum(jnp.where(onehot, gate, 0.0), axis=1, keepdims=True)
            out = out + g_col * o_b
        o_ref[...] = out


def _nsa_sample_tables(rel_bias, past_len, n_buf):
    far = rel_bias[REL_BUCKETS - 1]
    n_chunks = past_len // CMP_STRIDE
    n_sel = past_len // SEL_BLOCK + 1
    n_sel_pad = _round_up(n_sel, 8)
    cmp_end = jnp.arange(n_chunks) * CMP_STRIDE + CMP_LEN - 1
    bcmp = _bias_lookup(rel_bias, past_len - cmp_end) - far
    k_last = past_len - PAGE_SIZE + jnp.arange(PAGE_SIZE + SEL_PAD)
    bsel = (_bias_lookup(rel_bias, past_len - k_last) - far).T
    bwin = (_bias_lookup(rel_bias, n_buf - jnp.arange(n_buf + SEL_PAD)) - far).T
    per = SEL_BLOCK // CMP_STRIDE
    n_idx = np.arange(n_chunks)
    j_idx = np.arange(n_sel_pad)[:, None]
    mimp = 0.5 * ((n_idx // per == j_idx).astype(np.float32) + ((n_idx + 1) // per == j_idx).astype(np.float32))
    mimp[:, n_chunks - 1] = 0.0
    heads = np.arange(NSA_HEADS)
    hk = (heads[:, None] // NSA_GROUP == np.arange(NSA_KV_HEADS)[None, :]).astype(np.float32)
    rowk = np.arange(NSA_KV_W) // NSA_HEAD_DIM
    bd = (rowk[:, None] == heads[None, :] // NSA_GROUP).astype(np.float32)
    rep = (np.arange(NSA_KV_W)[:, None] % NSA_HEAD_DIM == np.arange(NSA_HEAD_DIM)[None, :]).astype(np.float32)
    eexp = (np.arange(past_len)[None, :] // SEL_BLOCK == np.arange(n_sel_pad)[:, None]).astype(np.float32)
    bf16 = jnp.bfloat16
    return (bcmp, bsel, bwin, jnp.asarray(mimp, bf16), jnp.asarray(hk, bf16), jnp.asarray(bd, jnp.float32),
            jnp.asarray(rep, bf16), jnp.asarray(bd.T, jnp.float32), jnp.asarray(hk.T, bf16), jnp.asarray(eexp, bf16),
            jnp.eye(n_sel_pad, dtype=bf16))


def _nsa_attn_sample(proj, kt_pages, win_t, win_row0, page_table, cmp_pos, w1, w2, tables):
    b = proj.shape[0]
    n_pages = page_table.shape[1]
    past_len = n_pages * PAGE_SIZE
    n_buf = win_t.shape[3]
    n_chunks = past_len // CMP_STRIDE
    n_groups = n_pages // PAGE_GROUP
    bcmp, bsel, bwin, mimp, hk, bd, rep, bdt, gsel, eexp, eyes = tables
    n_sel_pad = mimp.shape[0]
    w1cat, w1f, pos, w2b = _compress_weights(cmp_pos, w1, w2)
    w1cat = jnp.einsum('rlde,hg->rlhdge', w1cat, jnp.eye(2, dtype=w1cat.dtype)).reshape(
        2, CMP_STRIDE * LANES, 4 * CMP_HIDDEN)
    q = proj[:, :NSA_Q_W].reshape(b, NSA_KV_HEADS, NSA_GROUP, NSA_HEAD_DIM) * (NSA_HEAD_DIM ** -0.5)
    eye = jnp.eye(NSA_KV_HEADS, dtype=q.dtype)
    qbd = jnp.einsum('bkgd,kc->bkdcg', q, eye).reshape(b, NSA_KV_W, NSA_HEADS).astype(jnp.bfloat16)
    qbd_t = jnp.swapaxes(qbd, 1, 2)
    proj3 = proj.reshape(b, 1, proj.shape[1])

    def cmp_spec(u, r):
        def imap(i, ph, g, pt):
            gg = jnp.where(ph == 0, g, n_groups - 1)
            return (pt[i, gg * PAGE_GROUP + u], r, 0, 0)
        return pl.BlockSpec((None, NSA_KV_HEADS, NSA_HEAD_DIM, PAGE_SIZE), imap)

    def sel_spec(u, r):
        def imap(i, ph, g, pt):
            gg = jnp.where(ph == 1, g, 0)
            return (pt[i, gg * PAGE_GROUP + u], r, 0, 0)
        return pl.BlockSpec((None, NSA_KV_HEADS, NSA_HEAD_DIM, PAGE_SIZE), imap)

    def const_spec(a):
        nd = a.ndim
        return pl.BlockSpec(a.shape, lambda i, ph, g, pt: (0,) * nd)

    consts = (w1cat, w1f, pos, w2b, bcmp, bsel, bwin, mimp, hk, bd, rep, bdt, gsel, eexp, eyes)
    in_specs = ([cmp_spec(u, r) for u in range(PAGE_GROUP) for r in (0, 1)]
                + [sel_spec(u, r) for u in range(PAGE_GROUP) for r in (2, 3)]
                + [pl.BlockSpec((None, NSA_KV_W, NSA_HEADS), lambda i, ph, g, pt: (i, 0, 0)),
                   pl.BlockSpec((None, NSA_HEADS, NSA_KV_W), lambda i, ph, g, pt: (i, 0, 0)),
                   pl.BlockSpec((None, 1, proj.shape[1]), lambda i, ph, g, pt: (i, 0, 0)),
                   pl.BlockSpec((None, 2, NSA_KV_W, n_buf), lambda i, ph, g, pt: (win_row0 + i, 0, 0, 0))]
                + [const_spec(a) for a in consts])
    f32, bf16 = jnp.float32, jnp.bfloat16
    out = pl.pallas_call(
        _nsa_sample_kernel,
        grid_spec=pltpu.PrefetchScalarGridSpec(
            num_scalar_prefetch=1,
            grid=(b, 2, n_groups),
            in_specs=in_specs,
            out_specs=pl.BlockSpec((None, NSA_HEADS, NSA_HEAD_DIM), lambda i, ph, g, pt: (i, 0, 0)),
            scratch_shapes=[pltpu.VMEM((4, n_chunks, CMP_STRIDE * LANES), f32), pltpu.VMEM((4, PAGE_SIZE, LANES), f32),
                            pltpu.VMEM((n_chunks, NSA_KV_W), bf16), pltpu.VMEM((n_chunks, NSA_KV_W), bf16),
                            pltpu.VMEM((NSA_HEADS, past_len), f32),
                            pltpu.VMEM((SEL_PAD, NSA_KV_W), bf16), pltpu.VMEM((SEL_PAD, NSA_KV_W), bf16),
                            pltpu.VMEM((NSA_HEADS, NSA_HEAD_DIM), f32),
                            pltpu.VMEM((NSA_HEADS, 1), f32), pltpu.VMEM((NSA_HEADS, 1), f32),
                            pltpu.VMEM((NSA_HEADS, NSA_KV_W), f32)]),
        out_shape=jax.ShapeDtypeStruct((b, NSA_HEADS, NSA_HEAD_DIM), f32),
        compiler_params=pltpu.CompilerParams(
            dimension_semantics=("parallel", "arbitrary", "arbitrary"), vmem_limit_bytes=VMEM_LIMIT_BYTES),
        name="nsa_sample",
    )(page_table, *([kt_pages] * (4 * PAGE_GROUP)), qbd, qbd_t, proj3, win_t,
      *consts)
    return out.reshape(b, NSA_Q_W)


SSD_PROJ_W = _round_up(SSD_IN_W, 7 * LANES)
SSD_TILE = 128
SSD_GN = SSD_GROUPS * SSD_STATE
SSD_GW = SSD_HPG * SSD_HEAD_DIM
CONV_PAD = 8


def _silu(x):
    return x * jax.nn.sigmoid(x)


def _softplus(x):
    return jnp.maximum(x, 0.0) + jnp.log(1.0 + jnp.exp(-jnp.abs(x)))


def _dot3(x, table, dot=jnp.dot):
    return sum(dot(part, table, preferred_element_type=jnp.float32) for part in _split3_bf16(x))


def _dot3_tn(x, table):
    return sum(_dot_tn(part, table) for part in _split3_bf16(x))


def _ssd_prompt_kernel(z_ref, x_ref, bc_ref, dt_ref, cw_ref, cb_ref, dtb_ref, alog_ref, d_ref, nw_ref,
                       eh_ref, tril_ref, triu_ref, eye_ref, y_ref, st_ref, conv_ref, ux_s, ubc_s, st_s):
    c = pl.program_id(1)
    t = SSD_TILE
    di = SSD_D_INNER
    bf16 = jnp.bfloat16

    @pl.when(c == 0)
    def _():
        ux_s[0:CONV_PAD, :] = jnp.zeros((CONV_PAD, di), jnp.float32)
        ubc_s[0:CONV_PAD, :] = jnp.zeros((CONV_PAD, 2 * SSD_GN), jnp.float32)
        st_s[...] = jnp.zeros(st_s.shape, jnp.float32)

    ux_s[CONV_PAD:CONV_PAD + t, :] = x_ref[...]
    ubc_s[CONV_PAD:CONV_PAD + t, :] = bc_ref[...]

    def conv(buf, col0, width):
        y = cb_ref[:, col0:col0 + width]
        for i in range(SSD_CONV_W):
            y = y + buf[pl.ds(CONV_PAD - (SSD_CONV_W - 1 - i), t), :] * cw_ref[i:i + 1, col0:col0 + width]
        return _silu(y)

    xs = conv(ux_s, 0, di)
    bcs = conv(ubc_s, di, 2 * SSD_GN)
    dt = _softplus(dt_ref[:, 0:SSD_HEADS] + dtb_ref[...])
    dta = dt * (-jnp.exp(alog_ref[...]))
    cum = _dot3(dta, tril_ref[...], lambda a, b, **kw: jnp.dot(b, a, **kw))
    cum_t = _dot3_tn(dta, triu_ref[...])
    dt_t = _dot3_tn(dt, eye_ref[...])
    last = cum[t - 1:t, :]
    dec_in = _dot3(jnp.exp(cum), eh_ref[...])
    wgt = _dot3(jnp.exp(last - cum) * dt, eh_ref[...])
    st_scale = _dot3(jnp.broadcast_to(jnp.exp(last), (8, SSD_HEADS)), eh_ref[...])[0:1, :]
    causal = lax.broadcasted_iota(jnp.int32, (t, t), 1) <= lax.broadcasted_iota(jnp.int32, (t, t), 0)

    for g in range(SSD_GROUPS):
        gl = slice(g * SSD_GW, (g + 1) * SSD_GW)
        bg = bcs[:, g * SSD_STATE:(g + 1) * SSD_STATE].astype(bf16)
        cg = bcs[:, SSD_GN + g * SSD_STATE:SSD_GN + (g + 1) * SSD_STATE].astype(bf16)
        cb = _dot_nt(cg, bg)
        xg = xs[:, gl]
        y_heads = []
        for j in range(SSD_HPG):
            h = g * SSD_HPG + j
            decay = jnp.exp(jnp.minimum(cum[:, h:h + 1] - cum_t[h:h + 1, :], 0.0))
            w = jnp.where(causal, cb * decay * dt_t[h:h + 1, :], 0.0).astype(bf16)
            y_heads.append(jnp.dot(w, xg[:, j * SSD_HEAD_DIM:(j + 1) * SSD_HEAD_DIM].astype(bf16),
                                   preferred_element_type=jnp.float32))
        st = st_s[g]
        y = jnp.concatenate(y_heads, axis=1)
        y = y + jnp.dot(cg, st.astype(bf16), preferred_element_type=jnp.float32) * dec_in[:, gl]
        st_s[g] = st_scale[:, gl] * st + _dot_tn(bg, (xg * wgt[:, gl]).astype(bf16))
        y = (y + d_ref[:, gl] * xg) * _silu(z_ref[:, gl])
        y = y * lax.rsqrt(jnp.mean(y * y, axis=-1, keepdims=True) + NORM_EPS) * nw_ref[:, gl]
        y_ref[:, gl] = y.astype(y_ref.dtype)

    ux_s[0:CONV_PAD, :] = ux_s[t:t + CONV_PAD, :]
    ubc_s[0:CONV_PAD, :] = ubc_s[t:t + CONV_PAD, :]

    @pl.when(c == pl.num_programs(1) - 1)
    def _():
        st_ref[...] = st_s[...]
        keep = SSD_CONV_W - 1
        conv_ref[:, 0:di] = ux_s[CONV_PAD - keep:CONV_PAD, :]
        conv_ref[:, di:] = ubc_s[CONV_PAD - keep:CONV_PAD, :]


def _ssd_tables():
    eh = (np.arange(SSD_HEADS)[:, None] == np.arange(SSD_D_INNER)[None, :] // SSD_HEAD_DIM).astype(np.float32)
    tril = np.tril(np.ones((SSD_TILE, SSD_TILE), np.float32))
    bf16 = jnp.bfloat16
    return (jnp.asarray(eh, bf16), jnp.asarray(tril, bf16), jnp.asarray(tril.T, bf16),
            jnp.asarray(np.eye(SSD_TILE, dtype=np.float32), bf16))


def _state_from_transposed(st_t):
    b = st_t.shape[0]
    st = st_t.reshape(b, SSD_GROUPS, SSD_STATE, SSD_HPG, SSD_HEAD_DIM)
    return jnp.transpose(st, (0, 1, 3, 4, 2)).reshape(b, SSD_HEADS, SSD_HEAD_DIM, SSD_STATE)


def _ssd_prompt(proj, conv_w, conv_b, dt_bias, a_log, d_skip, norm_w):
    b, t, _ = proj.shape
    di = SSD_D_INNER
    tt = SSD_TILE
    d_exp = jnp.repeat(d_skip, SSD_HEAD_DIM).reshape(1, di)
    consts = (conv_w, conv_b.reshape(1, -1), dt_bias.reshape(1, -1), a_log.reshape(1, -1), d_exp,
              norm_w.reshape(1, di)) + _ssd_tables()

    def const_spec(a):
        nd = a.ndim
        return pl.BlockSpec(a.shape, lambda i, c: (0,) * nd)

    y, st_t, conv_new = pl.pallas_call(
        _ssd_prompt_kernel,
        grid=(b, t // tt),
        in_specs=[pl.BlockSpec((None, tt, di), lambda i, c: (i, c, 0)),
                  pl.BlockSpec((None, tt, di), lambda i, c: (i, c, 1)),
                  pl.BlockSpec((None, tt, 2 * SSD_GN), lambda i, c: (i, c, 2)),
                  pl.BlockSpec((None, tt, LANES), lambda i, c: (i, c, (di + SSD_CONV_DIM) // LANES))]
                 + [const_spec(a) for a in consts],
        out_specs=[pl.BlockSpec((None, tt, di), lambda i, c: (i, c, 0)),
                   pl.BlockSpec((None, SSD_GROUPS, SSD_STATE, SSD_GW), lambda i, c: (i, 0, 0, 0)),
                   pl.BlockSpec((None, SSD_CONV_W - 1, SSD_CONV_DIM), lambda i, c: (i, 0, 0))],
        out_shape=[jax.ShapeDtypeStruct((b, t, di), jnp.bfloat16),
                   jax.ShapeDtypeStruct((b, SSD_GROUPS, SSD_STATE, SSD_GW), jnp.float32),
                   jax.ShapeDtypeStruct((b, SSD_CONV_W - 1, SSD_CONV_DIM), jnp.float32)],
        scratch_shapes=[pltpu.VMEM((CONV_PAD + tt, di), jnp.float32),
                        pltpu.VMEM((CONV_PAD + tt, 2 * SSD_GN), jnp.float32),
                        pltpu.VMEM((SSD_GROUPS, SSD_STATE, SSD_GW), jnp.float32)],
        compiler_params=pltpu.CompilerParams(
            dimension_semantics=("parallel", "arbitrary"), vmem_limit_bytes=VMEM_LIMIT_BYTES),
        name="ssd_prompt",
    )(proj, proj, proj, proj, *consts)
    return y.reshape(b * t, di), _state_from_transposed(st_t), conv_new


ROW_PAD = 8


def _row8(x):
    return jnp.concatenate([x, jnp.zeros((ROW_PAD - 1, x.shape[1]), x.dtype)], axis=0)


def _ssd_step_kernel(p_ref, conv_ref, st_ref, cw_ref, cb_ref, dtb_ref, alog_ref, d_ref, nw_ref,
                     y_ref, st_out, conv_out, y_s):
    di = SSD_D_INNER
    bf16 = jnp.bfloat16
    u = p_ref[:, di:di + SSD_CONV_DIM]
    keep = SSD_CONV_W - 1
    y = cb_ref[...] + u * cw_ref[keep:keep + 1, :]
    for i in range(keep):
        y = y + conv_ref[i:i + 1, :] * cw_ref[i:i + 1, :]
    conv_out[0:keep - 1, :] = conv_ref[1:keep, :]
    conv_out[keep - 1:keep, :] = u
    xbc = _silu(y)
    xs = xbc[:, :di]
    dt = _softplus(p_ref[:, di + SSD_CONV_DIM:di + SSD_CONV_DIM + SSD_HEADS] + dtb_ref[...])
    decay = jnp.exp(dt * (-jnp.exp(alog_ref[...])))
    for g in range(SSD_GROUPS):
        bg = _row8(xbc[:, di + g * SSD_STATE:di + (g + 1) * SSD_STATE]).astype(bf16)
        cg = _row8(xbc[:, di + SSD_GN + g * SSD_STATE:di + SSD_GN + (g + 1) * SSD_STATE]).astype(bf16)
        for j in range(SSD_HPG):
            h = g * SSD_HPG + j
            cols = slice(h * SSD_HEAD_DIM, (h + 1) * SSD_HEAD_DIM)
            xh = _row8(xs[:, cols] * dt[:, h:h + 1]).astype(bf16)
            st = decay[:, h:h + 1] * st_ref[h] + _dot_tn(xh, bg)
            st_out[h] = st
            y_s[:, cols] = _dot_nt(cg, st.astype(bf16))
    yv = y_s[0:1, :]
    yv = (yv + d_ref[...] * xs) * _silu(p_ref[:, 0:di])
    for g in range(SSD_GROUPS):
        gl = slice(g * SSD_GW, (g + 1) * SSD_GW)
        yg = yv[:, gl]
        y_ref[:, gl] = yg * lax.rsqrt(jnp.mean(yg * yg, axis=-1, keepdims=True) + NORM_EPS) * nw_ref[:, gl]


def _ssd_step(proj, ssm0, conv0, conv_w, conv_b, dt_bias, a_log, d_skip, norm_w):
    b = proj.shape[0]
    di = SSD_D_INNER
    consts = (conv_w, conv_b.reshape(1, -1), dt_bias.reshape(1, -1), a_log.reshape(1, -1),
              jnp.repeat(d_skip, SSD_HEAD_DIM).reshape(1, di), norm_w.reshape(1, di))

    def const_spec(a):
        nd = a.ndim
        return pl.BlockSpec(a.shape, lambda i: (0,) * nd)

    y, st, conv_new = pl.pallas_call(
        _ssd_step_kernel,
        grid=(b,),
        in_specs=[pl.BlockSpec((None, 1, proj.shape[1]), lambda i: (i, 0, 0)),
                  pl.BlockSpec((None,) + conv0.shape[1:], lambda i: (i, 0, 0)),
                  pl.BlockSpec((None,) + ssm0.shape[1:], lambda i: (i, 0, 0, 0))]
                 + [const_spec(a) for a in consts],
        out_specs=[pl.BlockSpec((None, 1, di), lambda i: (i, 0, 0)),
                   pl.BlockSpec((None,) + ssm0.shape[1:], lambda i: (i, 0, 0, 0)),
                   pl.BlockSpec((None,) + conv0.shape[1:], lambda i: (i, 0, 0))],
        out_shape=[jax.ShapeDtypeStruct((b, 1, di), jnp.float32),
                   jax.ShapeDtypeStruct(ssm0.shape, jnp.float32),
                   jax.ShapeDtypeStruct(conv0.shape, jnp.float32)],
        scratch_shapes=[pltpu.VMEM((ROW_PAD, di), jnp.float32)],
        compiler_params=pltpu.CompilerParams(
            dimension_semantics=("parallel",), vmem_limit_bytes=VMEM_LIMIT_BYTES),
        name="ssd_step",
    )(proj.reshape(b, 1, -1), conv0, ssm0, *consts)
    return y.reshape(b, di), st, conv_new


def _hgrn_gates(p_ref, lb_ref):
    wk = HG_HEADS * HG_DK
    q = _silu(p_ref[:, 0:wk])
    f = lb_ref[...] + (1.0 - lb_ref[...]) * jax.nn.sigmoid(p_ref[:, wk:2 * wk])
    return q, f


HG_TILE = 128
HG_SUB = 16


def _hgrn_prompt_kernel(p_ref, lb_ref, gn_ref, tril_ref, subend_ref, ones_ref, y_ref, st_ref, st_s):
    c = pl.program_id(1)
    t = HG_TILE
    wk = HG_HEADS * HG_DK
    wv = HG_HEADS * HG_DV
    bf16 = jnp.bfloat16
    n_sub = t // HG_SUB

    @pl.when(c == 0)
    def _():
        st_s[...] = jnp.zeros(st_s.shape, jnp.float32)

    row = lax.broadcasted_iota(jnp.int32, (t, HG_DK), 0)
    sub_pos = row % HG_SUB
    row_sub = lax.broadcasted_iota(jnp.int32, (t, t), 0) // HG_SUB
    col_sub = lax.broadcasted_iota(jnp.int32, (t, t), 1) // HG_SUB
    left = lambda a, b, **kw: jnp.dot(b, a, **kw)

    def head(h, carry):
        kc = pl.ds(pl.multiple_of(h * HG_DK, HG_DK), HG_DK)
        q = _silu(p_ref[:, kc])
        lb = lb_ref[:, kc]
        f = lb + (1.0 - lb) * jax.nn.sigmoid(p_ref[:, pl.ds(pl.multiple_of(wk + h * HG_DK, HG_DK), HG_DK)])
        k = 1.0 - f
        v = p_ref[:, pl.ds(pl.multiple_of(2 * wk + h * HG_DV, HG_DV), HG_DV)]
        gate = p_ref[:, pl.ds(pl.multiple_of(2 * wk + wv + h * HG_DV, HG_DV), HG_DV)]
        cum = _dot3(jnp.log(f), tril_ref[...], left)
        sub_end = _dot3(cum, subend_ref[...], left)
        k_hat = k * jnp.exp(sub_end - cum)
        a_off = jnp.zeros((t, t), jnp.float32)
        for j in range(n_sub - 1):
            end_j = cum[(j + 1) * HG_SUB - 1:(j + 1) * HG_SUB, :]
            q_j = (q * jnp.exp(jnp.minimum(cum - end_j, 0.0))).astype(bf16)
            k_j = jnp.where(row // HG_SUB == j, k_hat, 0.0).astype(bf16)
            a_off = a_off + _dot_nt(q_j, k_j)
        a_off = jnp.where(col_sub < row_sub, a_off, 0.0)
        v16 = v.astype(bf16)
        o = jnp.dot(a_off.astype(bf16), v16, preferred_element_type=jnp.float32)
        decay = None
        for d in range(HG_SUB):
            k_d, v_d = (k, v) if d == 0 else (pltpu.roll(k, d, 0), pltpu.roll(v, d, 0))
            if d == 1:
                decay = f
            elif d > 1:
                decay = decay * pltpu.roll(f, d - 1, 0)
            e = (q * k_d if d == 0 else q * k_d * decay).astype(bf16)
            a_d = jnp.dot(e, ones_ref[...], preferred_element_type=jnp.float32)
            o = o + jnp.where(sub_pos >= d, a_d, 0.0) * v_d
        st = st_s[h]
        o = o + _dot_nt((q * jnp.exp(cum)).astype(bf16), st.astype(bf16))
        last = cum[t - 1:t, :]
        st_s[h] = st * jnp.exp(last) + _dot_tn(v16, (k * jnp.exp(last - cum)).astype(bf16))
        o = o * lax.rsqrt(jnp.mean(o * o, axis=-1, keepdims=True) + NORM_EPS) * gn_ref[...]
        y_ref[:, pl.ds(pl.multiple_of(h * HG_DV, HG_DV), HG_DV)] = (o * _silu(gate)).astype(y_ref.dtype)
        return carry

    lax.fori_loop(0, HG_HEADS, head, 0)

    @pl.when(c == pl.num_programs(1) - 1)
    def _():
        st_ref[...] = st_s[...]


def _hgrn_prompt(proj, lb, g_norm):
    b, t, w = proj.shape
    tt = HG_TILE
    wv = HG_HEADS * HG_DV
    idx = np.arange(tt)
    tril = np.tril(np.ones((tt, tt), np.float32))
    subend = (idx[None, :] == (idx[:, None] // HG_SUB) * HG_SUB + HG_SUB - 1).astype(np.float32)
    bf16 = jnp.bfloat16
    consts = (lb.reshape(1, -1), g_norm.reshape(1, -1), jnp.asarray(tril, bf16), jnp.asarray(subend, bf16),
              jnp.ones((HG_DK, HG_DK), bf16))

    def const_spec(a):
        nd = a.ndim
        return pl.BlockSpec(a.shape, lambda i, c: (0,) * nd)

    y, st_t = pl.pallas_call(
        _hgrn_prompt_kernel,
        grid=(b, t // tt),
        in_specs=[pl.BlockSpec((None, tt, w), lambda i, c: (i, c, 0))] + [const_spec(a) for a in consts],
        out_specs=[pl.BlockSpec((None, tt, wv), lambda i, c: (i, c, 0)),
                   pl.BlockSpec((None, HG_HEADS, HG_DV, HG_DK), lambda i, c: (i, 0, 0, 0))],
        out_shape=[jax.ShapeDtypeStruct((b, t, wv), bf16),
                   jax.ShapeDtypeStruct((b, HG_HEADS, HG_DV, HG_DK), jnp.float32)],
        scratch_shapes=[pltpu.VMEM((HG_HEADS, HG_DV, HG_DK), jnp.float32)],
        compiler_params=pltpu.CompilerParams(
            dimension_semantics=("parallel", "arbitrary"), vmem_limit_bytes=VMEM_LIMIT_BYTES),
        name="hgrn_prompt",
    )(proj, *consts)
    return y.reshape(b * t, wv), jnp.swapaxes(st_t, 2, 3)


def _hgrn_step_kernel(p_ref, st_ref, lb_ref, gn_ref, y_ref, st_out):
    wk = HG_HEADS * HG_DK
    wv = HG_HEADS * HG_DV
    bf16 = jnp.bfloat16
    q, f = _hgrn_gates(p_ref, lb_ref)
    eye = lax.broadcasted_iota(jnp.int32, (HG_DK, HG_DK), 0) == lax.broadcasted_iota(jnp.int32, (HG_DK, HG_DK), 1)
    for h in range(HG_HEADS):
        kc = slice(h * HG_DK, (h + 1) * HG_DK)
        vc = slice(2 * wk + h * HG_DV, 2 * wk + (h + 1) * HG_DV)
        gc = slice(2 * wk + wv + h * HG_DV, 2 * wk + wv + (h + 1) * HG_DV)
        fh = f[:, kc]
        f_col = jnp.sum(jnp.where(eye, fh, 0.0), axis=1, keepdims=True)
        kv = _dot_tn(_row8(1.0 - fh).astype(bf16), _row8(p_ref[:, vc]).astype(bf16))
        st = f_col * st_ref[h] + kv
        st_out[h] = st
        o = jnp.dot(_row8(q[:, kc]).astype(bf16), st.astype(bf16), preferred_element_type=jnp.float32)[0:1, :]
        o = o * lax.rsqrt(jnp.mean(o * o, axis=-1, keepdims=True) + NORM_EPS) * gn_ref[...]
        y_ref[:, h * HG_DV:(h + 1) * HG_DV] = o * _silu(p_ref[:, gc])


def _hgrn_step(proj, s0, lb, g_norm):
    b = proj.shape[0]
    wv = HG_HEADS * HG_DV
    y, st = pl.pallas_call(
        _hgrn_step_kernel,
        grid=(b,),
        in_specs=[pl.BlockSpec((None, 1, proj.shape[1]), lambda i: (i, 0, 0)),
                  pl.BlockSpec((None,) + s0.shape[1:], lambda i: (i, 0, 0, 0)),
                  pl.BlockSpec((1, HG_HEADS * HG_DK), lambda i: (0, 0)),
                  pl.BlockSpec((1, HG_DV), lambda i: (0, 0))],
        out_specs=[pl.BlockSpec((None, 1, wv), lambda i: (i, 0, 0)),
                   pl.BlockSpec((None,) + s0.shape[1:], lambda i: (i, 0, 0, 0))],
        out_shape=[jax.ShapeDtypeStruct((b, 1, wv), jnp.float32), jax.ShapeDtypeStruct(s0.shape, jnp.float32)],
        compiler_params=pltpu.CompilerParams(
            dimension_semantics=("parallel",), vmem_limit_bytes=VMEM_LIMIT_BYTES),
        name="hgrn_step",
    )(proj.reshape(b, 1, -1), s0, lb.reshape(1, -1), g_norm.reshape(1, -1))
    return y.reshape(b, wv), st


def _rel_bucket(dist):
    n = jnp.maximum(dist, 0)
    n_exact = REL_BUCKETS // 2
    nf = jnp.maximum(n, 1).astype(jnp.float32)
    large = n_exact + (jnp.log(nf / n_exact) / math.log(REL_MAX_DIST / n_exact)
                       * (REL_BUCKETS - n_exact)).astype(jnp.int32)
    return jnp.where(n < n_exact, n, jnp.minimum(large, REL_BUCKETS - 1))


def _nsa_prompt_core(proj, cmp_pos, cmp_w1, cmp_w2, tables):
    b, t, _ = proj.shape
    cmp = _nsa_compress_prompt(proj, cmp_pos, cmp_w1, cmp_w2)
    merged = _nsa_attn_prompt_t(proj, cmp, tables)
    o1 = NSA_Q_W
    o2 = o1 + 4 * NSA_KV_W
    o3 = o2 + 2 * NSA_KV_W
    kv_cs = proj[..., o1:o2].reshape(b, t, 4, NSA_KV_HEADS, NSA_HEAD_DIM)
    kv_win = proj[:, t - min(WINDOW, t):, o2:o3].reshape(b, min(WINDOW, t), 2, NSA_KV_HEADS, NSA_HEAD_DIM)
    return merged, kv_cs, kv_win


def _nsa_sample_core(proj, caches, layer, win_buf, page_table, cmp_pos, cmp_w1, cmp_w2, tables):
    b, t, _ = proj.shape
    assert t == 1 and win_buf.shape[1] == WINDOW and page_table.shape[1] % PAGE_GROUP == 0
    kt_pages, win_t, n_phys = caches
    merged = _nsa_attn_sample(proj.reshape(b, -1), kt_pages, win_t, layer * b,
                              page_table + layer * n_phys, cmp_pos, cmp_w1, cmp_w2, tables)
    o1 = NSA_Q_W
    o2 = o1 + 4 * NSA_KV_W
    o3 = o2 + 2 * NSA_KV_W
    kv_cs = proj[..., o1:o2].reshape(b, t, 4, NSA_KV_HEADS, NSA_HEAD_DIM)
    kv_win = proj[..., o2:o3].reshape(b, t, 2, NSA_KV_HEADS, NSA_HEAD_DIM)
    new_win = jnp.concatenate([win_buf[:, t:], kv_win], axis=1)
    return merged, kv_cs, new_win


def _pad_cols(w, n):
    return jnp.pad(w, ((0, 0), (0, n - w.shape[1])))


def kernel(x_prompt, x_sample, cache_nsa_kv, cache_nsa_win, state_hgrn, state_ssd, state_ssd_conv, page_table, c_prompt, c_sample, rel_bias, hgrn_lower_bounds, w_ada, b_ada, norm_gains, w_mlp_in, w_mlp_out, nsa_w_in, nsa_cmp_pos, nsa_cmp_w1, nsa_cmp_w2, nsa_w_out, hg_w_in, hg_norm, hg_w_out, ssd_w_in, ssd_conv_w, ssd_conv_b, ssd_dt_bias, ssd_a_log, ssd_d, ssd_norm, ssd_w_out):
    bf16 = jnp.bfloat16
    bp, tp, d = x_prompt.shape
    bs, ts, _ = x_sample.shape
    mp, ms = bp * tp, bs * ts
    lb_p = jax.nn.softmax(hgrn_lower_bounds, axis=0)
    lower_bounds = jnp.cumsum(lb_p, axis=0) - lb_p[0]

    mod = _ada_all(jnp.concatenate([c_prompt, c_sample], axis=0), w_ada, b_ada)
    mod = mod.reshape(DEPTH, bp + bs, ADA_CHUNKS, d)
    mod_p = mod[:, :bp].transpose(0, 2, 1, 3)[:, :, :, None, :]
    mod_s = mod[:, bp:].transpose(0, 2, 1, 3)[:, :, None, :, :]

    xp = x_prompt.reshape(mp, d)
    xs = x_sample.reshape(ms, d)
    tm_p, tm_s = PROMPT_ROW_TILE, ms
    nsa_tables = _nsa_prompt_tables_t(rel_bias, tp)
    nsa_tables_s = _nsa_sample_tables(rel_bias, page_table.shape[1] * PAGE_SIZE, cache_nsa_win.shape[2])
    n_phys = cache_nsa_kv.shape[1]
    n_all = cache_nsa_kv.shape[0] * n_phys
    kt_pages = jnp.transpose(cache_nsa_kv, (0, 1, 3, 4, 5, 2)).reshape(n_all, 4 * NSA_KV_HEADS, NSA_HEAD_DIM, PAGE_SIZE)
    win_t = jnp.transpose(cache_nsa_win, (0, 1, 3, 4, 5, 2)).reshape(-1, 2, NSA_KV_W, cache_nsa_win.shape[2])
    nsa_caches = (kt_pages, win_t, n_phys)

    kv_p, kv_s, win_p, win_s = [], [], [], []
    hg_p, hg_s, ssd_p, ssd_s, conv_p, conv_s = [], [], [], [], [], []
    for i in range(DEPTH):
        j = i // N_MIXERS
        kind = i % N_MIXERS
        g = norm_gains[i]
        shp_m, scp_m, gtp_m, shp_f, scp_f, gtp_f = [mod_p[i, c] for c in range(ADA_CHUNKS)]
        shs_m, scs_m, gts_m, shs_f, scs_f, gts_f = [mod_s[i, c] for c in range(ADA_CHUNKS)]
        if kind == 0:
            n_pad = NSA_PROJ_W
            w_in = _pad_cols(nsa_w_in[j], n_pad).astype(bf16)
            w_out = nsa_w_out[j].astype(bf16)
            pp = _norm_mod_matmul(xp, g[0], scp_m, shp_m, w_in, tp, tm_p).reshape(bp, tp, n_pad)
            ps = _norm_mod_matmul(xs, g[0], scs_m, shs_m, w_in, ts, tm_s).reshape(bs, ts, n_pad)
            ap, new_kv_p, new_win_p = _nsa_prompt_core(pp, nsa_cmp_pos[j], nsa_cmp_w1[j], nsa_cmp_w2[j], nsa_tables)
            as_, new_kv_s, new_win_s = _nsa_sample_core(ps, nsa_caches, j, cache_nsa_win[j], page_table,
                                                        nsa_cmp_pos[j], nsa_cmp_w1[j], nsa_cmp_w2[j], nsa_tables_s)
            kv_p.append(new_kv_p)
            kv_s.append(new_kv_s)
            win_p.append(new_win_p)
            win_s.append(new_win_s)
        elif kind == 1:
            w_in = hg_w_in[j].astype(bf16)
            w_out = hg_w_out[j].astype(bf16)
            pp = _norm_mod_matmul(xp, g[0], scp_m, shp_m, w_in, tp, tm_p).reshape(bp, tp, -1)
            ps = _norm_mod_matmul(xs, g[0], scs_m, shs_m, w_in, ts, tm_s).reshape(bs, ts, -1)
            ap, new_hp = _hgrn_prompt(pp, lower_bounds[i], hg_norm[j])
            as_, new_hs = _hgrn_step(ps.reshape(bs, -1), state_hgrn[j], lower_bounds[i], hg_norm[j])
            hg_p.append(new_hp)
            hg_s.append(new_hs)
        else:
            n_pad = SSD_PROJ_W
            w_in = _pad_cols(ssd_w_in[j], n_pad).astype(bf16)
            w_out = ssd_w_out[j].astype(bf16)
            pp = _norm_mod_matmul(xp, g[0], scp_m, shp_m, w_in, tp, tm_p).reshape(bp, tp, n_pad)
            ps = _norm_mod_matmul(xs, g[0], scs_m, shs_m, w_in, ts, tm_s).reshape(bs, ts, n_pad)
            ap, new_sp, new_cp = _ssd_prompt(pp, ssd_conv_w[j], ssd_conv_b[j], ssd_dt_bias[j],
                                             ssd_a_log[j], ssd_d[j], ssd_norm[j])
            as_, new_ss, new_cs = _ssd_step(ps.reshape(bs, -1), state_ssd[j], state_ssd_conv[j], ssd_conv_w[j],
                                            ssd_conv_b[j], ssd_dt_bias[j], ssd_a_log[j], ssd_d[j], ssd_norm[j])
            ssd_p.append(new_sp)
            ssd_s.append(new_ss)
            conv_p.append(new_cp)
            conv_s.append(new_cs)
        xp = _matmul_norm_res(ap, w_out, xp, g[1], gtp_m, tp, tm_p)
        xs = _matmul_norm_res(as_, w_out, xs, g[1], gts_m, ts, tm_s)
        w1 = w_mlp_in[i].astype(bf16)
        w2 = w_mlp_out[i].astype(bf16)
        xp = _mlp(xp, g[2], scp_f, shp_f, w1, w2, g[3], gtp_f, tp, tm_p)
        xs = _mlp(xs, g[2], scs_f, shs_f, w1, w2, g[3], gts_f, ts, tm_s)
    return (xp.reshape(bp, tp, d), xs.reshape(bs, ts, d),
            jnp.stack(kv_p), jnp.stack(kv_s), jnp.stack(win_p), jnp.stack(win_s),
            jnp.stack(hg_p), jnp.stack(hg_s), jnp.stack(ssd_p), jnp.stack(ssd_s),
            jnp.stack(conv_p), jnp.stack(conv_s))
```

```python
import functools
import math

import jax
import jax.numpy as jnp
import numpy as np
from jax import lax
from jax.experimental import pallas as pl
from jax.experimental.pallas import tpu as pltpu

D_MODEL = 1024
DEPTH = 4
PAGE_SIZE = 128
N_MIXERS = 3
ADA_CHUNKS = 6
NORM_EPS = 1e-6
D_FF = 4 * D_MODEL

NSA_HEADS = 16
NSA_HEAD_DIM = D_MODEL // NSA_HEADS
NSA_KV_HEADS = 4
NSA_GROUP = NSA_HEADS // NSA_KV_HEADS
CMP_STRIDE = 16
CMP_LEN = 2 * CMP_STRIDE
CMP_HIDDEN = 2 * NSA_HEAD_DIM
SEL_BLOCK = 64
SEL_TOP_N = 16
WINDOW = 512
NSA_Q_W = NSA_HEADS * NSA_HEAD_DIM
NSA_KV_W = NSA_KV_HEADS * NSA_HEAD_DIM
NSA_IN_W = NSA_Q_W + 6 * NSA_KV_W + 3 * NSA_HEADS

REL_BUCKETS = 32
REL_MAX_DIST = 128

HG_EXPAND = 128
HG_HEADS = D_MODEL // HG_EXPAND
HG_DK = HG_EXPAND
HG_DV = D_MODEL // HG_HEADS

SSD_D_INNER = 2 * D_MODEL
SSD_HEAD_DIM = 64
SSD_HEADS = SSD_D_INNER // SSD_HEAD_DIM
SSD_GROUPS = 8
SSD_HPG = SSD_HEADS // SSD_GROUPS
SSD_STATE = 128
SSD_CONV_W = 4
SSD_CONV_DIM = SSD_D_INNER + 2 * SSD_GROUPS * SSD_STATE
SSD_IN_W = SSD_D_INNER + SSD_CONV_DIM + SSD_HEADS

NEG_INF = -1e30
FORCE_SCORE = 1e4

LANES = 128
VMEM_LIMIT_BYTES = 48 * 1024 * 1024
PROMPT_ROW_TILE = 1024


def _round_up(n, m):
    return -(-n // m) * m


def _col_tile(n, cap=1536):
    best = LANES
    for t in range(LANES, cap + 1, LANES):
        if n % t == 0:
            best = t
    return best


def _rms(x, g):
    return x * lax.rsqrt(jnp.mean(x * x, axis=-1, keepdims=True) + NORM_EPS) * g


def _mod_spec(mod, rows_per_mod, tm, ngrid):
    r = mod.shape[1]
    if r == 1:
        per = rows_per_mod // tm
        if ngrid == 1:
            return pl.BlockSpec((None, 1, mod.shape[2]), lambda i: (i // per, 0, 0))
        return pl.BlockSpec((None, 1, mod.shape[2]), lambda i, j: (i // per, 0, 0))
    if ngrid == 1:
        return pl.BlockSpec((None, r, mod.shape[2]), lambda i: (0, 0, 0))
    return pl.BlockSpec((None, r, mod.shape[2]), lambda i, j: (0, 0, 0))


def _ada_kernel(c_ref, w_ref, b_ref, o_ref):
    c = c_ref[...]
    s = (c * jax.nn.sigmoid(c)).astype(jnp.bfloat16)
    o_ref[...] = jnp.dot(s, w_ref[...].astype(jnp.bfloat16),
                         preferred_element_type=jnp.float32) + b_ref[...]


def _ada_all(c_all, w_ada, b_ada):
    rows = c_all.shape[0]
    n = ADA_CHUNKS * D_MODEL
    tn = 1024
    return pl.pallas_call(
        _ada_kernel,
        grid=(DEPTH, n // tn),
        in_specs=[pl.BlockSpec((rows, D_MODEL), lambda l, j: (0, 0)),
                  pl.BlockSpec((None, D_MODEL, tn), lambda l, j: (l, 0, j)),
                  pl.BlockSpec((None, 1, tn), lambda l, j: (l, 0, j))],
        out_specs=pl.BlockSpec((None, rows, tn), lambda l, j: (l, 0, j)),
        out_shape=jax.ShapeDtypeStruct((DEPTH, rows, n), jnp.float32),
        compiler_params=pltpu.CompilerParams(
            dimension_semantics=("parallel", "parallel"), vmem_limit_bytes=VMEM_LIMIT_BYTES),
        name="ada",
    )(c_all, w_ada, b_ada.reshape(DEPTH, 1, n))


def _norm_mod_matmul_kernel(x_ref, g_ref, sc_ref, sh_ref, w_ref, o_ref, h_ref):
    @pl.when(pl.program_id(1) == 0)
    def _():
        h = _rms(x_ref[...], g_ref[...]) * (1.0 + sc_ref[...]) + sh_ref[...]
        h_ref[...] = h.astype(jnp.bfloat16)

    o_ref[...] = jnp.dot(h_ref[...], w_ref[...], preferred_element_type=jnp.float32)


def _norm_mod_matmul(x, g, sc, sh, w, rows_per_mod, tm):
    m, d = x.shape
    n = w.shape[1]
    tn = _col_tile(n)
    return pl.pallas_call(
        _norm_mod_matmul_kernel,
        grid=(m // tm, n // tn),
        in_specs=[pl.BlockSpec((tm, d), lambda i, j: (i, 0)),
                  pl.BlockSpec((1, d), lambda i, j: (0, 0)),
                  _mod_spec(sc, rows_per_mod, tm, 2),
                  _mod_spec(sh, rows_per_mod, tm, 2),
                  pl.BlockSpec((d, tn), lambda i, j: (0, j))],
        out_specs=pl.BlockSpec((tm, tn), lambda i, j: (i, j)),
        out_shape=jax.ShapeDtypeStruct((m, n), jnp.float32),
        scratch_shapes=[pltpu.VMEM((tm, d), jnp.bfloat16)],
        compiler_params=pltpu.CompilerParams(
            dimension_semantics=("parallel", "arbitrary"), vmem_limit_bytes=VMEM_LIMIT_BYTES),
        name="norm_mod_matmul",
    )(x, g.reshape(1, d), sc, sh, w)


def _matmul_norm_res_kernel(a_ref, w_ref, x_ref, g_ref, gt_ref, o_ref):
    y = jnp.dot(a_ref[...].astype(jnp.bfloat16), w_ref[...], preferred_element_type=jnp.float32)
    o_ref[...] = x_ref[...] + gt_ref[...] * _rms(y, g_ref[...])


def _matmul_norm_res(a, w, x, g, gate, rows_per_mod, tm):
    m, k = a.shape
    d = w.shape[1]
    return pl.pallas_call(
        _matmul_norm_res_kernel,
        grid=(m // tm,),
        in_specs=[pl.BlockSpec((tm, k), lambda i: (i, 0)),
                  pl.BlockSpec((k, d), lambda i: (0, 0)),
                  pl.BlockSpec((tm, d), lambda i: (i, 0)),
                  pl.BlockSpec((1, d), lambda i: (0, 0)),
                  _mod_spec(gate, rows_per_mod, tm, 1)],
        out_specs=pl.BlockSpec((tm, d), lambda i: (i, 0)),
        out_shape=jax.ShapeDtypeStruct((m, d), jnp.float32),
        compiler_params=pltpu.CompilerParams(
            dimension_semantics=("parallel",), vmem_limit_bytes=VMEM_LIMIT_BYTES),
        name="matmul_norm_res",
    )(a, w, x, g.reshape(1, d), gate)


def _mlp_kernel(x_ref, g2_ref, sc_ref, sh_ref, w1_ref, w2_ref, g3_ref, gt_ref, o_ref, h_ref, acc_ref):
    j = pl.program_id(1)

    @pl.when(j == 0)
    def _():
        h = _rms(x_ref[...], g2_ref[...]) * (1.0 + sc_ref[...]) + sh_ref[...]
        h_ref[...] = h.astype(jnp.bfloat16)

    u = jnp.dot(h_ref[...], w1_ref[...], preferred_element_type=jnp.float32)
    u = jnp.square(jnp.maximum(u, 0.0)).astype(jnp.bfloat16)
    part = jnp.dot(u, w2_ref[...], preferred_element_type=jnp.float32)

    @pl.when(j == 0)
    def _():
        acc_ref[...] = part

    @pl.when(j > 0)
    def _():
        acc_ref[...] += part

    @pl.when(j == pl.num_programs(1) - 1)
    def _():
        o_ref[...] = x_ref[...] + gt_ref[...] * _rms(acc_ref[...], g3_ref[...])


def _mlp(x, g2, sc, sh, w1, w2, g3, gate, rows_per_mod, tm):
    m, d = x.shape
    f = w1.shape[1]
    tf = 1024
    return pl.pallas_call(
        _mlp_kernel,
        grid=(m // tm, f // tf),
        in_specs=[pl.BlockSpec((tm, d), lambda i, j: (i, 0)),
                  pl.BlockSpec((1, d), lambda i, j: (0, 0)),
                  _mod_spec(sc, rows_per_mod, tm, 2),
                  _mod_spec(sh, rows_per_mod, tm, 2),
                  pl.BlockSpec((d, tf), lambda i, j: (0, j)),
                  pl.BlockSpec((tf, d), lambda i, j: (j, 0)),
                  pl.BlockSpec((1, d), lambda i, j: (0, 0)),
                  _mod_spec(gate, rows_per_mod, tm, 2)],
        out_specs=pl.BlockSpec((tm, d), lambda i, j: (i, 0)),
        out_shape=jax.ShapeDtypeStruct((m, d), jnp.float32),
        scratch_shapes=[pltpu.VMEM((tm, d), jnp.bfloat16), pltpu.VMEM((tm, d), jnp.float32)],
        compiler_params=pltpu.CompilerParams(
            dimension_semantics=("parallel", "arbitrary"), vmem_limit_bytes=VMEM_LIMIT_BYTES),
        name="mlp",
    )(x, g2.reshape(1, d), sc, sh, w1, w2, g3.reshape(1, d), gate)


NSA_COL_BLOCK = NSA_KV_W
NSA_PROJ_W = 11 * NSA_COL_BLOCK
NSA_GATE_BLOCK = (NSA_Q_W + 6 * NSA_KV_W) // NSA_COL_BLOCK
ATT_TILE = 128
ROWS = NSA_GROUP * ATT_TILE
ATT_TILE_GROUP = 2


def _dot_nt(a, b):
    return lax.dot_general(a, b, (((1,), (1,)), ((), ())), preferred_element_type=jnp.float32)


def _dot_tn(a, b):
    return lax.dot_general(a, b, (((0,), (0,)), ((), ())), preferred_element_type=jnp.float32)


def _gelu_tanh(x):
    return 0.5 * x * (1.0 + jnp.tanh(math.sqrt(2.0 / math.pi) * (x + 0.044715 * (x * x * x))))


def _split3_bf16(x):
    hi = x.astype(jnp.bfloat16)
    r1 = x - hi.astype(jnp.float32)
    mid = r1.astype(jnp.bfloat16)
    lo = (r1 - mid.astype(jnp.float32)).astype(jnp.bfloat16)
    return hi, mid, lo


def _nsa_compress_kernel(x0_ref, x1_ref, x2_ref, x3_ref, w1_ref, w1f_ref, pos_ref, w2_ref, o_ref):
    n = x0_ref.shape[0] // CMP_STRIDE
    hd = NSA_HEAD_DIM
    x_refs = ((x0_ref, x1_ref), (x2_ref, x3_ref))
    for r in range(2):
        pos_b = jnp.dot(pos_ref[r].astype(jnp.bfloat16), w1f_ref[r], preferred_element_type=jnp.float32)
        acc = [jnp.zeros((n, 2 * CMP_HIDDEN), jnp.float32) for _ in range(NSA_KV_HEADS)]
        for l in range(CMP_STRIDE):
            w = w1_ref[r, l]
            for pair in range(2):
                xl = x_refs[r][pair][pl.ds(l, n, stride=CMP_STRIDE), :].astype(jnp.bfloat16)
                for half in range(2):
                    k = 2 * pair + half
                    acc[k] = acc[k] + jnp.dot(xl[:, half * hd:(half + 1) * hd], w,
                                              preferred_element_type=jnp.float32)
        for k in range(NSA_KV_HEADS):
            pa = acc[k][:, :CMP_HIDDEN]
            pb_next = pltpu.roll(acc[k][:, CMP_HIDDEN:], n - 1, 0)
            hid = _gelu_tanh(pa + pb_next + pos_b)
            out = jnp.dot(hid.astype(jnp.bfloat16), w2_ref[r], preferred_element_type=jnp.float32)
            o_ref[r, :, k * hd:(k + 1) * hd] = out.astype(o_ref.dtype)


def _compress_weights(cmp_pos, w1, w2):
    bf16 = jnp.bfloat16
    w1r = w1.reshape(2, CMP_LEN, NSA_HEAD_DIM, CMP_HIDDEN)
    w1cat = jnp.concatenate([w1r[:, :CMP_STRIDE], w1r[:, CMP_STRIDE:]], axis=-1).astype(bf16)
    return w1cat, w1.astype(bf16), cmp_pos.reshape(2, 1, CMP_LEN * NSA_HEAD_DIM), w2.astype(bf16)


def _nsa_compress_prompt(proj, cmp_pos, w1, w2):
    b, t, _ = proj.shape
    n = t // CMP_STRIDE
    w1cat, w1f, pos, w2b = _compress_weights(cmp_pos, w1, w2)
    return pl.pallas_call(
        _nsa_compress_kernel,
        grid=(b,),
        in_specs=[pl.BlockSpec((None, t, LANES), lambda i, c=c: (i, 0, NSA_Q_W // LANES + c)) for c in range(4)]
                 + [pl.BlockSpec(w1cat.shape, lambda i: (0, 0, 0, 0)),
                  pl.BlockSpec(w1f.shape, lambda i: (0, 0, 0)),
                  pl.BlockSpec(pos.shape, lambda i: (0, 0, 0)),
                  pl.BlockSpec(w2b.shape, lambda i: (0, 0, 0))],
        out_specs=pl.BlockSpec((None, 2, n, NSA_KV_W), lambda i: (i, 0, 0, 0)),
        out_shape=jax.ShapeDtypeStruct((b, 2, n, NSA_KV_W), jnp.bfloat16),
        compiler_params=pltpu.CompilerParams(
            dimension_semantics=("parallel",), vmem_limit_bytes=VMEM_LIMIT_BYTES),
        name="nsa_compress",
    )(proj, proj, proj, proj, w1cat, w1f, pos, w2b)


def _bias_lookup(rel_bias, dist):
    onehot = jax.nn.one_hot(_rel_bucket(dist), REL_BUCKETS, dtype=jnp.float32)
    return jnp.einsum('...c,ch->...h', onehot, rel_bias, precision=lax.Precision.HIGHEST)


DEN_ROWS = 8


def _with_ones(v):
    return jnp.concatenate([v, jnp.ones((v.shape[0], DEN_ROWS), v.dtype)], axis=1)


def _key_softmax_step(s, v, m, acc):
    m_new = jnp.maximum(m, jnp.max(s, axis=0, keepdims=True))
    e = jnp.exp(s - m_new).astype(jnp.bfloat16)
    acc = jnp.exp(m - m_new) * acc + _dot_tn(_with_ones(v), e)
    return m_new, acc


def _softmax_out(acc):
    hd = acc.shape[0] - DEN_ROWS
    return acc[:hd] / acc[hd:hd + 1]


def _nsa_attn_t_kernel(q_ref, g_ref, c_ref, ks_ref, vs_ref, kw_ref, vw_ref, bc_ref, bt_ref,
                       mimp_ref, eg_ref, o_ref, oc_s, os_s, ow_s, sel_s, *, tile0, n_far):
    i = tile0 + pl.program_id(1)
    hd = NSA_HEAD_DIM
    tq = ATT_TILE
    bf16 = jnp.bfloat16
    n_cmp_pad = c_ref.shape[1]
    n_sel = mimp_ref.shape[0]
    kj = lax.broadcasted_iota(jnp.int32, (tq, ROWS), 0)
    qi = lax.broadcasted_iota(jnp.int32, (tq, ROWS), 1) % tq
    causal = kj <= qi
    win_edge = kj >= qi
    cmp_end = CMP_STRIDE * lax.broadcasted_iota(jnp.int32, (n_cmp_pad, ROWS), 0) + (CMP_LEN - 1)
    mask_c = cmp_end <= i * tq + lax.broadcasted_iota(jnp.int32, (n_cmp_pad, ROWS), 1) % tq
    blk = lax.broadcasted_iota(jnp.int32, (n_sel, tq), 0)
    cur = (i * tq + lax.broadcasted_iota(jnp.int32, (n_sel, tq), 1)) // SEL_BLOCK
    forced = (blk == 0) | (blk == cur) | (blk == cur - 1)
    valid = blk <= cur

    heads = range(NSA_KV_HEADS)
    lanes = [slice(k * hd, (k + 1) * hd) for k in heads]
    per_chunk = tq // SEL_BLOCK
    qk, o_cmp = [], []
    for k in heads:
        q = jnp.concatenate(
            [q_ref[:, (k * NSA_GROUP + g) * hd:(k * NSA_GROUP + g + 1) * hd] for g in range(NSA_GROUP)], axis=0)
        qk.append((q * (hd ** -0.5)).astype(bf16))

        s = jnp.where(mask_c, _dot_nt(c_ref[0, :, lanes[k]], qk[k]) + bc_ref[k], NEG_INF)
        m = jnp.max(s, axis=0, keepdims=True)
        e = jnp.where(mask_c, jnp.exp(s - m), 0.0)
        p = e * (1.0 / jnp.maximum(jnp.sum(e, axis=0, keepdims=True), 1e-30))
        o_cmp.append(_dot_tn(c_ref[1, :, lanes[k]], p.astype(bf16)))
        p_sum = sum(p[:, g * tq:(g + 1) * tq] for g in range(NSA_GROUP))
        imp = _dot3(p_sum, mimp_ref[...], lambda a, b, **kw: jnp.dot(b, a, **kw))
        score = jnp.where(valid, jnp.where(forced, FORCE_SCORE, imp), NEG_INF)
        rank = jnp.zeros((n_sel, tq), jnp.float32)
        for j in range(n_sel):
            row = score[j:j + 1, :]
            beats = (row > score) | ((row == score) & (blk > j))
            rank = rank + jnp.where(beats, 1.0, 0.0)
        sel = jnp.where((rank < SEL_TOP_N) & (score > 0.5 * NEG_INF), 1.0, 0.0)
        sel = jnp.concatenate([sel] * NSA_GROUP, axis=1)
        for c in range(n_sel // per_chunk):
            sel_s[k, c, 0:per_chunk, :] = sel[c * per_chunk:(c + 1) * per_chunk, :]

    def rows_of(c, n=1):
        return pl.ds(c * tq, n * tq) if isinstance(c, int) else pl.ds(pl.multiple_of(c * tq, tq), n * tq)

    def chunk(k_ref, v_ref, k, rows, carry, bias, mk):
        s = _dot_nt(k_ref[rows, lanes[k]].astype(bf16), qk[k])
        if bias is not None:
            s = s + bias
        return _key_softmax_step(jnp.where(mk, s, NEG_INF), v_ref[rows, lanes[k]].astype(bf16), *carry)

    def sel_mask(k, c, ok):
        pair = sel_s[k, c, 0:per_chunk, :]
        picked = jnp.concatenate([jnp.broadcast_to(pair[j:j + 1, :], (SEL_BLOCK, ROWS)) for j in range(per_chunk)],
                                 axis=0)
        return (picked > 0.5) & jnp.broadcast_to(ok, (tq, ROWS))

    init = (jnp.full((1, ROWS), NEG_INF, jnp.float32), jnp.zeros((hd + DEN_ROWS, ROWS), jnp.float32))
    carry = [init] * NSA_KV_HEADS
    c_prev = jnp.maximum(i - 1, 0)
    for k in heads:
        bias = jnp.concatenate([bt_ref[k, 1], bt_ref[k, 0]], axis=0)
        mk = jnp.concatenate([sel_mask(k, c_prev, i >= 1), sel_mask(k, i, True) & causal], axis=0)
        carry[k] = chunk(ks_ref, vs_ref, k, rows_of(i, 2), carry[k], bias, mk)
    for c in range(0, n_far, 2):
        for k in heads:
            mk = jnp.concatenate([sel_mask(k, c, c < i - 1), sel_mask(k, c + 1, c + 1 < i - 1)], axis=0)
            carry[k] = chunk(ks_ref, vs_ref, k, rows_of(c + 1, 2), carry[k], None, mk)
    o_sel = [_softmax_out(carry[k][1]) for k in heads]

    carry = [init] * NSA_KV_HEADS
    n_back = WINDOW // tq

    def tile_ok(c):
        return jnp.broadcast_to(c >= 0, (tq, ROWS))

    for k in heads:
        carry[k] = chunk(kw_ref, vw_ref, k, rows_of(i + n_back), carry[k], bt_ref[k, 0], causal)
    mk = jnp.concatenate([tile_ok(i - 2), tile_ok(i - 1)], axis=0)
    for k in heads:
        bias = jnp.concatenate([jnp.zeros((tq, ROWS), jnp.float32), bt_ref[k, 1]], axis=0)
        carry[k] = chunk(kw_ref, vw_ref, k, rows_of(i + n_back - 2, 2), carry[k], bias, mk)
    mk = jnp.concatenate([tile_ok(i - 4) & win_edge, tile_ok(i - 3)], axis=0)
    for k in heads:
        carry[k] = chunk(kw_ref, vw_ref, k, rows_of(i + n_back - 4, 2), carry[k], None, mk)
    o_win = [_softmax_out(carry[k][1]) for k in heads]

    for k in heads:
        for g in range(NSA_GROUP):
            rows = slice((k * NSA_GROUP + g) * hd, (k * NSA_GROUP + g + 1) * hd)
            oc_s[rows, :] = o_cmp[k][:, g * tq:(g + 1) * tq]
            os_s[rows, :] = o_sel[k][:, g * tq:(g + 1) * tq]
            ow_s[rows, :] = o_win[k][:, g * tq:(g + 1) * tq]

    gate = jax.nn.sigmoid(g_ref[...])
    g_hi = gate.astype(bf16)
    g_lo = (gate - g_hi.astype(jnp.float32)).astype(bf16)
    out = jnp.zeros((NSA_Q_W, tq), jnp.float32)
    for br, o_s in enumerate((oc_s, os_s, ow_s)):
        out = out + (_dot_nt(eg_ref[br], g_hi) + _dot_nt(eg_ref[br], g_lo)) * o_s[...]
    for r in range(NSA_Q_W // tq):
        o_ref[:, r * tq:(r + 1) * tq] = out[r * tq:(r + 1) * tq, :].T.astype(o_ref.dtype)


def _keys_by_kv_head(tab):
    *lead, q, t, _ = tab.shape
    tab = tab.reshape(*lead, q, t, NSA_KV_HEADS, NSA_GROUP)
    nl = len(lead)
    tab = jnp.transpose(tab, (*range(nl), nl + 2, nl + 1, nl + 3, nl))
    return tab.reshape(*lead, NSA_KV_HEADS, t, NSA_GROUP * q)


def _nsa_prompt_tables_t(rel_bias, t):
    tq = ATT_TILE
    n_chunks = t // CMP_STRIDE
    n_sel = t // SEL_BLOCK
    far = rel_bias[REL_BUCKETS - 1]
    ar = jnp.arange(tq)
    d_tile = (jnp.arange(2) * tq)[:, None, None] + ar[None, :, None] - ar[None, None, :]
    bt = _keys_by_kv_head(_bias_lookup(rel_bias, d_tile) - far)
    bt = jnp.transpose(bt, (1, 0, 2, 3))
    q_pos = jnp.arange(t).reshape(t // tq, tq)
    cmp_end = jnp.arange(n_chunks) * CMP_STRIDE + CMP_LEN - 1
    bc = _keys_by_kv_head(_bias_lookup(rel_bias, q_pos[:, :, None] - cmp_end[None, None, :]) - far)
    n_idx = np.arange(n_chunks)
    j_idx = np.arange(n_sel)[:, None]
    per = SEL_BLOCK // CMP_STRIDE
    mimp = 0.5 * ((n_idx // per == j_idx).astype(np.float32) + ((n_idx + 1) // per == j_idx).astype(np.float32))
    mimp[:, n_chunks - 1] = 0.0
    col = np.arange(NSA_Q_W) // NSA_HEAD_DIM
    eg = np.zeros((3, NSA_Q_W, NSA_COL_BLOCK), np.float32)
    for br in range(3):
        eg[br, np.arange(NSA_Q_W), br * NSA_HEADS + col] = 1.0
    return bt, bc, jnp.asarray(mimp, jnp.bfloat16), jnp.asarray(eg, jnp.bfloat16)


def _nsa_attn_prompt_t(proj, cmp, tables):
    b, t, _ = proj.shape
    bt, bc, mimp, eg = tables
    tq = ATT_TILE
    cb = NSA_COL_BLOCK

    def const_spec(a):
        nd = a.ndim
        return pl.BlockSpec(a.shape, lambda bi, i: (0,) * nd)

    assert WINDOW == 4 * tq

    def front_padded(col0, rows):
        a = jnp.pad(proj[:, :, col0:col0 + cb].astype(jnp.bfloat16), ((0, 0), (rows, 0), (0, 0)))
        return a, pl.BlockSpec((None, t + rows, cb), lambda bi, i: (bi, 0, 0))

    sel0 = NSA_Q_W + 2 * NSA_KV_W
    (ks, sel_spec), (vs, _) = front_padded(sel0, tq), front_padded(sel0 + cb, tq)
    (kw, win_spec), (vw, _) = front_padded(sel0 + 2 * cb, WINDOW), front_padded(sel0 + 3 * cb, WINDOW)

    def tile_group(tile0):
        n_far = max(tile0 + ATT_TILE_GROUP - 2, 0)
        return pl.pallas_call(
            functools.partial(_nsa_attn_t_kernel, tile0=tile0, n_far=n_far),
            grid=(b, ATT_TILE_GROUP),
            in_specs=[pl.BlockSpec((None, tq, NSA_Q_W), lambda bi, i: (bi, tile0 + i, 0)),
                      pl.BlockSpec((None, tq, cb), lambda bi, i: (bi, tile0 + i, NSA_GATE_BLOCK)),
                      pl.BlockSpec((None,) + cmp.shape[1:], lambda bi, i: (bi, 0, 0, 0)),
                      sel_spec, sel_spec, win_spec, win_spec,
                      pl.BlockSpec((None,) + bc.shape[1:], lambda bi, i: (tile0 + i, 0, 0, 0)),
                      const_spec(bt), const_spec(mimp), const_spec(eg)],
            out_specs=pl.BlockSpec((None, tq, NSA_Q_W), lambda bi, i: (bi, i, 0)),
            out_shape=jax.ShapeDtypeStruct((b, ATT_TILE_GROUP * tq, NSA_Q_W), jnp.bfloat16),
            scratch_shapes=[pltpu.VMEM((NSA_Q_W, tq), jnp.float32)] * 3
                           + [pltpu.VMEM((NSA_KV_HEADS, t // tq, 8, ROWS), jnp.float32)],
            compiler_params=pltpu.CompilerParams(
                dimension_semantics=("parallel", "arbitrary"), vmem_limit_bytes=VMEM_LIMIT_BYTES),
            name="nsa_attn",
        )(proj, proj, cmp, ks, vs, kw, vw, bc, bt, mimp, eg)

    parts = [tile_group(tile0) for tile0 in range(0, t // tq, ATT_TILE_GROUP)]
    return jnp.concatenate(parts, axis=1).reshape(b * t, NSA_Q_W)


PAGE_GROUP = 8
SEL_PAD = 8


def _nsa_sample_kernel(pt_ref, *refs):
    n_cmp_in = 2 * PAGE_GROUP
    n_sel_in = 2 * PAGE_GROUP
    cmp_pages = refs[:n_cmp_in]
    sel_pages = refs[n_cmp_in:n_cmp_in + n_sel_in]
    (qbd_ref, qbdt_ref, new_ref, win_ref, w1_ref, w1f_ref, pos_ref, w2_ref, bcmp_ref, bsel_ref, bwin_ref,
     mimp_ref, hk_ref, bd_ref, rep_ref, bdt_ref, gsel_ref, eexp_ref, eyes_ref, o_ref,
     a_s, stage_s, kc_s, vc_s, mask_s, kn_s, vn_s, ocmp_s, m_s, l_s, acc_s) = refs[n_cmp_in + n_sel_in:]
    del pt_ref
    ph = pl.program_id(1)
    g = pl.program_id(2)
    n_groups = pl.num_programs(2)
    bf16 = jnp.bfloat16
    hd = NSA_HEAD_DIM
    n_chunks = a_s.shape[1]
    per_page = PAGE_SIZE // CMP_STRIDE
    qbd = qbd_ref[...]

    def heads_out(acc_t, l):
        o = (acc_t / l) * bd_ref[...]
        hi = o.astype(bf16)
        lo = (o - hi.astype(jnp.float32)).astype(bf16)
        return _dot_tn(hi, rep_ref[...]) + _dot_tn(lo, rep_ref[...])

    @pl.when(ph == 0)
    def _():
        for u in range(PAGE_GROUP):
            row0 = pl.multiple_of((g * PAGE_GROUP + u) * per_page, per_page)
            for r in range(2):
                for pair in range(2):
                    c = 2 * r + pair
                    stage_s[c] = cmp_pages[u * 2 + r][2 * pair:2 * pair + 2].reshape(LANES, PAGE_SIZE).T
                    for l in range(CMP_STRIDE):
                        a_s[c, pl.ds(row0, per_page), l * LANES:(l + 1) * LANES] = (
                            stage_s[c, pl.ds(l, per_page, stride=CMP_STRIDE), :])

    @pl.when((ph == 0) & (g == n_groups - 1))
    def _():
        for r in range(2):
            pos_b = jnp.dot(pos_ref[r].astype(bf16), w1f_ref[r], preferred_element_type=jnp.float32)
            acc = [jnp.dot(a_s[2 * r + pair].astype(bf16), w1_ref[r], preferred_element_type=jnp.float32)
                   for pair in range(2)]
            dst = kc_s if r == 0 else vc_s
            for k in range(NSA_KV_HEADS):
                cols = (k % 2) * 2 * CMP_HIDDEN
                pa = acc[k // 2][:, cols:cols + CMP_HIDDEN]
                pb_next = pltpu.roll(acc[k // 2][:, cols + CMP_HIDDEN:cols + 2 * CMP_HIDDEN], n_chunks - 1, 0)
                hid = _gelu_tanh(pa + pb_next + pos_b)
                out = jnp.dot(hid.astype(bf16), w2_ref[r], preferred_element_type=jnp.float32)
                dst[:, k * hd:(k + 1) * hd] = out.astype(bf16)
        rows = lax.broadcasted_iota(jnp.int32, (n_chunks, NSA_HEADS), 0)
        mask_c = rows <= n_chunks - 2
        s = jnp.dot(kc_s[...], qbd, preferred_element_type=jnp.float32) + bcmp_ref[...]
        s = jnp.where(mask_c, s, NEG_INF)
        m = jnp.max(s, axis=0, keepdims=True)
        e = jnp.where(mask_c, jnp.exp(s - m), 0.0)
        l = jnp.maximum(jnp.sum(e, axis=0, keepdims=True), 1e-30)
        p = e / l
        ocmp_s[...] = heads_out(_dot_tn(vc_s[...], p.astype(bf16)), jnp.ones_like(l))
        p_kv = sum(jnp.dot(part, hk_ref[...], preferred_element_type=jnp.float32) for part in _split3_bf16(p))
        imp = sum(jnp.dot(mimp_ref[...], part, preferred_element_type=jnp.float32) for part in _split3_bf16(p_kv))
        blk = lax.broadcasted_iota(jnp.int32, imp.shape, 0)
        cur = n_chunks * CMP_STRIDE // SEL_BLOCK
        forced = (blk == 0) | (blk == cur) | (blk == cur - 1)
        score = jnp.where(blk <= cur, jnp.where(forced, FORCE_SCORE, imp), NEG_INF)
        n_blk = imp.shape[0]
        score_t = _dot3_tn(score, eyes_ref[...])
        other = lax.broadcasted_iota(jnp.int32, (n_blk, n_blk), 1)
        mine = lax.broadcasted_iota(jnp.int32, (n_blk, n_blk), 0)
        head = lax.broadcasted_iota(jnp.int32, imp.shape, 1)
        rank = jnp.zeros(imp.shape, jnp.float32)
        for k in range(NSA_KV_HEADS):
            row = score_t[k:k + 1, :]
            col = score[:, k:k + 1]
            beats = (row > col) | ((row == col) & (other < mine))
            rank = jnp.where(head == k, jnp.sum(jnp.where(beats, 1.0, 0.0), axis=1, keepdims=True), rank)
        sel = jnp.where((rank < SEL_TOP_N) & (score > 0.5 * NEG_INF), 1.0, 0.0)
        sel_h = jnp.dot(sel.astype(bf16), gsel_ref[...], preferred_element_type=jnp.float32)
        mask_s[...] = _dot_tn(sel_h.astype(bf16), eexp_ref[...])
        m_s[...] = jnp.full(m_s.shape, NEG_INF, jnp.float32)
        l_s[...] = jnp.zeros(l_s.shape, jnp.float32)
        acc_s[...] = jnp.zeros(acc_s.shape, jnp.float32)

    qbd_t = qbdt_ref[...]

    def lane_step(s, mask, state, pv):
        m, l, acc = state
        s = jnp.where(mask, s, NEG_INF)
        m_new = jnp.maximum(m, jnp.max(s, axis=1, keepdims=True))
        alpha = jnp.exp(m - m_new)
        e = jnp.where(mask, jnp.exp(s - m_new), 0.0)
        return m_new, alpha * l + jnp.sum(e, axis=1, keepdims=True), alpha * acc + pv(e.astype(bf16))

    def heads_out_t(acc, l):
        o = (acc / l) * bdt_ref[...]
        return sum(o[:, k * hd:(k + 1) * hd] for k in range(NSA_KV_HEADS))

    @pl.when(ph == 1)
    def _():
        span = PAGE_GROUP * PAGE_SIZE
        kt = jnp.concatenate([sel_pages[2 * u][...].reshape(NSA_KV_W, PAGE_SIZE).astype(bf16)
                              for u in range(PAGE_GROUP)], axis=1)
        vt = jnp.concatenate([sel_pages[2 * u + 1][...].reshape(NSA_KV_W, PAGE_SIZE).astype(bf16)
                              for u in range(PAGE_GROUP)], axis=1)
        mask = mask_s[:, pl.ds(pl.multiple_of(g * span, span), span)] > 0.5
        s = jnp.dot(qbd_t, kt, preferred_element_type=jnp.float32)
        near = jnp.where(g == n_groups - 1, 1.0, 0.0) * bsel_ref[:, 0:PAGE_SIZE]
        s = jnp.concatenate([s[:, :span - PAGE_SIZE], s[:, span - PAGE_SIZE:] + near], axis=1)
        m_s[...], l_s[...], acc_s[...] = lane_step(s, mask, (m_s[...], l_s[...], acc_s[...]),
                                                   lambda e: _dot_nt(e, vt))

    @pl.when((ph == 1) & (g == n_groups - 1))
    def _():
        kv0 = NSA_Q_W
        first = lax.broadcasted_iota(jnp.int32, (NSA_HEADS, SEL_PAD), 1) < 1

        def new_token_step(k_col, v_col, bias, state):
            kn_s[...] = jnp.zeros(kn_s.shape, bf16)
            vn_s[...] = jnp.zeros(vn_s.shape, bf16)
            kn_s[0:1, :] = new_ref[:, k_col:k_col + NSA_KV_W].astype(bf16)
            vn_s[0:1, :] = new_ref[:, v_col:v_col + NSA_KV_W].astype(bf16)
            s = _dot_nt(qbd_t, kn_s[...]) + bias
            vn = vn_s[...]
            return lane_step(s, first, state, lambda e: jnp.dot(e, vn, preferred_element_type=jnp.float32))

        state = new_token_step(kv0 + 2 * NSA_KV_W, kv0 + 3 * NSA_KV_W, bsel_ref[:, PAGE_SIZE:PAGE_SIZE + SEL_PAD],
                               (m_s[...], l_s[...], acc_s[...]))
        o_sel = heads_out_t(state[2], state[1])
        n_buf = win_ref.shape[2]
        kt = win_ref[0].astype(bf16)
        vt = win_ref[1].astype(bf16)
        init = (jnp.full((NSA_HEADS, 1), NEG_INF, jnp.float32), jnp.zeros((NSA_HEADS, 1), jnp.float32),
                jnp.zeros((NSA_HEADS, NSA_KV_W), jnp.float32))
        s = jnp.dot(qbd_t, kt, preferred_element_type=jnp.float32) + bwin_ref[:, 0:n_buf]
        state = lane_step(s, jnp.full(s.shape, True), init, lambda e: _dot_nt(e, vt))
        state = new_token_step(kv0 + 4 * NSA_KV_W, kv0 + 5 * NSA_KV_W, bwin_ref[:, n_buf:n_buf + SEL_PAD], state)
        o_win = heads_out_t(state[2], state[1])
        gate = jax.nn.sigmoid(new_ref[:, kv0 + 6 * NSA_KV_W:kv0 + 7 * NSA_KV_W])
        out = jnp.zeros(o_ref.shape, jnp.float32)
        for br, o_b in enumerate((ocmp_s[...], o_sel, o_win)):
            onehot = lax.broadcasted_iota(jnp.int32, (NSA_HEADS, NSA_COL_BLOCK), 1) == (
                lax.broadcasted_iota(jnp.int32, (NSA_HEADS, NSA_COL_BLOCK), 0) + br * NSA_HEADS)
            g_col = jnp.sum(jnp.where(onehot, gate, 0.0), axis=1, keepdims=True)
            out = out + g_col * o_b
        o_ref[...] = out


def _nsa_sample_tables(rel_bias, past_len, n_buf):
    far = rel_bias[REL_BUCKETS - 1]
    n_chunks = past_len // CMP_STRIDE
    n_sel = past_len // SEL_BLOCK + 1
    n_sel_pad = _round_up(n_sel, 8)
    cmp_end = jnp.arange(n_chunks) * CMP_STRIDE + CMP_LEN - 1
    bcmp = _bias_lookup(rel_bias, past_len - cmp_end) - far
    k_last = past_len - PAGE_SIZE + jnp.arange(PAGE_SIZE + SEL_PAD)
    bsel = (_bias_lookup(rel_bias, past_len - k_last) - far).T
    bwin = (_bias_lookup(rel_bias, n_buf - jnp.arange(n_buf + SEL_PAD)) - far).T
    per = SEL_BLOCK // CMP_STRIDE
    n_idx = np.arange(n_chunks)
    j_idx = np.arange(n_sel_pad)[:, None]
    mimp = 0.5 * ((n_idx // per == j_idx).astype(np.float32) + ((n_idx + 1) // per == j_idx).astype(np.float32))
    mimp[:, n_chunks - 1] = 0.0
    heads = np.arange(NSA_HEADS)
    hk = (heads[:, None] // NSA_GROUP == np.arange(NSA_KV_HEADS)[None, :]).astype(np.float32)
    rowk = np.arange(NSA_KV_W) // NSA_HEAD_DIM
    bd = (rowk[:, None] == heads[None, :] // NSA_GROUP).astype(np.float32)
    rep = (np.arange(NSA_KV_W)[:, None] % NSA_HEAD_DIM == np.arange(NSA_HEAD_DIM)[None, :]).astype(np.float32)
    eexp = (np.arange(past_len)[None, :] // SEL_BLOCK == np.arange(n_sel_pad)[:, None]).astype(np.float32)
    bf16 = jnp.bfloat16
    return (bcmp, bsel, bwin, jnp.asarray(mimp, bf16), jnp.asarray(hk, bf16), jnp.asarray(bd, jnp.float32),
            jnp.asarray(rep, bf16), jnp.asarray(bd.T, jnp.float32), jnp.asarray(hk.T, bf16), jnp.asarray(eexp, bf16),
            jnp.eye(n_sel_pad, dtype=bf16))


def _nsa_attn_sample(proj, kt_pages, win_t, win_row0, page_table, cmp_pos, w1, w2, tables):
    b = proj.shape[0]
    n_pages = page_table.shape[1]
    past_len = n_pages * PAGE_SIZE
    n_buf = win_t.shape[3]
    n_chunks = past_len // CMP_STRIDE
    n_groups = n_pages // PAGE_GROUP
    bcmp, bsel, bwin, mimp, hk, bd, rep, bdt, gsel, eexp, eyes = tables
    n_sel_pad = mimp.shape[0]
    w1cat, w1f, pos, w2b = _compress_weights(cmp_pos, w1, w2)
    w1cat = jnp.einsum('rlde,hg->rlhdge', w1cat, jnp.eye(2, dtype=w1cat.dtype)).reshape(
        2, CMP_STRIDE * LANES, 4 * CMP_HIDDEN)
    q = proj[:, :NSA_Q_W].reshape(b, NSA_KV_HEADS, NSA_GROUP, NSA_HEAD_DIM) * (NSA_HEAD_DIM ** -0.5)
    eye = jnp.eye(NSA_KV_HEADS, dtype=q.dtype)
    qbd = jnp.einsum('bkgd,kc->bkdcg', q, eye).reshape(b, NSA_KV_W, NSA_HEADS).astype(jnp.bfloat16)
    qbd_t = jnp.swapaxes(qbd, 1, 2)
    proj3 = proj.reshape(b, 1, proj.shape[1])

    def cmp_spec(u, r):
        def imap(i, ph, g, pt):
            gg = jnp.where(ph == 0, g, n_groups - 1)
            return (pt[i, gg * PAGE_GROUP + u], r, 0, 0)
        return pl.BlockSpec((None, NSA_KV_HEADS, NSA_HEAD_DIM, PAGE_SIZE), imap)

    def sel_spec(u, r):
        def imap(i, ph, g, pt):
            gg = jnp.where(ph == 1, g, 0)
            return (pt[i, gg * PAGE_GROUP + u], r, 0, 0)
        return pl.BlockSpec((None, NSA_KV_HEADS, NSA_HEAD_DIM, PAGE_SIZE), imap)

    def const_spec(a):
        nd = a.ndim
        return pl.BlockSpec(a.shape, lambda i, ph, g, pt: (0,) * nd)

    consts = (w1cat, w1f, pos, w2b, bcmp, bsel, bwin, mimp, hk, bd, rep, bdt, gsel, eexp, eyes)
    in_specs = ([cmp_spec(u, r) for u in range(PAGE_GROUP) for r in (0, 1)]
                + [sel_spec(u, r) for u in range(PAGE_GROUP) for r in (2, 3)]
                + [pl.BlockSpec((None, NSA_KV_W, NSA_HEADS), lambda i, ph, g, pt: (i, 0, 0)),
                   pl.BlockSpec((None, NSA_HEADS, NSA_KV_W), lambda i, ph, g, pt: (i, 0, 0)),
                   pl.BlockSpec((None, 1, proj.shape[1]), lambda i, ph, g, pt: (i, 0, 0)),
                   pl.BlockSpec((None, 2, NSA_KV_W, n_buf), lambda i, ph, g, pt: (win_row0 + i, 0, 0, 0))]
                + [const_spec(a) for a in consts])
    f32, bf16 = jnp.float32, jnp.bfloat16
    out = pl.pallas_call(
        _nsa_sample_kernel,
        grid_spec=pltpu.PrefetchScalarGridSpec(
            num_scalar_prefetch=1,
            grid=(b, 2, n_groups),
            in_specs=in_specs,
            out_specs=pl.BlockSpec((None, NSA_HEADS, NSA_HEAD_DIM), lambda i, ph, g, pt: (i, 0, 0)),
            scratch_shapes=[pltpu.VMEM((4, n_chunks, CMP_STRIDE * LANES), f32), pltpu.VMEM((4, PAGE_SIZE, LANES), f32),
                            pltpu.VMEM((n_chunks, NSA_KV_W), bf16), pltpu.VMEM((n_chunks, NSA_KV_W), bf16),
                            pltpu.VMEM((NSA_HEADS, past_len), f32),
                            pltpu.VMEM((SEL_PAD, NSA_KV_W), bf16), pltpu.VMEM((SEL_PAD, NSA_KV_W), bf16),
                            pltpu.VMEM((NSA_HEADS, NSA_HEAD_DIM), f32),
                            pltpu.VMEM((NSA_HEADS, 1), f32), pltpu.VMEM((NSA_HEADS, 1), f32),
                            pltpu.VMEM((NSA_HEADS, NSA_KV_W), f32)]),
        out_shape=jax.ShapeDtypeStruct((b, NSA_HEADS, NSA_HEAD_DIM), f32),
        compiler_params=pltpu.CompilerParams(
            dimension_semantics=("parallel", "arbitrary", "arbitrary"), vmem_limit_bytes=VMEM_LIMIT_BYTES),
        name="nsa_sample",
    )(page_table, *([kt_pages] * (4 * PAGE_GROUP)), qbd, qbd_t, proj3, win_t,
      *consts)
    return out.reshape(b, NSA_Q_W)


SSD_PROJ_W = _round_up(SSD_IN_W, 7 * LANES)
SSD_TILE = 128
SSD_GN = SSD_GROUPS * SSD_STATE
SSD_GW = SSD_HPG * SSD_HEAD_DIM
CONV_PAD = 8


def _silu(x):
    return x * jax.nn.sigmoid(x)


def _softplus(x):
    return jnp.maximum(x, 0.0) + jnp.log(1.0 + jnp.exp(-jnp.abs(x)))


def _dot3(x, table, dot=jnp.dot):
    return sum(dot(part, table, preferred_element_type=jnp.float32) for part in _split3_bf16(x))


def _dot3_tn(x, table):
    return sum(_dot_tn(part, table) for part in _split3_bf16(x))


def _ssd_prompt_kernel(z_ref, x_ref, bc_ref, dt_ref, cw_ref, cb_ref, dtb_ref, alog_ref, d_ref, nw_ref,
                       eh_ref, tril_ref, triu_ref, eye_ref, y_ref, st_ref, conv_ref, ux_s, ubc_s, st_s):
    c = pl.program_id(1)
    t = SSD_TILE
    di = SSD_D_INNER
    bf16 = jnp.bfloat16

    @pl.when(c == 0)
    def _():
        ux_s[0:CONV_PAD, :] = jnp.zeros((CONV_PAD, di), jnp.float32)
        ubc_s[0:CONV_PAD, :] = jnp.zeros((CONV_PAD, 2 * SSD_GN), jnp.float32)
        st_s[...] = jnp.zeros(st_s.shape, jnp.float32)

    ux_s[CONV_PAD:CONV_PAD + t, :] = x_ref[...]
    ubc_s[CONV_PAD:CONV_PAD + t, :] = bc_ref[...]

    def conv(buf, col0, width):
        y = cb_ref[:, col0:col0 + width]
        for i in range(SSD_CONV_W):
            y = y + buf[pl.ds(CONV_PAD - (SSD_CONV_W - 1 - i), t), :] * cw_ref[i:i + 1, col0:col0 + width]
        return _silu(y)

    xs = conv(ux_s, 0, di)
    bcs = conv(ubc_s, di, 2 * SSD_GN)
    dt = _softplus(dt_ref[:, 0:SSD_HEADS] + dtb_ref[...])
    dta = dt * (-jnp.exp(alog_ref[...]))
    cum = _dot3(dta, tril_ref[...], lambda a, b, **kw: jnp.dot(b, a, **kw))
    cum_t = _dot3_tn(dta, triu_ref[...])
    dt_t = _dot3_tn(dt, eye_ref[...])
    last = cum[t - 1:t, :]
    dec_in = _dot3(jnp.exp(cum), eh_ref[...])
    wgt = _dot3(jnp.exp(last - cum) * dt, eh_ref[...])
    st_scale = _dot3(jnp.broadcast_to(jnp.exp(last), (8, SSD_HEADS)), eh_ref[...])[0:1, :]
    causal = lax.broadcasted_iota(jnp.int32, (t, t), 1) <= lax.broadcasted_iota(jnp.int32, (t, t), 0)

    for g in range(SSD_GROUPS):
        gl = slice(g * SSD_GW, (g + 1) * SSD_GW)
        bg = bcs[:, g * SSD_STATE:(g + 1) * SSD_STATE].astype(bf16)
        cg = bcs[:, SSD_GN + g * SSD_STATE:SSD_GN + (g + 1) * SSD_STATE].astype(bf16)
        cb = _dot_nt(cg, bg)
        xg = xs[:, gl]
        y_heads = []
        for j in range(SSD_HPG):
            h = g * SSD_HPG + j
            decay = jnp.exp(jnp.minimum(cum[:, h:h + 1] - cum_t[h:h + 1, :], 0.0))
            w = jnp.where(causal, cb * decay * dt_t[h:h + 1, :], 0.0).astype(bf16)
            y_heads.append(jnp.dot(w, xg[:, j * SSD_HEAD_DIM:(j + 1) * SSD_HEAD_DIM].astype(bf16),
                                   preferred_element_type=jnp.float32))
        st = st_s[g]
        y = jnp.concatenate(y_heads, axis=1)
        y = y + jnp.dot(cg, st.astype(bf16), preferred_element_type=jnp.float32) * dec_in[:, gl]
        st_s[g] = st_scale[:, gl] * st + _dot_tn(bg, (xg * wgt[:, gl]).astype(bf16))
        y = (y + d_ref[:, gl] * xg) * _silu(z_ref[:, gl])
        y = y * lax.rsqrt(jnp.mean(y * y, axis=-1, keepdims=True) + NORM_EPS) * nw_ref[:, gl]
        y_ref[:, gl] = y.astype(y_ref.dtype)

    ux_s[0:CONV_PAD, :] = ux_s[t:t + CONV_PAD, :]
    ubc_s[0:CONV_PAD, :] = ubc_s[t:t + CONV_PAD, :]

    @pl.when(c == pl.num_programs(1) - 1)
    def _():
        st_ref[...] = st_s[...]
        keep = SSD_CONV_W - 1
        conv_ref[:, 0:di] = ux_s[CONV_PAD - keep:CONV_PAD, :]
        conv_ref[:, di:] = ubc_s[CONV_PAD - keep:CONV_PAD, :]


def _ssd_tables():
    eh = (np.arange(SSD_HEADS)[:, None] == np.arange(SSD_D_INNER)[None, :] // SSD_HEAD_DIM).astype(np.float32)
    tril = np.tril(np.ones((SSD_TILE, SSD_TILE), np.float32))
    bf16 = jnp.bfloat16
    return (jnp.asarray(eh, bf16), jnp.asarray(tril, bf16), jnp.asarray(tril.T, bf16),
            jnp.asarray(np.eye(SSD_TILE, dtype=np.float32), bf16))


def _state_from_transposed(st_t):
    b = st_t.shape[0]
    st = st_t.reshape(b, SSD_GROUPS, SSD_STATE, SSD_HPG, SSD_HEAD_DIM)
    return jnp.transpose(st, (0, 1, 3, 4, 2)).reshape(b, SSD_HEADS, SSD_HEAD_DIM, SSD_STATE)


def _ssd_prompt(proj, conv_w, conv_b, dt_bias, a_log, d_skip, norm_w):
    b, t, _ = proj.shape
    di = SSD_D_INNER
    tt = SSD_TILE
    d_exp = jnp.repeat(d_skip, SSD_HEAD_DIM).reshape(1, di)
    consts = (conv_w, conv_b.reshape(1, -1), dt_bias.reshape(1, -1), a_log.reshape(1, -1), d_exp,
              norm_w.reshape(1, di)) + _ssd_tables()

    def const_spec(a):
        nd = a.ndim
        return pl.BlockSpec(a.shape, lambda i, c: (0,) * nd)

    y, st_t, conv_new = pl.pallas_call(
        _ssd_prompt_kernel,
        grid=(b, t // tt),
        in_specs=[pl.BlockSpec((None, tt, di), lambda i, c: (i, c, 0)),
                  pl.BlockSpec((None, tt, di), lambda i, c: (i, c, 1)),
                  pl.BlockSpec((None, tt, 2 * SSD_GN), lambda i, c: (i, c, 2)),
                  pl.BlockSpec((None, tt, LANES), lambda i, c: (i, c, (di + SSD_CONV_DIM) // LANES))]
                 + [const_spec(a) for a in consts],
        out_specs=[pl.BlockSpec((None, tt, di), lambda i, c: (i, c, 0)),
                   pl.BlockSpec((None, SSD_GROUPS, SSD_STATE, SSD_GW), lambda i, c: (i, 0, 0, 0)),
                   pl.BlockSpec((None, SSD_CONV_W - 1, SSD_CONV_DIM), lambda i, c: (i, 0, 0))],
        out_shape=[jax.ShapeDtypeStruct((b, t, di), jnp.bfloat16),
                   jax.ShapeDtypeStruct((b, SSD_GROUPS, SSD_STATE, SSD_GW), jnp.float32),
                   jax.ShapeDtypeStruct((b, SSD_CONV_W - 1, SSD_CONV_DIM), jnp.float32)],
        scratch_shapes=[pltpu.VMEM((CONV_PAD + tt, di), jnp.float32),
                        pltpu.VMEM((CONV_PAD + tt, 2 * SSD_GN), jnp.float32),
                        pltpu.VMEM((SSD_GROUPS, SSD_STATE, SSD_GW), jnp.float32)],
        compiler_params=pltpu.CompilerParams(
            dimension_semantics=("parallel", "arbitrary"), vmem_limit_bytes=VMEM_LIMIT_BYTES),
        name="ssd_prompt",
    )(proj, proj, proj, proj, *consts)
    return y.reshape(b * t, di), _state_from_transposed(st_t), conv_new


ROW_PAD = 8


def _row8(x):
    return jnp.concatenate([x, jnp.zeros((ROW_PAD - 1, x.shape[1]), x.dtype)], axis=0)


def _ssd_step_kernel(p_ref, conv_ref, st_ref, cw_ref, cb_ref, dtb_ref, alog_ref, d_ref, nw_ref,
                     y_ref, st_out, conv_out, y_s):
    di = SSD_D_INNER
    bf16 = jnp.bfloat16
    u = p_ref[:, di:di + SSD_CONV_DIM]
    keep = SSD_CONV_W - 1
    y = cb_ref[...] + u * cw_ref[keep:keep + 1, :]
    for i in range(keep):
        y = y + conv_ref[i:i + 1, :] * cw_ref[i:i + 1, :]
    conv_out[0:keep - 1, :] = conv_ref[1:keep, :]
    conv_out[keep - 1:keep, :] = u
    xbc = _silu(y)
    xs = xbc[:, :di]
    dt = _softplus(p_ref[:, di + SSD_CONV_DIM:di + SSD_CONV_DIM + SSD_HEADS] + dtb_ref[...])
    decay = jnp.exp(dt * (-jnp.exp(alog_ref[...])))
    for g in range(SSD_GROUPS):
        bg = _row8(xbc[:, di + g * SSD_STATE:di + (g + 1) * SSD_STATE]).astype(bf16)
        cg = _row8(xbc[:, di + SSD_GN + g * SSD_STATE:di + SSD_GN + (g + 1) * SSD_STATE]).astype(bf16)
        for j in range(SSD_HPG):
            h = g * SSD_HPG + j
            cols = slice(h * SSD_HEAD_DIM, (h + 1) * SSD_HEAD_DIM)
            xh = _row8(xs[:, cols] * dt[:, h:h + 1]).astype(bf16)
            st = decay[:, h:h + 1] * st_ref[h] + _dot_tn(xh, bg)
            st_out[h] = st
            y_s[:, cols] = _dot_nt(cg, st.astype(bf16))
    yv = y_s[0:1, :]
    yv = (yv + d_ref[...] * xs) * _silu(p_ref[:, 0:di])
    for g in range(SSD_GROUPS):
        gl = slice(g * SSD_GW, (g + 1) * SSD_GW)
        yg = yv[:, gl]
        y_ref[:, gl] = yg * lax.rsqrt(jnp.mean(yg * yg, axis=-1, keepdims=True) + NORM_EPS) * nw_ref[:, gl]


def _ssd_step(proj, ssm0, conv0, conv_w, conv_b, dt_bias, a_log, d_skip, norm_w):
    b = proj.shape[0]
    di = SSD_D_INNER
    consts = (conv_w, conv_b.reshape(1, -1), dt_bias.reshape(1, -1), a_log.reshape(1, -1),
              jnp.repeat(d_skip, SSD_HEAD_DIM).reshape(1, di), norm_w.reshape(1, di))

    def const_spec(a):
        nd = a.ndim
        return pl.BlockSpec(a.shape, lambda i: (0,) * nd)

    y, st, conv_new = pl.pallas_call(
        _ssd_step_kernel,
        grid=(b,),
        in_specs=[pl.BlockSpec((None, 1, proj.shape[1]), lambda i: (i, 0, 0)),
                  pl.BlockSpec((None,) + conv0.shape[1:], lambda i: (i, 0, 0)),
                  pl.BlockSpec((None,) + ssm0.shape[1:], lambda i: (i, 0, 0, 0))]
                 + [const_spec(a) for a in consts],
        out_specs=[pl.BlockSpec((None, 1, di), lambda i: (i, 0, 0)),
                   pl.BlockSpec((None,) + ssm0.shape[1:], lambda i: (i, 0, 0, 0)),
                   pl.BlockSpec((None,) + conv0.shape[1:], lambda i: (i, 0, 0))],
        out_shape=[jax.ShapeDtypeStruct((b, 1, di), jnp.float32),
                   jax.ShapeDtypeStruct(ssm0.shape, jnp.float32),
                   jax.ShapeDtypeStruct(conv0.shape, jnp.float32)],
        scratch_shapes=[pltpu.VMEM((ROW_PAD, di), jnp.float32)],
        compiler_params=pltpu.CompilerParams(
            dimension_semantics=("parallel",), vmem_limit_bytes=VMEM_LIMIT_BYTES),
        name="ssd_step",
    )(proj.reshape(b, 1, -1), conv0, ssm0, *consts)
    return y.reshape(b, di), st, conv_new


def _hgrn_gates(p_ref, lb_ref):
    wk = HG_HEADS * HG_DK
    q = _silu(p_ref[:, 0:wk])
    f = lb_ref[...] + (1.0 - lb_ref[...]) * jax.nn.sigmoid(p_ref[:, wk:2 * wk])
    return q, f


HG_TILE = 128
HG_SUB = 16


def _hgrn_prompt_kernel(p_ref, lb_ref, gn_ref, tril_ref, subend_ref, ones_ref, y_ref, st_ref, st_s):
    c = pl.program_id(1)
    t = HG_TILE
    wk = HG_HEADS * HG_DK
    wv = HG_HEADS * HG_DV
    bf16 = jnp.bfloat16
    n_sub = t // HG_SUB

    @pl.when(c == 0)
    def _():
        st_s[...] = jnp.zeros(st_s.shape, jnp.float32)

    row = lax.broadcasted_iota(jnp.int32, (t, HG_DK), 0)
    sub_pos = row % HG_SUB
    row_sub = lax.broadcasted_iota(jnp.int32, (t, t), 0) // HG_SUB
    col_sub = lax.broadcasted_iota(jnp.int32, (t, t), 1) // HG_SUB
    left = lambda a, b, **kw: jnp.dot(b, a, **kw)

    def head(h, carry):
        kc = pl.ds(pl.multiple_of(h * HG_DK, HG_DK), HG_DK)
        q = _silu(p_ref[:, kc])
        lb = lb_ref[:, kc]
        f = lb + (1.0 - lb) * jax.nn.sigmoid(p_ref[:, pl.ds(pl.multiple_of(wk + h * HG_DK, HG_DK), HG_DK)])
        k = 1.0 - f
        v = p_ref[:, pl.ds(pl.multiple_of(2 * wk + h * HG_DV, HG_DV), HG_DV)]
        gate = p_ref[:, pl.ds(pl.multiple_of(2 * wk + wv + h * HG_DV, HG_DV), HG_DV)]
        cum = _dot3(jnp.log(f), tril_ref[...], left)
        sub_end = _dot3(cum, subend_ref[...], left)
        k_hat = k * jnp.exp(sub_end - cum)
        a_off = jnp.zeros((t, t), jnp.float32)
        for j in range(n_sub - 1):
            end_j = cum[(j + 1) * HG_SUB - 1:(j + 1) * HG_SUB, :]
            q_j = (q * jnp.exp(jnp.minimum(cum - end_j, 0.0))).astype(bf16)
            k_j = jnp.where(row // HG_SUB == j, k_hat, 0.0).astype(bf16)
            a_off = a_off + _dot_nt(q_j, k_j)
        a_off = jnp.where(col_sub < row_sub, a_off, 0.0)
        v16 = v.astype(bf16)
        o = jnp.dot(a_off.astype(bf16), v16, preferred_element_type=jnp.float32)
        decay = None
        for d in range(HG_SUB):
            k_d, v_d = (k, v) if d == 0 else (pltpu.roll(k, d, 0), pltpu.roll(v, d, 0))
            if d == 1:
                decay = f
            elif d > 1:
                decay = decay * pltpu.roll(f, d - 1, 0)
            e = (q * k_d if d == 0 else q * k_d * decay).astype(bf16)
            a_d = jnp.dot(e, ones_ref[...], preferred_element_type=jnp.float32)
            o = o + jnp.where(sub_pos >= d, a_d, 0.0) * v_d
        st = st_s[h]
        o = o + _dot_nt((q * jnp.exp(cum)).astype(bf16), st.astype(bf16))
        last = cum[t - 1:t, :]
        st_s[h] = st * jnp.exp(last) + _dot_tn(v16, (k * jnp.exp(last - cum)).astype(bf16))
        o = o * lax.rsqrt(jnp.mean(o * o, axis=-1, keepdims=True) + NORM_EPS) * gn_ref[...]
        y_ref[:, pl.ds(pl.multiple_of(h * HG_DV, HG_DV), HG_DV)] = (o * _silu(gate)).astype(y_ref.dtype)
        return carry

    lax.fori_loop(0, HG_HEADS, head, 0)

    @pl.when(c == pl.num_programs(1) - 1)
    def _():
        st_ref[...] = st_s[...]


def _hgrn_prompt(proj, lb, g_norm):
    b, t, w = proj.shape
    tt = HG_TILE
    wv = HG_HEADS * HG_DV
    idx = np.arange(tt)
    tril = np.tril(np.ones((tt, tt), np.float32))
    subend = (idx[None, :] == (idx[:, None] // HG_SUB) * HG_SUB + HG_SUB - 1).astype(np.float32)
    bf16 = jnp.bfloat16
    consts = (lb.reshape(1, -1), g_norm.reshape(1, -1), jnp.asarray(tril, bf16), jnp.asarray(subend, bf16),
              jnp.ones((HG_DK, HG_DK), bf16))

    def const_spec(a):
        nd = a.ndim
        return pl.BlockSpec(a.shape, lambda i, c: (0,) * nd)

    y, st_t = pl.pallas_call(
        _hgrn_prompt_kernel,
        grid=(b, t // tt),
        in_specs=[pl.BlockSpec((None, tt, w), lambda i, c: (i, c, 0))] + [const_spec(a) for a in consts],
        out_specs=[pl.BlockSpec((None, tt, wv), lambda i, c: (i, c, 0)),
                   pl.BlockSpec((None, HG_HEADS, HG_DV, HG_DK), lambda i, c: (i, 0, 0, 0))],
        out_shape=[jax.ShapeDtypeStruct((b, t, wv), bf16),
                   jax.ShapeDtypeStruct((b, HG_HEADS, HG_DV, HG_DK), jnp.float32)],
        scratch_shapes=[pltpu.VMEM((HG_HEADS, HG_DV, HG_DK), jnp.float32)],
        compiler_params=pltpu.CompilerParams(
            dimension_semantics=("parallel", "arbitrary"), vmem_limit_bytes=VMEM_LIMIT_BYTES),
        name="hgrn_prompt",
    )(proj, *consts)
    return y.reshape(b * t, wv), jnp.swapaxes(st_t, 2, 3)


def _hgrn_step_kernel(p_ref, st_ref, lb_ref, gn_ref, y_ref, st_out):
    wk = HG_HEADS * HG_DK
    wv = HG_HEADS * HG_DV
    bf16 = jnp.bfloat16
    q, f = _hgrn_gates(p_ref, lb_ref)
    eye = lax.broadcasted_iota(jnp.int32, (HG_DK, HG_DK), 0) == lax.broadcasted_iota(jnp.int32, (HG_DK, HG_DK), 1)
    for h in range(HG_HEADS):
        kc = slice(h * HG_DK, (h + 1) * HG_DK)
        vc = slice(2 * wk + h * HG_DV, 2 * wk + (h + 1) * HG_DV)
        gc = slice(2 * wk + wv + h * HG_DV, 2 * wk + wv + (h + 1) * HG_DV)
        fh = f[:, kc]
        f_col = jnp.sum(jnp.where(eye, fh, 0.0), axis=1, keepdims=True)
        kv = _dot_tn(_row8(1.0 - fh).astype(bf16), _row8(p_ref[:, vc]).astype(bf16))
        st = f_col * st_ref[h] + kv
        st_out[h] = st
        o = jnp.dot(_row8(q[:, kc]).astype(bf16), st.astype(bf16), preferred_element_type=jnp.float32)[0:1, :]
        o = o * lax.rsqrt(jnp.mean(o * o, axis=-1, keepdims=True) + NORM_EPS) * gn_ref[...]
        y_ref[:, h * HG_DV:(h + 1) * HG_DV] = o * _silu(p_ref[:, gc])


def _hgrn_step(proj, s0, lb, g_norm):
    b = proj.shape[0]
    wv = HG_HEADS * HG_DV
    y, st = pl.pallas_call(
        _hgrn_step_kernel,
        grid=(b,),
        in_specs=[pl.BlockSpec((None, 1, proj.shape[1]), lambda i: (i, 0, 0)),
                  pl.BlockSpec((None,) + s0.shape[1:], lambda i: (i, 0, 0, 0)),
                  pl.BlockSpec((1, HG_HEADS * HG_DK), lambda i: (0, 0)),
                  pl.BlockSpec((1, HG_DV), lambda i: (0, 0))],
        out_specs=[pl.BlockSpec((None, 1, wv), lambda i: (i, 0, 0)),
                   pl.BlockSpec((None,) + s0.shape[1:], lambda i: (i, 0, 0, 0))],
        out_shape=[jax.ShapeDtypeStruct((b, 1, wv), jnp.float32), jax.ShapeDtypeStruct(s0.shape, jnp.float32)],
        compiler_params=pltpu.CompilerParams(
            dimension_semantics=("parallel",), vmem_limit_bytes=VMEM_LIMIT_BYTES),
        name="hgrn_step",
    )(proj.reshape(b, 1, -1), s0, lb.reshape(1, -1), g_norm.reshape(1, -1))
    return y.reshape(b, wv), st


def _rel_bucket(dist):
    n = jnp.maximum(dist, 0)
    n_exact = REL_BUCKETS // 2
    nf = jnp.maximum(n, 1).astype(jnp.float32)
    large = n_exact + (jnp.log(nf / n_exact) / math.log(REL_MAX_DIST / n_exact)
                       * (REL_BUCKETS - n_exact)).astype(jnp.int32)
    return jnp.where(n < n_exact, n, jnp.minimum(large, REL_BUCKETS - 1))


def _nsa_prompt_core(proj, cmp_pos, cmp_w1, cmp_w2, tables):
    b, t, _ = proj.shape
    cmp = _nsa_compress_prompt(proj, cmp_pos, cmp_w1, cmp_w2)
    merged = _nsa_attn_prompt_t(proj, cmp, tables)
    o1 = NSA_Q_W
    o2 = o1 + 4 * NSA_KV_W
    o3 = o2 + 2 * NSA_KV_W
    kv_cs = proj[..., o1:o2].reshape(b, t, 4, NSA_KV_HEADS, NSA_HEAD_DIM)
    kv_win = proj[:, t - min(WINDOW, t):, o2:o3].reshape(b, min(WINDOW, t), 2, NSA_KV_HEADS, NSA_HEAD_DIM)
    return merged, kv_cs, kv_win


def _nsa_sample_core(proj, caches, layer, win_buf, page_table, cmp_pos, cmp_w1, cmp_w2, tables):
    b, t, _ = proj.shape
    assert t == 1 and win_buf.shape[1] == WINDOW and page_table.shape[1] % PAGE_GROUP == 0
    kt_pages, win_t, n_phys = caches
    merged = _nsa_attn_sample(proj.reshape(b, -1), kt_pages, win_t, layer * b,
                              page_table + layer * n_phys, cmp_pos, cmp_w1, cmp_w2, tables)
    o1 = NSA_Q_W
    o2 = o1 + 4 * NSA_KV_W
    o3 = o2 + 2 * NSA_KV_W
    kv_cs = proj[..., o1:o2].reshape(b, t, 4, NSA_KV_HEADS, NSA_HEAD_DIM)
    kv_win = proj[..., o2:o3].reshape(b, t, 2, NSA_KV_HEADS, NSA_HEAD_DIM)
    new_win = jnp.concatenate([win_buf[:, t:], kv_win], axis=1)
    return merged, kv_cs, new_win


def _pad_cols(w, n):
    return jnp.pad(w, ((0, 0), (0, n - w.shape[1])))


def kernel(x_prompt, x_sample, cache_nsa_kv, cache_nsa_win, state_hgrn, state_ssd, state_ssd_conv, page_table, c_prompt, c_sample, rel_bias, hgrn_lower_bounds, w_ada, b_ada, norm_gains, w_mlp_in, w_mlp_out, nsa_w_in, nsa_cmp_pos, nsa_cmp_w1, nsa_cmp_w2, nsa_w_out, hg_w_in, hg_norm, hg_w_out, ssd_w_in, ssd_conv_w, ssd_conv_b, ssd_dt_bias, ssd_a_log, ssd_d, ssd_norm, ssd_w_out):
    bf16 = jnp.bfloat16
    bp, tp, d = x_prompt.shape
    bs, ts, _ = x_sample.shape
    mp, ms = bp * tp, bs * ts
    lb_p = jax.nn.softmax(hgrn_lower_bounds, axis=0)
    lower_bounds = jnp.cumsum(lb_p, axis=0) - lb_p[0]

    mod = _ada_all(jnp.concatenate([c_prompt, c_sample], axis=0), w_ada, b_ada)
    mod = mod.reshape(DEPTH, bp + bs, ADA_CHUNKS, d)
    mod_p = mod[:, :bp].transpose(0, 2, 1, 3)[:, :, :, None, :]
    mod_s = mod[:, bp:].transpose(0, 2, 1, 3)[:, :, None, :, :]

    xp = x_prompt.reshape(mp, d)
    xs = x_sample.reshape(ms, d)
    tm_p, tm_s = PROMPT_ROW_TILE, ms
    nsa_tables = _nsa_prompt_tables_t(rel_bias, tp)
    nsa_tables_s = _nsa_sample_tables(rel_bias, page_table.shape[1] * PAGE_SIZE, cache_nsa_win.shape[2])
    n_phys = cache_nsa_kv.shape[1]
    n_all = cache_nsa_kv.shape[0] * n_phys
    kt_pages = jnp.transpose(cache_nsa_kv, (0, 1, 3, 4, 5, 2)).reshape(n_all, 4 * NSA_KV_HEADS, NSA_HEAD_DIM, PAGE_SIZE)
    win_t = jnp.transpose(cache_nsa_win, (0, 1, 3, 4, 5, 2)).reshape(-1, 2, NSA_KV_W, cache_nsa_win.shape[2])
    nsa_caches = (kt_pages, win_t, n_phys)

    kv_p, kv_s, win_p, win_s = [], [], [], []
    hg_p, hg_s, ssd_p, ssd_s, conv_p, conv_s = [], [], [], [], [], []
    for i in range(DEPTH):
        j = i // N_MIXERS
        kind = i % N_MIXERS
        g = norm_gains[i]
        shp_m, scp_m, gtp_m, shp_f, scp_f, gtp_f = [mod_p[i, c] for c in range(ADA_CHUNKS)]
        shs_m, scs_m, gts_m, shs_f, scs_f, gts_f = [mod_s[i, c] for c in range(ADA_CHUNKS)]
        if kind == 0:
            n_pad = NSA_PROJ_W
            w_in = _pad_cols(nsa_w_in[j], n_pad).astype(bf16)
            w_out = nsa_w_out[j].astype(bf16)
            pp = _norm_mod_matmul(xp, g[0], scp_m, shp_m, w_in, tp, tm_p).reshape(bp, tp, n_pad)
            ps = _norm_mod_matmul(xs, g[0], scs_m, shs_m, w_in, ts, tm_s).reshape(bs, ts, n_pad)
            ap, new_kv_p, new_win_p = _nsa_prompt_core(pp, nsa_cmp_pos[j], nsa_cmp_w1[j], nsa_cmp_w2[j], nsa_tables)
            as_, new_kv_s, new_win_s = _nsa_sample_core(ps, nsa_caches, j, cache_nsa_win[j], page_table,
                                                        nsa_cmp_pos[j], nsa_cmp_w1[j], nsa_cmp_w2[j], nsa_tables_s)
            kv_p.append(new_kv_p)
            kv_s.append(new_kv_s)
            win_p.append(new_win_p)
            win_s.append(new_win_s)
        elif kind == 1:
            w_in = hg_w_in[j].astype(bf16)
            w_out = hg_w_out[j].astype(bf16)
            pp = _norm_mod_matmul(xp, g[0], scp_m, shp_m, w_in, tp, tm_p).reshape(bp, tp, -1)
            ps = _norm_mod_matmul(xs, g[0], scs_m, shs_m, w_in, ts, tm_s).reshape(bs, ts, -1)
            ap, new_hp = _hgrn_prompt(pp, lower_bounds[i], hg_norm[j])
            as_, new_hs = _hgrn_step(ps.reshape(bs, -1), state_hgrn[j], lower_bounds[i], hg_norm[j])
            hg_p.append(new_hp)
            hg_s.append(new_hs)
        else:
            n_pad = SSD_PROJ_W
            w_in = _pad_cols(ssd_w_in[j], n_pad).astype(bf16)
            w_out = ssd_w_out[j].astype(bf16)
            pp = _norm_mod_matmul(xp, g[0], scp_m, shp_m, w_in, tp, tm_p).reshape(bp, tp, n_pad)
            ps = _norm_mod_matmul(xs, g[0], scs_m, shs_m, w_in, ts, tm_s).reshape(bs, ts, n_pad)
            ap, new_sp, new_cp = _ssd_prompt(pp, ssd_conv_w[j], ssd_conv_b[j], ssd_dt_bias[j],
                                             ssd_a_log[j], ssd_d[j], ssd_norm[j])
            as_, new_ss, new_cs = _ssd_step(ps.reshape(bs, -1), state_ssd[j], state_ssd_conv[j], ssd_conv_w[j],
                                            ssd_conv_b[j], ssd_dt_bias[j], ssd_a_log[j], ssd_d[j], ssd_norm[j])
            ssd_p.append(new_sp)
            ssd_s.append(new_ss)
            conv_p.append(new_cp)
            conv_s.append(new_cs)
        xp = _matmul_norm_res(ap, w_out, xp, g[1], gtp_m, tp, tm_p)
        xs = _matmul_norm_res(as_, w_out, xs, g[1], gts_m, ts, tm_s)
        w1 = w_mlp_in[i].astype(bf16)
        w2 = w_mlp_out[i].astype(bf16)
        xp = _mlp(xp, g[2], scp_f, shp_f, w1, w2, g[3], gtp_f, tp, tm_p)
        xs = _mlp(xs, g[2], scs_f, shs_f, w1, w2, g[3], gts_f, ts, tm_s)
    return (xp.reshape(bp, tp, d), xs.reshape(bs, ts, d),
            jnp.stack(kv_p), jnp.stack(kv_s), jnp.stack(win_p), jnp.stack(win_s),
            jnp.stack(hg_p), jnp.stack(hg_s), jnp.stack(ssd_p), jnp.stack(ssd_s),
            jnp.stack(conv_p), jnp.stack(conv_s))
```
